```python
import math
import jax, jax.numpy as jnp
from jax import lax
import numpy as np

D_MODEL = 1024
BATCH = 8
SEQ = 8192
DEPTH = 2

CTX_LEN = 256
GRID_W = 64
ROPE_BASE = 10000.0
QBLOCK = 128

DEEPNORM_ALPHA = (2 * DEPTH) ** 0.25
DEEPNORM_BETA = (8 * DEPTH) ** -0.25

CHUNK = 128
GMLP_GROUPS = 4
GMLP_GROUP_CH = 128
GMLP_WIDTH = GMLP_GROUPS * GMLP_GROUP_CH

MLA_HEADS = 8
MLA_NOPE = 64
MLA_ROPE = 32
MLA_V = 64
MLA_Q_RANK = 256
MLA_KV_RANK = 128
MLA_WIDTH = MLA_HEADS * MLA_V
EVEN_SPLITS = [GMLP_WIDTH, 2 * GMLP_WIDTH, 2 * GMLP_WIDTH + MLA_Q_RANK,
               2 * GMLP_WIDTH + MLA_Q_RANK + MLA_KV_RANK]
EVEN_IN = 2 * GMLP_WIDTH + MLA_Q_RANK + MLA_KV_RANK + MLA_ROPE

DIFF_HEADS = 8
DIFF_HEAD_DIM = D_MODEL // DIFF_HEADS // 2
DIFF_WIDTH = DIFF_HEADS * 2 * DIFF_HEAD_DIM
ODD_IN = 3 * DIFF_WIDTH

N_EXPERTS = 16
EXPERT_FF = 1024
CAPACITY_FACTOR = 2

kernel_name = "hybrid_gmlp_mla_diffattn_ecmoe_dit"


def layer_norm(x, g, b, eps=1e-5):
    xf = x.astype(jnp.float32)
    mu = jnp.mean(xf, -1, keepdims=True)
    var = jnp.mean(jnp.square(xf - mu), -1, keepdims=True)
    return ((xf - mu) * lax.rsqrt(var + eps) * g + b).astype(x.dtype)


def rms_norm(x, g, eps=1e-6):
    xf = x.astype(jnp.float32)
    return (xf * lax.rsqrt(jnp.mean(jnp.square(xf), -1, keepdims=True) + eps) * g).astype(x.dtype)


def axial_rope_tables(n_tokens, dim, dtype):
    n_rows = n_tokens // GRID_W
    row = jnp.repeat(jnp.arange(n_rows, dtype=jnp.float32), GRID_W)
    col = jnp.tile(jnp.arange(GRID_W, dtype=jnp.float32), n_rows)
    n_freq = dim // 4
    inv_freq = ROPE_BASE ** (-jnp.arange(n_freq, dtype=jnp.float32) / n_freq)
    ang = jnp.concatenate([row[:, None] * inv_freq, col[:, None] * inv_freq], -1)
    return jnp.cos(ang).astype(dtype), jnp.sin(ang).astype(dtype)


def apply_rope(x, cos, sin):
    x1 = x[..., 0::2]
    x2 = x[..., 1::2]
    cc = cos[None, :, None, :]
    ss = sin[None, :, None, :]
    return jnp.stack([x1 * cc - x2 * ss, x1 * ss + x2 * cc], -1).reshape(x.shape)


def attention_blocked(qs, ks, v, coeffs):
    bn, n, h, d = qs[0].shape
    scale = d ** -0.5
    nblk = n // QBLOCK
    qb = tuple(q.reshape(bn, nblk, QBLOCK, h, d).swapaxes(0, 1) for q in qs)

    def block(qblk):
        p = 0.0
        for coef, q, k in zip(coeffs, qblk, ks):
            s = jnp.einsum('bqhd,bkhd->bhqk', q, k).astype(jnp.float32) * scale
            p = p + coef * jax.nn.softmax(s, axis=-1)
        return jnp.einsum('bhqk,bkhe->bqhe', p.astype(v.dtype), v)

    out = lax.map(block, qb)
    return out.swapaxes(0, 1).reshape(bn, n, h, v.shape[-1])


def modulation(cond, w_mod, b_mod):
    m = jax.nn.silu(cond) @ w_mod + b_mod
    return jnp.split(m[..., None, :], 6, axis=-1)


def chunk_gmlp(u, v, ln_g, ln_b, ws, bs):
    bn, n, w = v.shape
    vn = layer_norm(v, ln_g, ln_b).reshape(bn, n // CHUNK, CHUNK, GMLP_GROUPS, GMLP_GROUP_CH)
    mixed = jnp.einsum('gpq,bnqgc->bnpgc', ws, vn) + bs.T[None, None, :, :, None]
    return u * mixed.reshape(bn, n, w)


def even_project(h, p, rope):
    bn, n, _ = h.shape
    u, v, cq, ckv, kr = jnp.split(h @ p['w_in'], EVEN_SPLITS, axis=-1)
    a = chunk_gmlp(jax.nn.gelu(u, approximate=False), jax.nn.gelu(v, approximate=False),
                   p['gmlp_ln_g'], p['gmlp_ln_b'], p['gmlp_ws'], p['gmlp_bs'])
    q = (rms_norm(cq, p['mla_q_norm']) @ p['mla_w_uq']).reshape(bn, n, MLA_HEADS, MLA_NOPE + MLA_ROPE)
    kv = (rms_norm(ckv, p['mla_kv_norm']) @ p['mla_w_ukv']).reshape(bn, n, MLA_HEADS, MLA_NOPE + MLA_V)
    q_nope, q_rope = q[..., :MLA_NOPE], q[..., MLA_NOPE:]
    k_nope, val = kv[..., :MLA_NOPE], kv[..., MLA_NOPE:]
    kr = kr[:, :, None, :]
    if rope is not None:
        q_rope = apply_rope(q_rope, *rope)
        kr = apply_rope(kr, *rope)
    q = jnp.concatenate([q_nope, q_rope], -1)
    k = jnp.concatenate([k_nope, jnp.broadcast_to(kr, (bn, n, MLA_HEADS, MLA_ROPE))], -1)
    return a, q, k, val


def even_mixer(h_lat, h_ctx, p, need_ctx):
    bn, n, _ = h_lat.shape
    rope = axial_rope_tables(n, MLA_ROPE, h_lat.dtype)
    a_l, q_l, k_l, v_l = even_project(h_lat, p, rope)
    a_c, q_c, k_c, v_c = even_project(h_ctx, p, None)
    o_l = attention_blocked((q_l,), (jnp.concatenate([k_l, k_c], 1),),
                            jnp.concatenate([v_l, v_c], 1), (1.0,))
    out_l = jnp.concatenate([a_l, o_l.reshape(bn, n, MLA_WIDTH)], -1) @ p['w_out']
    out_c = None
    if need_ctx:
        o_c = attention_blocked((q_c,), (k_c,), v_c, (1.0,))
        out_c = jnp.concatenate([a_c, o_c.reshape(bn, h_ctx.shape[1], MLA_WIDTH)], -1) @ p['w_out']
    return out_l, out_c


def odd_mixer(h_lat, h_ctx, p, layer_idx, need_ctx):
    lam_init = 0.8 - 0.6 * math.exp(-0.3 * layer_idx)
    lam = (jnp.exp(jnp.sum(p['lambda_q1'].astype(jnp.float32) * p['lambda_k1']))
           - jnp.exp(jnp.sum(p['lambda_q2'].astype(jnp.float32) * p['lambda_k2'])) + lam_init)
    bn, n, _ = h_lat.shape
    m = h_ctx.shape[1]
    rope = axial_rope_tables(n, DIFF_HEAD_DIM, h_lat.dtype)
    q, k, v = jnp.split(h_lat @ p['w_in'], 3, axis=-1)
    q = apply_rope(q.reshape(bn, n, 2 * DIFF_HEADS, DIFF_HEAD_DIM), *rope).reshape(bn, n, DIFF_HEADS, 2, DIFF_HEAD_DIM)
    k = apply_rope(k.reshape(bn, n, 2 * DIFF_HEADS, DIFF_HEAD_DIM), *rope).reshape(bn, n, DIFF_HEADS, 2, DIFF_HEAD_DIM)
    v = v.reshape(bn, n, DIFF_HEADS, 2 * DIFF_HEAD_DIM)
    k_c, v_c = jnp.split(h_ctx @ p['w_in'][:, DIFF_WIDTH:], 2, axis=-1)
    k_c = k_c.reshape(bn, m, DIFF_HEADS, 2, DIFF_HEAD_DIM)
    v_c = v_c.reshape(bn, m, DIFF_HEADS, 2 * DIFF_HEAD_DIM)
    kk = jnp.concatenate([k, k_c], 1)
    vv = jnp.concatenate([v, v_c], 1)
    o = attention_blocked((q[..., 0, :], q[..., 1, :]), (kk[..., 0, :], kk[..., 1, :]), vv, (1.0, -lam))
    out_l = (rms_norm(o, p['subln_g']) * (1.0 - lam_init)).reshape(bn, n, DIFF_WIDTH) @ p['w_out']
    out_c = None
    if need_ctx:
        q_c = (h_ctx @ p['w_in'][:, :DIFF_WIDTH]).reshape(bn, m, DIFF_HEADS, 2, DIFF_HEAD_DIM)
        o_c = attention_blocked((q_c[..., 0, :], q_c[..., 1, :]), (k_c[..., 0, :], k_c[..., 1, :]), v_c, (1.0, -lam))
        out_c = (rms_norm(o_c, p['subln_g']) * (1.0 - lam_init)).reshape(bn, m, DIFF_WIDTH) @ p['w_out']
    return out_l, out_c


def expert_choice_ffn(h, router, w_gate, w_up, w_down):
    n, d = h.shape[1], h.shape[2]
    cap = CAPACITY_FACTOR * n // N_EXPERTS
    aff = jax.nn.softmax(jnp.einsum('bnd,de->bne', h, router).astype(jnp.float32), axis=-1)
    g, idx = lax.top_k(aff.transpose(0, 2, 1), cap)
    xs = jax.vmap(lambda hb, ib: hb[ib])(h, idx)
    hid = jax.nn.silu(jnp.einsum('becd,edf->becf', xs, w_gate)) * jnp.einsum('becd,edf->becf', xs, w_up)
    y = jnp.einsum('becf,efd->becd', hid, w_down) * g[..., None].astype(h.dtype)
    return jax.vmap(lambda yb, ib: jnp.zeros((n, d), yb.dtype).at[ib.reshape(-1)].add(yb.reshape(-1, d)))(y, idx)


def setup_inputs(seed: int = 0) -> dict:
    key = jax.random.key(seed)
    ks = iter(jax.random.split(key, 64))
    D = D_MODEL

    def nrm(shape, scale=1.0):
        return jax.random.normal(next(ks), shape, jnp.float32) * scale

    def gain(n):
        return 1.0 + nrm((n,), 0.02)

    inp = {}
    inp['x'] = nrm((BATCH, SEQ, D))
    inp['c'] = nrm((BATCH, D))
    inp['ctx'] = nrm((BATCH, CTX_LEN, D))
    inp['c_ctx'] = nrm((D,))
    inp['w_mod_0'] = nrm((D, 6 * D), 0.25 * D ** -0.5)
    inp['b_mod_0'] = nrm((6 * D,), 0.01)
    inp['w_in_0'] = nrm((D, EVEN_IN), D ** -0.5)
    inp['gmlp_ln_g_0'] = gain(GMLP_WIDTH)
    inp['gmlp_ln_b_0'] = nrm((GMLP_WIDTH,), 0.02)
    inp['gmlp_ws_0'] = nrm((GMLP_GROUPS, CHUNK, CHUNK), CHUNK ** -0.5)
    inp['gmlp_bs_0'] = 1.0 + nrm((GMLP_GROUPS, CHUNK), 0.02)
    inp['mla_q_norm_0'] = gain(MLA_Q_RANK)
    inp['mla_w_uq_0'] = nrm((MLA_Q_RANK, MLA_HEADS * (MLA_NOPE + MLA_ROPE)), MLA_Q_RANK ** -0.5)
    inp['mla_kv_norm_0'] = gain(MLA_KV_RANK)
    inp['mla_w_ukv_0'] = nrm((MLA_KV_RANK, MLA_HEADS * (MLA_NOPE + MLA_V)), MLA_KV_RANK ** -0.5)
    inp['w_out_0'] = nrm((GMLP_WIDTH + MLA_WIDTH, D), DEEPNORM_BETA * (GMLP_WIDTH + MLA_WIDTH) ** -0.5)
    inp['ln_mix_g_0'] = gain(D)
    inp['ln_mix_b_0'] = nrm((D,), 0.02)
    inp['router_0'] = nrm((D, N_EXPERTS), D ** -0.5)
    inp['w_gate_0'] = nrm((N_EXPERTS, D, EXPERT_FF), D ** -0.5)
    inp['w_up_0'] = nrm((N_EXPERTS, D, EXPERT_FF), D ** -0.5)
    inp['w_down_0'] = nrm((N_EXPERTS, EXPERT_FF, D), DEEPNORM_BETA * EXPERT_FF ** -0.5)
    inp['ln_ffn_g_0'] = gain(D)
    inp['ln_ffn_b_0'] = nrm((D,), 0.02)
    inp['w_mod_1'] = nrm((D, 6 * D), 0.25 * D ** -0.5)
    inp['b_mod_1'] = nrm((6 * D,), 0.01)
    inp['w_in_1'] = nrm((D, ODD_IN), D ** -0.5)
    inp['lambda_q1_1'] = nrm((DIFF_HEAD_DIM,), 0.1)
    inp['lambda_k1_1'] = nrm((DIFF_HEAD_DIM,), 0.1)
    inp['lambda_q2_1'] = nrm((DIFF_HEAD_DIM,), 0.1)
    inp['lambda_k2_1'] = nrm((DIFF_HEAD_DIM,), 0.1)
    inp['subln_g_1'] = gain(2 * DIFF_HEAD_DIM)
    inp['w_out_1'] = nrm((DIFF_WIDTH, D), DEEPNORM_BETA * DIFF_WIDTH ** -0.5)
    inp['ln_mix_g_1'] = gain(D)
    inp['ln_mix_b_1'] = nrm((D,), 0.02)
    inp['router_1'] = nrm((D, N_EXPERTS), D ** -0.5)
    inp['w_gate_1'] = nrm((N_EXPERTS, D, EXPERT_FF), D ** -0.5)
    inp['w_up_1'] = nrm((N_EXPERTS, D, EXPERT_FF), D ** -0.5)
    inp['w_down_1'] = nrm((N_EXPERTS, EXPERT_FF, D), DEEPNORM_BETA * EXPERT_FF ** -0.5)
    inp['ln_ffn_g_1'] = gain(D)
    inp['ln_ffn_b_1'] = nrm((D,), 0.02)
    return inp


def reference(x, c, ctx, c_ctx,
              w_mod_0, b_mod_0, w_in_0, gmlp_ln_g_0, gmlp_ln_b_0, gmlp_ws_0, gmlp_bs_0,
              mla_q_norm_0, mla_w_uq_0, mla_kv_norm_0, mla_w_ukv_0, w_out_0,
              ln_mix_g_0, ln_mix_b_0, router_0, w_gate_0, w_up_0, w_down_0, ln_ffn_g_0, ln_ffn_b_0,
              w_mod_1, b_mod_1, w_in_1, lambda_q1_1, lambda_k1_1, lambda_q2_1, lambda_k2_1, subln_g_1,
              w_out_1, ln_mix_g_1, ln_mix_b_1, router_1, w_gate_1, w_up_1, w_down_1, ln_ffn_g_1, ln_ffn_b_1):
    p0 = dict(w_mod=w_mod_0, b_mod=b_mod_0, w_in=w_in_0, gmlp_ln_g=gmlp_ln_g_0, gmlp_ln_b=gmlp_ln_b_0,
              gmlp_ws=gmlp_ws_0, gmlp_bs=gmlp_bs_0, mla_q_norm=mla_q_norm_0, mla_w_uq=mla_w_uq_0,
              mla_kv_norm=mla_kv_norm_0, mla_w_ukv=mla_w_ukv_0, w_out=w_out_0,
              ln_mix_g=ln_mix_g_0, ln_mix_b=ln_mix_b_0, router=router_0, w_gate=w_gate_0,
              w_up=w_up_0, w_down=w_down_0, ln_ffn_g=ln_ffn_g_0, ln_ffn_b=ln_ffn_b_0)
    p1 = dict(w_mod=w_mod_1, b_mod=b_mod_1, w_in=w_in_1, lambda_q1=lambda_q1_1, lambda_k1=lambda_k1_1,
              lambda_q2=lambda_q2_1, lambda_k2=lambda_k2_1, subln_g=subln_g_1, w_out=w_out_1,
              ln_mix_g=ln_mix_g_1, ln_mix_b=ln_mix_b_1, router=router_1, w_gate=w_gate_1,
              w_up=w_up_1, w_down=w_down_1, ln_ffn_g=ln_ffn_g_1, ln_ffn_b=ln_ffn_b_1)
    layers = (p0, p1)
    x_lat, x_ctx = x, ctx
    for l in range(DEPTH):
        p = layers[l]
        need_ctx = l < DEPTH - 1
        sh_a, sc_a, g_a, sh_f, sc_f, g_f = modulation(c, p['w_mod'], p['b_mod'])
        csh_a, csc_a, cg_a, csh_f, csc_f, cg_f = modulation(c_ctx, p['w_mod'], p['b_mod'])
        h_lat = x_lat * (1.0 + sc_a) + sh_a
        h_ctx = x_ctx * (1.0 + csc_a) + csh_a
        if l % 2 == 0:
            m_lat, m_ctx = even_mixer(h_lat, h_ctx, p, need_ctx)
        else:
            m_lat, m_ctx = odd_mixer(h_lat, h_ctx, p, l, need_ctx)
        x_lat = layer_norm(DEEPNORM_ALPHA * x_lat + (1.0 + g_a) * m_lat, p['ln_mix_g'], p['ln_mix_b'])
        f_lat = expert_choice_ffn(x_lat * (1.0 + sc_f) + sh_f, p['router'], p['w_gate'], p['w_up'], p['w_down'])
        x_lat = layer_norm(DEEPNORM_ALPHA * x_lat + (1.0 + g_f) * f_lat, p['ln_ffn_g'], p['ln_ffn_b'])
        if need_ctx:
            x_ctx = layer_norm(DEEPNORM_ALPHA * x_ctx + (1.0 + cg_a) * m_ctx, p['ln_mix_g'], p['ln_mix_b'])
            f_ctx = expert_choice_ffn(x_ctx * (1.0 + csc_f) + csh_f, p['router'], p['w_gate'], p['w_up'], p['w_down'])
            x_ctx = layer_norm(DEEPNORM_ALPHA * x_ctx + (1.0 + cg_f) * f_ctx, p['ln_ffn_g'], p['ln_ffn_b'])
    return x_lat
```

```python
import functools
import math

import jax
import jax.numpy as jnp
from jax import lax
from jax.experimental import pallas as pl
from jax.experimental.pallas import tpu as pltpu

F32 = jnp.float32
BF16 = jnp.bfloat16

DEPTH = 2
GRID_W = 64
ROPE_BASE = 10000.0
CHUNK = 128
GMLP_GROUPS = 4
GMLP_GROUP_CH = 128
GMLP_WIDTH = GMLP_GROUPS * GMLP_GROUP_CH
MLA_HEADS = 8
MLA_NOPE = 64
MLA_ROPE = 32
MLA_V = 64
MLA_Q_RANK = 256
MLA_KV_RANK = 128
DIFF_HEADS = 8
DIFF_HEAD_DIM = 64
CAPACITY_FACTOR = 2
DEEPNORM_ALPHA = (2 * DEPTH) ** 0.25
LN_EPS = 1e-5
RMS_EPS = 1e-6
LOG2E = 1.4426950408889634

HEAD_SLOT = 128
NEG_BIG = -1e30
VMEM_LIMIT = 56 * 1024 * 1024


def _cparams(sem):
    return pltpu.CompilerParams(dimension_semantics=sem, vmem_limit_bytes=VMEM_LIMIT)


def _tile(n, pref):
    return pref if n % pref == 0 else n


def _ln(x, g, b):
    mu = jnp.mean(x, axis=-1, keepdims=True)
    xc = x - mu
    var = jnp.mean(xc * xc, axis=-1, keepdims=True)
    return xc * lax.rsqrt(var + LN_EPS) * g + b


def _rms(x, g):
    return x * lax.rsqrt(jnp.mean(x * x, axis=-1, keepdims=True) + RMS_EPS) * g


def _gelu(x):
    return 0.5 * x * (1.0 + lax.erf(x * (2.0 ** -0.5)))


def _dot(a, b):
    return jnp.dot(a, b, preferred_element_type=F32)


def _dot_nt(a, b):
    return lax.dot_general(a, b, (((1,), (1,)), ((), ())), preferred_element_type=F32)


def _dot_tn(a, b):
    return lax.dot_general(a, b, (((0,), (0,)), ((), ())), preferred_element_type=F32)


def _mod_kernel(c_ref, w_ref, b_ref, o_ref):
    c = c_ref[...]
    s = c / (1.0 + jnp.exp(-c))
    o_ref[...] = _dot(s.astype(BF16), w_ref[...].astype(BF16)) + b_ref[...]


def _modulation(cond, w_mod, b_mod):
    r, d = cond.shape
    n = w_mod.shape[1]
    tn = _tile(n, 1024)
    return pl.pallas_call(
        _mod_kernel,
        out_shape=jax.ShapeDtypeStruct((r, n), F32),
        grid=(n // tn,),
        in_specs=[pl.BlockSpec((r, d), lambda j: (0, 0)),
                  pl.BlockSpec((d, tn), lambda j: (0, j)),
                  pl.BlockSpec((1, tn), lambda j: (0, j))],
        out_specs=pl.BlockSpec((r, tn), lambda j: (0, j)),
        compiler_params=_cparams(("parallel",)),
        name="modulation",
    )(cond, w_mod, b_mod.reshape(1, n))


def _proj0_kernel(x_ref, sc_ref, sh_ref, win_ref, lng_ref, lnb_ref, ws_ref, bs_ref,
                  qn_ref, wq_ref, kvn_ref, wk_ref, wv_ref, ck_ref, sk_ref, cq_ref, sq_ref,
                  a_ref, q_ref, k_ref, v_ref, *, q_scale):
    tm = x_ref.shape[1]
    h = (x_ref[0] * (1.0 + sc_ref[0]) + sh_ref[0]).astype(BF16)
    y = _dot(h, win_ref[...])
    w = GMLP_WIDTH
    u = _gelu(y[:, :w])
    vn = _ln(_gelu(y[:, w:2 * w]), lng_ref[...], lnb_ref[...]).astype(BF16)
    for ci in range(tm // CHUNK):
        r0 = ci * CHUNK
        for g in range(GMLP_GROUPS):
            c0 = g * GMLP_GROUP_CH
            mixed = _dot(ws_ref[g], vn[r0:r0 + CHUNK, c0:c0 + GMLP_GROUP_CH]) + bs_ref[g]
            a_ref[0, r0:r0 + CHUNK, c0:c0 + GMLP_GROUP_CH] = (
                u[r0:r0 + CHUNK, c0:c0 + GMLP_GROUP_CH] * mixed).astype(BF16)

    o = 2 * w
    cq = _rms(y[:, o:o + MLA_Q_RANK], qn_ref[...]).astype(BF16)
    qt = _dot_nt(wq_ref[...], cq) * q_scale
    q_ref[0] = qt.astype(BF16)
    hr = MLA_ROPE // 2
    cq_t = cq_ref[...]
    sq_t = sq_ref[...]
    for hd in range(MLA_HEADS):
        r0 = hd * HEAD_SLOT + MLA_NOPE
        x1 = qt[r0:r0 + hr]
        x2 = qt[r0 + hr:r0 + 2 * hr]
        blk = jnp.concatenate([x1, x2], axis=0)
        swp = jnp.concatenate([x2, x1], axis=0)
        q_ref[0, r0:r0 + 2 * hr, :] = (blk * cq_t + swp * sq_t).astype(BF16)

    o += MLA_Q_RANK
    ckv = _rms(y[:, o:o + MLA_KV_RANK], kvn_ref[...]).astype(BF16)
    o += MLA_KV_RANK
    kr = y[:, o:o + HEAD_SLOT] * ck_ref[...] + y[:, o + HEAD_SLOT:o + 2 * HEAD_SLOT] * sk_ref[...]
    k_ref[0] = (_dot(ckv, wk_ref[...]) + jnp.tile(kr, (1, MLA_HEADS))).astype(BF16)
    v_ref[0] = _dot_nt(wv_ref[...], ckv).astype(BF16)


def _proj0(x, sc, sh, wts, tabs):
    b, n, d = x.shape
    tm = _tile(n, 512)
    ck, sk, cq, sq = tabs
    win, lng, lnb, ws, bsb, qn, wq, kvn, wk, wv = wts
    hw = MLA_HEADS * HEAD_SLOT
    vw = MLA_HEADS * MLA_V
    full = lambda a: pl.BlockSpec(a.shape, lambda bi, ti: (0,) * a.ndim)
    q_scale = (MLA_NOPE + MLA_ROPE) ** -0.5 * LOG2E
    return pl.pallas_call(
        functools.partial(_proj0_kernel, q_scale=q_scale),
        out_shape=(jax.ShapeDtypeStruct((b, n, GMLP_WIDTH), BF16),
                   jax.ShapeDtypeStruct((b, hw, n), BF16),
                   jax.ShapeDtypeStruct((b, n, hw), BF16),
                   jax.ShapeDtypeStruct((b, vw, n), BF16)),
        grid=(b, n // tm),
        in_specs=[pl.BlockSpec((1, tm, d), lambda bi, ti: (bi, ti, 0)),
                  pl.BlockSpec((1, 1, d), lambda bi, ti: (bi, 0, 0)),
                  pl.BlockSpec((1, 1, d), lambda bi, ti: (bi, 0, 0)),
                  full(win), full(lng), full(lnb), full(ws), full(bsb),
                  full(qn), full(wq), full(kvn), full(wk), full(wv),
                  pl.BlockSpec((tm, HEAD_SLOT), lambda bi, ti: (ti, 0)),
                  pl.BlockSpec((tm, HEAD_SLOT), lambda bi, ti: (ti, 0)),
                  pl.BlockSpec((MLA_ROPE, tm), lambda bi, ti: (0, ti)),
                  pl.BlockSpec((MLA_ROPE, tm), lambda bi, ti: (0, ti))],
        out_specs=(pl.BlockSpec((1, tm, GMLP_WIDTH), lambda bi, ti: (bi, ti, 0)),
                   pl.BlockSpec((1, hw, tm), lambda bi, ti: (bi, 0, ti)),
                   pl.BlockSpec((1, tm, hw), lambda bi, ti: (bi, ti, 0)),
                   pl.BlockSpec((1, vw, tm), lambda bi, ti: (bi, 0, ti))),
        compiler_params=_cparams(("parallel", "parallel")),
        name="proj0",
    )(x, sc, sh, win, lng, lnb, ws, bsb, qn, wq, kvn, wk, wv, ck, sk, cq, sq)


def _proj1_kernel(x_ref, sc_ref, sh_ref, wq_ref, wk_ref, wv_ref, ck_ref, sk_ref, cq_ref, sq_ref,
                  q_ref, k_ref, v_ref, *, q_scale):
    h = (x_ref[0] * (1.0 + sc_ref[0]) + sh_ref[0]).astype(BF16)
    n_sub = 2 * DIFF_HEADS
    hd = DIFF_HEAD_DIM
    hr = hd // 2
    qt = _dot_nt(wq_ref[...], h) * q_scale
    cq_t = cq_ref[...]
    sq_t = sq_ref[...]
    for s in range(n_sub):
        r0 = s * hd
        x1 = qt[r0:r0 + hr]
        x2 = qt[r0 + hr:r0 + hd]
        swp = jnp.concatenate([x2, x1], axis=0)
        q_ref[0, r0:r0 + hd, :] = (qt[r0:r0 + hd] * cq_t + swp * sq_t).astype(BF16)
    k = _dot(h, wk_ref[...])
    width = k.shape[1]
    lane = lax.broadcasted_iota(jnp.int32, k.shape, 1)
    partner = jnp.where((lane % hd) < hr, pltpu.roll(k, width - hr, 1), pltpu.roll(k, hr, 1))
    reps = width // HEAD_SLOT
    k_ref[0] = (k * jnp.tile(ck_ref[...], (1, reps)) + partner * jnp.tile(sk_ref[...], (1, reps))).astype(BF16)
    v_ref[0] = _dot_nt(wv_ref[...], h).astype(BF16)


def _proj1(x, sc, sh, wts, tabs):
    b, n, d = x.shape
    tm = _tile(n, 512)
    wq, wk, wv = wts
    ck, sk, cq, sq = tabs
    dw = wq.shape[0]
    full = lambda a: pl.BlockSpec(a.shape, lambda bi, ti: (0,) * a.ndim)
    q_scale = DIFF_HEAD_DIM ** -0.5 * LOG2E
    return pl.pallas_call(
        functools.partial(_proj1_kernel, q_scale=q_scale),
        out_shape=(jax.ShapeDtypeStruct((b, dw, n), BF16),
                   jax.ShapeDtypeStruct((b, n, dw), BF16),
                   jax.ShapeDtypeStruct((b, dw, n), BF16)),
        grid=(b, n // tm),
        in_specs=[pl.BlockSpec((1, tm, d), lambda bi, ti: (bi, ti, 0)),
                  pl.BlockSpec((1, 1, d), lambda bi, ti: (bi, 0, 0)),
                  pl.BlockSpec((1, 1, d), lambda bi, ti: (bi, 0, 0)),
                  full(wq), full(wk), full(wv),
                  pl.BlockSpec((tm, HEAD_SLOT), lambda bi, ti: (ti, 0)),
                  pl.BlockSpec((tm, HEAD_SLOT), lambda bi, ti: (ti, 0)),
                  pl.BlockSpec((DIFF_HEAD_DIM, tm), lambda bi, ti: (0, ti)),
                  pl.BlockSpec((DIFF_HEAD_DIM, tm), lambda bi, ti: (0, ti))],
        out_specs=(pl.BlockSpec((1, dw, tm), lambda bi, ti: (bi, 0, ti)),
                   pl.BlockSpec((1, tm, dw), lambda bi, ti: (bi, ti, 0)),
                   pl.BlockSpec((1, dw, tm), lambda bi, ti: (bi, 0, ti))),
        compiler_params=_cparams(("parallel", "parallel")),
        name="proj1",
    )(x, sc, sh, wq, wk, wv, ck, sk, cq, sq)


def _softmax_step(s, m, l):
    mn = jnp.maximum(m, jnp.max(s, axis=0, keepdims=True))
    alpha = jnp.exp2(m - mn)
    p = jnp.exp2(s - mn)
    return mn, alpha, p, alpha * l + jnp.sum(p, axis=0, keepdims=True)


def _attn0_kernel(q_ref, k_ref, v_ref, o_ref, *, tk):
    q = q_ref[0]
    tq = q.shape[1]
    dv = v_ref.shape[1]
    nchunk = k_ref.shape[1] // tk

    def body(c, carry):
        m, l, acc = carry
        off = pl.multiple_of(c * tk, tk)
        s = _dot(k_ref[0, pl.ds(off, tk), :], q)
        m, alpha, p, l = _softmax_step(s, m, l)
        acc = alpha * acc + _dot(v_ref[0, :, pl.ds(off, tk)], p.astype(BF16))
        return m, l, acc

    init = (jnp.full((1, tq), NEG_BIG, F32), jnp.zeros((1, tq), F32), jnp.zeros((dv, tq), F32))
    _, l, acc = lax.fori_loop(0, nchunk, body, init)
    o_ref[0] = (acc / l).astype(BF16)


def _attn1_kernel(q_ref, k_ref, v_ref, lq1_ref, lk1_ref, lq2_ref, lk2_ref, g_ref, o_ref, *, tk, lam_init):
    q = q_ref[0]
    tq = q.shape[1]
    dv = v_ref.shape[1]
    nchunk = k_ref.shape[1] // tk
    row = lax.broadcasted_iota(jnp.int32, q.shape, 0)
    zero = jnp.zeros_like(q)
    q1 = jnp.where(row < DIFF_HEAD_DIM, q, zero)
    q2 = jnp.where(row >= DIFF_HEAD_DIM, q, zero)

    def body(c, carry):
        m1, l1, a1, m2, l2, a2 = carry
        off = pl.multiple_of(c * tk, tk)
        kc = k_ref[0, pl.ds(off, tk), :]
        vc = v_ref[0, :, pl.ds(off, tk)]
        m1, al1, p1, l1 = _softmax_step(_dot(kc, q1), m1, l1)
        a1 = al1 * a1 + _dot(vc, p1.astype(BF16))
        m2, al2, p2, l2 = _softmax_step(_dot(kc, q2), m2, l2)
        a2 = al2 * a2 + _dot(vc, p2.astype(BF16))
        return m1, l1, a1, m2, l2, a2

    neg = jnp.full((1, tq), NEG_BIG, F32)
    z1 = jnp.zeros((1, tq), F32)
    za = jnp.zeros((dv, tq), F32)
    _, l1, a1, _, l2, a2 = lax.fori_loop(0, nchunk, body, (neg, z1, za, neg, z1, za))
    lam = (jnp.exp(jnp.sum(lq1_ref[...] * lk1_ref[...], axis=-1, keepdims=True))
           - jnp.exp(jnp.sum(lq2_ref[...] * lk2_ref[...], axis=-1, keepdims=True)) + lam_init)
    o = a1 / l1 - lam * (a2 / l2)
    o = o * lax.rsqrt(jnp.mean(o * o, axis=0, keepdims=True) + RMS_EPS) * g_ref[...] * (1.0 - lam_init)
    o_ref[0] = o.astype(BF16)


def _attention(kernel_fn, qt, k, vt, extra=(), *, dv, name):
    b, hw, n = qt.shape
    t = k.shape[1]
    heads = hw // HEAD_SLOT
    tq = _tile(n, 512)
    tk = 256 if t % 256 == 0 else 128
    full = lambda a: pl.BlockSpec(a.shape, lambda bi, hi, qi: (0,) * a.ndim)
    return pl.pallas_call(
        functools.partial(kernel_fn, tk=tk),
        out_shape=jax.ShapeDtypeStruct((b, heads * dv, n), BF16),
        grid=(b, heads, n // tq),
        in_specs=[pl.BlockSpec((1, HEAD_SLOT, tq), lambda bi, hi, qi: (bi, hi, qi)),
                  pl.BlockSpec((1, t, HEAD_SLOT), lambda bi, hi, qi: (bi, 0, hi)),
                  pl.BlockSpec((1, dv, t), lambda bi, hi, qi: (bi, hi, 0))] + [full(a) for a in extra],
        out_specs=pl.BlockSpec((1, dv, tq), lambda bi, hi, qi: (bi, hi, qi)),
        compiler_params=_cparams(("parallel", "parallel", "arbitrary")),
        name=name,
    )(qt, k, vt, *extra)


def _out_kernel(*refs, n_rowmajor):
    (x_ref, g_ref, lng_ref, lnb_ref, scf_ref, shf_ref, rt_ref), rest = refs[:7], refs[7:]
    x1_ref, aff_ref = rest[-2:]
    ops = rest[:-2]
    m = None
    for i in range(len(ops) // 2):
        a = ops[2 * i][0]
        w = ops[2 * i + 1][...]
        part = _dot(a, w) if i < n_rowmajor else _dot_tn(a, w)
        m = part if m is None else m + part
    x1 = _ln(DEEPNORM_ALPHA * x_ref[0] + (1.0 + g_ref[0]) * m, lng_ref[...], lnb_ref[...])
    x1_ref[0] = x1
    hf = (x1 * (1.0 + scf_ref[0]) + shf_ref[0]).astype(BF16)
    lg = _dot_nt(rt_ref[...], hf)
    e = jnp.exp(lg - jnp.max(lg, axis=0, keepdims=True))
    aff_ref[0] = e / jnp.sum(e, axis=0, keepdims=True)


def _mixer_out(x, g, lng, lnb, scf, shf, router_t, rowmajor_ops, chanmajor_ops):
    b, n, d = x.shape
    e = router_t.shape[0]
    tm = _tile(n, 512)
    vec = pl.BlockSpec((1, 1, d), lambda bi, ti: (bi, 0, 0))
    full = lambda a: pl.BlockSpec(a.shape, lambda bi, ti: (0,) * a.ndim)
    args = [x, g, lng, lnb, scf, shf, router_t]
    specs = [pl.BlockSpec((1, tm, d), lambda bi, ti: (bi, ti, 0)), vec, full(lng), full(lnb), vec, vec,
             full(router_t)]
    for a, w in rowmajor_ops:
        args += [a, w]
        specs += [pl.BlockSpec((1, tm, a.shape[2]), lambda bi, ti: (bi, ti, 0)), full(w)]
    for a, w in chanmajor_ops:
        args += [a, w]
        specs += [pl.BlockSpec((1, a.shape[1], tm), lambda bi, ti: (bi, 0, ti)), full(w)]
    return pl.pallas_call(
        functools.partial(_out_kernel, n_rowmajor=len(rowmajor_ops)),
        out_shape=(jax.ShapeDtypeStruct((b, n, d), F32), jax.ShapeDtypeStruct((b, e, n), F32)),
        grid=(b, n // tm),
        in_specs=specs,
        out_specs=(pl.BlockSpec((1, tm, d), lambda bi, ti: (bi, ti, 0)),
                   pl.BlockSpec((1, e, tm), lambda bi, ti: (bi, 0, ti))),
        compiler_params=_cparams(("parallel", "parallel")),
        name="mixer_out",
    )(*args)


ROW_UNROLL = 8


def _gather_kernel(idx_ref, x_ref, sc_ref, sh_ref, o_ref, rows_ref):
    cap = o_ref.shape[2]

    def body(jb, carry):
        for u in range(ROW_UNROLL):
            j = jb * ROW_UNROLL + u
            rows_ref[pl.ds(j, 1), :] = x_ref[0, pl.ds(idx_ref[0, 0, j], 1), :]
        return carry

    lax.fori_loop(0, cap // ROW_UNROLL, body, 0)
    o_ref[0, 0] = (rows_ref[...] * (1.0 + sc_ref[0]) + sh_ref[0]).astype(BF16)


def _moe_gather(x1, idx3, scf, shf, n_exp, cap):
    b, n, d = x1.shape
    dh = d // 2
    return pl.pallas_call(
        _gather_kernel,
        out_shape=jax.ShapeDtypeStruct((b, n_exp, cap, d), BF16),
        grid=(b, 2, n_exp),
        in_specs=[pl.BlockSpec((1, 1, cap), lambda bi, di, ei: (bi * n_exp + ei, 0, 0), memory_space=pltpu.SMEM),
                  pl.BlockSpec((1, n, dh), lambda bi, di, ei: (bi, 0, di)),
                  pl.BlockSpec((1, 1, dh), lambda bi, di, ei: (bi, 0, di)),
                  pl.BlockSpec((1, 1, dh), lambda bi, di, ei: (bi, 0, di))],
        out_specs=pl.BlockSpec((1, 1, cap, dh), lambda bi, di, ei: (bi, ei, 0, di)),
        scratch_shapes=[pltpu.VMEM((cap, dh), F32)],
        compiler_params=_cparams(("parallel", "parallel", "arbitrary")),
        name="moe_gather",
    )(idx3, x1, scf, shf)


def _ffn_kernel(xs_ref, g_ref, wg_ref, wu_ref, wd_ref, y_ref):
    xs = xs_ref[0, 0]
    gate = _dot(xs, wg_ref[0])
    hid = gate / (1.0 + jnp.exp(-gate)) * _dot(xs, wu_ref[0])
    y_ref[0, 0] = (_dot(hid.astype(BF16), wd_ref[0]) * g_ref[0]).astype(BF16)


def _moe_ffn(xs, gates3, wg, wu, wd):
    b, n_exp, cap, d = xs.shape
    ff = wg.shape[2]
    return pl.pallas_call(
        _ffn_kernel,
        out_shape=jax.ShapeDtypeStruct((b, n_exp, cap, d), BF16),
        grid=(n_exp, b),
        in_specs=[pl.BlockSpec((1, 1, cap, d), lambda ei, bi: (bi, ei, 0, 0)),
                  pl.BlockSpec((1, cap, 1), lambda ei, bi: (bi * n_exp + ei, 0, 0)),
                  pl.BlockSpec((1, d, ff), lambda ei, bi: (ei, 0, 0)),
                  pl.BlockSpec((1, d, ff), lambda ei, bi: (ei, 0, 0)),
                  pl.BlockSpec((1, ff, d), lambda ei, bi: (ei, 0, 0))],
        out_specs=pl.BlockSpec((1, 1, cap, d), lambda ei, bi: (bi, ei, 0, 0)),
        compiler_params=_cparams(("parallel", "parallel")),
        name="moe_ffn",
    )(xs, gates3, wg, wu, wd)


def _scatter_kernel(idx_ref, y_ref, f_ref, rows_ref):
    cap = y_ref.shape[2]

    @pl.when(pl.program_id(2) == 0)
    def _():
        f_ref[...] = jnp.zeros_like(f_ref)

    rows_ref[...] = y_ref[0, 0].astype(F32)

    def body(jb, carry):
        base = jb * ROW_UNROLL
        ids = [idx_ref[0, 0, base + u] for u in range(ROW_UNROLL)]
        sums = [f_ref[0, pl.ds(ids[u], 1), :] + rows_ref[pl.ds(base + u, 1), :] for u in range(ROW_UNROLL)]
        for u in range(ROW_UNROLL):
            f_ref[0, pl.ds(ids[u], 1), :] = sums[u]
        return carry

    lax.fori_loop(0, cap // ROW_UNROLL, body, 0)


def _moe_scatter(y, idx3, n):
    b, n_exp, cap, d = y.shape
    dh = d // 2
    return pl.pallas_call(
        _scatter_kernel,
        out_shape=jax.ShapeDtypeStruct((b, n, d), F32),
        grid=(b, 2, n_exp),
        in_specs=[pl.BlockSpec((1, 1, cap), lambda bi, di, ei: (bi * n_exp + ei, 0, 0), memory_space=pltpu.SMEM),
                  pl.BlockSpec((1, 1, cap, dh), lambda bi, di, ei: (bi, ei, 0, di))],
        out_specs=pl.BlockSpec((1, n, dh), lambda bi, di, ei: (bi, 0, di)),
        scratch_shapes=[pltpu.VMEM((cap, dh), F32)],
        compiler_params=_cparams(("parallel", "parallel", "arbitrary")),
        name="moe_scatter",
    )(idx3, y)


def _post_kernel(x_ref, f_ref, g_ref, lng_ref, lnb_ref, o_ref):
    o_ref[0] = _ln(DEEPNORM_ALPHA * x_ref[0] + (1.0 + g_ref[0]) * f_ref[0], lng_ref[...], lnb_ref[...])


def _post(x1, f, g, lng, lnb):
    b, n, d = x1.shape
    tm = _tile(n, 512)
    blk = pl.BlockSpec((1, tm, d), lambda bi, ti: (bi, ti, 0))
    full = lambda a: pl.BlockSpec(a.shape, lambda bi, ti: (0,) * a.ndim)
    return pl.pallas_call(
        _post_kernel,
        out_shape=jax.ShapeDtypeStruct((b, n, d), F32),
        grid=(b, n // tm),
        in_specs=[blk, blk, pl.BlockSpec((1, 1, d), lambda bi, ti: (bi, 0, 0)), full(lng), full(lnb)],
        out_specs=blk,
        compiler_params=_cparams(("parallel", "parallel")),
        name="ffn_post",
    )(x1, f, g, lng, lnb)


def _moe(x1, aff_t, scf, shf, gf, wg, wu, wd, lng, lnb):
    b, n, d = x1.shape
    n_exp = aff_t.shape[1]
    cap = CAPACITY_FACTOR * n // n_exp
    gates, idx = lax.top_k(aff_t, cap)
    idx3 = idx.reshape(b * n_exp, 1, cap).astype(jnp.int32)
    gates3 = gates.reshape(b * n_exp, cap, 1)
    xs = _moe_gather(x1, idx3, scf, shf, n_exp, cap)
    y = _moe_ffn(xs, gates3, wg, wu, wd)
    f = _moe_scatter(y, idx3, n)
    return _post(x1, f, gf, lng, lnb)


def _rope_angles(n_tokens, dim):
    n_rows = n_tokens // GRID_W
    row = jnp.repeat(jnp.arange(n_rows, dtype=F32), GRID_W)
    col = jnp.tile(jnp.arange(GRID_W, dtype=F32), n_rows)
    n_freq = dim // 4
    inv_freq = ROPE_BASE ** (-jnp.arange(n_freq, dtype=F32) / n_freq)
    ang = jnp.concatenate([row[:, None] * inv_freq, col[:, None] * inv_freq], -1)
    return jnp.cos(ang), jnp.sin(ang)


def _rope_tables(n_tokens, dim, lane_offset, identity):
    half = dim // 2
    if identity:
        cos = jnp.ones((n_tokens, half), F32)
        sin = jnp.zeros((n_tokens, half), F32)
    else:
        cos, sin = _rope_angles(n_tokens, dim)
    c2 = jnp.concatenate([cos, cos], -1)
    s2 = jnp.concatenate([-sin, sin], -1)
    reps = (HEAD_SLOT - lane_offset) // dim if lane_offset == 0 else 1
    ck = jnp.zeros((n_tokens, HEAD_SLOT), F32)
    sk = jnp.zeros((n_tokens, HEAD_SLOT), F32)
    for r in range(reps):
        ck = ck.at[:, lane_offset + r * dim:lane_offset + (r + 1) * dim].set(c2)
        sk = sk.at[:, lane_offset + r * dim:lane_offset + (r + 1) * dim].set(s2)
    return ck, sk, c2.T, s2.T


def _deinterleave(w, axis):
    return jnp.concatenate([jnp.take(w, jnp.arange(0, w.shape[axis], 2), axis=axis),
                            jnp.take(w, jnp.arange(1, w.shape[axis], 2), axis=axis)], axis=axis)


def _prep_layer0(w_in, ln_g, ln_b, ws, bs, q_norm, w_uq, kv_norm, w_ukv):
    d = w_in.shape[0]
    w = GMLP_WIDTH
    o_kr = 2 * w + MLA_Q_RANK + MLA_KV_RANK
    kr = _deinterleave(w_in[:, o_kr:o_kr + MLA_ROPE], 1)
    kr_sw = jnp.concatenate([kr[:, MLA_ROPE // 2:], kr[:, :MLA_ROPE // 2]], 1)
    slot = lambda m: jnp.zeros((d, HEAD_SLOT), F32).at[:, MLA_NOPE:MLA_NOPE + MLA_ROPE].set(m)
    win = jnp.concatenate([w_in[:, :o_kr], slot(kr), slot(kr_sw)], 1).astype(BF16)

    qd = MLA_NOPE + MLA_ROPE
    wq = w_uq.reshape(MLA_Q_RANK, MLA_HEADS, qd)
    wq = jnp.concatenate([wq[..., :MLA_NOPE], _deinterleave(wq[..., MLA_NOPE:], 2),
                          jnp.zeros((MLA_Q_RANK, MLA_HEADS, HEAD_SLOT - qd), F32)], -1)
    wq_t = wq.reshape(MLA_Q_RANK, MLA_HEADS * HEAD_SLOT).T.astype(BF16)

    wkv = w_ukv.reshape(MLA_KV_RANK, MLA_HEADS, MLA_NOPE + MLA_V)
    wk = jnp.concatenate([wkv[..., :MLA_NOPE], jnp.zeros((MLA_KV_RANK, MLA_HEADS, HEAD_SLOT - MLA_NOPE), F32)], -1)
    wk = wk.reshape(MLA_KV_RANK, MLA_HEADS * HEAD_SLOT).astype(BF16)
    wv_t = wkv[..., MLA_NOPE:].reshape(MLA_KV_RANK, MLA_HEADS * MLA_V).T.astype(BF16)

    bsb = jnp.broadcast_to(bs[:, :, None], (GMLP_GROUPS, CHUNK, GMLP_GROUP_CH))
    return (win, ln_g.reshape(1, -1), ln_b.reshape(1, -1), ws.astype(BF16), bsb,
            q_norm.reshape(1, -1), wq_t, kv_norm.reshape(1, -1), wk, wv_t)


def _prep_layer1(w_in):
    d = w_in.shape[0]
    dw = 2 * DIFF_HEADS * DIFF_HEAD_DIM
    perm = lambda m: _deinterleave(m.reshape(d, 2 * DIFF_HEADS, DIFF_HEAD_DIM), 2).reshape(d, dw)
    wq_t = perm(w_in[:, :dw]).T.astype(BF16)
    wk = perm(w_in[:, dw:2 * dw]).astype(BF16)
    wv_t = w_in[:, 2 * dw:].T.astype(BF16)
    return wq_t, wk, wv_t


def _split6(m_row):
    return [v[:, None, :] for v in jnp.split(m_row, 6, axis=-1)]


def kernel(x, c, ctx, c_ctx, w_mod_0, b_mod_0, w_in_0, gmlp_ln_g_0, gmlp_ln_b_0, gmlp_ws_0, gmlp_bs_0, mla_q_norm_0, mla_w_uq_0, mla_kv_norm_0, mla_w_ukv_0, w_out_0, ln_mix_g_0, ln_mix_b_0, router_0, w_gate_0, w_up_0, w_down_0, ln_ffn_g_0, ln_ffn_b_0, w_mod_1, b_mod_1, w_in_1, lambda_q1_1, lambda_k1_1, lambda_q2_1, lambda_k2_1, subln_g_1, w_out_1, ln_mix_g_1, ln_mix_b_1, router_1, w_gate_1, w_up_1, w_down_1, ln_ffn_g_1, ln_ffn_b_1):
    b, n, d = x.shape
    m_ctx = ctx.shape[1]
    row = lambda v: v.reshape(1, -1)

    cond = jnp.concatenate([c, c_ctx[None, :], jnp.zeros((-(b + 1) % 8, d), F32)], 0)
    mod0 = _modulation(cond, w_mod_0, b_mod_0)
    mod1 = _modulation(cond, w_mod_1, b_mod_1)
    sh_a0, sc_a0, g_a0, sh_f0, sc_f0, g_f0 = _split6(mod0[:b])
    csh_a0, csc_a0, cg_a0, csh_f0, csc_f0, cg_f0 = [jnp.broadcast_to(v, (b, 1, d)) for v in _split6(mod0[b:b + 1])]
    sh_a1, sc_a1, g_a1, sh_f1, sc_f1, g_f1 = _split6(mod1[:b])
    csh_a1, csc_a1 = [jnp.broadcast_to(v, (b, 1, d)) for v in _split6(mod1[b:b + 1])[:2]]

    wts0 = _prep_layer0(w_in_0, gmlp_ln_g_0, gmlp_ln_b_0, gmlp_ws_0, gmlp_bs_0,
                        mla_q_norm_0, mla_w_uq_0, mla_kv_norm_0, mla_w_ukv_0)
    tabs_l = _rope_tables(n, MLA_ROPE, MLA_NOPE, identity=False)
    tabs_c = _rope_tables(m_ctx, MLA_ROPE, MLA_NOPE, identity=True)
    a_l, q_l, k_l, v_l = _proj0(x, sc_a0, sh_a0, wts0, tabs_l)
    a_c, q_c, k_c, v_c = _proj0(ctx, csc_a0, csh_a0, wts0, tabs_c)
    o_l = _attention(_attn0_kernel, q_l, jnp.concatenate([k_l, k_c], 1), jnp.concatenate([v_l, v_c], 2),
                     dv=MLA_V, name="mla_attn")
    o_c = _attention(_attn0_kernel, q_c, k_c, v_c, dv=MLA_V, name="mla_attn_ctx")

    w_out_a = w_out_0[:GMLP_WIDTH].astype(BF16)
    w_out_o = w_out_0[GMLP_WIDTH:].astype(BF16)
    router_t0 = router_0.T.astype(BF16)
    wg0, wu0, wd0 = w_gate_0.astype(BF16), w_up_0.astype(BF16), w_down_0.astype(BF16)
    lng, lnb = row(ln_mix_g_0), row(ln_mix_b_0)
    x1, aff = _mixer_out(x, g_a0, lng, lnb, sc_f0, sh_f0, router_t0, [(a_l, w_out_a)], [(o_l, w_out_o)])
    x_lat = _moe(x1, aff, sc_f0, sh_f0, g_f0, wg0, wu0, wd0, row(ln_ffn_g_0), row(ln_ffn_b_0))
    c1, caff = _mixer_out(ctx, cg_a0, lng, lnb, csc_f0, csh_f0, router_t0, [(a_c, w_out_a)], [(o_c, w_out_o)])
    x_ctx = _moe(c1, caff, csc_f0, csh_f0, cg_f0, wg0, wu0, wd0, row(ln_ffn_g_0), row(ln_ffn_b_0))

    lam_init = 0.8 - 0.6 * math.exp(-0.3 * 1)
    wts1 = _prep_layer1(w_in_1)
    tabs_l = _rope_tables(n, DIFF_HEAD_DIM, 0, identity=False)
    tabs_c = _rope_tables(m_ctx, DIFF_HEAD_DIM, 0, identity=True)
    q_l, k_l, v_l = _proj1(x_lat, sc_a1, sh_a1, wts1, tabs_l)
    _, k_c, v_c = _proj1(x_ctx, csc_a1, csh_a1, wts1, tabs_c)
    extra = (row(lambda_q1_1), row(lambda_k1_1), row(lambda_q2_1), row(lambda_k2_1), subln_g_1.reshape(-1, 1))
    o_l = _attention(functools.partial(_attn1_kernel, lam_init=lam_init), q_l,
                     jnp.concatenate([k_l, k_c], 1), jnp.concatenate([v_l, v_c], 2), extra, dv=2 * DIFF_HEAD_DIM,
                     name="diff_attn")
    x1, aff = _mixer_out(x_lat, g_a1, row(ln_mix_g_1), row(ln_mix_b_1), sc_f1, sh_f1, router_1.T.astype(BF16),
                         [], [(o_l, w_out_1.astype(BF16))])
    return _moe(x1, aff, sc_f1, sh_f1, g_f1, w_gate_1.astype(BF16), w_up_1.astype(BF16), w_down_1.astype(BF16),
                row(ln_ffn_g_1), row(ln_ffn_b_1))
```

```python
import functools
import math

import jax
import jax.numpy as jnp
from jax import lax
from jax.experimental import pallas as pl
from jax.experimental.pallas import tpu as pltpu

F32 = jnp.float32
BF16 = jnp.bfloat16

DEPTH = 2
GRID_W = 64
ROPE_BASE = 10000.0
CHUNK = 128
GMLP_GROUPS = 4
GMLP_GROUP_CH = 128
GMLP_WIDTH = GMLP_GROUPS * GMLP_GROUP_CH
MLA_HEADS = 8
MLA_NOPE = 64
MLA_ROPE = 32
MLA_V = 64
MLA_Q_RANK = 256
MLA_KV_RANK = 128
DIFF_HEADS = 8
DIFF_HEAD_DIM = 64
CAPACITY_FACTOR = 2
DEEPNORM_ALPHA = (2 * DEPTH) ** 0.25
LN_EPS = 1e-5
RMS_EPS = 1e-6
LOG2E = 1.4426950408889634

HEAD_SLOT = 128
NEG_BIG = -1e30
VMEM_LIMIT = 56 * 1024 * 1024


def _cparams(sem):
    return pltpu.CompilerParams(dimension_semantics=sem, vmem_limit_bytes=VMEM_LIMIT)


def _tile(n, pref):
    return pref if n % pref == 0 else n


def _ln(x, g, b):
    mu = jnp.mean(x, axis=-1, keepdims=True)
    xc = x - mu
    var = jnp.mean(xc * xc, axis=-1, keepdims=True)
    return xc * lax.rsqrt(var + LN_EPS) * g + b


def _rms(x, g):
    return x * lax.rsqrt(jnp.mean(x * x, axis=-1, keepdims=True) + RMS_EPS) * g


def _gelu(x):
    return 0.5 * x * (1.0 + lax.erf(x * (2.0 ** -0.5)))


def _dot(a, b):
    return jnp.dot(a, b, preferred_element_type=F32)


def _dot_nt(a, b):
    return lax.dot_general(a, b, (((1,), (1,)), ((), ())), preferred_element_type=F32)


def _dot_tn(a, b):
    return lax.dot_general(a, b, (((0,), (0,)), ((), ())), preferred_element_type=F32)


def _mod_kernel(c_ref, w_ref, b_ref, o_ref):
    c = c_ref[...]
    s = c / (1.0 + jnp.exp(-c))
    o_ref[...] = _dot(s.astype(BF16), w_ref[...].astype(BF16)) + b_ref[...]


def _modulation(cond, w_mod, b_mod):
    r, d = cond.shape
    n = w_mod.shape[1]
    tn = _tile(n, 1024)
    return pl.pallas_call(
        _mod_kernel,
        out_shape=jax.ShapeDtypeStruct((r, n), F32),
        grid=(n // tn,),
        in_specs=[pl.BlockSpec((r, d), lambda j: (0, 0)),
                  pl.BlockSpec((d, tn), lambda j: (0, j)),
                  pl.BlockSpec((1, tn), lambda j: (0, j))],
        out_specs=pl.BlockSpec((r, tn), lambda j: (0, j)),
        compiler_params=_cparams(("parallel",)),
        name="modulation",
    )(cond, w_mod, b_mod.reshape(1, n))


def _proj0_kernel(x_ref, sc_ref, sh_ref, win_ref, lng_ref, lnb_ref, ws_ref, bs_ref,
                  qn_ref, wq_ref, kvn_ref, wk_ref, wv_ref, ck_ref, sk_ref, cq_ref, sq_ref,
                  a_ref, q_ref, k_ref, v_ref, *, q_scale):
    tm = x_ref.shape[1]
    h = (x_ref[0] * (1.0 + sc_ref[0]) + sh_ref[0]).astype(BF16)
    y = _dot(h, win_ref[...])
    w = GMLP_WIDTH
    u = _gelu(y[:, :w])
    vn = _ln(_gelu(y[:, w:2 * w]), lng_ref[...], lnb_ref[...]).astype(BF16)
    for ci in range(tm // CHUNK):
        r0 = ci * CHUNK
        for g in range(GMLP_GROUPS):
            c0 = g * GMLP_GROUP_CH
            mixed = _dot(ws_ref[g], vn[r0:r0 + CHUNK, c0:c0 + GMLP_GROUP_CH]) + bs_ref[g]
            a_ref[0, r0:r0 + CHUNK, c0:c0 + GMLP_GROUP_CH] = (
                u[r0:r0 + CHUNK, c0:c0 + GMLP_GROUP_CH] * mixed).astype(BF16)

    o = 2 * w
    cq = _rms(y[:, o:o + MLA_Q_RANK], qn_ref[...]).astype(BF16)
    qt = _dot_nt(wq_ref[...], cq) * q_scale
    q_ref[0] = qt.astype(BF16)
    hr = MLA_ROPE // 2
    cq_t = cq_ref[...]
    sq_t = sq_ref[...]
    for hd in range(MLA_HEADS):
        r0 = hd * HEAD_SLOT + MLA_NOPE
        x1 = qt[r0:r0 + hr]
        x2 = qt[r0 + hr:r0 + 2 * hr]
        blk = jnp.concatenate([x1, x2], axis=0)
        swp = jnp.concatenate([x2, x1], axis=0)
        q_ref[0, r0:r0 + 2 * hr, :] = (blk * cq_t + swp * sq_t).astype(BF16)

    o += MLA_Q_RANK
    ckv = _rms(y[:, o:o + MLA_KV_RANK], kvn_ref[...]).astype(BF16)
    o += MLA_KV_RANK
    kr = y[:, o:o + HEAD_SLOT] * ck_ref[...] + y[:, o + HEAD_SLOT:o + 2 * HEAD_SLOT] * sk_ref[...]
    k_ref[0] = (_dot(ckv, wk_ref[...]) + jnp.tile(kr, (1, MLA_HEADS))).astype(BF16)
    v_ref[0] = _dot_nt(wv_ref[...], ckv).astype(BF16)


def _proj0(x, sc, sh, wts, tabs):
    b, n, d = x.shape
    tm = _tile(n, 512)
    ck, sk, cq, sq = tabs
    win, lng, lnb, ws, bsb, qn, wq, kvn, wk, wv = wts
    hw = MLA_HEADS * HEAD_SLOT
    vw = MLA_HEADS * MLA_V
    full = lambda a: pl.BlockSpec(a.shape, lambda bi, ti: (0,) * a.ndim)
    q_scale = (MLA_NOPE + MLA_ROPE) ** -0.5 * LOG2E
    return pl.pallas_call(
        functools.partial(_proj0_kernel, q_scale=q_scale),
        out_shape=(jax.ShapeDtypeStruct((b, n, GMLP_WIDTH), BF16),
                   jax.ShapeDtypeStruct((b, hw, n), BF16),
                   jax.ShapeDtypeStruct((b, n, hw), BF16),
                   jax.ShapeDtypeStruct((b, vw, n), BF16)),
        grid=(b, n // tm),
        in_specs=[pl.BlockSpec((1, tm, d), lambda bi, ti: (bi, ti, 0)),
                  pl.BlockSpec((1, 1, d), lambda bi, ti: (bi, 0, 0)),
                  pl.BlockSpec((1, 1, d), lambda bi, ti: (bi, 0, 0)),
                  full(win), full(lng), full(lnb), full(ws), full(bsb),
                  full(qn), full(wq), full(kvn), full(wk), full(wv),
                  pl.BlockSpec((tm, HEAD_SLOT), lambda bi, ti: (ti, 0)),
                  pl.BlockSpec((tm, HEAD_SLOT), lambda bi, ti: (ti, 0)),
                  pl.BlockSpec((MLA_ROPE, tm), lambda bi, ti: (0, ti)),
                  pl.BlockSpec((MLA_ROPE, tm), lambda bi, ti: (0, ti))],
        out_specs=(pl.BlockSpec((1, tm, GMLP_WIDTH), lambda bi, ti: (bi, ti, 0)),
                   pl.BlockSpec((1, hw, tm), lambda bi, ti: (bi, 0, ti)),
                   pl.BlockSpec((1, tm, hw), lambda bi, ti: (bi, ti, 0)),
                   pl.BlockSpec((1, vw, tm), lambda bi, ti: (bi, 0, ti))),
        compiler_params=_cparams(("parallel", "parallel")),
        name="proj0",
    )(x, sc, sh, win, lng, lnb, ws, bsb, qn, wq, kvn, wk, wv, ck, sk, cq, sq)


def _proj1_kernel(x_ref, sc_ref, sh_ref, wq_ref, wk_ref, wv_ref, ck_ref, sk_ref, cq_ref, sq_ref,
                  q_ref, k_ref, v_ref, *, q_scale):
    h = (x_ref[0] * (1.0 + sc_ref[0]) + sh_ref[0]).astype(BF16)
    n_sub = 2 * DIFF_HEADS
    hd = DIFF_HEAD_DIM
    hr = hd // 2
    qt = _dot_nt(wq_ref[...], h) * q_scale
    cq_t = cq_ref[...]
    sq_t = sq_ref[...]
    for s in range(n_sub):
        r0 = s * hd
        x1 = qt[r0:r0 + hr]
        x2 = qt[r0 + hr:r0 + hd]
        swp = jnp.concatenate([x2, x1], axis=0)
        q_ref[0, r0:r0 + hd, :] = (qt[r0:r0 + hd] * cq_t + swp * sq_t).astype(BF16)
    k = _dot(h, wk_ref[...])
    width = k.shape[1]
    lane = lax.broadcasted_iota(jnp.int32, k.shape, 1)
    partner = jnp.where((lane % hd) < hr, pltpu.roll(k, width - hr, 1), pltpu.roll(k, hr, 1))
    reps = width // HEAD_SLOT
    k_ref[0] = (k * jnp.tile(ck_ref[...], (1, reps)) + partner * jnp.tile(sk_ref[...], (1, reps))).astype(BF16)
    v_ref[0] = _dot_nt(wv_ref[...], h).astype(BF16)


def _proj1(x, sc, sh, wts, tabs):
    b, n, d = x.shape
    tm = _tile(n, 512)
    wq, wk, wv = wts
    ck, sk, cq, sq = tabs
    dw = wq.shape[0]
    full = lambda a: pl.BlockSpec(a.shape, lambda bi, ti: (0,) * a.ndim)
    q_scale = DIFF_HEAD_DIM ** -0.5 * LOG2E
    return pl.pallas_call(
        functools.partial(_proj1_kernel, q_scale=q_scale),
        out_shape=(jax.ShapeDtypeStruct((b, dw, n), BF16),
                   jax.ShapeDtypeStruct((b, n, dw), BF16),
                   jax.ShapeDtypeStruct((b, dw, n), BF16)),
        grid=(b, n // tm),
        in_specs=[pl.BlockSpec((1, tm, d), lambda bi, ti: (bi, ti, 0)),
                  pl.BlockSpec((1, 1, d), lambda bi, ti: (bi, 0, 0)),
                  pl.BlockSpec((1, 1, d), lambda bi, ti: (bi, 0, 0)),
                  full(wq), full(wk), full(wv),
                  pl.BlockSpec((tm, HEAD_SLOT), lambda bi, ti: (ti, 0)),
                  pl.BlockSpec((tm, HEAD_SLOT), lambda bi, ti: (ti, 0)),
                  pl.BlockSpec((DIFF_HEAD_DIM, tm), lambda bi, ti: (0, ti)),
                  pl.BlockSpec((DIFF_HEAD_DIM, tm), lambda bi, ti: (0, ti))],
        out_specs=(pl.BlockSpec((1, dw, tm), lambda bi, ti: (bi, 0, ti)),
                   pl.BlockSpec((1, tm, dw), lambda bi, ti: (bi, ti, 0)),
                   pl.BlockSpec((1, dw, tm), lambda bi, ti: (bi, 0, ti))),
        compiler_params=_cparams(("parallel", "parallel")),
        name="proj1",
    )(x, sc, sh, wq, wk, wv, ck, sk, cq, sq)


SHIFT_OK_LO = 2.0 ** -85
SHIFT_OK_HI = 2.0 ** 100
ATTN_UNROLL = 4


def _key_norm_max(k_ref, kmax_ref, *, tk, n_sub):
    nchunk = k_ref.shape[1] // tk
    sub_w = HEAD_SLOT // n_sub
    row = lax.broadcasted_iota(jnp.int32, (8, HEAD_SLOT), 0)
    lane = lax.broadcasted_iota(jnp.int32, (8, HEAD_SLOT), 1)
    sel = jnp.where(lane // sub_w == row, 1.0, 0.0).astype(BF16)

    def body(c, mx):
        off = pl.multiple_of(c * tk, tk)
        kc = k_ref[0, pl.ds(off, tk), :].astype(F32)
        return jnp.maximum(mx, _dot_nt(sel, (kc * kc).astype(BF16)))

    mx = lax.fori_loop(0, nchunk, body, jnp.zeros((8, tk), F32))
    kmax_ref[...] = jnp.broadcast_to(jnp.max(mx, axis=1, keepdims=True), kmax_ref.shape)


def _sweep_bounded(qs, shifts, k_ref, v_ref, s_ref, *, tk, unroll):
    tq = qs[0].shape[1]
    dv = v_ref.shape[1]
    nchunk = k_ref.shape[1] // tk

    def aligned(off):
        return off if isinstance(off, int) else pl.multiple_of(off, tk)

    def scores(off, buf):
        kc = k_ref[0, pl.ds(aligned(off), tk), :]
        for r, q in enumerate(qs):
            s_ref[buf, r] = _dot(kc, q)

    def consume(off, buf, carry):
        vc = v_ref[0, :, pl.ds(aligned(off), tk)]
        nxt = []
        for r, ((l8, acc), m) in enumerate(zip(carry, shifts)):
            p = jnp.exp2(s_ref[buf, r] - m)
            l8 = l8 + jnp.sum(p.reshape(tk // 8, 8, tq), axis=0)
            nxt.append((l8, acc + _dot(vc, p.astype(BF16))))
        return tuple(nxt)

    def body(i, carry):
        for u in range(unroll):
            off = pl.multiple_of((i * unroll + u) * tk, tk)
            scores(off + tk, (u + 1) % 2)
            carry = consume(off, u % 2, carry)
        return carry

    carry = tuple((jnp.zeros((8, tq), F32), jnp.zeros((dv, tq), F32)) for _ in qs)
    scores(0, 0)
    n_loop = (nchunk - 1) // unroll * unroll
    if n_loop:
        carry = lax.fori_loop(0, n_loop // unroll, body, carry)
    for c in range(n_loop, nchunk):
        if c + 1 < nchunk:
            scores((c + 1) * tk, (c + 1) % 2)
        carry = consume(c * tk, c % 2, carry)
    return [(jnp.sum(l8, axis=0, keepdims=True), acc) for l8, acc in carry]


def _sweep_running_max(qs, k_ref, v_ref, *, tk):
    tq = qs[0].shape[1]
    dv = v_ref.shape[1]
    nchunk = k_ref.shape[1] // tk

    def body(c, carry):
        off = pl.multiple_of(c * tk, tk)
        kc = k_ref[0, pl.ds(off, tk), :]
        vc = v_ref[0, :, pl.ds(off, tk)]
        nxt = []
        for (m, l, acc), q in zip(carry, qs):
            s = _dot(kc, q)
            mn = jnp.maximum(m, jnp.max(s, axis=0, keepdims=True))
            alpha = jnp.exp2(m - mn)
            p = jnp.exp2(s - mn)
            nxt.append((mn, alpha * l + jnp.sum(p, axis=0, keepdims=True), alpha * acc + _dot(vc, p.astype(BF16))))
        return tuple(nxt)

    init = tuple((jnp.full((1, tq), NEG_BIG, F32), jnp.zeros((1, tq), F32), jnp.zeros((dv, tq), F32)) for _ in qs)
    res = lax.fori_loop(0, nchunk, body, init)
    return [(l, acc) for _, l, acc in res]


def _attn_kernel(*refs, tk, unroll, n_sub, finalize):
    q_ref, k_ref, v_ref = refs[:3]
    extra = refs[3:-3]
    o_ref, kmax_ref, s_ref = refs[-3:]

    @pl.when(pl.program_id(2) == 0)
    def _():
        _key_norm_max(k_ref, kmax_ref, tk=tk, n_sub=n_sub)

    q = q_ref[0]
    sub_w = HEAD_SLOT // n_sub
    row = lax.broadcasted_iota(jnp.int32, q.shape, 0)
    qf = q.astype(F32)
    qs, shifts = [], []
    for r in range(n_sub):
        mine = (row // sub_w) == r
        qs.append(jnp.where(mine, q, jnp.zeros_like(q)) if n_sub > 1 else q)
        qn2 = jnp.sum(jnp.where(mine, qf * qf, 0.0), axis=0, keepdims=True)
        shifts.append(jnp.sqrt(qn2 * kmax_ref[r:r + 1, 0:1]))

    res = _sweep_bounded(qs, shifts, k_ref, v_ref, s_ref, tk=tk, unroll=unroll)
    o_ref[0] = finalize(res, extra).astype(BF16)
    ok = None
    for l, _ in res:
        ok_r = (jnp.min(l) >= SHIFT_OK_LO) & (jnp.max(l) <= SHIFT_OK_HI)
        ok = ok_r if ok is None else ok & ok_r

    @pl.when(jnp.logical_not(ok))
    def _():
        o_ref[0] = finalize(_sweep_running_max(qs, k_ref, v_ref, tk=tk), extra).astype(BF16)


def _finalize_plain(res, extra):
    (l, acc), = res
    return acc / l


def _finalize_diff(res, extra, *, lam_init):
    lq1_ref, lk1_ref, lq2_ref, lk2_ref, g_ref = extra
    (l1, a1), (l2, a2) = res
    lam = (jnp.exp(jnp.sum(lq1_ref[...] * lk1_ref[...], axis=-1, keepdims=True))
           - jnp.exp(jnp.sum(lq2_ref[...] * lk2_ref[...], axis=-1, keepdims=True)) + lam_init)
    o = a1 / l1 - lam * (a2 / l2)
    return o * lax.rsqrt(jnp.mean(o * o, axis=0, keepdims=True) + RMS_EPS) * g_ref[...] * (1.0 - lam_init)


def _attention(qt, k, vt, extra=(), *, dv, n_sub, finalize, name):
    b, hw, n = qt.shape
    t = k.shape[1]
    heads = hw // HEAD_SLOT
    tq = _tile(n, 512)
    tk = 256 if t % 256 == 0 else 128
    nchunk = t // tk
    unroll = ATTN_UNROLL
    full = lambda a: pl.BlockSpec(a.shape, lambda bi, hi, qi: (0,) * a.ndim)
    return pl.pallas_call(
        functools.partial(_attn_kernel, tk=tk, unroll=unroll, n_sub=n_sub, finalize=finalize),
        out_shape=jax.ShapeDtypeStruct((b, heads * dv, n), BF16),
        grid=(b, heads, n // tq),
        in_specs=[pl.BlockSpec((1, HEAD_SLOT, tq), lambda bi, hi, qi: (bi, hi, qi)),
                  pl.BlockSpec((1, t, HEAD_SLOT), lambda bi, hi, qi: (bi, 0, hi)),
                  pl.BlockSpec((1, dv, t), lambda bi, hi, qi: (bi, hi, 0))] + [full(a) for a in extra],
        out_specs=pl.BlockSpec((1, dv, tq), lambda bi, hi, qi: (bi, hi, qi)),
        scratch_shapes=[pltpu.VMEM((8, HEAD_SLOT), F32), pltpu.VMEM((2, n_sub, tk, tq), F32)],
        compiler_params=_cparams(("parallel", "parallel", "arbitrary")),
        name=name,
    )(qt, k, vt, *extra)


def _out_kernel(*refs, n_rowmajor):
    (x_ref, g_ref, lng_ref, lnb_ref, scf_ref, shf_ref, rt_ref), rest = refs[:7], refs[7:]
    x1_ref, aff_ref = rest[-2:]
    ops = rest[:-2]
    m = None
    for i in range(len(ops) // 2):
        a = ops[2 * i][0]
        w = ops[2 * i + 1][...]
        part = _dot(a, w) if i < n_rowmajor else _dot_tn(a, w)
        m = part if m is None else m + part
    x1 = _ln(DEEPNORM_ALPHA * x_ref[0] + (1.0 + g_ref[0]) * m, lng_ref[...], lnb_ref[...])
    x1_ref[0] = x1
    hf = (x1 * (1.0 + scf_ref[0]) + shf_ref[0]).astype(BF16)
    lg = _dot_nt(rt_ref[...], hf)
    e = jnp.exp(lg - jnp.max(lg, axis=0, keepdims=True))
    aff_ref[0] = e / jnp.sum(e, axis=0, keepdims=True)


def _mixer_out(x, g, lng, lnb, scf, shf, router_t, rowmajor_ops, chanmajor_ops):
    b, n, d = x.shape
    e = router_t.shape[0]
    tm = _tile(n, 512)
    vec = pl.BlockSpec((1, 1, d), lambda bi, ti: (bi, 0, 0))
    full = lambda a: pl.BlockSpec(a.shape, lambda bi, ti: (0,) * a.ndim)
    args = [x, g, lng, lnb, scf, shf, router_t]
    specs = [pl.BlockSpec((1, tm, d), lambda bi, ti: (bi, ti, 0)), vec, full(lng), full(lnb), vec, vec,
             full(router_t)]
    for a, w in rowmajor_ops:
        args += [a, w]
        specs += [pl.BlockSpec((1, tm, a.shape[2]), lambda bi, ti: (bi, ti, 0)), full(w)]
    for a, w in chanmajor_ops:
        args += [a, w]
        specs += [pl.BlockSpec((1, a.shape[1], tm), lambda bi, ti: (bi, 0, ti)), full(w)]
    return pl.pallas_call(
        functools.partial(_out_kernel, n_rowmajor=len(rowmajor_ops)),
        out_shape=(jax.ShapeDtypeStruct((b, n, d), F32), jax.ShapeDtypeStruct((b, e, n), F32)),
        grid=(b, n // tm),
        in_specs=specs,
        out_specs=(pl.BlockSpec((1, tm, d), lambda bi, ti: (bi, ti, 0)),
                   pl.BlockSpec((1, e, tm), lambda bi, ti: (bi, 0, ti))),
        compiler_params=_cparams(("parallel", "parallel")),
        name="mixer_out",
    )(*args)


ROW_UNROLL = 8


def _gather_kernel(idx_ref, x_ref, sc_ref, sh_ref, o_ref, rows_ref):
    cap = o_ref.shape[2]

    def body(jb, carry):
        for u in range(ROW_UNROLL):
            j = jb * ROW_UNROLL + u
            rows_ref[pl.ds(j, 1), :] = x_ref[0, pl.ds(idx_ref[0, 0, j], 1), :]
        return carry

    lax.fori_loop(0, cap // ROW_UNROLL, body, 0)
    o_ref[0, 0] = (rows_ref[...] * (1.0 + sc_ref[0]) + sh_ref[0]).astype(BF16)


def _moe_gather(x1, idx3, scf, shf, n_exp, cap):
    b, n, d = x1.shape
    dh = d // 2
    return pl.pallas_call(
        _gather_kernel,
        out_shape=jax.ShapeDtypeStruct((b, n_exp, cap, d), BF16),
        grid=(b, 2, n_exp),
        in_specs=[pl.BlockSpec((1, 1, cap), lambda bi, di, ei: (bi * n_exp + ei, 0, 0), memory_space=pltpu.SMEM),
                  pl.BlockSpec((1, n, dh), lambda bi, di, ei: (bi, 0, di)),
                  pl.BlockSpec((1, 1, dh), lambda bi, di, ei: (bi, 0, di)),
                  pl.BlockSpec((1, 1, dh), lambda bi, di, ei: (bi, 0, di))],
        out_specs=pl.BlockSpec((1, 1, cap, dh), lambda bi, di, ei: (bi, ei, 0, di)),
        scratch_shapes=[pltpu.VMEM((cap, dh), F32)],
        compiler_params=_cparams(("parallel", "parallel", "arbitrary")),
        name="moe_gather",
    )(idx3, x1, scf, shf)


def _ffn_kernel(xs_ref, g_ref, wg_ref, wu_ref, wd_ref, y_ref):
    xs = xs_ref[0, 0]
    gate = _dot(xs, wg_ref[0])
    hid = gate / (1.0 + jnp.exp(-gate)) * _dot(xs, wu_ref[0])
    y_ref[0, 0] = (_dot(hid.astype(BF16), wd_ref[0]) * g_ref[0]).astype(BF16)


def _moe_ffn(xs, gates3, wg, wu, wd):
    b, n_exp, cap, d = xs.shape
    ff = wg.shape[2]
    return pl.pallas_call(
        _ffn_kernel,
        out_shape=jax.ShapeDtypeStruct((b, n_exp, cap, d), BF16),
        grid=(n_exp, b),
        in_specs=[pl.BlockSpec((1, 1, cap, d), lambda ei, bi: (bi, ei, 0, 0)),
                  pl.BlockSpec((1, cap, 1), lambda ei, bi: (bi * n_exp + ei, 0, 0)),
                  pl.BlockSpec((1, d, ff), lambda ei, bi: (ei, 0, 0)),
                  pl.BlockSpec((1, d, ff), lambda ei, bi: (ei, 0, 0)),
                  pl.BlockSpec((1, ff, d), lambda ei, bi: (ei, 0, 0))],
        out_specs=pl.BlockSpec((1, 1, cap, d), lambda ei, bi: (bi, ei, 0, 0)),
        compiler_params=_cparams(("parallel", "parallel")),
        name="moe_ffn",
    )(xs, gates3, wg, wu, wd)


def _scatter_kernel(idx_ref, y_ref, f_ref, rows_ref):
    cap = y_ref.shape[2]

    @pl.when(pl.program_id(2) == 0)
    def _():
        f_ref[...] = jnp.zeros_like(f_ref)

    rows_ref[...] = y_ref[0, 0].astype(F32)

    def body(jb, carry):
        base = jb * ROW_UNROLL
        ids = [idx_ref[0, 0, base + u] for u in range(ROW_UNROLL)]
        sums = [f_ref[0, pl.ds(ids[u], 1), :] + rows_ref[pl.ds(base + u, 1), :] for u in range(ROW_UNROLL)]
        for u in range(ROW_UNROLL):
            f_ref[0, pl.ds(ids[u], 1), :] = sums[u]
        return carry

    lax.fori_loop(0, cap // ROW_UNROLL, body, 0)


def _moe_scatter(y, idx3, n):
    b, n_exp, cap, d = y.shape
    dh = d // 2
    return pl.pallas_call(
        _scatter_kernel,
        out_shape=jax.ShapeDtypeStruct((b, n, d), F32),
        grid=(b, 2, n_exp),
        in_specs=[pl.BlockSpec((1, 1, cap), lambda bi, di, ei: (bi * n_exp + ei, 0, 0), memory_space=pltpu.SMEM),
                  pl.BlockSpec((1, 1, cap, dh), lambda bi, di, ei: (bi, ei, 0, di))],
        out_specs=pl.BlockSpec((1, n, dh), lambda bi, di, ei: (bi, 0, di)),
        scratch_shapes=[pltpu.VMEM((cap, dh), F32)],
        compiler_params=_cparams(("parallel", "parallel", "arbitrary")),
        name="moe_scatter",
    )(idx3, y)


def _post_kernel(x_ref, f_ref, g_ref, lng_ref, lnb_ref, o_ref):
    o_ref[0] = _ln(DEEPNORM_ALPHA * x_ref[0] + (1.0 + g_ref[0]) * f_ref[0], lng_ref[...], lnb_ref[...])


def _post(x1, f, g, lng, lnb):
    b, n, d = x1.shape
    tm = _tile(n, 512)
    blk = pl.BlockSpec((1, tm, d), lambda bi, ti: (bi, ti, 0))
    full = lambda a: pl.BlockSpec(a.shape, lambda bi, ti: (0,) * a.ndim)
    return pl.pallas_call(
        _post_kernel,
        out_shape=jax.ShapeDtypeStruct((b, n, d), F32),
        grid=(b, n // tm),
        in_specs=[blk, blk, pl.BlockSpec((1, 1, d), lambda bi, ti: (bi, 0, 0)), full(lng), full(lnb)],
        out_specs=blk,
        compiler_params=_cparams(("parallel", "parallel")),
        name="ffn_post",
    )(x1, f, g, lng, lnb)


def _moe(x1, aff_t, scf, shf, gf, wg, wu, wd, lng, lnb):
    b, n, d = x1.shape
    n_exp = aff_t.shape[1]
    cap = CAPACITY_FACTOR * n // n_exp
    gates, idx = lax.top_k(aff_t, cap)
    idx3 = idx.reshape(b * n_exp, 1, cap).astype(jnp.int32)
    gates3 = gates.reshape(b * n_exp, cap, 1)
    xs = _moe_gather(x1, idx3, scf, shf, n_exp, cap)
    y = _moe_ffn(xs, gates3, wg, wu, wd)
    f = _moe_scatter(y, idx3, n)
    return _post(x1, f, gf, lng, lnb)


def _rope_angles(n_tokens, dim):
    n_rows = n_tokens // GRID_W
    row = jnp.repeat(jnp.arange(n_rows, dtype=F32), GRID_W)
    col = jnp.tile(jnp.arange(GRID_W, dtype=F32), n_rows)
    n_freq = dim // 4
    inv_freq = ROPE_BASE ** (-jnp.arange(n_freq, dtype=F32) / n_freq)
    ang = jnp.concatenate([row[:, None] * inv_freq, col[:, None] * inv_freq], -1)
    return jnp.cos(ang), jnp.sin(ang)


def _rope_tables(n_tokens, dim, lane_offset, identity):
    half = dim // 2
    if identity:
        cos = jnp.ones((n_tokens, half), F32)
        sin = jnp.zeros((n_tokens, half), F32)
    else:
        cos, sin = _rope_angles(n_tokens, dim)
    c2 = jnp.concatenate([cos, cos], -1)
    s2 = jnp.concatenate([-sin, sin], -1)
    reps = (HEAD_SLOT - lane_offset) // dim if lane_offset == 0 else 1
    ck = jnp.zeros((n_tokens, HEAD_SLOT), F32)
    sk = jnp.zeros((n_tokens, HEAD_SLOT), F32)
    for r in range(reps):
        ck = ck.at[:, lane_offset + r * dim:lane_offset + (r + 1) * dim].set(c2)
        sk = sk.at[:, lane_offset + r * dim:lane_offset + (r + 1) * dim].set(s2)
    return ck, sk, c2.T, s2.T


def _deinterleave(w, axis):
    return jnp.concatenate([jnp.take(w, jnp.arange(0, w.shape[axis], 2), axis=axis),
                            jnp.take(w, jnp.arange(1, w.shape[axis], 2), axis=axis)], axis=axis)


def _prep_layer0(w_in, ln_g, ln_b, ws, bs, q_norm, w_uq, kv_norm, w_ukv):
    d = w_in.shape[0]
    w = GMLP_WIDTH
    o_kr = 2 * w + MLA_Q_RANK + MLA_KV_RANK
    kr = _deinterleave(w_in[:, o_kr:o_kr + MLA_ROPE], 1)
    kr_sw = jnp.concatenate([kr[:, MLA_ROPE // 2:], kr[:, :MLA_ROPE // 2]], 1)
    slot = lambda m: jnp.zeros((d, HEAD_SLOT), F32).at[:, MLA_NOPE:MLA_NOPE + MLA_ROPE].set(m)
    win = jnp.concatenate([w_in[:, :o_kr], slot(kr), slot(kr_sw)], 1).astype(BF16)

    qd = MLA_NOPE + MLA_ROPE
    wq = w_uq.reshape(MLA_Q_RANK, MLA_HEADS, qd)
    wq = jnp.concatenate([wq[..., :MLA_NOPE], _deinterleave(wq[..., MLA_NOPE:], 2),
                          jnp.zeros((MLA_Q_RANK, MLA_HEADS, HEAD_SLOT - qd), F32)], -1)
    wq_t = wq.reshape(MLA_Q_RANK, MLA_HEADS * HEAD_SLOT).T.astype(BF16)

    wkv = w_ukv.reshape(MLA_KV_RANK, MLA_HEADS, MLA_NOPE + MLA_V)
    wk = jnp.concatenate([wkv[..., :MLA_NOPE], jnp.zeros((MLA_KV_RANK, MLA_HEADS, HEAD_SLOT - MLA_NOPE), F32)], -1)
    wk = wk.reshape(MLA_KV_RANK, MLA_HEADS * HEAD_SLOT).astype(BF16)
    wv_t = wkv[..., MLA_NOPE:].reshape(MLA_KV_RANK, MLA_HEADS * MLA_V).T.astype(BF16)

    bsb = jnp.broadcast_to(bs[:, :, None], (GMLP_GROUPS, CHUNK, GMLP_GROUP_CH))
    return (win, ln_g.reshape(1, -1), ln_b.reshape(1, -1), ws.astype(BF16), bsb,
            q_norm.reshape(1, -1), wq_t, kv_norm.reshape(1, -1), wk, wv_t)


def _prep_layer1(w_in):
    d = w_in.shape[0]
    dw = 2 * DIFF_HEADS * DIFF_HEAD_DIM
    perm = lambda m: _deinterleave(m.reshape(d, 2 * DIFF_HEADS, DIFF_HEAD_DIM), 2).reshape(d, dw)
    wq_t = perm(w_in[:, :dw]).T.astype(BF16)
    wk = perm(w_in[:, dw:2 * dw]).astype(BF16)
    wv_t = w_in[:, 2 * dw:].T.astype(BF16)
    return wq_t, wk, wv_t


def _split6(m_row):
    return [v[:, None, :] for v in jnp.split(m_row, 6, axis=-1)]


def kernel(x, c, ctx, c_ctx, w_mod_0, b_mod_0, w_in_0, gmlp_ln_g_0, gmlp_ln_b_0, gmlp_ws_0, gmlp_bs_0, mla_q_norm_0, mla_w_uq_0, mla_kv_norm_0, mla_w_ukv_0, w_out_0, ln_mix_g_0, ln_mix_b_0, router_0, w_gate_0, w_up_0, w_down_0, ln_ffn_g_0, ln_ffn_b_0, w_mod_1, b_mod_1, w_in_1, lambda_q1_1, lambda_k1_1, lambda_q2_1, lambda_k2_1, subln_g_1, w_out_1, ln_mix_g_1, ln_mix_b_1, router_1, w_gate_1, w_up_1, w_down_1, ln_ffn_g_1, ln_ffn_b_1):
    b, n, d = x.shape
    m_ctx = ctx.shape[1]
    row = lambda v: v.reshape(1, -1)

    cond = jnp.concatenate([c, c_ctx[None, :], jnp.zeros((-(b + 1) % 8, d), F32)], 0)
    mod0 = _modulation(cond, w_mod_0, b_mod_0)
    mod1 = _modulation(cond, w_mod_1, b_mod_1)
    sh_a0, sc_a0, g_a0, sh_f0, sc_f0, g_f0 = _split6(mod0[:b])
    csh_a0, csc_a0, cg_a0, csh_f0, csc_f0, cg_f0 = [jnp.broadcast_to(v, (b, 1, d)) for v in _split6(mod0[b:b + 1])]
    sh_a1, sc_a1, g_a1, sh_f1, sc_f1, g_f1 = _split6(mod1[:b])
    csh_a1, csc_a1 = [jnp.broadcast_to(v, (b, 1, d)) for v in _split6(mod1[b:b + 1])[:2]]

    wts0 = _prep_layer0(w_in_0, gmlp_ln_g_0, gmlp_ln_b_0, gmlp_ws_0, gmlp_bs_0,
                        mla_q_norm_0, mla_w_uq_0, mla_kv_norm_0, mla_w_ukv_0)
    tabs_l = _rope_tables(n, MLA_ROPE, MLA_NOPE, identity=False)
    tabs_c = _rope_tables(m_ctx, MLA_ROPE, MLA_NOPE, identity=True)
    a_l, q_l, k_l, v_l = _proj0(x, sc_a0, sh_a0, wts0, tabs_l)
    a_c, q_c, k_c, v_c = _proj0(ctx, csc_a0, csh_a0, wts0, tabs_c)
    o_l = _attention(q_l, jnp.concatenate([k_l, k_c], 1), jnp.concatenate([v_l, v_c], 2),
                     dv=MLA_V, n_sub=1, finalize=_finalize_plain, name="mla_attn")
    o_c = _attention(q_c, k_c, v_c, dv=MLA_V, n_sub=1, finalize=_finalize_plain, name="mla_attn_ctx")

    w_out_a = w_out_0[:GMLP_WIDTH].astype(BF16)
    w_out_o = w_out_0[GMLP_WIDTH:].astype(BF16)
    router_t0 = router_0.T.astype(BF16)
    wg0, wu0, wd0 = w_gate_0.astype(BF16), w_up_0.astype(BF16), w_down_0.astype(BF16)
    lng, lnb = row(ln_mix_g_0), row(ln_mix_b_0)
    x1, aff = _mixer_out(x, g_a0, lng, lnb, sc_f0, sh_f0, router_t0, [(a_l, w_out_a)], [(o_l, w_out_o)])
    x_lat = _moe(x1, aff, sc_f0, sh_f0, g_f0, wg0, wu0, wd0, row(ln_ffn_g_0), row(ln_ffn_b_0))
    c1, caff = _mixer_out(ctx, cg_a0, lng, lnb, csc_f0, csh_f0, router_t0, [(a_c, w_out_a)], [(o_c, w_out_o)])
    x_ctx = _moe(c1, caff, csc_f0, csh_f0, cg_f0, wg0, wu0, wd0, row(ln_ffn_g_0), row(ln_ffn_b_0))

    lam_init = 0.8 - 0.6 * math.exp(-0.3 * 1)
    wts1 = _prep_layer1(w_in_1)
    tabs_l = _rope_tables(n, DIFF_HEAD_DIM, 0, identity=False)
    tabs_c = _rope_tables(m_ctx, DIFF_HEAD_DIM, 0, identity=True)
    q_l, k_l, v_l = _proj1(x_lat, sc_a1, sh_a1, wts1, tabs_l)
    _, k_c, v_c = _proj1(x_ctx, csc_a1, csh_a1, wts1, tabs_c)
    extra = (row(lambda_q1_1), row(lambda_k1_1), row(lambda_q2_1), row(lambda_k2_1), subln_g_1.reshape(-1, 1))
    o_l = _attention(q_l, jnp.concatenate([k_l, k_c], 1), jnp.concatenate([v_l, v_c], 2), extra,
                     dv=2 * DIFF_HEAD_DIM, n_sub=2, finalize=functools.partial(_finalize_diff, lam_init=lam_init),
                     name="diff_attn")
    x1, aff = _mixer_out(x_lat, g_a1, row(ln_mix_g_1), row(ln_mix_b_1), sc_f1, sh_f1, router_1.T.astype(BF16),
                         [], [(o_l, w_out_1.astype(BF16))])
    return _moe(x1, aff, sc_f1, sh_f1, g_f1, w_gate_1.astype(BF16), w_up_1.astype(BF16), w_down_1.astype(BF16),
                row(ln_ffn_g_1), row(ln_ffn_b_1))
```

```python
import functools
import math

import jax
import jax.numpy as jnp
from jax import lax
from jax.experimental import pallas as pl
from jax.experimental.pallas import tpu as pltpu

F32 = jnp.float32
BF16 = jnp.bfloat16

DEPTH = 2
GRID_W = 64
ROPE_BASE = 10000.0
CHUNK = 128
GMLP_GROUPS = 4
GMLP_GROUP_CH = 128
GMLP_WIDTH = GMLP_GROUPS * GMLP_GROUP_CH
MLA_HEADS = 8
MLA_NOPE = 64
MLA_ROPE = 32
MLA_V = 64
MLA_Q_RANK = 256
MLA_KV_RANK = 128
DIFF_HEADS = 8
DIFF_HEAD_DIM = 64
CAPACITY_FACTOR = 2
DEEPNORM_ALPHA = (2 * DEPTH) ** 0.25
LN_EPS = 1e-5
RMS_EPS = 1e-6
LOG2E = 1.4426950408889634

HEAD_SLOT = 128
NEG_BIG = -1e30
VMEM_LIMIT = 56 * 1024 * 1024


def _cparams(sem):
    return pltpu.CompilerParams(dimension_semantics=sem, vmem_limit_bytes=VMEM_LIMIT)


def _tile(n, pref):
    return pref if n % pref == 0 else n


def _ln(x, g, b):
    mu = jnp.mean(x, axis=-1, keepdims=True)
    xc = x - mu
    var = jnp.mean(xc * xc, axis=-1, keepdims=True)
    return xc * lax.rsqrt(var + LN_EPS) * g + b


def _rms(x, g):
    return x * lax.rsqrt(jnp.mean(x * x, axis=-1, keepdims=True) + RMS_EPS) * g


def _gelu(x):
    return 0.5 * x * (1.0 + lax.erf(x * (2.0 ** -0.5)))


def _dot(a, b):
    return jnp.dot(a, b, preferred_element_type=F32)


def _dot_nt(a, b):
    return lax.dot_general(a, b, (((1,), (1,)), ((), ())), preferred_element_type=F32)


def _dot_tn(a, b):
    return lax.dot_general(a, b, (((0,), (0,)), ((), ())), preferred_element_type=F32)


def _mod_kernel(c_ref, w_ref, b_ref, o_ref):
    c = c_ref[...]
    s = c / (1.0 + jnp.exp(-c))
    o_ref[...] = _dot(s.astype(BF16), w_ref[...].astype(BF16)) + b_ref[...]


def _modulation(cond, w_mod, b_mod):
    r, d = cond.shape
    n = w_mod.shape[1]
    tn = _tile(n, 1024)
    return pl.pallas_call(
        _mod_kernel,
        out_shape=jax.ShapeDtypeStruct((r, n), F32),
        grid=(n // tn,),
        in_specs=[pl.BlockSpec((r, d), lambda j: (0, 0)),
                  pl.BlockSpec((d, tn), lambda j: (0, j)),
                  pl.BlockSpec((1, tn), lambda j: (0, j))],
        out_specs=pl.BlockSpec((r, tn), lambda j: (0, j)),
        compiler_params=_cparams(("parallel",)),
        name="modulation",
    )(cond, w_mod, b_mod.reshape(1, n))


def _proj0_kernel(x_ref, sc_ref, sh_ref, win_ref, lng_ref, lnb_ref, ws_ref, bs_ref,
                  qn_ref, wq_ref, kvn_ref, wk_ref, wv_ref, ck_ref, sk_ref, cq_ref, sq_ref,
                  a_ref, q_ref, k_ref, v_ref, *, q_scale):
    tm = x_ref.shape[1]
    h = (x_ref[0] * (1.0 + sc_ref[0]) + sh_ref[0]).astype(BF16)
    y = _dot(h, win_ref[...])
    w = GMLP_WIDTH
    u = _gelu(y[:, :w])
    vn = _ln(_gelu(y[:, w:2 * w]), lng_ref[...], lnb_ref[...]).astype(BF16)
    for ci in range(tm // CHUNK):
        r0 = ci * CHUNK
        for g in range(GMLP_GROUPS):
            c0 = g * GMLP_GROUP_CH
            mixed = _dot(ws_ref[g], vn[r0:r0 + CHUNK, c0:c0 + GMLP_GROUP_CH]) + bs_ref[g]
            a_ref[0, r0:r0 + CHUNK, c0:c0 + GMLP_GROUP_CH] = (
                u[r0:r0 + CHUNK, c0:c0 + GMLP_GROUP_CH] * mixed).astype(BF16)

    o = 2 * w
    cq = _rms(y[:, o:o + MLA_Q_RANK], qn_ref[...]).astype(BF16)
    qt = _dot_nt(wq_ref[...], cq) * q_scale
    q_ref[0] = qt.astype(BF16)
    hr = MLA_ROPE // 2
    cq_t = cq_ref[...]
    sq_t = sq_ref[...]
    for hd in range(MLA_HEADS):
        r0 = hd * HEAD_SLOT + MLA_NOPE
        x1 = qt[r0:r0 + hr]
        x2 = qt[r0 + hr:r0 + 2 * hr]
        blk = jnp.concatenate([x1, x2], axis=0)
        swp = jnp.concatenate([x2, x1], axis=0)
        q_ref[0, r0:r0 + 2 * hr, :] = (blk * cq_t + swp * sq_t).astype(BF16)

    o += MLA_Q_RANK
    ckv = _rms(y[:, o:o + MLA_KV_RANK], kvn_ref[...]).astype(BF16)
    o += MLA_KV_RANK
    kr = y[:, o:o + HEAD_SLOT] * ck_ref[...] + y[:, o + HEAD_SLOT:o + 2 * HEAD_SLOT] * sk_ref[...]
    k_ref[0] = (_dot(ckv, wk_ref[...]) + jnp.tile(kr, (1, MLA_HEADS))).astype(BF16)
    v_ref[0] = _dot_nt(wv_ref[...], ckv).astype(BF16)


def _proj0(x, sc, sh, wts, tabs):
    b, n, d = x.shape
    tm = _tile(n, 512)
    ck, sk, cq, sq = tabs
    win, lng, lnb, ws, bsb, qn, wq, kvn, wk, wv = wts
    hw = MLA_HEADS * HEAD_SLOT
    vw = MLA_HEADS * MLA_V
    full = lambda a: pl.BlockSpec(a.shape, lambda bi, ti: (0,) * a.ndim)
    q_scale = (MLA_NOPE + MLA_ROPE) ** -0.5 * LOG2E
    return pl.pallas_call(
        functools.partial(_proj0_kernel, q_scale=q_scale),
        out_shape=(jax.ShapeDtypeStruct((b, n, GMLP_WIDTH), BF16),
                   jax.ShapeDtypeStruct((b, hw, n), BF16),
                   jax.ShapeDtypeStruct((b, n, hw), BF16),
                   jax.ShapeDtypeStruct((b, vw, n), BF16)),
        grid=(b, n // tm),
        in_specs=[pl.BlockSpec((1, tm, d), lambda bi, ti: (bi, ti, 0)),
                  pl.BlockSpec((1, 1, d), lambda bi, ti: (bi, 0, 0)),
                  pl.BlockSpec((1, 1, d), lambda bi, ti: (bi, 0, 0)),
                  full(win), full(lng), full(lnb), full(ws), full(bsb),
                  full(qn), full(wq), full(kvn), full(wk), full(wv),
                  pl.BlockSpec((tm, HEAD_SLOT), lambda bi, ti: (ti, 0)),
                  pl.BlockSpec((tm, HEAD_SLOT), lambda bi, ti: (ti, 0)),
                  pl.BlockSpec((MLA_ROPE, tm), lambda bi, ti: (0, ti)),
                  pl.BlockSpec((MLA_ROPE, tm), lambda bi, ti: (0, ti))],
        out_specs=(pl.BlockSpec((1, tm, GMLP_WIDTH), lambda bi, ti: (bi, ti, 0)),
                   pl.BlockSpec((1, hw, tm), lambda bi, ti: (bi, 0, ti)),
                   pl.BlockSpec((1, tm, hw), lambda bi, ti: (bi, ti, 0)),
                   pl.BlockSpec((1, vw, tm), lambda bi, ti: (bi, 0, ti))),
        compiler_params=_cparams(("parallel", "parallel")),
        name="proj0",
    )(x, sc, sh, win, lng, lnb, ws, bsb, qn, wq, kvn, wk, wv, ck, sk, cq, sq)


def _proj1_kernel(x_ref, sc_ref, sh_ref, wq_ref, wk_ref, wv_ref, ck_ref, sk_ref, cq_ref, sq_ref,
                  q_ref, k_ref, v_ref, *, q_scale):
    h = (x_ref[0] * (1.0 + sc_ref[0]) + sh_ref[0]).astype(BF16)
    n_sub = 2 * DIFF_HEADS
    hd = DIFF_HEAD_DIM
    hr = hd // 2
    qt = _dot_nt(wq_ref[...], h) * q_scale
    cq_t = cq_ref[...]
    sq_t = sq_ref[...]
    for s in range(n_sub):
        r0 = s * hd
        x1 = qt[r0:r0 + hr]
        x2 = qt[r0 + hr:r0 + hd]
        swp = jnp.concatenate([x2, x1], axis=0)
        q_ref[0, r0:r0 + hd, :] = (qt[r0:r0 + hd] * cq_t + swp * sq_t).astype(BF16)
    k = _dot(h, wk_ref[...])
    width = k.shape[1]
    lane = lax.broadcasted_iota(jnp.int32, k.shape, 1)
    partner = jnp.where((lane % hd) < hr, pltpu.roll(k, width - hr, 1), pltpu.roll(k, hr, 1))
    reps = width // HEAD_SLOT
    k_ref[0] = (k * jnp.tile(ck_ref[...], (1, reps)) + partner * jnp.tile(sk_ref[...], (1, reps))).astype(BF16)
    v_ref[0] = _dot_nt(wv_ref[...], h).astype(BF16)


def _proj1(x, sc, sh, wts, tabs):
    b, n, d = x.shape
    tm = _tile(n, 512)
    wq, wk, wv = wts
    ck, sk, cq, sq = tabs
    dw = wq.shape[0]
    full = lambda a: pl.BlockSpec(a.shape, lambda bi, ti: (0,) * a.ndim)
    q_scale = DIFF_HEAD_DIM ** -0.5 * LOG2E
    return pl.pallas_call(
        functools.partial(_proj1_kernel, q_scale=q_scale),
        out_shape=(jax.ShapeDtypeStruct((b, dw, n), BF16),
                   jax.ShapeDtypeStruct((b, n, dw), BF16),
                   jax.ShapeDtypeStruct((b, dw, n), BF16)),
        grid=(b, n // tm),
        in_specs=[pl.BlockSpec((1, tm, d), lambda bi, ti: (bi, ti, 0)),
                  pl.BlockSpec((1, 1, d), lambda bi, ti: (bi, 0, 0)),
                  pl.BlockSpec((1, 1, d), lambda bi, ti: (bi, 0, 0)),
                  full(wq), full(wk), full(wv),
                  pl.BlockSpec((tm, HEAD_SLOT), lambda bi, ti: (ti, 0)),
                  pl.BlockSpec((tm, HEAD_SLOT), lambda bi, ti: (ti, 0)),
                  pl.BlockSpec((DIFF_HEAD_DIM, tm), lambda bi, ti: (0, ti)),
                  pl.BlockSpec((DIFF_HEAD_DIM, tm), lambda bi, ti: (0, ti))],
        out_specs=(pl.BlockSpec((1, dw, tm), lambda bi, ti: (bi, 0, ti)),
                   pl.BlockSpec((1, tm, dw), lambda bi, ti: (bi, ti, 0)),
                   pl.BlockSpec((1, dw, tm), lambda bi, ti: (bi, 0, ti))),
        compiler_params=_cparams(("parallel", "parallel")),
        name="proj1",
    )(x, sc, sh, wq, wk, wv, ck, sk, cq, sq)


SHIFT_OK_LO = 2.0 ** -85
SHIFT_OK_HI = 2.0 ** 100
ATTN_UNROLL = 8


def _key_norm_max(k_ref, kmax_ref, *, tk, n_sub):
    nchunk = k_ref.shape[1] // tk
    sub_w = HEAD_SLOT // n_sub
    row = lax.broadcasted_iota(jnp.int32, (8, HEAD_SLOT), 0)
    lane = lax.broadcasted_iota(jnp.int32, (8, HEAD_SLOT), 1)
    sel = jnp.where(lane // sub_w == row, 1.0, 0.0).astype(BF16)

    def body(c, mx):
        off = pl.multiple_of(c * tk, tk)
        kc = k_ref[0, pl.ds(off, tk), :].astype(F32)
        return jnp.maximum(mx, _dot_nt(sel, (kc * kc).astype(BF16)))

    mx = lax.fori_loop(0, nchunk, body, jnp.zeros((8, tk), F32))
    kmax_ref[...] = jnp.broadcast_to(jnp.max(mx, axis=1, keepdims=True), kmax_ref.shape)


def _sweep_bounded(qs, shifts, k_ref, v_ref, s_ref, *, tk, unroll):
    tq = qs[0].shape[1]
    dv = v_ref.shape[1]
    nchunk = k_ref.shape[1] // tk

    def aligned(off):
        return off if isinstance(off, int) else pl.multiple_of(off, tk)

    def scores(off, buf):
        kc = k_ref[0, pl.ds(aligned(off), tk), :]
        for r, q in enumerate(qs):
            s_ref[buf, r] = _dot(kc, q)

    def consume(off, buf, carry):
        vc = v_ref[0, :, pl.ds(aligned(off), tk)]
        nxt = []
        for r, ((l8, acc), m) in enumerate(zip(carry, shifts)):
            p = jnp.exp2(s_ref[buf, r] - m)
            l8 = l8 + jnp.sum(p.reshape(tk // 8, 8, tq), axis=0)
            nxt.append((l8, acc + _dot(vc, p.astype(BF16))))
        return tuple(nxt)

    def body(i, carry):
        for u in range(unroll):
            off = pl.multiple_of((i * unroll + u) * tk, tk)
            scores(off + tk, (u + 1) % 2)
            carry = consume(off, u % 2, carry)
        return carry

    carry = tuple((jnp.zeros((8, tq), F32), jnp.zeros((dv, tq), F32)) for _ in qs)
    scores(0, 0)
    n_loop = (nchunk - 1) // unroll * unroll
    if n_loop:
        carry = lax.fori_loop(0, n_loop // unroll, body, carry)
    for c in range(n_loop, nchunk):
        if c + 1 < nchunk:
            scores((c + 1) * tk, (c + 1) % 2)
        carry = consume(c * tk, c % 2, carry)
    return [(jnp.sum(l8, axis=0, keepdims=True), acc) for l8, acc in carry]


def _sweep_running_max(qs, k_ref, v_ref, *, tk):
    tq = qs[0].shape[1]
    dv = v_ref.shape[1]
    nchunk = k_ref.shape[1] // tk

    def body(c, carry):
        off = pl.multiple_of(c * tk, tk)
        kc = k_ref[0, pl.ds(off, tk), :]
        vc = v_ref[0, :, pl.ds(off, tk)]
        nxt = []
        for (m, l, acc), q in zip(carry, qs):
            s = _dot(kc, q)
            mn = jnp.maximum(m, jnp.max(s, axis=0, keepdims=True))
            alpha = jnp.exp2(m - mn)
            p = jnp.exp2(s - mn)
            nxt.append((mn, alpha * l + jnp.sum(p, axis=0, keepdims=True), alpha * acc + _dot(vc, p.astype(BF16))))
        return tuple(nxt)

    init = tuple((jnp.full((1, tq), NEG_BIG, F32), jnp.zeros((1, tq), F32), jnp.zeros((dv, tq), F32)) for _ in qs)
    res = lax.fori_loop(0, nchunk, body, init)
    return [(l, acc) for _, l, acc in res]


def _attn_kernel(*refs, tk, unroll, n_sub, finalize):
    q_ref, k_ref, v_ref = refs[:3]
    extra = refs[3:-3]
    o_ref, kmax_ref, s_ref = refs[-3:]

    @pl.when(pl.program_id(2) == 0)
    def _():
        _key_norm_max(k_ref, kmax_ref, tk=tk, n_sub=n_sub)

    q = q_ref[0]
    sub_w = HEAD_SLOT // n_sub
    row = lax.broadcasted_iota(jnp.int32, q.shape, 0)
    qf = q.astype(F32)
    qs, shifts = [], []
    for r in range(n_sub):
        mine = (row // sub_w) == r
        qs.append(jnp.where(mine, q, jnp.zeros_like(q)) if n_sub > 1 else q)
        qn2 = jnp.sum(jnp.where(mine, qf * qf, 0.0), axis=0, keepdims=True)
        shifts.append(jnp.sqrt(qn2 * kmax_ref[r:r + 1, 0:1]))

    res = _sweep_bounded(qs, shifts, k_ref, v_ref, s_ref, tk=tk, unroll=unroll)
    o_ref[0] = finalize(res, extra).astype(BF16)
    ok = None
    for l, _ in res:
        ok_r = (jnp.min(l) >= SHIFT_OK_LO) & (jnp.max(l) <= SHIFT_OK_HI)
        ok = ok_r if ok is None else ok & ok_r

    @pl.when(jnp.logical_not(ok))
    def _():
        o_ref[0] = finalize(_sweep_running_max(qs, k_ref, v_ref, tk=tk), extra).astype(BF16)


def _finalize_plain(res, extra):
    (l, acc), = res
    return acc / l


def _finalize_diff(res, extra, *, lam_init):
    lq1_ref, lk1_ref, lq2_ref, lk2_ref, g_ref = extra
    (l1, a1), (l2, a2) = res
    lam = (jnp.exp(jnp.sum(lq1_ref[...] * lk1_ref[...], axis=-1, keepdims=True))
           - jnp.exp(jnp.sum(lq2_ref[...] * lk2_ref[...], axis=-1, keepdims=True)) + lam_init)
    o = a1 / l1 - lam * (a2 / l2)
    return o * lax.rsqrt(jnp.mean(o * o, axis=0, keepdims=True) + RMS_EPS) * g_ref[...] * (1.0 - lam_init)


def _attention(qt, k, vt, extra=(), *, dv, n_sub, finalize, name):
    b, hw, n = qt.shape
    t = k.shape[1]
    heads = hw // HEAD_SLOT
    tq = _tile(n, 512)
    tk = 256 if t % 256 == 0 else 128
    nchunk = t // tk
    unroll = ATTN_UNROLL
    full = lambda a: pl.BlockSpec(a.shape, lambda bi, hi, qi: (0,) * a.ndim)
    return pl.pallas_call(
        functools.partial(_attn_kernel, tk=tk, unroll=unroll, n_sub=n_sub, finalize=finalize),
        out_shape=jax.ShapeDtypeStruct((b, heads * dv, n), BF16),
        grid=(b, heads, n // tq),
        in_specs=[pl.BlockSpec((1, HEAD_SLOT, tq), lambda bi, hi, qi: (bi, hi, qi)),
                  pl.BlockSpec((1, t, HEAD_SLOT), lambda bi, hi, qi: (bi, 0, hi)),
                  pl.BlockSpec((1, dv, t), lambda bi, hi, qi: (bi, hi, 0))] + [full(a) for a in extra],
        out_specs=pl.BlockSpec((1, dv, tq), lambda bi, hi, qi: (bi, hi, qi)),
        scratch_shapes=[pltpu.VMEM((8, HEAD_SLOT), F32), pltpu.VMEM((2, n_sub, tk, tq), F32)],
        compiler_params=_cparams(("parallel", "parallel", "arbitrary")),
        name=name,
    )(qt, k, vt, *extra)


def _out_kernel(*refs, n_rowmajor):
    (x_ref, g_ref, lng_ref, lnb_ref, scf_ref, shf_ref, rt_ref), rest = refs[:7], refs[7:]
    x1_ref, aff_ref = rest[-2:]
    ops = rest[:-2]
    m = None
    for i in range(len(ops) // 2):
        a = ops[2 * i][0]
        w = ops[2 * i + 1][...]
        part = _dot(a, w) if i < n_rowmajor else _dot_tn(a, w)
        m = part if m is None else m + part
    x1 = _ln(DEEPNORM_ALPHA * x_ref[0] + (1.0 + g_ref[0]) * m, lng_ref[...], lnb_ref[...])
    x1_ref[0] = x1
    hf = (x1 * (1.0 + scf_ref[0]) + shf_ref[0]).astype(BF16)
    lg = _dot_nt(rt_ref[...], hf)
    e = jnp.exp(lg - jnp.max(lg, axis=0, keepdims=True))
    aff_ref[0] = e / jnp.sum(e, axis=0, keepdims=True)


def _mixer_out(x, g, lng, lnb, scf, shf, router_t, rowmajor_ops, chanmajor_ops):
    b, n, d = x.shape
    e = router_t.shape[0]
    tm = _tile(n, 512)
    vec = pl.BlockSpec((1, 1, d), lambda bi, ti: (bi, 0, 0))
    full = lambda a: pl.BlockSpec(a.shape, lambda bi, ti: (0,) * a.ndim)
    args = [x, g, lng, lnb, scf, shf, router_t]
    specs = [pl.BlockSpec((1, tm, d), lambda bi, ti: (bi, ti, 0)), vec, full(lng), full(lnb), vec, vec,
             full(router_t)]
    for a, w in rowmajor_ops:
        args += [a, w]
        specs += [pl.BlockSpec((1, tm, a.shape[2]), lambda bi, ti: (bi, ti, 0)), full(w)]
    for a, w in chanmajor_ops:
        args += [a, w]
        specs += [pl.BlockSpec((1, a.shape[1], tm), lambda bi, ti: (bi, 0, ti)), full(w)]
    return pl.pallas_call(
        functools.partial(_out_kernel, n_rowmajor=len(rowmajor_ops)),
        out_shape=(jax.ShapeDtypeStruct((b, n, d), F32), jax.ShapeDtypeStruct((b, e, n), F32)),
        grid=(b, n // tm),
        in_specs=specs,
        out_specs=(pl.BlockSpec((1, tm, d), lambda bi, ti: (bi, ti, 0)),
                   pl.BlockSpec((1, e, tm), lambda bi, ti: (bi, 0, ti))),
        compiler_params=_cparams(("parallel", "parallel")),
        name="mixer_out",
    )(*args)


ROW_UNROLL = 8


def _gather_kernel(idx_ref, x_ref, sc_ref, sh_ref, o_ref, rows_ref):
    cap = o_ref.shape[2]

    def body(jb, carry):
        for u in range(ROW_UNROLL):
            j = jb * ROW_UNROLL + u
            rows_ref[pl.ds(j, 1), :] = x_ref[0, pl.ds(idx_ref[0, 0, j], 1), :]
        return carry

    lax.fori_loop(0, cap // ROW_UNROLL, body, 0)
    o_ref[0, 0] = (rows_ref[...] * (1.0 + sc_ref[0]) + sh_ref[0]).astype(BF16)


def _moe_gather(x1, idx3, scf, shf, n_exp, cap):
    b, n, d = x1.shape
    dh = d // 2
    return pl.pallas_call(
        _gather_kernel,
        out_shape=jax.ShapeDtypeStruct((b, n_exp, cap, d), BF16),
        grid=(b, 2, n_exp),
        in_specs=[pl.BlockSpec((1, 1, cap), lambda bi, di, ei: (bi * n_exp + ei, 0, 0), memory_space=pltpu.SMEM),
                  pl.BlockSpec((1, n, dh), lambda bi, di, ei: (bi, 0, di)),
                  pl.BlockSpec((1, 1, dh), lambda bi, di, ei: (bi, 0, di)),
                  pl.BlockSpec((1, 1, dh), lambda bi, di, ei: (bi, 0, di))],
        out_specs=pl.BlockSpec((1, 1, cap, dh), lambda bi, di, ei: (bi, ei, 0, di)),
        scratch_shapes=[pltpu.VMEM((cap, dh), F32)],
        compiler_params=_cparams(("parallel", "parallel", "arbitrary")),
        name="moe_gather",
    )(idx3, x1, scf, shf)


def _ffn_kernel(xs_ref, g_ref, wg_ref, wu_ref, wd_ref, y_ref):
    xs = xs_ref[0, 0]
    gate = _dot(xs, wg_ref[0])
    hid = gate / (1.0 + jnp.exp(-gate)) * _dot(xs, wu_ref[0])
    y_ref[0, 0] = (_dot(hid.astype(BF16), wd_ref[0]) * g_ref[0]).astype(BF16)


def _moe_ffn(xs, gates3, wg, wu, wd):
    b, n_exp, cap, d = xs.shape
    ff = wg.shape[2]
    return pl.pallas_call(
        _ffn_kernel,
        out_shape=jax.ShapeDtypeStruct((b, n_exp, cap, d), BF16),
        grid=(n_exp, b),
        in_specs=[pl.BlockSpec((1, 1, cap, d), lambda ei, bi: (bi, ei, 0, 0)),
                  pl.BlockSpec((1, cap, 1), lambda ei, bi: (bi * n_exp + ei, 0, 0)),
                  pl.BlockSpec((1, d, ff), lambda ei, bi: (ei, 0, 0)),
                  pl.BlockSpec((1, d, ff), lambda ei, bi: (ei, 0, 0)),
                  pl.BlockSpec((1, ff, d), lambda ei, bi: (ei, 0, 0))],
        out_specs=pl.BlockSpec((1, 1, cap, d), lambda ei, bi: (bi, ei, 0, 0)),
        compiler_params=_cparams(("parallel", "parallel")),
        name="moe_ffn",
    )(xs, gates3, wg, wu, wd)


def _scatter_kernel(idx_ref, y_ref, f_ref, rows_ref):
    cap = y_ref.shape[2]

    @pl.when(pl.program_id(2) == 0)
    def _():
        f_ref[...] = jnp.zeros_like(f_ref)

    rows_ref[...] = y_ref[0, 0].astype(F32)

    def body(jb, carry):
        base = jb * ROW_UNROLL
        ids = [idx_ref[0, 0, base + u] for u in range(ROW_UNROLL)]
        sums = [f_ref[0, pl.ds(ids[u], 1), :] + rows_ref[pl.ds(base + u, 1), :] for u in range(ROW_UNROLL)]
        for u in range(ROW_UNROLL):
            f_ref[0, pl.ds(ids[u], 1), :] = sums[u]
        return carry

    lax.fori_loop(0, cap // ROW_UNROLL, body, 0)


def _moe_scatter(y, idx3, n):
    b, n_exp, cap, d = y.shape
    dh = d // 2
    return pl.pallas_call(
        _scatter_kernel,
        out_shape=jax.ShapeDtypeStruct((b, n, d), F32),
        grid=(b, 2, n_exp),
        in_specs=[pl.BlockSpec((1, 1, cap), lambda bi, di, ei: (bi * n_exp + ei, 0, 0), memory_space=pltpu.SMEM),
                  pl.BlockSpec((1, 1, cap, dh), lambda bi, di, ei: (bi, ei, 0, di))],
        out_specs=pl.BlockSpec((1, n, dh), lambda bi, di, ei: (bi, 0, di)),
        scratch_shapes=[pltpu.VMEM((cap, dh), F32)],
        compiler_params=_cparams(("parallel", "parallel", "arbitrary")),
        name="moe_scatter",
    )(idx3, y)


def _post_kernel(x_ref, f_ref, g_ref, lng_ref, lnb_ref, o_ref):
    o_ref[0] = _ln(DEEPNORM_ALPHA * x_ref[0] + (1.0 + g_ref[0]) * f_ref[0], lng_ref[...], lnb_ref[...])


def _post(x1, f, g, lng, lnb):
    b, n, d = x1.shape
    tm = _tile(n, 512)
    blk = pl.BlockSpec((1, tm, d), lambda bi, ti: (bi, ti, 0))
    full = lambda a: pl.BlockSpec(a.shape, lambda bi, ti: (0,) * a.ndim)
    return pl.pallas_call(
        _post_kernel,
        out_shape=jax.ShapeDtypeStruct((b, n, d), F32),
        grid=(b, n // tm),
        in_specs=[blk, blk, pl.BlockSpec((1, 1, d), lambda bi, ti: (bi, 0, 0)), full(lng), full(lnb)],
        out_specs=blk,
        compiler_params=_cparams(("parallel", "parallel")),
        name="ffn_post",
    )(x1, f, g, lng, lnb)


def _select_kernel(aff_ref, idx_ref, gate_ref, *, cap, n_exp):
    a = aff_ref[0]
    er, lanes = a.shape
    r_n = er // n_exp
    e_pad = 16
    bits = pltpu.bitcast(a, jnp.int32)

    member = (lax.broadcasted_iota(jnp.int32, (e_pad, er), 1) // r_n
              == lax.broadcasted_iota(jnp.int32, (e_pad, er), 0))
    gs = jnp.where(member, 1.0, 0.0).astype(BF16)
    member_t = (lax.broadcasted_iota(jnp.int32, (er, e_pad), 0) // r_n
                == lax.broadcasted_iota(jnp.int32, (er, e_pad), 1))
    gst = jnp.where(member_t, 1.0, 0.0).astype(BF16)
    ri = lax.broadcasted_iota(jnp.int32, (er, er), 0)
    rj = lax.broadcasted_iota(jnp.int32, (er, er), 1)
    rows_before = jnp.where((ri // r_n == rj // r_n) & (rj < ri), 1.0, 0.0).astype(BF16)
    li = lax.broadcasted_iota(jnp.int32, (lanes, lanes), 0)
    lj = lax.broadcasted_iota(jnp.int32, (lanes, lanes), 1)
    lanes_upto = jnp.where(li <= lj, 1.0, 0.0).astype(BF16)

    def lane_total(m):
        return jnp.broadcast_to(jnp.sum(m, axis=1, keepdims=True), m.shape)

    def expert_total(m):
        return _dot(gs, lane_total(m).astype(BF16))

    def to_rows(ev):
        hi = jnp.floor(ev * (1.0 / 128.0))
        return 128.0 * _dot(gst, hi.astype(BF16)) + _dot(gst, (ev - 128.0 * hi).astype(BF16))

    def search(i, t):
        cand = t | lax.shift_left(jnp.int32(1), 30 - i)
        cnt = expert_total(jnp.where(bits >= cand, 1.0, 0.0))
        return jnp.where(to_rows(jnp.where(cnt >= cap, 1.0, 0.0)) > 0.5, cand, t)

    t = lax.fori_loop(0, 31, search, jnp.zeros((er, lanes), jnp.int32))
    gt = jnp.where(bits > t, 1.0, 0.0)
    eq = jnp.where(bits == t, 1.0, 0.0)

    def prefix(m):
        rc = _dot(m.astype(BF16), lanes_upto)
        off = _dot(rows_before, jnp.broadcast_to(rc[:, lanes - 1:lanes], m.shape).astype(BF16))
        return rc, off

    rc_eq, off_eq = prefix(eq)
    need = cap - to_rows(expert_total(gt))
    sel = jnp.maximum(gt, eq * jnp.where(off_eq + rc_eq - eq < need, 1.0, 0.0))
    rc, off = prefix(sel)
    row_end = off + jnp.broadcast_to(rc[:, lanes - 1:lanes], rc.shape)

    slot = lax.broadcasted_iota(jnp.int32, (cap, lanes), 0).astype(F32) + 1.0
    lane_c = lax.broadcasted_iota(jnp.int32, (cap, lanes), 1)
    lane_cf = lane_c.astype(F32)
    diag = lax.broadcasted_iota(jnp.int32, (r_n, lanes), 0) == lax.broadcasted_iota(jnp.int32, (r_n, lanes), 1)
    lane_r = lax.broadcasted_iota(jnp.int32, (1, lanes), 1)
    pad = jnp.zeros((lanes - r_n, lanes), F32)

    def table(m):
        return (jnp.concatenate([m, pad], axis=0) if r_n < lanes else m).astype(BF16)

    idx_all = jnp.zeros((cap, lanes), jnp.int32)
    gate_all = jnp.zeros((cap, lanes), F32)
    for e in range(n_exp):
        r0 = e * r_n
        rc_e = rc[r0:r0 + r_n]
        end_lane = jnp.sum(jnp.where(diag, row_end[r0:r0 + r_n], 0.0), axis=0, keepdims=True)
        end_lane = jnp.where(lane_r < r_n, end_lane, 3.0e38)
        off_lane = jnp.sum(jnp.where(diag, off[r0:r0 + r_n], 0.0), axis=0, keepdims=True)
        row_j = jnp.sum(jnp.where(end_lane < slot, 1.0, 0.0), axis=1, keepdims=True)
        onehot = jnp.where(lane_cf == row_j, 1.0, 0.0)
        off_j = jnp.sum(onehot * off_lane, axis=1, keepdims=True)
        oh = onehot.astype(BF16)
        cum_j = _dot(oh, table(rc_e)) + off_j
        lane_j = jnp.sum(jnp.where(cum_j < slot, 1.0, 0.0), axis=1, keepdims=True)
        a_e = a[r0:r0 + r_n]
        a1 = a_e.astype(BF16).astype(F32)
        a2 = (a_e - a1).astype(BF16).astype(F32)
        a3 = a_e - a1 - a2
        aff_j = _dot(oh, table(a1)) + _dot(oh, table(a2)) + _dot(oh, table(a3))
        gate_j = jnp.sum(jnp.where(lane_cf == lane_j, aff_j, 0.0), axis=1, keepdims=True)
        idx_j = (row_j * float(lanes) + lane_j).astype(jnp.int32)
        idx_all = jnp.where(lane_c == e, idx_j, idx_all)
        gate_all = jnp.where(lane_c == e, gate_j, gate_all)
    idx_ref[0] = idx_all
    gate_ref[0] = gate_all


def _moe_select(aff_t, cap):
    b, n_exp, n = aff_t.shape
    lanes = HEAD_SLOT
    er = n_exp * (n // lanes)
    idx, gates = pl.pallas_call(
        functools.partial(_select_kernel, cap=cap, n_exp=n_exp),
        out_shape=(jax.ShapeDtypeStruct((b, cap, lanes), jnp.int32), jax.ShapeDtypeStruct((b, cap, lanes), F32)),
        grid=(b,),
        in_specs=[pl.BlockSpec((1, er, lanes), lambda bi: (bi, 0, 0))],
        out_specs=(pl.BlockSpec((1, cap, lanes), lambda bi: (bi, 0, 0)),
                   pl.BlockSpec((1, cap, lanes), lambda bi: (bi, 0, 0))),
        compiler_params=_cparams(("parallel",)),
        name="moe_select",
    )(aff_t.reshape(b, er, lanes))
    to_expert_major = lambda m: jnp.swapaxes(m[:, :, :n_exp], 1, 2)
    return to_expert_major(gates), to_expert_major(idx)


def _moe(x1, aff_t, scf, shf, gf, wg, wu, wd, lng, lnb):
    b, n, d = x1.shape
    n_exp = aff_t.shape[1]
    cap = CAPACITY_FACTOR * n // n_exp
    gates, idx = _moe_select(aff_t, cap)
    idx3 = idx.reshape(b * n_exp, 1, cap)
    gates3 = gates.reshape(b * n_exp, cap, 1)
    xs = _moe_gather(x1, idx3, scf, shf, n_exp, cap)
    y = _moe_ffn(xs, gates3, wg, wu, wd)
    f = _moe_scatter(y, idx3, n)
    return _post(x1, f, gf, lng, lnb)


def _rope_angles(n_tokens, dim):
    n_rows = n_tokens // GRID_W
    row = jnp.repeat(jnp.arange(n_rows, dtype=F32), GRID_W)
    col = jnp.tile(jnp.arange(GRID_W, dtype=F32), n_rows)
    n_freq = dim // 4
    inv_freq = ROPE_BASE ** (-jnp.arange(n_freq, dtype=F32) / n_freq)
    ang = jnp.concatenate([row[:, None] * inv_freq, col[:, None] * inv_freq], -1)
    return jnp.cos(ang), jnp.sin(ang)


def _rope_tables(n_tokens, dim, lane_offset, identity):
    half = dim // 2
    if identity:
        cos = jnp.ones((n_tokens, half), F32)
        sin = jnp.zeros((n_tokens, half), F32)
    else:
        cos, sin = _rope_angles(n_tokens, dim)
    c2 = jnp.concatenate([cos, cos], -1)
    s2 = jnp.concatenate([-sin, sin], -1)
    reps = (HEAD_SLOT - lane_offset) // dim if lane_offset == 0 else 1
    ck = jnp.zeros((n_tokens, HEAD_SLOT), F32)
    sk = jnp.zeros((n_tokens, HEAD_SLOT), F32)
    for r in range(reps):
        ck = ck.at[:, lane_offset + r * dim:lane_offset + (r + 1) * dim].set(c2)
        sk = sk.at[:, lane_offset + r * dim:lane_offset + (r + 1) * dim].set(s2)
    return ck, sk, c2.T, s2.T


def _deinterleave(w, axis):
    return jnp.concatenate([jnp.take(w, jnp.arange(0, w.shape[axis], 2), axis=axis),
                            jnp.take(w, jnp.arange(1, w.shape[axis], 2), axis=axis)], axis=axis)


def _prep_layer0(w_in, ln_g, ln_b, ws, bs, q_norm, w_uq, kv_norm, w_ukv):
    d = w_in.shape[0]
    w = GMLP_WIDTH
    o_kr = 2 * w + MLA_Q_RANK + MLA_KV_RANK
    kr = _deinterleave(w_in[:, o_kr:o_kr + MLA_ROPE], 1)
    kr_sw = jnp.concatenate([kr[:, MLA_ROPE // 2:], kr[:, :MLA_ROPE // 2]], 1)
    slot = lambda m: jnp.zeros((d, HEAD_SLOT), F32).at[:, MLA_NOPE:MLA_NOPE + MLA_ROPE].set(m)
    win = jnp.concatenate([w_in[:, :o_kr], slot(kr), slot(kr_sw)], 1).astype(BF16)

    qd = MLA_NOPE + MLA_ROPE
    wq = w_uq.reshape(MLA_Q_RANK, MLA_HEADS, qd)
    wq = jnp.concatenate([wq[..., :MLA_NOPE], _deinterleave(wq[..., MLA_NOPE:], 2),
                          jnp.zeros((MLA_Q_RANK, MLA_HEADS, HEAD_SLOT - qd), F32)], -1)
    wq_t = wq.reshape(MLA_Q_RANK, MLA_HEADS * HEAD_SLOT).T.astype(BF16)

    wkv = w_ukv.reshape(MLA_KV_RANK, MLA_HEADS, MLA_NOPE + MLA_V)
    wk = jnp.concatenate([wkv[..., :MLA_NOPE], jnp.zeros((MLA_KV_RANK, MLA_HEADS, HEAD_SLOT - MLA_NOPE), F32)], -1)
    wk = wk.reshape(MLA_KV_RANK, MLA_HEADS * HEAD_SLOT).astype(BF16)
    wv_t = wkv[..., MLA_NOPE:].reshape(MLA_KV_RANK, MLA_HEADS * MLA_V).T.astype(BF16)

    bsb = jnp.broadcast_to(bs[:, :, None], (GMLP_GROUPS, CHUNK, GMLP_GROUP_CH))
    return (win, ln_g.reshape(1, -1), ln_b.reshape(1, -1), ws.astype(BF16), bsb,
            q_norm.reshape(1, -1), wq_t, kv_norm.reshape(1, -1), wk, wv_t)


def _prep_layer1(w_in):
    d = w_in.shape[0]
    dw = 2 * DIFF_HEADS * DIFF_HEAD_DIM
    perm = lambda m: _deinterleave(m.reshape(d, 2 * DIFF_HEADS, DIFF_HEAD_DIM), 2).reshape(d, dw)
    wq_t = perm(w_in[:, :dw]).T.astype(BF16)
    wk = perm(w_in[:, dw:2 * dw]).astype(BF16)
    wv_t = w_in[:, 2 * dw:].T.astype(BF16)
    return wq_t, wk, wv_t


def _split6(m_row):
    return [v[:, None, :] for v in jnp.split(m_row, 6, axis=-1)]


def kernel(x, c, ctx, c_ctx, w_mod_0, b_mod_0, w_in_0, gmlp_ln_g_0, gmlp_ln_b_0, gmlp_ws_0, gmlp_bs_0, mla_q_norm_0, mla_w_uq_0, mla_kv_norm_0, mla_w_ukv_0, w_out_0, ln_mix_g_0, ln_mix_b_0, router_0, w_gate_0, w_up_0, w_down_0, ln_ffn_g_0, ln_ffn_b_0, w_mod_1, b_mod_1, w_in_1, lambda_q1_1, lambda_k1_1, lambda_q2_1, lambda_k2_1, subln_g_1, w_out_1, ln_mix_g_1, ln_mix_b_1, router_1, w_gate_1, w_up_1, w_down_1, ln_ffn_g_1, ln_ffn_b_1):
    b, n, d = x.shape
    m_ctx = ctx.shape[1]
    row = lambda v: v.reshape(1, -1)

    cond = jnp.concatenate([c, c_ctx[None, :], jnp.zeros((-(b + 1) % 8, d), F32)], 0)
    mod0 = _modulation(cond, w_mod_0, b_mod_0)
    mod1 = _modulation(cond, w_mod_1, b_mod_1)
    sh_a0, sc_a0, g_a0, sh_f0, sc_f0, g_f0 = _split6(mod0[:b])
    csh_a0, csc_a0, cg_a0, csh_f0, csc_f0, cg_f0 = [jnp.broadcast_to(v, (b, 1, d)) for v in _split6(mod0[b:b + 1])]
    sh_a1, sc_a1, g_a1, sh_f1, sc_f1, g_f1 = _split6(mod1[:b])
    csh_a1, csc_a1 = [jnp.broadcast_to(v, (b, 1, d)) for v in _split6(mod1[b:b + 1])[:2]]

    wts0 = _prep_layer0(w_in_0, gmlp_ln_g_0, gmlp_ln_b_0, gmlp_ws_0, gmlp_bs_0,
                        mla_q_norm_0, mla_w_uq_0, mla_kv_norm_0, mla_w_ukv_0)
    tabs_l = _rope_tables(n, MLA_ROPE, MLA_NOPE, identity=False)
    tabs_c = _rope_tables(m_ctx, MLA_ROPE, MLA_NOPE, identity=True)
    a_l, q_l, k_l, v_l = _proj0(x, sc_a0, sh_a0, wts0, tabs_l)
    a_c, q_c, k_c, v_c = _proj0(ctx, csc_a0, csh_a0, wts0, tabs_c)
    o_l = _attention(q_l, jnp.concatenate([k_l, k_c], 1), jnp.concatenate([v_l, v_c], 2),
                     dv=MLA_V, n_sub=1, finalize=_finalize_plain, name="mla_attn")
    o_c = _attention(q_c, k_c, v_c, dv=MLA_V, n_sub=1, finalize=_finalize_plain, name="mla_attn_ctx")

    w_out_a = w_out_0[:GMLP_WIDTH].astype(BF16)
    w_out_o = w_out_0[GMLP_WIDTH:].astype(BF16)
    router_t0 = router_0.T.astype(BF16)
    wg0, wu0, wd0 = w_gate_0.astype(BF16), w_up_0.astype(BF16), w_down_0.astype(BF16)
    lng, lnb = row(ln_mix_g_0), row(ln_mix_b_0)
    x1, aff = _mixer_out(x, g_a0, lng, lnb, sc_f0, sh_f0, router_t0, [(a_l, w_out_a)], [(o_l, w_out_o)])
    x_lat = _moe(x1, aff, sc_f0, sh_f0, g_f0, wg0, wu0, wd0, row(ln_ffn_g_0), row(ln_ffn_b_0))
    c1, caff = _mixer_out(ctx, cg_a0, lng, lnb, csc_f0, csh_f0, router_t0, [(a_c, w_out_a)], [(o_c, w_out_o)])
    x_ctx = _moe(c1, caff, csc_f0, csh_f0, cg_f0, wg0, wu0, wd0, row(ln_ffn_g_0), row(ln_ffn_b_0))

    lam_init = 0.8 - 0.6 * math.exp(-0.3 * 1)
    wts1 = _prep_layer1(w_in_1)
    tabs_l = _rope_tables(n, DIFF_HEAD_DIM, 0, identity=False)
    tabs_c = _rope_tables(m_ctx, DIFF_HEAD_DIM, 0, identity=True)
    q_l, k_l, v_l = _proj1(x_lat, sc_a1, sh_a1, wts1, tabs_l)
    _, k_c, v_c = _proj1(x_ctx, csc_a1, csh_a1, wts1, tabs_c)
    extra = (row(lambda_q1_1), row(lambda_k1_1), row(lambda_q2_1), row(lambda_k2_1), subln_g_1.reshape(-1, 1))
    o_l = _attention(q_l, jnp.concatenate([k_l, k_c], 1), jnp.concatenate([v_l, v_c], 2), extra,
                     dv=2 * DIFF_HEAD_DIM, n_sub=2, finalize=functools.partial(_finalize_diff, lam_init=lam_init),
                     name="diff_attn")
    x1, aff = _mixer_out(x_lat, g_a1, row(ln_mix_g_1), row(ln_mix_b_1), sc_f1, sh_f1, router_1.T.astype(BF16),
                         [], [(o_l, w_out_1.astype(BF16))])
    return _moe(x1, aff, sc_f1, sh_f1, g_f1, w_gate_1.astype(BF16), w_up_1.astype(BF16), w_down_1.astype(BF16),
                row(ln_ffn_g_1), row(ln_ffn_b_1))
```

```python
import functools
import math

import jax
import jax.numpy as jnp
from jax import lax
from jax.experimental import pallas as pl
from jax.experimental.pallas import tpu as pltpu

F32 = jnp.float32
BF16 = jnp.bfloat16

DEPTH = 2
GRID_W = 64
ROPE_BASE = 10000.0
CHUNK = 128
GMLP_GROUPS = 4
GMLP_GROUP_CH = 128
GMLP_WIDTH = GMLP_GROUPS * GMLP_GROUP_CH
MLA_HEADS = 8
MLA_NOPE = 64
MLA_ROPE = 32
MLA_V = 64
MLA_Q_RANK = 256
MLA_KV_RANK = 128
DIFF_HEADS = 8
DIFF_HEAD_DIM = 64
CAPACITY_FACTOR = 2
DEEPNORM_ALPHA = (2 * DEPTH) ** 0.25
LN_EPS = 1e-5
RMS_EPS = 1e-6
LOG2E = 1.4426950408889634

HEAD_SLOT = 128
NEG_BIG = -1e30
VMEM_LIMIT = 56 * 1024 * 1024


def _cparams(sem):
    return pltpu.CompilerParams(dimension_semantics=sem, vmem_limit_bytes=VMEM_LIMIT)


def _tile(n, pref):
    return pref if n % pref == 0 else n


def _ln(x, g, b):
    mu = jnp.mean(x, axis=-1, keepdims=True)
    xc = x - mu
    var = jnp.mean(xc * xc, axis=-1, keepdims=True)
    return xc * lax.rsqrt(var + LN_EPS) * g + b


def _rms(x, g):
    return x * lax.rsqrt(jnp.mean(x * x, axis=-1, keepdims=True) + RMS_EPS) * g


def _gelu(x):
    return 0.5 * x * (1.0 + lax.erf(x * (2.0 ** -0.5)))


def _dot(a, b):
    return jnp.dot(a, b, preferred_element_type=F32)


def _dot_nt(a, b):
    return lax.dot_general(a, b, (((1,), (1,)), ((), ())), preferred_element_type=F32)


def _dot_tn(a, b):
    return lax.dot_general(a, b, (((0,), (0,)), ((), ())), preferred_element_type=F32)


def _mod_kernel(c_ref, w_ref, b_ref, o_ref):
    c = c_ref[...]
    s = c / (1.0 + jnp.exp(-c))
    o_ref[...] = _dot(s.astype(BF16), w_ref[...].astype(BF16)) + b_ref[...]


def _modulation(cond, w_mod, b_mod):
    r, d = cond.shape
    n = w_mod.shape[1]
    tn = _tile(n, 1024)
    return pl.pallas_call(
        _mod_kernel,
        out_shape=jax.ShapeDtypeStruct((r, n), F32),
        grid=(n // tn,),
        in_specs=[pl.BlockSpec((r, d), lambda j: (0, 0)),
                  pl.BlockSpec((d, tn), lambda j: (0, j)),
                  pl.BlockSpec((1, tn), lambda j: (0, j))],
        out_specs=pl.BlockSpec((r, tn), lambda j: (0, j)),
        compiler_params=_cparams(("parallel",)),
        name="modulation",
    )(cond, w_mod, b_mod.reshape(1, n))


def _proj0_kernel(x_ref, sc_ref, sh_ref, win_ref, lng_ref, lnb_ref, ws_ref, bs_ref,
                  qn_ref, wq_ref, kvn_ref, wk_ref, wv_ref, ck_ref, sk_ref, cq_ref, sq_ref,
                  a_ref, q_ref, k_ref, v_ref, *, q_scale):
    tm = x_ref.shape[1]
    h = (x_ref[0] * (1.0 + sc_ref[0]) + sh_ref[0]).astype(BF16)
    y = _dot(h, win_ref[...])
    w = GMLP_WIDTH
    u = _gelu(y[:, :w])
    vn = _ln(_gelu(y[:, w:2 * w]), lng_ref[...], lnb_ref[...]).astype(BF16)
    for ci in range(tm // CHUNK):
        r0 = ci * CHUNK
        for g in range(GMLP_GROUPS):
            c0 = g * GMLP_GROUP_CH
            mixed = _dot(ws_ref[g], vn[r0:r0 + CHUNK, c0:c0 + GMLP_GROUP_CH]) + bs_ref[g]
            a_ref[0, r0:r0 + CHUNK, c0:c0 + GMLP_GROUP_CH] = (
                u[r0:r0 + CHUNK, c0:c0 + GMLP_GROUP_CH] * mixed).astype(BF16)

    o = 2 * w
    cq = _rms(y[:, o:o + MLA_Q_RANK], qn_ref[...]).astype(BF16)
    qt = _dot_nt(wq_ref[...], cq) * q_scale
    q_ref[0] = qt.astype(BF16)
    hr = MLA_ROPE // 2
    cq_t = cq_ref[...]
    sq_t = sq_ref[...]
    for hd in range(MLA_HEADS):
        r0 = hd * HEAD_SLOT + MLA_NOPE
        x1 = qt[r0:r0 + hr]
        x2 = qt[r0 + hr:r0 + 2 * hr]
        blk = jnp.concatenate([x1, x2], axis=0)
        swp = jnp.concatenate([x2, x1], axis=0)
        q_ref[0, r0:r0 + 2 * hr, :] = (blk * cq_t + swp * sq_t).astype(BF16)

    o += MLA_Q_RANK
    ckv = _rms(y[:, o:o + MLA_KV_RANK], kvn_ref[...]).astype(BF16)
    o += MLA_KV_RANK
    kr = y[:, o:o + HEAD_SLOT] * ck_ref[...] + y[:, o + HEAD_SLOT:o + 2 * HEAD_SLOT] * sk_ref[...]
    k_ref[0] = (_dot(ckv, wk_ref[...]) + jnp.tile(kr, (1, MLA_HEADS))).astype(BF16)
    v_ref[0] = _dot_nt(wv_ref[...], ckv).astype(BF16)


def _proj0(x, sc, sh, wts, tabs):
    b, n, d = x.shape
    tm = _tile(n, 512)
    ck, sk, cq, sq = tabs
    win, lng, lnb, ws, bsb, qn, wq, kvn, wk, wv = wts
    hw = MLA_HEADS * HEAD_SLOT
    vw = MLA_HEADS * MLA_V
    full = lambda a: pl.BlockSpec(a.shape, lambda bi, ti: (0,) * a.ndim)
    q_scale = (MLA_NOPE + MLA_ROPE) ** -0.5 * LOG2E
    return pl.pallas_call(
        functools.partial(_proj0_kernel, q_scale=q_scale),
        out_shape=(jax.ShapeDtypeStruct((b, n, GMLP_WIDTH), BF16),
                   jax.ShapeDtypeStruct((b, hw, n), BF16),
                   jax.ShapeDtypeStruct((b, n, hw), BF16),
                   jax.ShapeDtypeStruct((b, vw, n), BF16)),
        grid=(b, n // tm),
        in_specs=[pl.BlockSpec((1, tm, d), lambda bi, ti: (bi, ti, 0)),
                  pl.BlockSpec((1, 1, d), lambda bi, ti: (bi, 0, 0)),
                  pl.BlockSpec((1, 1, d), lambda bi, ti: (bi, 0, 0)),
                  full(win), full(lng), full(lnb), full(ws), full(bsb),
                  full(qn), full(wq), full(kvn), full(wk), full(wv),
                  pl.BlockSpec((tm, HEAD_SLOT), lambda bi, ti: (ti, 0)),
                  pl.BlockSpec((tm, HEAD_SLOT), lambda bi, ti: (ti, 0)),
                  pl.BlockSpec((MLA_ROPE, tm), lambda bi, ti: (0, ti)),
                  pl.BlockSpec((MLA_ROPE, tm), lambda bi, ti: (0, ti))],
        out_specs=(pl.BlockSpec((1, tm, GMLP_WIDTH), lambda bi, ti: (bi, ti, 0)),
                   pl.BlockSpec((1, hw, tm), lambda bi, ti: (bi, 0, ti)),
                   pl.BlockSpec((1, tm, hw), lambda bi, ti: (bi, ti, 0)),
                   pl.BlockSpec((1, vw, tm), lambda bi, ti: (bi, 0, ti))),
        compiler_params=_cparams(("parallel", "parallel")),
        name="proj0",
    )(x, sc, sh, win, lng, lnb, ws, bsb, qn, wq, kvn, wk, wv, ck, sk, cq, sq)


def _proj1_kernel(x_ref, sc_ref, sh_ref, wq_ref, wk_ref, wv_ref, ck_ref, sk_ref, cq_ref, sq_ref,
                  q_ref, k_ref, v_ref, *, q_scale):
    h = (x_ref[0] * (1.0 + sc_ref[0]) + sh_ref[0]).astype(BF16)
    n_sub = 2 * DIFF_HEADS
    hd = DIFF_HEAD_DIM
    hr = hd // 2
    qt = _dot_nt(wq_ref[...], h) * q_scale
    cq_t = cq_ref[...]
    sq_t = sq_ref[...]
    for s in range(n_sub):
        r0 = s * hd
        x1 = qt[r0:r0 + hr]
        x2 = qt[r0 + hr:r0 + hd]
        swp = jnp.concatenate([x2, x1], axis=0)
        q_ref[0, r0:r0 + hd, :] = (qt[r0:r0 + hd] * cq_t + swp * sq_t).astype(BF16)
    k = _dot(h, wk_ref[...])
    width = k.shape[1]
    lane = lax.broadcasted_iota(jnp.int32, k.shape, 1)
    partner = jnp.where((lane % hd) < hr, pltpu.roll(k, width - hr, 1), pltpu.roll(k, hr, 1))
    reps = width // HEAD_SLOT
    k_ref[0] = (k * jnp.tile(ck_ref[...], (1, reps)) + partner * jnp.tile(sk_ref[...], (1, reps))).astype(BF16)
    v_ref[0] = _dot_nt(wv_ref[...], h).astype(BF16)


def _proj1(x, sc, sh, wts, tabs):
    b, n, d = x.shape
    tm = _tile(n, 512)
    wq, wk, wv = wts
    ck, sk, cq, sq = tabs
    dw = wq.shape[0]
    full = lambda a: pl.BlockSpec(a.shape, lambda bi, ti: (0,) * a.ndim)
    q_scale = DIFF_HEAD_DIM ** -0.5 * LOG2E
    return pl.pallas_call(
        functools.partial(_proj1_kernel, q_scale=q_scale),
        out_shape=(jax.ShapeDtypeStruct((b, dw, n), BF16),
                   jax.ShapeDtypeStruct((b, n, dw), BF16),
                   jax.ShapeDtypeStruct((b, dw, n), BF16)),
        grid=(b, n // tm),
        in_specs=[pl.BlockSpec((1, tm, d), lambda bi, ti: (bi, ti, 0)),
                  pl.BlockSpec((1, 1, d), lambda bi, ti: (bi, 0, 0)),
                  pl.BlockSpec((1, 1, d), lambda bi, ti: (bi, 0, 0)),
                  full(wq), full(wk), full(wv),
                  pl.BlockSpec((tm, HEAD_SLOT), lambda bi, ti: (ti, 0)),
                  pl.BlockSpec((tm, HEAD_SLOT), lambda bi, ti: (ti, 0)),
                  pl.BlockSpec((DIFF_HEAD_DIM, tm), lambda bi, ti: (0, ti)),
                  pl.BlockSpec((DIFF_HEAD_DIM, tm), lambda bi, ti: (0, ti))],
        out_specs=(pl.BlockSpec((1, dw, tm), lambda bi, ti: (bi, 0, ti)),
                   pl.BlockSpec((1, tm, dw), lambda bi, ti: (bi, ti, 0)),
                   pl.BlockSpec((1, dw, tm), lambda bi, ti: (bi, 0, ti))),
        compiler_params=_cparams(("parallel", "parallel")),
        name="proj1",
    )(x, sc, sh, wq, wk, wv, ck, sk, cq, sq)


SHIFT_OK_LO = 2.0 ** -85
SHIFT_OK_HI = 2.0 ** 100
ATTN_TQ = 1024
ATTN_UNROLL = 8


def _key_norm_max(k_ref, kmax_ref, *, tk, n_sub):
    nchunk = k_ref.shape[1] // tk
    sub_w = HEAD_SLOT // n_sub
    row = lax.broadcasted_iota(jnp.int32, (8, HEAD_SLOT), 0)
    lane = lax.broadcasted_iota(jnp.int32, (8, HEAD_SLOT), 1)
    sel = jnp.where(lane // sub_w == row, 1.0, 0.0).astype(BF16)

    group = max(g for g in range(1, 12) if nchunk % g == 0)
    rows = group * tk

    def body(c, mx):
        off = pl.multiple_of(c * rows, rows)
        kc = k_ref[0, pl.ds(off, rows), :].astype(F32)
        return jnp.maximum(mx, _dot_nt(sel, (kc * kc).astype(BF16)))

    mx = lax.fori_loop(0, nchunk // group, body, jnp.zeros((8, rows), F32))
    kmax_ref[...] = jnp.broadcast_to(jnp.max(mx, axis=1, keepdims=True), kmax_ref.shape)


def _sweep_bounded(qs, shifts, k_ref, v_ref, s_ref, *, tk, unroll):
    tq = qs[0].shape[1]
    dv = v_ref.shape[1]
    nchunk = k_ref.shape[1] // tk

    def aligned(off):
        return off if isinstance(off, int) else pl.multiple_of(off, tk)

    def scores(off, buf):
        kc = k_ref[0, pl.ds(aligned(off), tk), :]
        for r, q in enumerate(qs):
            s_ref[buf, r] = _dot(kc, q)

    def consume(off, buf, carry):
        vc = v_ref[0, :, pl.ds(aligned(off), tk)]
        nxt = []
        for r, ((l8, acc), m) in enumerate(zip(carry, shifts)):
            p = jnp.exp2(s_ref[buf, r] - m)
            l8 = l8 + jnp.sum(p.reshape(tk // 8, 8, tq), axis=0)
            nxt.append((l8, acc + _dot(vc, p.astype(BF16))))
        return tuple(nxt)

    def body(i, carry):
        for u in range(unroll):
            off = pl.multiple_of((i * unroll + u) * tk, tk)
            scores(off + tk, (u + 1) % 2)
            carry = consume(off, u % 2, carry)
        return carry

    carry = tuple((jnp.zeros((8, tq), F32), jnp.zeros((dv, tq), F32)) for _ in qs)
    scores(0, 0)
    n_loop = (nchunk - 1) // unroll * unroll
    if n_loop:
        carry = lax.fori_loop(0, n_loop // unroll, body, carry)
    for c in range(n_loop, nchunk):
        if c + 1 < nchunk:
            scores((c + 1) * tk, (c + 1) % 2)
        carry = consume(c * tk, c % 2, carry)
    return [(jnp.sum(l8, axis=0, keepdims=True), acc) for l8, acc in carry]


def _sweep_running_max(qs, k_ref, v_ref, *, tk):
    tq = qs[0].shape[1]
    dv = v_ref.shape[1]
    nchunk = k_ref.shape[1] // tk

    def body(c, carry):
        off = pl.multiple_of(c * tk, tk)
        kc = k_ref[0, pl.ds(off, tk), :]
        vc = v_ref[0, :, pl.ds(off, tk)]
        nxt = []
        for (m, l, acc), q in zip(carry, qs):
            s = _dot(kc, q)
            mn = jnp.maximum(m, jnp.max(s, axis=0, keepdims=True))
            alpha = jnp.exp2(m - mn)
            p = jnp.exp2(s - mn)
            nxt.append((mn, alpha * l + jnp.sum(p, axis=0, keepdims=True), alpha * acc + _dot(vc, p.astype(BF16))))
        return tuple(nxt)

    init = tuple((jnp.full((1, tq), NEG_BIG, F32), jnp.zeros((1, tq), F32), jnp.zeros((dv, tq), F32)) for _ in qs)
    res = lax.fori_loop(0, nchunk, body, init)
    return [(l, acc) for _, l, acc in res]


def _attn_kernel(*refs, tk, unroll, n_sub, finalize):
    q_ref, k_ref, v_ref = refs[:3]
    extra = refs[3:-3]
    o_ref, kmax_ref, s_ref = refs[-3:]

    @pl.when(pl.program_id(2) == 0)
    def _():
        _key_norm_max(k_ref, kmax_ref, tk=tk, n_sub=n_sub)

    q = q_ref[0]
    sub_w = HEAD_SLOT // n_sub
    row = lax.broadcasted_iota(jnp.int32, q.shape, 0)
    qf = q.astype(F32)
    qs, shifts = [], []
    for r in range(n_sub):
        mine = (row // sub_w) == r
        qs.append(jnp.where(mine, q, jnp.zeros_like(q)) if n_sub > 1 else q)
        qn2 = jnp.sum(jnp.where(mine, qf * qf, 0.0), axis=0, keepdims=True)
        shifts.append(jnp.sqrt(qn2 * kmax_ref[r:r + 1, 0:1]))

    res = _sweep_bounded(qs, shifts, k_ref, v_ref, s_ref, tk=tk, unroll=unroll)
    o_ref[0] = finalize(res, extra).astype(BF16)
    ok = None
    for l, _ in res:
        ok_r = (jnp.min(l) >= SHIFT_OK_LO) & (jnp.max(l) <= SHIFT_OK_HI)
        ok = ok_r if ok is None else ok & ok_r

    @pl.when(jnp.logical_not(ok))
    def _():
        o_ref[0] = finalize(_sweep_running_max(qs, k_ref, v_ref, tk=tk), extra).astype(BF16)


def _finalize_plain(res, extra):
    (l, acc), = res
    return acc / l


def _finalize_diff(res, extra, *, lam_init):
    lq1_ref, lk1_ref, lq2_ref, lk2_ref, g_ref = extra
    (l1, a1), (l2, a2) = res
    lam = (jnp.exp(jnp.sum(lq1_ref[...] * lk1_ref[...], axis=-1, keepdims=True))
           - jnp.exp(jnp.sum(lq2_ref[...] * lk2_ref[...], axis=-1, keepdims=True)) + lam_init)
    o = a1 / l1 - lam * (a2 / l2)
    return o * lax.rsqrt(jnp.mean(o * o, axis=0, keepdims=True) + RMS_EPS) * g_ref[...] * (1.0 - lam_init)


def _attention(qt, k, vt, extra=(), *, dv, n_sub, finalize, name):
    b, hw, n = qt.shape
    t = k.shape[1]
    heads = hw // HEAD_SLOT
    tq = _tile(n, ATTN_TQ)
    tk = 256 if t % 256 == 0 else 128
    nchunk = t // tk
    unroll = ATTN_UNROLL
    full = lambda a: pl.BlockSpec(a.shape, lambda bi, hi, qi: (0,) * a.ndim)
    return pl.pallas_call(
        functools.partial(_attn_kernel, tk=tk, unroll=unroll, n_sub=n_sub, finalize=finalize),
        out_shape=jax.ShapeDtypeStruct((b, heads * dv, n), BF16),
        grid=(b, heads, n // tq),
        in_specs=[pl.BlockSpec((1, HEAD_SLOT, tq), lambda bi, hi, qi: (bi, hi, qi)),
                  pl.BlockSpec((1, t, HEAD_SLOT), lambda bi, hi, qi: (bi, 0, hi)),
                  pl.BlockSpec((1, dv, t), lambda bi, hi, qi: (bi, hi, 0))] + [full(a) for a in extra],
        out_specs=pl.BlockSpec((1, dv, tq), lambda bi, hi, qi: (bi, hi, qi)),
        scratch_shapes=[pltpu.VMEM((8, HEAD_SLOT), F32), pltpu.VMEM((2, n_sub, tk, tq), F32)],
        compiler_params=_cparams(("parallel", "parallel", "arbitrary")),
        name=name,
    )(qt, k, vt, *extra)


def _out_kernel(*refs, n_rowmajor):
    (x_ref, g_ref, lng_ref, lnb_ref, scf_ref, shf_ref, rt_ref), rest = refs[:7], refs[7:]
    x1_ref, aff_ref, hp_ref = rest[-3:]
    ops = rest[:-3]
    m = None
    for i in range(len(ops) // 2):
        a = ops[2 * i][0]
        w = ops[2 * i + 1][...]
        part = _dot(a, w) if i < n_rowmajor else _dot_tn(a, w)
        m = part if m is None else m + part
    x1 = _ln(DEEPNORM_ALPHA * x_ref[0] + (1.0 + g_ref[0]) * m, lng_ref[...], lnb_ref[...])
    x1_ref[0] = x1
    hf = (x1 * (1.0 + scf_ref[0]) + shf_ref[0]).astype(BF16)
    lg = _dot_nt(rt_ref[...], hf)
    e = jnp.exp(lg - jnp.max(lg, axis=0, keepdims=True))
    aff_ref[0] = e / jnp.sum(e, axis=0, keepdims=True)
    dh = hf.shape[1] // 2
    lo = pltpu.bitcast(hf[:, :dh].astype(F32), jnp.uint32)
    hi = pltpu.bitcast(hf[:, dh:].astype(F32), jnp.uint32)
    hp_ref[0] = lax.shift_right_logical(lo, jnp.uint32(16)) | (hi & jnp.uint32(0xFFFF0000))


def _mixer_out(x, g, lng, lnb, scf, shf, router_t, rowmajor_ops, chanmajor_ops):
    b, n, d = x.shape
    e = router_t.shape[0]
    tm = _tile(n, 512)
    vec = pl.BlockSpec((1, 1, d), lambda bi, ti: (bi, 0, 0))
    full = lambda a: pl.BlockSpec(a.shape, lambda bi, ti: (0,) * a.ndim)
    args = [x, g, lng, lnb, scf, shf, router_t]
    specs = [pl.BlockSpec((1, tm, d), lambda bi, ti: (bi, ti, 0)), vec, full(lng), full(lnb), vec, vec,
             full(router_t)]
    for a, w in rowmajor_ops:
        args += [a, w]
        specs += [pl.BlockSpec((1, tm, a.shape[2]), lambda bi, ti: (bi, ti, 0)), full(w)]
    for a, w in chanmajor_ops:
        args += [a, w]
        specs += [pl.BlockSpec((1, a.shape[1], tm), lambda bi, ti: (bi, 0, ti)), full(w)]
    return pl.pallas_call(
        functools.partial(_out_kernel, n_rowmajor=len(rowmajor_ops)),
        out_shape=(jax.ShapeDtypeStruct((b, n, d), F32), jax.ShapeDtypeStruct((b, e, n), F32),
                   jax.ShapeDtypeStruct((b, n, d // 2), jnp.uint32)),
        grid=(b, n // tm),
        in_specs=specs,
        out_specs=(pl.BlockSpec((1, tm, d), lambda bi, ti: (bi, ti, 0)),
                   pl.BlockSpec((1, e, tm), lambda bi, ti: (bi, 0, ti)),
                   pl.BlockSpec((1, tm, d // 2), lambda bi, ti: (bi, ti, 0))),
        compiler_params=_cparams(("parallel", "parallel")),
        name="mixer_out",
    )(*args)


ROW_UNROLL = 8
FFN_ROWS = 512
TOKEN_SPLIT = 4096


def _ffn_kernel(idx_ref, hp_ref, g_ref, wg_ref, wu_ref, wd_ref, y_ref, rows_ref):
    cap = y_ref.shape[2]

    def body(jb, carry):
        for u in range(ROW_UNROLL):
            j = jb * ROW_UNROLL + u
            rows_ref[pl.ds(j, 1), :] = hp_ref[0, pl.ds(idx_ref[0, 0, j], 1), :]
        return carry

    lax.fori_loop(0, cap // ROW_UNROLL, body, 0)
    blk = min(cap, FFN_ROWS)
    for c0 in range(0, cap, blk):
        w = rows_ref[c0:c0 + blk]
        lo = pltpu.bitcast(lax.shift_left(w, jnp.uint32(16)), F32)
        hi = pltpu.bitcast(w & jnp.uint32(0xFFFF0000), F32)
        xs = jnp.concatenate([lo, hi], axis=1).astype(BF16)
        gate = _dot(xs, wg_ref[0])
        hid = gate / (1.0 + jnp.exp(-gate)) * _dot(xs, wu_ref[0])
        y_ref[0, 0, c0:c0 + blk] = (_dot(hid.astype(BF16), wd_ref[0]) * g_ref[0, c0:c0 + blk]).astype(BF16)


def _moe_ffn(hp, idx3, gates3, wg, wu, wd, cap):
    b, n, dh = hp.shape
    n_exp, d, ff = wg.shape
    return pl.pallas_call(
        _ffn_kernel,
        out_shape=jax.ShapeDtypeStruct((b, n_exp, cap, d), BF16),
        grid=(b, n_exp),
        in_specs=[pl.BlockSpec((1, 1, cap), lambda bi, ei: (bi * n_exp + ei, 0, 0), memory_space=pltpu.SMEM),
                  pl.BlockSpec((1, n, dh), lambda bi, ei: (bi, 0, 0), pipeline_mode=pl.Buffered(1)),
                  pl.BlockSpec((1, cap, 1), lambda bi, ei: (bi * n_exp + ei, 0, 0)),
                  pl.BlockSpec((1, d, ff), lambda bi, ei: (ei, 0, 0)),
                  pl.BlockSpec((1, d, ff), lambda bi, ei: (ei, 0, 0)),
                  pl.BlockSpec((1, ff, d), lambda bi, ei: (ei, 0, 0))],
        out_specs=pl.BlockSpec((1, 1, cap, d), lambda bi, ei: (bi, ei, 0, 0)),
        scratch_shapes=[pltpu.VMEM((cap, dh), jnp.uint32)],
        compiler_params=_cparams(("parallel", "arbitrary")),
        name="moe_ffn",
    )(idx3, hp, gates3, wg, wu, wd)


def _scatter_kernel(idx_ref, bnd_ref, y_ref, f_ref, rows_ref):
    nt = f_ref.shape[1]
    base = pl.program_id(1) * nt

    @pl.when(pl.program_id(2) == 0)
    def _():
        f_ref[...] = jnp.zeros_like(f_ref)

    rows_ref[...] = y_ref[0, 0].astype(F32)
    lo = bnd_ref[0, 0, pl.program_id(1)]
    hi = bnd_ref[0, 0, pl.program_id(1) + 1]

    def one(j, carry):
        i = idx_ref[0, 0, j] - base
        f_ref[0, pl.ds(i, 1), :] = f_ref[0, pl.ds(i, 1), :] + rows_ref[pl.ds(j, 1), :]
        return carry

    def batch(jb, carry):
        j0 = jb * ROW_UNROLL
        ids = [idx_ref[0, 0, j0 + u] - base for u in range(ROW_UNROLL)]
        sums = [f_ref[0, pl.ds(ids[u], 1), :] + rows_ref[pl.ds(j0 + u, 1), :] for u in range(ROW_UNROLL)]
        for u in range(ROW_UNROLL):
            f_ref[0, pl.ds(ids[u], 1), :] = sums[u]
        return carry

    head_end = jnp.minimum((lo + ROW_UNROLL - 1) // ROW_UNROLL * ROW_UNROLL, hi)
    tail_start = jnp.maximum(head_end, hi // ROW_UNROLL * ROW_UNROLL)
    lax.fori_loop(lo, head_end, one, 0)
    lax.fori_loop(head_end // ROW_UNROLL, tail_start // ROW_UNROLL, batch, 0)
    lax.fori_loop(tail_start, hi, one, 0)


def _moe_scatter(y, idx3, bounds3, n):
    b, n_exp, cap, d = y.shape
    nt = _tile(n, TOKEN_SPLIT)
    return pl.pallas_call(
        _scatter_kernel,
        out_shape=jax.ShapeDtypeStruct((b, n, d), F32),
        grid=(b, n // nt, n_exp),
        in_specs=[pl.BlockSpec((1, 1, cap), lambda bi, ki, ei: (bi * n_exp + ei, 0, 0), memory_space=pltpu.SMEM),
                  pl.BlockSpec((1, 1, n // nt + 1), lambda bi, ki, ei: (bi * n_exp + ei, 0, 0),
                               memory_space=pltpu.SMEM),
                  pl.BlockSpec((1, 1, cap, d), lambda bi, ki, ei: (bi, ei, 0, 0))],
        out_specs=pl.BlockSpec((1, nt, d), lambda bi, ki, ei: (bi, ki, 0)),
        scratch_shapes=[pltpu.VMEM((cap, d), F32)],
        compiler_params=_cparams(("parallel", "parallel", "arbitrary")),
        name="moe_scatter",
    )(idx3, bounds3, y)


def _post_kernel(x_ref, f_ref, g_ref, lng_ref, lnb_ref, o_ref):
    o_ref[0] = _ln(DEEPNORM_ALPHA * x_ref[0] + (1.0 + g_ref[0]) * f_ref[0], lng_ref[...], lnb_ref[...])


def _post(x1, f, g, lng, lnb):
    b, n, d = x1.shape
    tm = _tile(n, 512)
    blk = pl.BlockSpec((1, tm, d), lambda bi, ti: (bi, ti, 0))
    full = lambda a: pl.BlockSpec(a.shape, lambda bi, ti: (0,) * a.ndim)
    return pl.pallas_call(
        _post_kernel,
        out_shape=jax.ShapeDtypeStruct((b, n, d), F32),
        grid=(b, n // tm),
        in_specs=[blk, blk, pl.BlockSpec((1, 1, d), lambda bi, ti: (bi, 0, 0)), full(lng), full(lnb)],
        out_specs=blk,
        compiler_params=_cparams(("parallel", "parallel")),
        name="ffn_post",
    )(x1, f, g, lng, lnb)


def _select_kernel(aff_ref, idx_ref, gate_ref, *, cap, n_exp):
    a = aff_ref[0]
    er, lanes = a.shape
    r_n = er // n_exp
    e_pad = 16
    bits = pltpu.bitcast(a, jnp.int32)

    member = (lax.broadcasted_iota(jnp.int32, (e_pad, er), 1) // r_n
              == lax.broadcasted_iota(jnp.int32, (e_pad, er), 0))
    gs = jnp.where(member, 1.0, 0.0).astype(BF16)
    member_t = (lax.broadcasted_iota(jnp.int32, (er, e_pad), 0) // r_n
                == lax.broadcasted_iota(jnp.int32, (er, e_pad), 1))
    gst = jnp.where(member_t, 1.0, 0.0).astype(BF16)
    ri = lax.broadcasted_iota(jnp.int32, (er, er), 0)
    rj = lax.broadcasted_iota(jnp.int32, (er, er), 1)
    rows_before = jnp.where((ri // r_n == rj // r_n) & (rj < ri), 1.0, 0.0).astype(BF16)
    li = lax.broadcasted_iota(jnp.int32, (lanes, lanes), 0)
    lj = lax.broadcasted_iota(jnp.int32, (lanes, lanes), 1)
    lanes_upto = jnp.where(li <= lj, 1.0, 0.0).astype(BF16)

    def lane_total(m):
        return jnp.broadcast_to(jnp.sum(m, axis=1, keepdims=True), m.shape)

    def expert_total(m):
        return _dot(gs, lane_total(m).astype(BF16))

    def to_rows(ev):
        hi = jnp.floor(ev * (1.0 / 128.0))
        return 128.0 * _dot(gst, hi.astype(BF16)) + _dot(gst, (ev - 128.0 * hi).astype(BF16))

    def search(i, t):
        cand = t | lax.shift_left(jnp.int32(1), 30 - i)
        cnt = expert_total(jnp.where(bits >= cand, 1.0, 0.0))
        return jnp.where(to_rows(jnp.where(cnt >= cap, 1.0, 0.0)) > 0.5, cand, t)

    t = lax.fori_loop(0, 31, search, jnp.zeros((er, lanes), jnp.int32))
    gt = jnp.where(bits > t, 1.0, 0.0)
    eq = jnp.where(bits == t, 1.0, 0.0)

    def prefix(m):
        rc = _dot(m.astype(BF16), lanes_upto)
        off = _dot(rows_before, jnp.broadcast_to(rc[:, lanes - 1:lanes], m.shape).astype(BF16))
        return rc, off

    rc_eq, off_eq = prefix(eq)
    need = cap - to_rows(expert_total(gt))
    sel = jnp.maximum(gt, eq * jnp.where(off_eq + rc_eq - eq < need, 1.0, 0.0))
    rc, off = prefix(sel)
    row_end = off + jnp.broadcast_to(rc[:, lanes - 1:lanes], rc.shape)

    slot = lax.broadcasted_iota(jnp.int32, (cap, lanes), 0).astype(F32) + 1.0
    lane_c = lax.broadcasted_iota(jnp.int32, (cap, lanes), 1)
    lane_cf = lane_c.astype(F32)
    diag = lax.broadcasted_iota(jnp.int32, (r_n, lanes), 0) == lax.broadcasted_iota(jnp.int32, (r_n, lanes), 1)
    lane_r = lax.broadcasted_iota(jnp.int32, (1, lanes), 1)
    pad = jnp.zeros((lanes - r_n, lanes), F32)

    def table(m):
        return (jnp.concatenate([m, pad], axis=0) if r_n < lanes else m).astype(BF16)

    idx_all = jnp.zeros((cap, lanes), jnp.int32)
    gate_all = jnp.zeros((cap, lanes), F32)
    for e in range(n_exp):
        r0 = e * r_n
        rc_e = rc[r0:r0 + r_n]
        end_lane = jnp.sum(jnp.where(diag, row_end[r0:r0 + r_n], 0.0), axis=0, keepdims=True)
        end_lane = jnp.where(lane_r < r_n, end_lane, 3.0e38)
        off_lane = jnp.sum(jnp.where(diag, off[r0:r0 + r_n], 0.0), axis=0, keepdims=True)
        row_j = jnp.sum(jnp.where(end_lane < slot, 1.0, 0.0), axis=1, keepdims=True)
        onehot = jnp.where(lane_cf == row_j, 1.0, 0.0)
        off_j = jnp.sum(onehot * off_lane, axis=1, keepdims=True)
        oh = onehot.astype(BF16)
        cum_j = _dot(oh, table(rc_e)) + off_j
        lane_j = jnp.sum(jnp.where(cum_j < slot, 1.0, 0.0), axis=1, keepdims=True)
        a_e = a[r0:r0 + r_n]
        a1 = a_e.astype(BF16).astype(F32)
        a2 = (a_e - a1).astype(BF16).astype(F32)
        a3 = a_e - a1 - a2
        aff_j = _dot(oh, table(a1)) + _dot(oh, table(a2)) + _dot(oh, table(a3))
        gate_j = jnp.sum(jnp.where(lane_cf == lane_j, aff_j, 0.0), axis=1, keepdims=True)
        idx_j = (row_j * float(lanes) + lane_j).astype(jnp.int32)
        idx_all = jnp.where(lane_c == e, idx_j, idx_all)
        gate_all = jnp.where(lane_c == e, gate_j, gate_all)
    idx_ref[0] = idx_all
    gate_ref[0] = gate_all


def _moe_select(aff_t, cap):
    b, n_exp, n = aff_t.shape
    lanes = HEAD_SLOT
    er = n_exp * (n // lanes)
    idx, gates = pl.pallas_call(
        functools.partial(_select_kernel, cap=cap, n_exp=n_exp),
        out_shape=(jax.ShapeDtypeStruct((b, cap, lanes), jnp.int32), jax.ShapeDtypeStruct((b, cap, lanes), F32)),
        grid=(b,),
        in_specs=[pl.BlockSpec((1, er, lanes), lambda bi: (bi, 0, 0))],
        out_specs=(pl.BlockSpec((1, cap, lanes), lambda bi: (bi, 0, 0)),
                   pl.BlockSpec((1, cap, lanes), lambda bi: (bi, 0, 0))),
        compiler_params=_cparams(("parallel",)),
        name="moe_select",
    )(aff_t.reshape(b, er, lanes))
    to_expert_major = lambda m: jnp.swapaxes(m[:, :, :n_exp], 1, 2)
    return to_expert_major(gates), to_expert_major(idx)


def _moe(x1, aff_t, hp, gf, wg, wu, wd, lng, lnb):
    b, n, d = x1.shape
    n_exp = aff_t.shape[1]
    cap = CAPACITY_FACTOR * n // n_exp
    gates, idx = _moe_select(aff_t, cap)
    idx3 = idx.reshape(b * n_exp, 1, cap)
    gates3 = gates.reshape(b * n_exp, cap, 1)
    nt = _tile(n, TOKEN_SPLIT)
    edges = jnp.arange(0, n + 1, nt, dtype=jnp.int32)
    bounds3 = jnp.sum(idx[..., None] < edges, axis=2, dtype=jnp.int32).reshape(b * n_exp, 1, n // nt + 1)
    y = _moe_ffn(hp, idx3, gates3, wg, wu, wd, cap)
    f = _moe_scatter(y, idx3, bounds3, n)
    return _post(x1, f, gf, lng, lnb)


def _rope_angles(n_tokens, dim):
    n_rows = n_tokens // GRID_W
    row = jnp.repeat(jnp.arange(n_rows, dtype=F32), GRID_W)
    col = jnp.tile(jnp.arange(GRID_W, dtype=F32), n_rows)
    n_freq = dim // 4
    inv_freq = ROPE_BASE ** (-jnp.arange(n_freq, dtype=F32) / n_freq)
    ang = jnp.concatenate([row[:, None] * inv_freq, col[:, None] * inv_freq], -1)
    return jnp.cos(ang), jnp.sin(ang)


def _rope_tables(n_tokens, dim, lane_offset, identity):
    half = dim // 2
    if identity:
        cos = jnp.ones((n_tokens, half), F32)
        sin = jnp.zeros((n_tokens, half), F32)
    else:
        cos, sin = _rope_angles(n_tokens, dim)
    c2 = jnp.concatenate([cos, cos], -1)
    s2 = jnp.concatenate([-sin, sin], -1)
    reps = (HEAD_SLOT - lane_offset) // dim if lane_offset == 0 else 1
    ck = jnp.zeros((n_tokens, HEAD_SLOT), F32)
    sk = jnp.zeros((n_tokens, HEAD_SLOT), F32)
    for r in range(reps):
        ck = ck.at[:, lane_offset + r * dim:lane_offset + (r + 1) * dim].set(c2)
        sk = sk.at[:, lane_offset + r * dim:lane_offset + (r + 1) * dim].set(s2)
    return ck, sk, c2.T, s2.T


def _deinterleave(w, axis):
    return jnp.concatenate([jnp.take(w, jnp.arange(0, w.shape[axis], 2), axis=axis),
                            jnp.take(w, jnp.arange(1, w.shape[axis], 2), axis=axis)], axis=axis)


def _prep_layer0(w_in, ln_g, ln_b, ws, bs, q_norm, w_uq, kv_norm, w_ukv):
    d = w_in.shape[0]
    w = GMLP_WIDTH
    o_kr = 2 * w + MLA_Q_RANK + MLA_KV_RANK
    kr = _deinterleave(w_in[:, o_kr:o_kr + MLA_ROPE], 1)
    kr_sw = jnp.concatenate([kr[:, MLA_ROPE // 2:], kr[:, :MLA_ROPE // 2]], 1)
    slot = lambda m: jnp.zeros((d, HEAD_SLOT), F32).at[:, MLA_NOPE:MLA_NOPE + MLA_ROPE].set(m)
    win = jnp.concatenate([w_in[:, :o_kr], slot(kr), slot(kr_sw)], 1).astype(BF16)

    qd = MLA_NOPE + MLA_ROPE
    wq = w_uq.reshape(MLA_Q_RANK, MLA_HEADS, qd)
    wq = jnp.concatenate([wq[..., :MLA_NOPE], _deinterleave(wq[..., MLA_NOPE:], 2),
                          jnp.zeros((MLA_Q_RANK, MLA_HEADS, HEAD_SLOT - qd), F32)], -1)
    wq_t = wq.reshape(MLA_Q_RANK, MLA_HEADS * HEAD_SLOT).T.astype(BF16)

    wkv = w_ukv.reshape(MLA_KV_RANK, MLA_HEADS, MLA_NOPE + MLA_V)
    wk = jnp.concatenate([wkv[..., :MLA_NOPE], jnp.zeros((MLA_KV_RANK, MLA_HEADS, HEAD_SLOT - MLA_NOPE), F32)], -1)
    wk = wk.reshape(MLA_KV_RANK, MLA_HEADS * HEAD_SLOT).astype(BF16)
    wv_t = wkv[..., MLA_NOPE:].reshape(MLA_KV_RANK, MLA_HEADS * MLA_V).T.astype(BF16)

    bsb = jnp.broadcast_to(bs[:, :, None], (GMLP_GROUPS, CHUNK, GMLP_GROUP_CH))
    return (win, ln_g.reshape(1, -1), ln_b.reshape(1, -1), ws.astype(BF16), bsb,
            q_norm.reshape(1, -1), wq_t, kv_norm.reshape(1, -1), wk, wv_t)


def _prep_layer1(w_in):
    d = w_in.shape[0]
    dw = 2 * DIFF_HEADS * DIFF_HEAD_DIM
    perm = lambda m: _deinterleave(m.reshape(d, 2 * DIFF_HEADS, DIFF_HEAD_DIM), 2).reshape(d, dw)
    wq_t = perm(w_in[:, :dw]).T.astype(BF16)
    wk = perm(w_in[:, dw:2 * dw]).astype(BF16)
    wv_t = w_in[:, 2 * dw:].T.astype(BF16)
    return wq_t, wk, wv_t


def _split6(m_row):
    return [v[:, None, :] for v in jnp.split(m_row, 6, axis=-1)]


def kernel(x, c, ctx, c_ctx, w_mod_0, b_mod_0, w_in_0, gmlp_ln_g_0, gmlp_ln_b_0, gmlp_ws_0, gmlp_bs_0, mla_q_norm_0, mla_w_uq_0, mla_kv_norm_0, mla_w_ukv_0, w_out_0, ln_mix_g_0, ln_mix_b_0, router_0, w_gate_0, w_up_0, w_down_0, ln_ffn_g_0, ln_ffn_b_0, w_mod_1, b_mod_1, w_in_1, lambda_q1_1, lambda_k1_1, lambda_q2_1, lambda_k2_1, subln_g_1, w_out_1, ln_mix_g_1, ln_mix_b_1, router_1, w_gate_1, w_up_1, w_down_1, ln_ffn_g_1, ln_ffn_b_1):
    b, n, d = x.shape
    m_ctx = ctx.shape[1]
    row = lambda v: v.reshape(1, -1)

    cond = jnp.concatenate([c, c_ctx[None, :], jnp.zeros((-(b + 1) % 8, d), F32)], 0)
    mod0 = _modulation(cond, w_mod_0, b_mod_0)
    mod1 = _modulation(cond, w_mod_1, b_mod_1)
    sh_a0, sc_a0, g_a0, sh_f0, sc_f0, g_f0 = _split6(mod0[:b])
    csh_a0, csc_a0, cg_a0, csh_f0, csc_f0, cg_f0 = [jnp.broadcast_to(v, (b, 1, d)) for v in _split6(mod0[b:b + 1])]
    sh_a1, sc_a1, g_a1, sh_f1, sc_f1, g_f1 = _split6(mod1[:b])
    csh_a1, csc_a1 = [jnp.broadcast_to(v, (b, 1, d)) for v in _split6(mod1[b:b + 1])[:2]]

    wts0 = _prep_layer0(w_in_0, gmlp_ln_g_0, gmlp_ln_b_0, gmlp_ws_0, gmlp_bs_0,
                        mla_q_norm_0, mla_w_uq_0, mla_kv_norm_0, mla_w_ukv_0)
    tabs_l = _rope_tables(n, MLA_ROPE, MLA_NOPE, identity=False)
    tabs_c = _rope_tables(m_ctx, MLA_ROPE, MLA_NOPE, identity=True)
    a_l, q_l, k_l, v_l = _proj0(x, sc_a0, sh_a0, wts0, tabs_l)
    a_c, q_c, k_c, v_c = _proj0(ctx, csc_a0, csh_a0, wts0, tabs_c)
    o_l = _attention(q_l, jnp.concatenate([k_l, k_c], 1), jnp.concatenate([v_l, v_c], 2),
                     dv=MLA_V, n_sub=1, finalize=_finalize_plain, name="mla_attn")
    o_c = _attention(q_c, k_c, v_c, dv=MLA_V, n_sub=1, finalize=_finalize_plain, name="mla_attn_ctx")

    w_out_a = w_out_0[:GMLP_WIDTH].astype(BF16)
    w_out_o = w_out_0[GMLP_WIDTH:].astype(BF16)
    router_t0 = router_0.T.astype(BF16)
    wg0, wu0, wd0 = w_gate_0.astype(BF16), w_up_0.astype(BF16), w_down_0.astype(BF16)
    lng, lnb = row(ln_mix_g_0), row(ln_mix_b_0)
    x1, aff, hp = _mixer_out(x, g_a0, lng, lnb, sc_f0, sh_f0, router_t0, [(a_l, w_out_a)], [(o_l, w_out_o)])
    x_lat = _moe(x1, aff, hp, g_f0, wg0, wu0, wd0, row(ln_ffn_g_0), row(ln_ffn_b_0))
    c1, caff, chp = _mixer_out(ctx, cg_a0, lng, lnb, csc_f0, csh_f0, router_t0, [(a_c, w_out_a)], [(o_c, w_out_o)])
    x_ctx = _moe(c1, caff, chp, cg_f0, wg0, wu0, wd0, row(ln_ffn_g_0), row(ln_ffn_b_0))

    lam_init = 0.8 - 0.6 * math.exp(-0.3 * 1)
    wts1 = _prep_layer1(w_in_1)
    tabs_l = _rope_tables(n, DIFF_HEAD_DIM, 0, identity=False)
    tabs_c = _rope_tables(m_ctx, DIFF_HEAD_DIM, 0, identity=True)
    q_l, k_l, v_l = _proj1(x_lat, sc_a1, sh_a1, wts1, tabs_l)
    _, k_c, v_c = _proj1(x_ctx, csc_a1, csh_a1, wts1, tabs_c)
    extra = (row(lambda_q1_1), row(lambda_k1_1), row(lambda_q2_1), row(lambda_k2_1), subln_g_1.reshape(-1, 1))
    o_l = _attention(q_l, jnp.concatenate([k_l, k_c], 1), jnp.concatenate([v_l, v_c], 2), extra,
                     dv=2 * DIFF_HEAD_DIM, n_sub=2, finalize=functools.partial(_finalize_diff, lam_init=lam_init),
                     name="diff_attn")
    x1, aff, hp = _mixer_out(x_lat, g_a1, row(ln_mix_g_1), row(ln_mix_b_1), sc_f1, sh_f1, router_1.T.astype(BF16),
                             [], [(o_l, w_out_1.astype(BF16))])
    return _moe(x1, aff, hp, g_f1, w_gate_1.astype(BF16), w_up_1.astype(BF16), w_down_1.astype(BF16),
                row(ln_ffn_g_1), row(ln_ffn_b_1))
```

```python
import functools
import math

import jax
import jax.numpy as jnp
from jax import lax
from jax.experimental import pallas as pl
from jax.experimental.pallas import tpu as pltpu

F32 = jnp.float32
BF16 = jnp.bfloat16

DEPTH = 2
GRID_W = 64
ROPE_BASE = 10000.0
CHUNK = 128
GMLP_GROUPS = 4
GMLP_GROUP_CH = 128
GMLP_WIDTH = GMLP_GROUPS * GMLP_GROUP_CH
MLA_HEADS = 8
MLA_NOPE = 64
MLA_ROPE = 32
MLA_V = 64
MLA_Q_RANK = 256
MLA_KV_RANK = 128
DIFF_HEADS = 8
DIFF_HEAD_DIM = 64
CAPACITY_FACTOR = 2
DEEPNORM_ALPHA = (2 * DEPTH) ** 0.25
LN_EPS = 1e-5
RMS_EPS = 1e-6
LOG2E = 1.4426950408889634

LANES = 128
SUBLANES = 8
HEAD_SLOT = LANES
NEG_BIG = -1e30
VMEM_LIMIT = 56 * 1024 * 1024


def _cparams(sem):
    return pltpu.CompilerParams(dimension_semantics=sem, vmem_limit_bytes=VMEM_LIMIT)


def _tile(n, pref):
    return pref if n % pref == 0 else n


def _ln(x, g, b):
    mu = jnp.mean(x, axis=-1, keepdims=True)
    xc = x - mu
    var = jnp.mean(xc * xc, axis=-1, keepdims=True)
    return xc * lax.rsqrt(var + LN_EPS) * g + b


def _rms(x, g):
    return x * lax.rsqrt(jnp.mean(x * x, axis=-1, keepdims=True) + RMS_EPS) * g


def _gelu(x):
    return 0.5 * x * (1.0 + lax.erf(x * (2.0 ** -0.5)))


def _to_tiled(v):
    m, w = v.shape
    nt = w // LANES
    tiles = [v[:, c * LANES:(c + 1) * LANES].reshape(m // SUBLANES, SUBLANES, LANES) for c in range(nt)]
    return jnp.stack(tiles, axis=1).reshape(m * nt, LANES)


def _from_tiled(t, w):
    nt = w // LANES
    m = t.shape[0] // nt
    t4 = t.reshape(m // SUBLANES, nt, SUBLANES, LANES)
    return jnp.concatenate([t4[:, c].reshape(m, LANES) for c in range(nt)], axis=1)


def _tiled_row(i, w):
    return (i // SUBLANES) * (SUBLANES * (w // LANES)) + i % SUBLANES


def _dot(a, b):
    return jnp.dot(a, b, preferred_element_type=F32)


def _dot_nt(a, b):
    return lax.dot_general(a, b, (((1,), (1,)), ((), ())), preferred_element_type=F32)


def _dot_tn(a, b):
    return lax.dot_general(a, b, (((0,), (0,)), ((), ())), preferred_element_type=F32)


def _mod_kernel(c_ref, w_ref, b_ref, o_ref):
    c = c_ref[...]
    s = c / (1.0 + jnp.exp(-c))
    o_ref[...] = _dot(s.astype(BF16), w_ref[...].astype(BF16)) + b_ref[...]


def _modulation(cond, w_mod, b_mod):
    r, d = cond.shape
    n = w_mod.shape[1]
    tn = _tile(n, 1024)
    return pl.pallas_call(
        _mod_kernel,
        out_shape=jax.ShapeDtypeStruct((r, n), F32),
        grid=(n // tn,),
        in_specs=[pl.BlockSpec((r, d), lambda j: (0, 0)),
                  pl.BlockSpec((d, tn), lambda j: (0, j)),
                  pl.BlockSpec((1, tn), lambda j: (0, j))],
        out_specs=pl.BlockSpec((r, tn), lambda j: (0, j)),
        compiler_params=_cparams(("parallel",)),
        name="modulation",
    )(cond, w_mod, b_mod.reshape(1, n))


def _proj0_kernel(x_ref, sc_ref, sh_ref, win_ref, lng_ref, lnb_ref, ws_ref, bs_ref,
                  qn_ref, wq_ref, kvn_ref, wk_ref, wv_ref, ck_ref, sk_ref, cq_ref, sq_ref,
                  a_ref, q_ref, k_ref, v_ref, *, q_scale):
    tm = x_ref.shape[1]
    h = (x_ref[0] * (1.0 + sc_ref[0]) + sh_ref[0]).astype(BF16)
    y = _dot(h, win_ref[...])
    w = GMLP_WIDTH
    u = _gelu(y[:, :w])
    vn = _ln(_gelu(y[:, w:2 * w]), lng_ref[...], lnb_ref[...]).astype(BF16)
    for ci in range(tm // CHUNK):
        r0 = ci * CHUNK
        for g in range(GMLP_GROUPS):
            c0 = g * GMLP_GROUP_CH
            mixed = _dot(ws_ref[g], vn[r0:r0 + CHUNK, c0:c0 + GMLP_GROUP_CH]) + bs_ref[g]
            a_ref[0, r0:r0 + CHUNK, c0:c0 + GMLP_GROUP_CH] = (
                u[r0:r0 + CHUNK, c0:c0 + GMLP_GROUP_CH] * mixed).astype(BF16)

    o = 2 * w
    cq = _rms(y[:, o:o + MLA_Q_RANK], qn_ref[...]).astype(BF16)
    qt = _dot_nt(wq_ref[...], cq) * q_scale
    q_ref[0] = qt.astype(BF16)
    hr = MLA_ROPE // 2
    cq_t = cq_ref[...]
    sq_t = sq_ref[...]
    for hd in range(MLA_HEADS):
        r0 = hd * HEAD_SLOT + MLA_NOPE
        x1 = qt[r0:r0 + hr]
        x2 = qt[r0 + hr:r0 + 2 * hr]
        blk = jnp.concatenate([x1, x2], axis=0)
        swp = jnp.concatenate([x2, x1], axis=0)
        q_ref[0, r0:r0 + 2 * hr, :] = (blk * cq_t + swp * sq_t).astype(BF16)

    o += MLA_Q_RANK
    ckv = _rms(y[:, o:o + MLA_KV_RANK], kvn_ref[...]).astype(BF16)
    o += MLA_KV_RANK
    kr = y[:, o:o + HEAD_SLOT] * ck_ref[...] + y[:, o + HEAD_SLOT:o + 2 * HEAD_SLOT] * sk_ref[...]
    k_ref[0] = (_dot(ckv, wk_ref[...]) + jnp.tile(kr, (1, MLA_HEADS))).astype(BF16)
    v_ref[0] = _dot_nt(wv_ref[...], ckv).astype(BF16)


def _proj0(x, sc, sh, wts, tabs):
    b, n, d = x.shape
    tm = _tile(n, 512)
    ck, sk, cq, sq = tabs
    win, lng, lnb, ws, bsb, qn, wq, kvn, wk, wv = wts
    hw = MLA_HEADS * HEAD_SLOT
    vw = MLA_HEADS * MLA_V
    full = lambda a: pl.BlockSpec(a.shape, lambda bi, ti: (0,) * a.ndim)
    q_scale = (MLA_NOPE + MLA_ROPE) ** -0.5 * LOG2E
    return pl.pallas_call(
        functools.partial(_proj0_kernel, q_scale=q_scale),
        out_shape=(jax.ShapeDtypeStruct((b, n, GMLP_WIDTH), BF16),
                   jax.ShapeDtypeStruct((b, hw, n), BF16),
                   jax.ShapeDtypeStruct((b, n, hw), BF16),
                   jax.ShapeDtypeStruct((b, vw, n), BF16)),
        grid=(b, n // tm),
        in_specs=[pl.BlockSpec((1, tm, d), lambda bi, ti: (bi, ti, 0)),
                  pl.BlockSpec((1, 1, d), lambda bi, ti: (bi, 0, 0)),
                  pl.BlockSpec((1, 1, d), lambda bi, ti: (bi, 0, 0)),
                  full(win), full(lng), full(lnb), full(ws), full(bsb),
                  full(qn), full(wq), full(kvn), full(wk), full(wv),
                  pl.BlockSpec((tm, HEAD_SLOT), lambda bi, ti: (ti, 0)),
                  pl.BlockSpec((tm, HEAD_SLOT), lambda bi, ti: (ti, 0)),
                  pl.BlockSpec((MLA_ROPE, tm), lambda bi, ti: (0, ti)),
                  pl.BlockSpec((MLA_ROPE, tm), lambda bi, ti: (0, ti))],
        out_specs=(pl.BlockSpec((1, tm, GMLP_WIDTH), lambda bi, ti: (bi, ti, 0)),
                   pl.BlockSpec((1, hw, tm), lambda bi, ti: (bi, 0, ti)),
                   pl.BlockSpec((1, tm, hw), lambda bi, ti: (bi, ti, 0)),
                   pl.BlockSpec((1, vw, tm), lambda bi, ti: (bi, 0, ti))),
        compiler_params=_cparams(("parallel", "parallel")),
        name="proj0",
    )(x, sc, sh, win, lng, lnb, ws, bsb, qn, wq, kvn, wk, wv, ck, sk, cq, sq)


def _proj1_kernel(x_ref, sc_ref, sh_ref, wq_ref, wk_ref, wv_ref, ck_ref, sk_ref, cq_ref, sq_ref,
                  q_ref, k_ref, v_ref, *, q_scale):
    h = (x_ref[0] * (1.0 + sc_ref[0]) + sh_ref[0]).astype(BF16)
    n_sub = 2 * DIFF_HEADS
    hd = DIFF_HEAD_DIM
    hr = hd // 2
    qt = _dot_nt(wq_ref[...], h) * q_scale
    cq_t = cq_ref[...]
    sq_t = sq_ref[...]
    for s in range(n_sub):
        r0 = s * hd
        x1 = qt[r0:r0 + hr]
        x2 = qt[r0 + hr:r0 + hd]
        swp = jnp.concatenate([x2, x1], axis=0)
        q_ref[0, r0:r0 + hd, :] = (qt[r0:r0 + hd] * cq_t + swp * sq_t).astype(BF16)
    k = _dot(h, wk_ref[...])
    width = k.shape[1]
    lane = lax.broadcasted_iota(jnp.int32, k.shape, 1)
    partner = jnp.where((lane % hd) < hr, pltpu.roll(k, width - hr, 1), pltpu.roll(k, hr, 1))
    reps = width // HEAD_SLOT
    k_ref[0] = (k * jnp.tile(ck_ref[...], (1, reps)) + partner * jnp.tile(sk_ref[...], (1, reps))).astype(BF16)
    v_ref[0] = _dot_nt(wv_ref[...], h).astype(BF16)


def _proj1(x, sc, sh, wts, tabs):
    b, n, d = x.shape
    tm = _tile(n, 512)
    wq, wk, wv = wts
    ck, sk, cq, sq = tabs
    dw = wq.shape[0]
    full = lambda a: pl.BlockSpec(a.shape, lambda bi, ti: (0,) * a.ndim)
    q_scale = DIFF_HEAD_DIM ** -0.5 * LOG2E
    return pl.pallas_call(
        functools.partial(_proj1_kernel, q_scale=q_scale),
        out_shape=(jax.ShapeDtypeStruct((b, dw, n), BF16),
                   jax.ShapeDtypeStruct((b, n, dw), BF16),
                   jax.ShapeDtypeStruct((b, dw, n), BF16)),
        grid=(b, n // tm),
        in_specs=[pl.BlockSpec((1, tm, d), lambda bi, ti: (bi, ti, 0)),
                  pl.BlockSpec((1, 1, d), lambda bi, ti: (bi, 0, 0)),
                  pl.BlockSpec((1, 1, d), lambda bi, ti: (bi, 0, 0)),
                  full(wq), full(wk), full(wv),
                  pl.BlockSpec((tm, HEAD_SLOT), lambda bi, ti: (ti, 0)),
                  pl.BlockSpec((tm, HEAD_SLOT), lambda bi, ti: (ti, 0)),
                  pl.BlockSpec((DIFF_HEAD_DIM, tm), lambda bi, ti: (0, ti)),
                  pl.BlockSpec((DIFF_HEAD_DIM, tm), lambda bi, ti: (0, ti))],
        out_specs=(pl.BlockSpec((1, dw, tm), lambda bi, ti: (bi, 0, ti)),
                   pl.BlockSpec((1, tm, dw), lambda bi, ti: (bi, ti, 0)),
                   pl.BlockSpec((1, dw, tm), lambda bi, ti: (bi, 0, ti))),
        compiler_params=_cparams(("parallel", "parallel")),
        name="proj1",
    )(x, sc, sh, wq, wk, wv, ck, sk, cq, sq)


SHIFT_OK_LO = 2.0 ** -85
SHIFT_OK_HI = 2.0 ** 100
ATTN_TQ = 1024
ATTN_UNROLL = 8


def _key_norm_max(k_ref, kmax_ref, *, tk, n_sub):
    nchunk = k_ref.shape[1] // tk
    sub_w = HEAD_SLOT // n_sub
    row = lax.broadcasted_iota(jnp.int32, (8, HEAD_SLOT), 0)
    lane = lax.broadcasted_iota(jnp.int32, (8, HEAD_SLOT), 1)
    sel = jnp.where(lane // sub_w == row, 1.0, 0.0).astype(BF16)

    group = max(g for g in range(1, 12) if nchunk % g == 0)
    rows = group * tk

    def body(c, mx):
        off = pl.multiple_of(c * rows, rows)
        kc = k_ref[0, pl.ds(off, rows), :].astype(F32)
        return jnp.maximum(mx, _dot_nt(sel, (kc * kc).astype(BF16)))

    mx = lax.fori_loop(0, nchunk // group, body, jnp.zeros((8, rows), F32))
    kmax_ref[...] = jnp.broadcast_to(jnp.max(mx, axis=1, keepdims=True), kmax_ref.shape)


def _sweep_bounded(qs, shifts, k_ref, v_ref, s_ref, *, tk, unroll):
    tq = qs[0].shape[1]
    dv = v_ref.shape[1]
    nchunk = k_ref.shape[1] // tk

    def aligned(off):
        return off if isinstance(off, int) else pl.multiple_of(off, tk)

    def scores(off, buf):
        kc = k_ref[0, pl.ds(aligned(off), tk), :]
        for r, q in enumerate(qs):
            s_ref[buf, r] = _dot(kc, q)

    def consume(off, buf, carry):
        vc = v_ref[0, :, pl.ds(aligned(off), tk)]
        nxt = []
        for r, ((l8, acc), m) in enumerate(zip(carry, shifts)):
            p = jnp.exp2(s_ref[buf, r] - m)
            l8 = l8 + jnp.sum(p.reshape(tk // 8, 8, tq), axis=0)
            nxt.append((l8, acc + _dot(vc, p.astype(BF16))))
        return tuple(nxt)

    def body(i, carry):
        for u in range(unroll):
            off = pl.multiple_of((i * unroll + u) * tk, tk)
            scores(off + tk, (u + 1) % 2)
            carry = consume(off, u % 2, carry)
        return carry

    carry = tuple((jnp.zeros((8, tq), F32), jnp.zeros((dv, tq), F32)) for _ in qs)
    scores(0, 0)
    n_loop = (nchunk - 1) // unroll * unroll
    if n_loop:
        carry = lax.fori_loop(0, n_loop // unroll, body, carry)
    for c in range(n_loop, nchunk):
        if c + 1 < nchunk:
            scores((c + 1) * tk, (c + 1) % 2)
        carry = consume(c * tk, c % 2, carry)
    return [(jnp.sum(l8, axis=0, keepdims=True), acc) for l8, acc in carry]


def _sweep_running_max(qs, k_ref, v_ref, *, tk):
    tq = qs[0].shape[1]
    dv = v_ref.shape[1]
    nchunk = k_ref.shape[1] // tk

    def body(c, carry):
        off = pl.multiple_of(c * tk, tk)
        kc = k_ref[0, pl.ds(off, tk), :]
        vc = v_ref[0, :, pl.ds(off, tk)]
        nxt = []
        for (m, l, acc), q in zip(carry, qs):
            s = _dot(kc, q)
            mn = jnp.maximum(m, jnp.max(s, axis=0, keepdims=True))
            alpha = jnp.exp2(m - mn)
            p = jnp.exp2(s - mn)
            nxt.append((mn, alpha * l + jnp.sum(p, axis=0, keepdims=True), alpha * acc + _dot(vc, p.astype(BF16))))
        return tuple(nxt)

    init = tuple((jnp.full((1, tq), NEG_BIG, F32), jnp.zeros((1, tq), F32), jnp.zeros((dv, tq), F32)) for _ in qs)
    res = lax.fori_loop(0, nchunk, body, init)
    return [(l, acc) for _, l, acc in res]


def _attn_kernel(*refs, tk, unroll, n_sub, finalize):
    q_ref, k_ref, v_ref = refs[:3]
    extra = refs[3:-3]
    o_ref, kmax_ref, s_ref = refs[-3:]

    @pl.when(pl.program_id(2) == 0)
    def _():
        _key_norm_max(k_ref, kmax_ref, tk=tk, n_sub=n_sub)

    q = q_ref[0]
    sub_w = HEAD_SLOT // n_sub
    row = lax.broadcasted_iota(jnp.int32, q.shape, 0)
    qf = q.astype(F32)
    qs, shifts = [], []
    for r in range(n_sub):
        mine = (row // sub_w) == r
        qs.append(jnp.where(mine, q, jnp.zeros_like(q)) if n_sub > 1 else q)
        qn2 = jnp.sum(jnp.where(mine, qf * qf, 0.0), axis=0, keepdims=True)
        shifts.append(jnp.sqrt(qn2 * kmax_ref[r:r + 1, 0:1]))

    res = _sweep_bounded(qs, shifts, k_ref, v_ref, s_ref, tk=tk, unroll=unroll)
    o_ref[0] = finalize(res, extra).astype(BF16)
    ok = None
    for l, _ in res:
        ok_r = (jnp.min(l) >= SHIFT_OK_LO) & (jnp.max(l) <= SHIFT_OK_HI)
        ok = ok_r if ok is None else ok & ok_r

    @pl.when(jnp.logical_not(ok))
    def _():
        o_ref[0] = finalize(_sweep_running_max(qs, k_ref, v_ref, tk=tk), extra).astype(BF16)


def _finalize_plain(res, extra):
    (l, acc), = res
    return acc / l


def _finalize_diff(res, extra, *, lam_init):
    lq1_ref, lk1_ref, lq2_ref, lk2_ref, g_ref = extra
    (l1, a1), (l2, a2) = res
    lam = (jnp.exp(jnp.sum(lq1_ref[...] * lk1_ref[...], axis=-1, keepdims=True))
           - jnp.exp(jnp.sum(lq2_ref[...] * lk2_ref[...], axis=-1, keepdims=True)) + lam_init)
    o = a1 / l1 - lam * (a2 / l2)
    return o * lax.rsqrt(jnp.mean(o * o, axis=0, keepdims=True) + RMS_EPS) * g_ref[...] * (1.0 - lam_init)


def _attention(qt, k, vt, extra=(), *, dv, n_sub, finalize, name):
    b, hw, n = qt.shape
    t = k.shape[1]
    heads = hw // HEAD_SLOT
    tq = _tile(n, ATTN_TQ)
    tk = 256 if t % 256 == 0 else 128
    nchunk = t // tk
    unroll = ATTN_UNROLL
    full = lambda a: pl.BlockSpec(a.shape, lambda bi, hi, qi: (0,) * a.ndim)
    return pl.pallas_call(
        functools.partial(_attn_kernel, tk=tk, unroll=unroll, n_sub=n_sub, finalize=finalize),
        out_shape=jax.ShapeDtypeStruct((b, heads * dv, n), BF16),
        grid=(b, heads, n // tq),
        in_specs=[pl.BlockSpec((1, HEAD_SLOT, tq), lambda bi, hi, qi: (bi, hi, qi)),
                  pl.BlockSpec((1, t, HEAD_SLOT), lambda bi, hi, qi: (bi, 0, hi)),
                  pl.BlockSpec((1, dv, t), lambda bi, hi, qi: (bi, hi, 0))] + [full(a) for a in extra],
        out_specs=pl.BlockSpec((1, dv, tq), lambda bi, hi, qi: (bi, hi, qi)),
        scratch_shapes=[pltpu.VMEM((8, HEAD_SLOT), F32), pltpu.VMEM((2, n_sub, tk, tq), F32)],
        compiler_params=_cparams(("parallel", "parallel", "arbitrary")),
        name=name,
    )(qt, k, vt, *extra)


def _out_kernel(*refs, n_rowmajor):
    (x_ref, g_ref, lng_ref, lnb_ref, scf_ref, shf_ref, rt_ref), rest = refs[:7], refs[7:]
    x1_ref, aff_ref, hp_ref = rest[-3:]
    ops = rest[:-3]
    m = None
    for i in range(len(ops) // 2):
        a = ops[2 * i][0]
        w = ops[2 * i + 1][...]
        part = _dot(a, w) if i < n_rowmajor else _dot_tn(a, w)
        m = part if m is None else m + part
    x1 = _ln(DEEPNORM_ALPHA * x_ref[0] + (1.0 + g_ref[0]) * m, lng_ref[...], lnb_ref[...])
    x1_ref[0] = x1
    hf = (x1 * (1.0 + scf_ref[0]) + shf_ref[0]).astype(BF16)
    lg = _dot_nt(rt_ref[...], hf)
    e = jnp.exp(lg - jnp.max(lg, axis=0, keepdims=True))
    aff_ref[0] = e / jnp.sum(e, axis=0, keepdims=True)
    dh = hf.shape[1] // 2
    lo = pltpu.bitcast(hf[:, :dh].astype(F32), jnp.uint32)
    hi = pltpu.bitcast(hf[:, dh:].astype(F32), jnp.uint32)
    hp_ref[0] = _to_tiled(lax.shift_right_logical(lo, jnp.uint32(16)) | (hi & jnp.uint32(0xFFFF0000)))


def _mixer_out(x, g, lng, lnb, scf, shf, router_t, rowmajor_ops, chanmajor_ops):
    b, n, d = x.shape
    e = router_t.shape[0]
    tm = _tile(n, 512)
    vec = pl.BlockSpec((1, 1, d), lambda bi, ti: (bi, 0, 0))
    full = lambda a: pl.BlockSpec(a.shape, lambda bi, ti: (0,) * a.ndim)
    args = [x, g, lng, lnb, scf, shf, router_t]
    specs = [pl.BlockSpec((1, tm, d), lambda bi, ti: (bi, ti, 0)), vec, full(lng), full(lnb), vec, vec,
             full(router_t)]
    for a, w in rowmajor_ops:
        args += [a, w]
        specs += [pl.BlockSpec((1, tm, a.shape[2]), lambda bi, ti: (bi, ti, 0)), full(w)]
    for a, w in chanmajor_ops:
        args += [a, w]
        specs += [pl.BlockSpec((1, a.shape[1], tm), lambda bi, ti: (bi, 0, ti)), full(w)]
    return pl.pallas_call(
        functools.partial(_out_kernel, n_rowmajor=len(rowmajor_ops)),
        out_shape=(jax.ShapeDtypeStruct((b, n, d), F32), jax.ShapeDtypeStruct((b, e, n), F32),
                   jax.ShapeDtypeStruct((b, n * (d // 2) // LANES, LANES), jnp.uint32)),
        grid=(b, n // tm),
        in_specs=specs,
        out_specs=(pl.BlockSpec((1, tm, d), lambda bi, ti: (bi, ti, 0)),
                   pl.BlockSpec((1, e, tm), lambda bi, ti: (bi, 0, ti)),
                   pl.BlockSpec((1, tm * (d // 2) // LANES, LANES), lambda bi, ti: (bi, ti, 0))),
        compiler_params=_cparams(("parallel", "parallel")),
        name="mixer_out",
    )(*args)


ROW_UNROLL = 8
FFN_ROWS = 512
TOKEN_SPLIT = 4096


def _ffn_kernel(src_ref, hp_ref, g_ref, wg_ref, wu_ref, wd_ref, y_ref, rows_ref):
    cap = y_ref.shape[2]
    nt = rows_ref.shape[0] // cap
    dh = nt * LANES

    def body(jb, carry):
        for u in range(ROW_UNROLL):
            rows_ref[pl.ds(jb * (ROW_UNROLL * nt) + u, nt, stride=SUBLANES), :] = (
                hp_ref[0, pl.ds(src_ref[0, 0, jb * ROW_UNROLL + u], nt, stride=SUBLANES), :])
        return carry

    lax.fori_loop(0, cap // ROW_UNROLL, body, 0)
    blk = min(cap, FFN_ROWS)
    for c0 in range(0, cap, blk):
        w = _from_tiled(rows_ref[c0 * nt:(c0 + blk) * nt], dh)
        lo = pltpu.bitcast(lax.shift_left(w, jnp.uint32(16)), F32)
        hi = pltpu.bitcast(w & jnp.uint32(0xFFFF0000), F32)
        xs = jnp.concatenate([lo, hi], axis=1).astype(BF16)
        gate = _dot(xs, wg_ref[0])
        hid = gate / (1.0 + jnp.exp(-gate)) * _dot(xs, wu_ref[0])
        y_ref[0, 0, c0:c0 + blk] = (_dot(hid.astype(BF16), wd_ref[0]) * g_ref[0, c0:c0 + blk]).astype(BF16)


def _moe_ffn(hp, src3, gates3, wg, wu, wd, cap):
    b, hp_rows, _ = hp.shape
    n_exp, d, ff = wg.shape
    nt = d // 2 // LANES
    return pl.pallas_call(
        _ffn_kernel,
        out_shape=jax.ShapeDtypeStruct((b, n_exp, cap, d), BF16),
        grid=(b, n_exp),
        in_specs=[pl.BlockSpec((1, 1, cap), lambda bi, ei: (bi * n_exp + ei, 0, 0), memory_space=pltpu.SMEM),
                  pl.BlockSpec((1, hp_rows, LANES), lambda bi, ei: (bi, 0, 0), pipeline_mode=pl.Buffered(1)),
                  pl.BlockSpec((1, cap, 1), lambda bi, ei: (bi * n_exp + ei, 0, 0)),
                  pl.BlockSpec((1, d, ff), lambda bi, ei: (ei, 0, 0)),
                  pl.BlockSpec((1, d, ff), lambda bi, ei: (ei, 0, 0)),
                  pl.BlockSpec((1, ff, d), lambda bi, ei: (ei, 0, 0))],
        out_specs=pl.BlockSpec((1, 1, cap, d), lambda bi, ei: (bi, ei, 0, 0)),
        scratch_shapes=[pltpu.VMEM((cap * nt, LANES), jnp.uint32)],
        compiler_params=_cparams(("parallel", "arbitrary")),
        name="moe_ffn",
    )(src3, hp, gates3, wg, wu, wd)


def _scatter_kernel(dst_ref, bnd_ref, y_ref, f_ref, rows_ref):
    d = y_ref.shape[3]
    nt = d // LANES

    @pl.when(pl.program_id(2) == 0)
    def _():
        f_ref[...] = jnp.zeros_like(f_ref)

    rows_ref[...] = _to_tiled(y_ref[0, 0].astype(F32))
    lo = bnd_ref[0, 0, pl.program_id(1)]
    hi = bnd_ref[0, 0, pl.program_id(1) + 1]

    def row(ref, start):
        return ref[pl.ds(start, nt, stride=SUBLANES), :]

    def one(j, carry):
        dst = dst_ref[0, 0, j]
        f_ref[0, pl.ds(dst, nt, stride=SUBLANES), :] = row(f_ref.at[0], dst) + row(rows_ref, _tiled_row(j, d))
        return carry

    def batch(jb, carry):
        dsts = [dst_ref[0, 0, jb * ROW_UNROLL + u] for u in range(ROW_UNROLL)]
        sums = [row(f_ref.at[0], dsts[u]) + row(rows_ref, jb * (ROW_UNROLL * nt) + u) for u in range(ROW_UNROLL)]
        for u in range(ROW_UNROLL):
            f_ref[0, pl.ds(dsts[u], nt, stride=SUBLANES), :] = sums[u]
        return carry

    head_end = jnp.minimum((lo + ROW_UNROLL - 1) // ROW_UNROLL * ROW_UNROLL, hi)
    tail_start = jnp.maximum(head_end, hi // ROW_UNROLL * ROW_UNROLL)
    lax.fori_loop(lo, head_end, one, 0)
    lax.fori_loop(head_end // ROW_UNROLL, tail_start // ROW_UNROLL, batch, 0)
    lax.fori_loop(tail_start, hi, one, 0)


def _moe_scatter(y, dst3, bounds3, n):
    b, n_exp, cap, d = y.shape
    nt = _tile(n, TOKEN_SPLIT)
    lt = d // LANES
    return pl.pallas_call(
        _scatter_kernel,
        out_shape=jax.ShapeDtypeStruct((b, n * lt, LANES), F32),
        grid=(b, n // nt, n_exp),
        in_specs=[pl.BlockSpec((1, 1, cap), lambda bi, ki, ei: (bi * n_exp + ei, 0, 0), memory_space=pltpu.SMEM),
                  pl.BlockSpec((1, 1, n // nt + 1), lambda bi, ki, ei: (bi * n_exp + ei, 0, 0),
                               memory_space=pltpu.SMEM),
                  pl.BlockSpec((1, 1, cap, d), lambda bi, ki, ei: (bi, ei, 0, 0))],
        out_specs=pl.BlockSpec((1, nt * lt, LANES), lambda bi, ki, ei: (bi, ki, 0)),
        scratch_shapes=[pltpu.VMEM((cap * lt, LANES), F32)],
        compiler_params=_cparams(("parallel", "parallel", "arbitrary")),
        name="moe_scatter",
    )(dst3, bounds3, y)


def _post_kernel(x_ref, f_ref, g_ref, lng_ref, lnb_ref, o_ref):
    f = _from_tiled(f_ref[0], x_ref.shape[2])
    o_ref[0] = _ln(DEEPNORM_ALPHA * x_ref[0] + (1.0 + g_ref[0]) * f, lng_ref[...], lnb_ref[...])


def _post(x1, f_tiled, g, lng, lnb):
    b, n, d = x1.shape
    tm = _tile(n, 512)
    blk = pl.BlockSpec((1, tm, d), lambda bi, ti: (bi, ti, 0))
    f_blk = pl.BlockSpec((1, tm * d // LANES, LANES), lambda bi, ti: (bi, ti, 0))
    full = lambda a: pl.BlockSpec(a.shape, lambda bi, ti: (0,) * a.ndim)
    return pl.pallas_call(
        _post_kernel,
        out_shape=jax.ShapeDtypeStruct((b, n, d), F32),
        grid=(b, n // tm),
        in_specs=[blk, f_blk, pl.BlockSpec((1, 1, d), lambda bi, ti: (bi, 0, 0)), full(lng), full(lnb)],
        out_specs=blk,
        compiler_params=_cparams(("parallel", "parallel")),
        name="ffn_post",
    )(x1, f_tiled, g, lng, lnb)


def _select_kernel(aff_ref, idx_ref, gate_ref, *, cap, n_exp):
    a = aff_ref[0]
    er, lanes = a.shape
    r_n = er // n_exp
    e_pad = 16
    bits = pltpu.bitcast(a, jnp.int32)

    member = (lax.broadcasted_iota(jnp.int32, (e_pad, er), 1) // r_n
              == lax.broadcasted_iota(jnp.int32, (e_pad, er), 0))
    gs = jnp.where(member, 1.0, 0.0).astype(BF16)
    member_t = (lax.broadcasted_iota(jnp.int32, (er, e_pad), 0) // r_n
                == lax.broadcasted_iota(jnp.int32, (er, e_pad), 1))
    gst = jnp.where(member_t, 1.0, 0.0).astype(BF16)
    ri = lax.broadcasted_iota(jnp.int32, (er, er), 0)
    rj = lax.broadcasted_iota(jnp.int32, (er, er), 1)
    rows_before = jnp.where((ri // r_n == rj // r_n) & (rj < ri), 1.0, 0.0).astype(BF16)
    li = lax.broadcasted_iota(jnp.int32, (lanes, lanes), 0)
    lj = lax.broadcasted_iota(jnp.int32, (lanes, lanes), 1)
    lanes_upto = jnp.where(li <= lj, 1.0, 0.0).astype(BF16)

    def lane_total(m):
        return jnp.broadcast_to(jnp.sum(m, axis=1, keepdims=True), m.shape)

    def expert_total(m):
        return _dot(gs, lane_total(m).astype(BF16))

    def to_rows(ev):
        hi = jnp.floor(ev * (1.0 / 128.0))
        return 128.0 * _dot(gst, hi.astype(BF16)) + _dot(gst, (ev - 128.0 * hi).astype(BF16))

    def search(i, t):
        cand = t | lax.shift_left(jnp.int32(1), 30 - i)
        cnt = expert_total(jnp.where(bits >= cand, 1.0, 0.0))
        return jnp.where(to_rows(jnp.where(cnt >= cap, 1.0, 0.0)) > 0.5, cand, t)

    t = lax.fori_loop(0, 31, search, jnp.zeros((er, lanes), jnp.int32))
    gt = jnp.where(bits > t, 1.0, 0.0)
    eq = jnp.where(bits == t, 1.0, 0.0)

    def prefix(m):
        rc = _dot(m.astype(BF16), lanes_upto)
        off = _dot(rows_before, jnp.broadcast_to(rc[:, lanes - 1:lanes], m.shape).astype(BF16))
        return rc, off

    rc_eq, off_eq = prefix(eq)
    need = cap - to_rows(expert_total(gt))
    sel = jnp.maximum(gt, eq * jnp.where(off_eq + rc_eq - eq < need, 1.0, 0.0))
    rc, off = prefix(sel)
    row_end = off + jnp.broadcast_to(rc[:, lanes - 1:lanes], rc.shape)

    slot = lax.broadcasted_iota(jnp.int32, (cap, lanes), 0).astype(F32) + 1.0
    lane_c = lax.broadcasted_iota(jnp.int32, (cap, lanes), 1)
    lane_cf = lane_c.astype(F32)
    diag = lax.broadcasted_iota(jnp.int32, (r_n, lanes), 0) == lax.broadcasted_iota(jnp.int32, (r_n, lanes), 1)
    lane_r = lax.broadcasted_iota(jnp.int32, (1, lanes), 1)
    pad = jnp.zeros((lanes - r_n, lanes), F32)

    def table(m):
        return (jnp.concatenate([m, pad], axis=0) if r_n < lanes else m).astype(BF16)

    idx_all = jnp.zeros((cap, lanes), jnp.int32)
    gate_all = jnp.zeros((cap, lanes), F32)
    for e in range(n_exp):
        r0 = e * r_n
        rc_e = rc[r0:r0 + r_n]
        end_lane = jnp.sum(jnp.where(diag, row_end[r0:r0 + r_n], 0.0), axis=0, keepdims=True)
        end_lane = jnp.where(lane_r < r_n, end_lane, 3.0e38)
        off_lane = jnp.sum(jnp.where(diag, off[r0:r0 + r_n], 0.0), axis=0, keepdims=True)
        row_j = jnp.sum(jnp.where(end_lane < slot, 1.0, 0.0), axis=1, keepdims=True)
        onehot = jnp.where(lane_cf == row_j, 1.0, 0.0)
        off_j = jnp.sum(onehot * off_lane, axis=1, keepdims=True)
        oh = onehot.astype(BF16)
        cum_j = _dot(oh, table(rc_e)) + off_j
        lane_j = jnp.sum(jnp.where(cum_j < slot, 1.0, 0.0), axis=1, keepdims=True)
        a_e = a[r0:r0 + r_n]
        a1 = a_e.astype(BF16).astype(F32)
        a2 = (a_e - a1).astype(BF16).astype(F32)
        a3 = a_e - a1 - a2
        aff_j = _dot(oh, table(a1)) + _dot(oh, table(a2)) + _dot(oh, table(a3))
        gate_j = jnp.sum(jnp.where(lane_cf == lane_j, aff_j, 0.0), axis=1, keepdims=True)
        idx_j = (row_j * float(lanes) + lane_j).astype(jnp.int32)
        idx_all = jnp.where(lane_c == e, idx_j, idx_all)
        gate_all = jnp.where(lane_c == e, gate_j, gate_all)
    idx_ref[0] = idx_all
    gate_ref[0] = gate_all


def _moe_select(aff_t, cap):
    b, n_exp, n = aff_t.shape
    lanes = HEAD_SLOT
    er = n_exp * (n // lanes)
    idx, gates = pl.pallas_call(
        functools.partial(_select_kernel, cap=cap, n_exp=n_exp),
        out_shape=(jax.ShapeDtypeStruct((b, cap, lanes), jnp.int32), jax.ShapeDtypeStruct((b, cap, lanes), F32)),
        grid=(b,),
        in_specs=[pl.BlockSpec((1, er, lanes), lambda bi: (bi, 0, 0))],
        out_specs=(pl.BlockSpec((1, cap, lanes), lambda bi: (bi, 0, 0)),
                   pl.BlockSpec((1, cap, lanes), lambda bi: (bi, 0, 0))),
        compiler_params=_cparams(("parallel",)),
        name="moe_select",
    )(aff_t.reshape(b, er, lanes))
    to_expert_major = lambda m: jnp.swapaxes(m[:, :, :n_exp], 1, 2)
    return to_expert_major(gates), to_expert_major(idx)


def _moe(x1, aff_t, hp, gf, wg, wu, wd, lng, lnb):
    b, n, d = x1.shape
    n_exp = aff_t.shape[1]
    cap = CAPACITY_FACTOR * n // n_exp
    gates, idx = _moe_select(aff_t, cap)
    gates3 = gates.reshape(b * n_exp, cap, 1)
    nt = _tile(n, TOKEN_SPLIT)
    edges = jnp.arange(0, n + 1, nt, dtype=jnp.int32)
    bounds3 = jnp.sum(idx[..., None] < edges, axis=2, dtype=jnp.int32).reshape(b * n_exp, 1, n // nt + 1)
    src3 = _tiled_row(idx, d // 2).reshape(b * n_exp, 1, cap)
    dst3 = _tiled_row(idx % nt, d).reshape(b * n_exp, 1, cap)
    y = _moe_ffn(hp, src3, gates3, wg, wu, wd, cap)
    f_tiled = _moe_scatter(y, dst3, bounds3, n)
    return _post(x1, f_tiled, gf, lng, lnb)


def _rope_angles(n_tokens, dim):
    n_rows = n_tokens // GRID_W
    row = jnp.repeat(jnp.arange(n_rows, dtype=F32), GRID_W)
    col = jnp.tile(jnp.arange(GRID_W, dtype=F32), n_rows)
    n_freq = dim // 4
    inv_freq = ROPE_BASE ** (-jnp.arange(n_freq, dtype=F32) / n_freq)
    ang = jnp.concatenate([row[:, None] * inv_freq, col[:, None] * inv_freq], -1)
    return jnp.cos(ang), jnp.sin(ang)


def _rope_tables(n_tokens, dim, lane_offset, identity):
    half = dim // 2
    if identity:
        cos = jnp.ones((n_tokens, half), F32)
        sin = jnp.zeros((n_tokens, half), F32)
    else:
        cos, sin = _rope_angles(n_tokens, dim)
    c2 = jnp.concatenate([cos, cos], -1)
    s2 = jnp.concatenate([-sin, sin], -1)
    reps = (HEAD_SLOT - lane_offset) // dim if lane_offset == 0 else 1
    ck = jnp.zeros((n_tokens, HEAD_SLOT), F32)
    sk = jnp.zeros((n_tokens, HEAD_SLOT), F32)
    for r in range(reps):
        ck = ck.at[:, lane_offset + r * dim:lane_offset + (r + 1) * dim].set(c2)
        sk = sk.at[:, lane_offset + r * dim:lane_offset + (r + 1) * dim].set(s2)
    return ck, sk, c2.T, s2.T


def _deinterleave(w, axis):
    return jnp.concatenate([jnp.take(w, jnp.arange(0, w.shape[axis], 2), axis=axis),
                            jnp.take(w, jnp.arange(1, w.shape[axis], 2), axis=axis)], axis=axis)


def _prep_layer0(w_in, ln_g, ln_b, ws, bs, q_norm, w_uq, kv_norm, w_ukv):
    d = w_in.shape[0]
    w = GMLP_WIDTH
    o_kr = 2 * w + MLA_Q_RANK + MLA_KV_RANK
    kr = _deinterleave(w_in[:, o_kr:o_kr + MLA_ROPE], 1)
    kr_sw = jnp.concatenate([kr[:, MLA_ROPE // 2:], kr[:, :MLA_ROPE // 2]], 1)
    slot = lambda m: jnp.zeros((d, HEAD_SLOT), F32).at[:, MLA_NOPE:MLA_NOPE + MLA_ROPE].set(m)
    win = jnp.concatenate([w_in[:, :o_kr], slot(kr), slot(kr_sw)], 1).astype(BF16)

    qd = MLA_NOPE + MLA_ROPE
    wq = w_uq.reshape(MLA_Q_RANK, MLA_HEADS, qd)
    wq = jnp.concatenate([wq[..., :MLA_NOPE], _deinterleave(wq[..., MLA_NOPE:], 2),
                          jnp.zeros((MLA_Q_RANK, MLA_HEADS, HEAD_SLOT - qd), F32)], -1)
    wq_t = wq.reshape(MLA_Q_RANK, MLA_HEADS * HEAD_SLOT).T.astype(BF16)

    wkv = w_ukv.reshape(MLA_KV_RANK, MLA_HEADS, MLA_NOPE + MLA_V)
    wk = jnp.concatenate([wkv[..., :MLA_NOPE], jnp.zeros((MLA_KV_RANK, MLA_HEADS, HEAD_SLOT - MLA_NOPE), F32)], -1)
    wk = wk.reshape(MLA_KV_RANK, MLA_HEADS * HEAD_SLOT).astype(BF16)
    wv_t = wkv[..., MLA_NOPE:].reshape(MLA_KV_RANK, MLA_HEADS * MLA_V).T.astype(BF16)

    bsb = jnp.broadcast_to(bs[:, :, None], (GMLP_GROUPS, CHUNK, GMLP_GROUP_CH))
    return (win, ln_g.reshape(1, -1), ln_b.reshape(1, -1), ws.astype(BF16), bsb,
            q_norm.reshape(1, -1), wq_t, kv_norm.reshape(1, -1), wk, wv_t)


def _prep_layer1(w_in):
    d = w_in.shape[0]
    dw = 2 * DIFF_HEADS * DIFF_HEAD_DIM
    perm = lambda m: _deinterleave(m.reshape(d, 2 * DIFF_HEADS, DIFF_HEAD_DIM), 2).reshape(d, dw)
    wq_t = perm(w_in[:, :dw]).T.astype(BF16)
    wk = perm(w_in[:, dw:2 * dw]).astype(BF16)
    wv_t = w_in[:, 2 * dw:].T.astype(BF16)
    return wq_t, wk, wv_t


def _split6(m_row):
    return [v[:, None, :] for v in jnp.split(m_row, 6, axis=-1)]


def kernel(x, c, ctx, c_ctx, w_mod_0, b_mod_0, w_in_0, gmlp_ln_g_0, gmlp_ln_b_0, gmlp_ws_0, gmlp_bs_0, mla_q_norm_0, mla_w_uq_0, mla_kv_norm_0, mla_w_ukv_0, w_out_0, ln_mix_g_0, ln_mix_b_0, router_0, w_gate_0, w_up_0, w_down_0, ln_ffn_g_0, ln_ffn_b_0, w_mod_1, b_mod_1, w_in_1, lambda_q1_1, lambda_k1_1, lambda_q2_1, lambda_k2_1, subln_g_1, w_out_1, ln_mix_g_1, ln_mix_b_1, router_1, w_gate_1, w_up_1, w_down_1, ln_ffn_g_1, ln_ffn_b_1):
    b, n, d = x.shape
    m_ctx = ctx.shape[1]
    row = lambda v: v.reshape(1, -1)

    cond = jnp.concatenate([c, c_ctx[None, :], jnp.zeros((-(b + 1) % 8, d), F32)], 0)
    mod0 = _modulation(cond, w_mod_0, b_mod_0)
    mod1 = _modulation(cond, w_mod_1, b_mod_1)
    sh_a0, sc_a0, g_a0, sh_f0, sc_f0, g_f0 = _split6(mod0[:b])
    csh_a0, csc_a0, cg_a0, csh_f0, csc_f0, cg_f0 = [jnp.broadcast_to(v, (b, 1, d)) for v in _split6(mod0[b:b + 1])]
    sh_a1, sc_a1, g_a1, sh_f1, sc_f1, g_f1 = _split6(mod1[:b])
    csh_a1, csc_a1 = [jnp.broadcast_to(v, (b, 1, d)) for v in _split6(mod1[b:b + 1])[:2]]

    wts0 = _prep_layer0(w_in_0, gmlp_ln_g_0, gmlp_ln_b_0, gmlp_ws_0, gmlp_bs_0,
                        mla_q_norm_0, mla_w_uq_0, mla_kv_norm_0, mla_w_ukv_0)
    tabs_l = _rope_tables(n, MLA_ROPE, MLA_NOPE, identity=False)
    tabs_c = _rope_tables(m_ctx, MLA_ROPE, MLA_NOPE, identity=True)
    a_l, q_l, k_l, v_l = _proj0(x, sc_a0, sh_a0, wts0, tabs_l)
    a_c, q_c, k_c, v_c = _proj0(ctx, csc_a0, csh_a0, wts0, tabs_c)
    o_l = _attention(q_l, jnp.concatenate([k_l, k_c], 1), jnp.concatenate([v_l, v_c], 2),
                     dv=MLA_V, n_sub=1, finalize=_finalize_plain, name="mla_attn")
    o_c = _attention(q_c, k_c, v_c, dv=MLA_V, n_sub=1, finalize=_finalize_plain, name="mla_attn_ctx")

    w_out_a = w_out_0[:GMLP_WIDTH].astype(BF16)
    w_out_o = w_out_0[GMLP_WIDTH:].astype(BF16)
    router_t0 = router_0.T.astype(BF16)
    wg0, wu0, wd0 = w_gate_0.astype(BF16), w_up_0.astype(BF16), w_down_0.astype(BF16)
    lng, lnb = row(ln_mix_g_0), row(ln_mix_b_0)
    x1, aff, hp = _mixer_out(x, g_a0, lng, lnb, sc_f0, sh_f0, router_t0, [(a_l, w_out_a)], [(o_l, w_out_o)])
    x_lat = _moe(x1, aff, hp, g_f0, wg0, wu0, wd0, row(ln_ffn_g_0), row(ln_ffn_b_0))
    c1, caff, chp = _mixer_out(ctx, cg_a0, lng, lnb, csc_f0, csh_f0, router_t0, [(a_c, w_out_a)], [(o_c, w_out_o)])
    x_ctx = _moe(c1, caff, chp, cg_f0, wg0, wu0, wd0, row(ln_ffn_g_0), row(ln_ffn_b_0))

    lam_init = 0.8 - 0.6 * math.exp(-0.3 * 1)
    wts1 = _prep_layer1(w_in_1)
    tabs_l = _rope_tables(n, DIFF_HEAD_DIM, 0, identity=False)
    tabs_c = _rope_tables(m_ctx, DIFF_HEAD_DIM, 0, identity=True)
    q_l, k_l, v_l = _proj1(x_lat, sc_a1, sh_a1, wts1, tabs_l)
    _, k_c, v_c = _proj1(x_ctx, csc_a1, csh_a1, wts1, tabs_c)
    extra = (row(lambda_q1_1), row(lambda_k1_1), row(lambda_q2_1), row(lambda_k2_1), subln_g_1.reshape(-1, 1))
    o_l = _attention(q_l, jnp.concatenate([k_l, k_c], 1), jnp.concatenate([v_l, v_c], 2), extra,
                     dv=2 * DIFF_HEAD_DIM, n_sub=2, finalize=functools.partial(_finalize_diff, lam_init=lam_init),
                     name="diff_attn")
    x1, aff, hp = _mixer_out(x_lat, g_a1, row(ln_mix_g_1), row(ln_mix_b_1), sc_f1, sh_f1, router_1.T.astype(BF16),
                             [], [(o_l, w_out_1.astype(BF16))])
    return _moe(x1, aff, hp, g_f1, w_gate_1.astype(BF16), w_up_1.astype(BF16), w_down_1.astype(BF16),
                row(ln_ffn_g_1), row(ln_ffn_b_1))
```

```python
import functools
import math

import jax
import jax.numpy as jnp
from jax import lax
from jax.experimental import pallas as pl
from jax.experimental.pallas import tpu as pltpu

F32 = jnp.float32
BF16 = jnp.bfloat16

DEPTH = 2
GRID_W = 64
ROPE_BASE = 10000.0
CHUNK = 128
GMLP_GROUPS = 4
GMLP_GROUP_CH = 128
GMLP_WIDTH = GMLP_GROUPS * GMLP_GROUP_CH
MLA_HEADS = 8
MLA_NOPE = 64
MLA_ROPE = 32
MLA_V = 64
MLA_Q_RANK = 256
MLA_KV_RANK = 128
DIFF_HEADS = 8
DIFF_HEAD_DIM = 64
CAPACITY_FACTOR = 2
DEEPNORM_ALPHA = (2 * DEPTH) ** 0.25
LN_EPS = 1e-5
RMS_EPS = 1e-6
LOG2E = 1.4426950408889634

LANES = 128
SUBLANES = 8
HEAD_SLOT = LANES
NEG_BIG = -1e30
VMEM_LIMIT = 56 * 1024 * 1024


def _cparams(sem):
    return pltpu.CompilerParams(dimension_semantics=sem, vmem_limit_bytes=VMEM_LIMIT)


def _tile(n, pref):
    return pref if n % pref == 0 else n


def _ln(x, g, b):
    mu = jnp.mean(x, axis=-1, keepdims=True)
    xc = x - mu
    var = jnp.mean(xc * xc, axis=-1, keepdims=True)
    return xc * lax.rsqrt(var + LN_EPS) * g + b


def _rms(x, g):
    return x * lax.rsqrt(jnp.mean(x * x, axis=-1, keepdims=True) + RMS_EPS) * g


def _gelu(x):
    return 0.5 * x * (1.0 + lax.erf(x * (2.0 ** -0.5)))


def _to_tiled(v):
    m, w = v.shape
    nt = w // LANES
    tiles = [v[:, c * LANES:(c + 1) * LANES].reshape(m // SUBLANES, SUBLANES, LANES) for c in range(nt)]
    return jnp.stack(tiles, axis=1).reshape(m * nt, LANES)


def _from_tiled(t, w):
    nt = w // LANES
    m = t.shape[0] // nt
    t4 = t.reshape(m // SUBLANES, nt, SUBLANES, LANES)
    return jnp.concatenate([t4[:, c].reshape(m, LANES) for c in range(nt)], axis=1)


def _tiled_row(i, w):
    return (i // SUBLANES) * (SUBLANES * (w // LANES)) + i % SUBLANES


def _dot(a, b):
    return jnp.dot(a, b, preferred_element_type=F32)


def _dot_nt(a, b):
    return lax.dot_general(a, b, (((1,), (1,)), ((), ())), preferred_element_type=F32)


def _dot_tn(a, b):
    return lax.dot_general(a, b, (((0,), (0,)), ((), ())), preferred_element_type=F32)


def _mod_kernel(c_ref, w_ref, b_ref, o_ref):
    c = c_ref[...]
    s = c / (1.0 + jnp.exp(-c))
    o_ref[...] = _dot(s.astype(BF16), w_ref[...].astype(BF16)) + b_ref[...]


def _modulation(cond, w_mod, b_mod):
    r, d = cond.shape
    n = w_mod.shape[1]
    tn = _tile(n, 1024)
    return pl.pallas_call(
        _mod_kernel,
        out_shape=jax.ShapeDtypeStruct((r, n), F32),
        grid=(n // tn,),
        in_specs=[pl.BlockSpec((r, d), lambda j: (0, 0)),
                  pl.BlockSpec((d, tn), lambda j: (0, j)),
                  pl.BlockSpec((1, tn), lambda j: (0, j))],
        out_specs=pl.BlockSpec((r, tn), lambda j: (0, j)),
        compiler_params=_cparams(("parallel",)),
        name="modulation",
    )(cond, w_mod, b_mod.reshape(1, n))


def _proj0_kernel(x_ref, sc_ref, sh_ref, win_ref, lng_ref, lnb_ref, ws_ref, bs_ref,
                  qn_ref, wq_ref, kvn_ref, wk_ref, wv_ref, ck_ref, sk_ref, cq_ref, sq_ref,
                  a_ref, q_ref, k_ref, v_ref, *, q_scale):
    tm = x_ref.shape[1]
    h = (x_ref[0] * (1.0 + sc_ref[0]) + sh_ref[0]).astype(BF16)
    y = _dot(h, win_ref[...])
    w = GMLP_WIDTH
    u = _gelu(y[:, :w])
    vn = _ln(_gelu(y[:, w:2 * w]), lng_ref[...], lnb_ref[...]).astype(BF16)
    for ci in range(tm // CHUNK):
        r0 = ci * CHUNK
        for g in range(GMLP_GROUPS):
            c0 = g * GMLP_GROUP_CH
            mixed = _dot(ws_ref[g], vn[r0:r0 + CHUNK, c0:c0 + GMLP_GROUP_CH]) + bs_ref[g]
            a_ref[0, r0:r0 + CHUNK, c0:c0 + GMLP_GROUP_CH] = (
                u[r0:r0 + CHUNK, c0:c0 + GMLP_GROUP_CH] * mixed).astype(BF16)

    o = 2 * w
    cq = _rms(y[:, o:o + MLA_Q_RANK], qn_ref[...]).astype(BF16)
    qt = _dot_nt(wq_ref[...], cq) * q_scale
    q_ref[0] = qt.astype(BF16)
    hr = MLA_ROPE // 2
    cq_t = cq_ref[...]
    sq_t = sq_ref[...]
    for hd in range(MLA_HEADS):
        r0 = hd * HEAD_SLOT + MLA_NOPE
        x1 = qt[r0:r0 + hr]
        x2 = qt[r0 + hr:r0 + 2 * hr]
        blk = jnp.concatenate([x1, x2], axis=0)
        swp = jnp.concatenate([x2, x1], axis=0)
        q_ref[0, r0:r0 + 2 * hr, :] = (blk * cq_t + swp * sq_t).astype(BF16)

    o += MLA_Q_RANK
    ckv = _rms(y[:, o:o + MLA_KV_RANK], kvn_ref[...]).astype(BF16)
    o += MLA_KV_RANK
    kr = y[:, o:o + HEAD_SLOT] * ck_ref[...] + y[:, o + HEAD_SLOT:o + 2 * HEAD_SLOT] * sk_ref[...]
    k_ref[0] = (_dot(ckv, wk_ref[...]) + jnp.tile(kr, (1, MLA_HEADS))).astype(BF16)
    v_ref[0] = _dot_nt(wv_ref[...], ckv).astype(BF16)


def _proj0(x, sc, sh, wts, tabs):
    b, n, d = x.shape
    tm = _tile(n, 512)
    ck, sk, cq, sq = tabs
    win, lng, lnb, ws, bsb, qn, wq, kvn, wk, wv = wts
    hw = MLA_HEADS * HEAD_SLOT
    vw = MLA_HEADS * MLA_V
    full = lambda a: pl.BlockSpec(a.shape, lambda bi, ti: (0,) * a.ndim)
    q_scale = (MLA_NOPE + MLA_ROPE) ** -0.5 * LOG2E
    return pl.pallas_call(
        functools.partial(_proj0_kernel, q_scale=q_scale),
        out_shape=(jax.ShapeDtypeStruct((b, n, GMLP_WIDTH), BF16),
                   jax.ShapeDtypeStruct((b, hw, n), BF16),
                   jax.ShapeDtypeStruct((b, n, hw), BF16),
                   jax.ShapeDtypeStruct((b, vw, n), BF16)),
        grid=(b, n // tm),
        in_specs=[pl.BlockSpec((1, tm, d), lambda bi, ti: (bi, ti, 0)),
                  pl.BlockSpec((1, 1, d), lambda bi, ti: (bi, 0, 0)),
                  pl.BlockSpec((1, 1, d), lambda bi, ti: (bi, 0, 0)),
                  full(win), full(lng), full(lnb), full(ws), full(bsb),
                  full(qn), full(wq), full(kvn), full(wk), full(wv),
                  pl.BlockSpec((tm, HEAD_SLOT), lambda bi, ti: (ti, 0)),
                  pl.BlockSpec((tm, HEAD_SLOT), lambda bi, ti: (ti, 0)),
                  pl.BlockSpec((MLA_ROPE, tm), lambda bi, ti: (0, ti)),
                  pl.BlockSpec((MLA_ROPE, tm), lambda bi, ti: (0, ti))],
        out_specs=(pl.BlockSpec((1, tm, GMLP_WIDTH), lambda bi, ti: (bi, ti, 0)),
                   pl.BlockSpec((1, hw, tm), lambda bi, ti: (bi, 0, ti)),
                   pl.BlockSpec((1, tm, hw), lambda bi, ti: (bi, ti, 0)),
                   pl.BlockSpec((1, vw, tm), lambda bi, ti: (bi, 0, ti))),
        compiler_params=_cparams(("parallel", "parallel")),
        name="proj0",
    )(x, sc, sh, win, lng, lnb, ws, bsb, qn, wq, kvn, wk, wv, ck, sk, cq, sq)


def _proj1_kernel(x_ref, sc_ref, sh_ref, wq_ref, wk_ref, wv_ref, ck_ref, sk_ref, cq_ref, sq_ref,
                  q_ref, k_ref, v_ref, *, q_scale):
    h = (x_ref[0] * (1.0 + sc_ref[0]) + sh_ref[0]).astype(BF16)
    n_sub = 2 * DIFF_HEADS
    hd = DIFF_HEAD_DIM
    hr = hd // 2
    qt = _dot_nt(wq_ref[...], h) * q_scale
    cq_t = cq_ref[...]
    sq_t = sq_ref[...]
    for s in range(n_sub):
        r0 = s * hd
        x1 = qt[r0:r0 + hr]
        x2 = qt[r0 + hr:r0 + hd]
        swp = jnp.concatenate([x2, x1], axis=0)
        q_ref[0, r0:r0 + hd, :] = (qt[r0:r0 + hd] * cq_t + swp * sq_t).astype(BF16)
    k = _dot(h, wk_ref[...])
    width = k.shape[1]
    lane = lax.broadcasted_iota(jnp.int32, k.shape, 1)
    partner = jnp.where((lane % hd) < hr, pltpu.roll(k, width - hr, 1), pltpu.roll(k, hr, 1))
    reps = width // HEAD_SLOT
    k_ref[0] = (k * jnp.tile(ck_ref[...], (1, reps)) + partner * jnp.tile(sk_ref[...], (1, reps))).astype(BF16)
    v_ref[0] = _dot_nt(wv_ref[...], h).astype(BF16)


def _proj1(x, sc, sh, wts, tabs):
    b, n, d = x.shape
    tm = _tile(n, 512)
    wq, wk, wv = wts
    ck, sk, cq, sq = tabs
    dw = wq.shape[0]
    full = lambda a: pl.BlockSpec(a.shape, lambda bi, ti: (0,) * a.ndim)
    q_scale = DIFF_HEAD_DIM ** -0.5 * LOG2E
    return pl.pallas_call(
        functools.partial(_proj1_kernel, q_scale=q_scale),
        out_shape=(jax.ShapeDtypeStruct((b, dw, n), BF16),
                   jax.ShapeDtypeStruct((b, n, dw), BF16),
                   jax.ShapeDtypeStruct((b, dw, n), BF16)),
        grid=(b, n // tm),
        in_specs=[pl.BlockSpec((1, tm, d), lambda bi, ti: (bi, ti, 0)),
                  pl.BlockSpec((1, 1, d), lambda bi, ti: (bi, 0, 0)),
                  pl.BlockSpec((1, 1, d), lambda bi, ti: (bi, 0, 0)),
                  full(wq), full(wk), full(wv),
                  pl.BlockSpec((tm, HEAD_SLOT), lambda bi, ti: (ti, 0)),
                  pl.BlockSpec((tm, HEAD_SLOT), lambda bi, ti: (ti, 0)),
                  pl.BlockSpec((DIFF_HEAD_DIM, tm), lambda bi, ti: (0, ti)),
                  pl.BlockSpec((DIFF_HEAD_DIM, tm), lambda bi, ti: (0, ti))],
        out_specs=(pl.BlockSpec((1, dw, tm), lambda bi, ti: (bi, 0, ti)),
                   pl.BlockSpec((1, tm, dw), lambda bi, ti: (bi, ti, 0)),
                   pl.BlockSpec((1, dw, tm), lambda bi, ti: (bi, 0, ti))),
        compiler_params=_cparams(("parallel", "parallel")),
        name="proj1",
    )(x, sc, sh, wq, wk, wv, ck, sk, cq, sq)


SHIFT_OK_LO = 2.0 ** -85
SHIFT_OK_HI = 2.0 ** 100
ATTN_TQ = 1024
ATTN_UNROLL = 16
ATTN_STATIC_CHUNKS = 8


def _key_norm_max(kv_refs, kmax_ref, *, tk, n_sub):
    sub_w = HEAD_SLOT // n_sub
    row = lax.broadcasted_iota(jnp.int32, (8, HEAD_SLOT), 0)
    lane = lax.broadcasted_iota(jnp.int32, (8, HEAD_SLOT), 1)
    sel = jnp.where(lane // sub_w == row, 1.0, 0.0).astype(BF16)
    best = None
    for k_ref, _ in kv_refs:
        nchunk = k_ref.shape[1] // tk
        group = max(g for g in range(1, 12) if nchunk % g == 0)
        rows = group * tk

        def body(c, mx, k_ref=k_ref, rows=rows):
            off = pl.multiple_of(c * rows, rows)
            kc = k_ref[0, pl.ds(off, rows), :].astype(F32)
            return jnp.maximum(mx, _dot_nt(sel, (kc * kc).astype(BF16)))

        mx = lax.fori_loop(0, nchunk // group, body, jnp.zeros((8, rows), F32))
        mx = jnp.max(mx, axis=1, keepdims=True)
        best = mx if best is None else jnp.maximum(best, mx)
    kmax_ref[...] = jnp.broadcast_to(best, kmax_ref.shape)


def _sweep_bounded(qs, shifts, kv_refs, s_ref, *, tk, unroll):
    tq = qs[0].shape[1]
    dv = kv_refs[0][1].shape[1]
    chunks = [(si, c * tk) for si, (k_ref, _) in enumerate(kv_refs) for c in range(k_ref.shape[1] // tk)]

    def aligned(off):
        return off if isinstance(off, int) else pl.multiple_of(off, tk)

    def scores(si, off, buf):
        kc = kv_refs[si][0][0, pl.ds(aligned(off), tk), :]
        for r, q in enumerate(qs):
            s_ref[buf, r] = _dot(kc, q)

    def consume(si, off, buf, carry):
        vc = kv_refs[si][1][0, :, pl.ds(aligned(off), tk)]
        nxt = []
        for r, ((l8, acc), m) in enumerate(zip(carry, shifts)):
            p = jnp.exp2(s_ref[buf, r] - m)
            l8 = l8 + jnp.sum(p.reshape(tk // 8, 8, tq), axis=0)
            nxt.append((l8, acc + _dot(vc, p.astype(BF16))))
        return tuple(nxt)

    def body(i, carry):
        for u in range(unroll):
            off = pl.multiple_of((i * unroll + u) * tk, tk)
            scores(0, off + tk, (u + 1) % 2)
            carry = consume(0, off, u % 2, carry)
        return carry

    carry = tuple((jnp.zeros((8, tq), F32), jnp.zeros((dv, tq), F32)) for _ in qs)
    scores(*chunks[0], 0)
    n_first = kv_refs[0][0].shape[1] // tk
    n_loop = 0 if len(chunks) <= ATTN_STATIC_CHUNKS else (n_first - 1) // unroll * unroll
    if n_loop:
        carry = lax.fori_loop(0, n_loop // unroll, body, carry)
    for c in range(n_loop, len(chunks)):
        if c + 1 < len(chunks):
            scores(*chunks[c + 1], (c + 1) % 2)
        carry = consume(*chunks[c], c % 2, carry)
    return [(jnp.sum(l8, axis=0, keepdims=True), acc) for l8, acc in carry]


def _sweep_running_max(qs, kv_refs, *, tk):
    tq = qs[0].shape[1]
    dv = kv_refs[0][1].shape[1]
    carry = tuple((jnp.full((1, tq), NEG_BIG, F32), jnp.zeros((1, tq), F32), jnp.zeros((dv, tq), F32)) for _ in qs)
    for k_ref, v_ref in kv_refs:

        def body(c, carry, k_ref=k_ref, v_ref=v_ref):
            off = pl.multiple_of(c * tk, tk)
            kc = k_ref[0, pl.ds(off, tk), :]
            vc = v_ref[0, :, pl.ds(off, tk)]
            nxt = []
            for (m, l, acc), q in zip(carry, qs):
                s = _dot(kc, q)
                mn = jnp.maximum(m, jnp.max(s, axis=0, keepdims=True))
                alpha = jnp.exp2(m - mn)
                p = jnp.exp2(s - mn)
                nxt.append((mn, alpha * l + jnp.sum(p, axis=0, keepdims=True),
                            alpha * acc + _dot(vc, p.astype(BF16))))
            return tuple(nxt)

        carry = lax.fori_loop(0, k_ref.shape[1] // tk, body, carry)
    return [(l, acc) for _, l, acc in carry]


def _attn_kernel(*refs, tk, unroll, n_sub, n_src, finalize):
    q_ref = refs[0]
    kv_refs = [(refs[1 + 2 * i], refs[2 + 2 * i]) for i in range(n_src)]
    extra = refs[1 + 2 * n_src:-3]
    o_ref, kmax_ref, s_ref = refs[-3:]

    @pl.when(pl.program_id(2) == 0)
    def _():
        _key_norm_max(kv_refs, kmax_ref, tk=tk, n_sub=n_sub)

    q = q_ref[0]
    sub_w = HEAD_SLOT // n_sub
    row = lax.broadcasted_iota(jnp.int32, q.shape, 0)
    qf = q.astype(F32)
    qs, shifts = [], []
    for r in range(n_sub):
        mine = (row // sub_w) == r
        qs.append(jnp.where(mine, q, jnp.zeros_like(q)) if n_sub > 1 else q)
        qn2 = jnp.sum(jnp.where(mine, qf * qf, 0.0), axis=0, keepdims=True)
        shifts.append(jnp.sqrt(qn2 * kmax_ref[r:r + 1, 0:1]))

    res = _sweep_bounded(qs, shifts, kv_refs, s_ref, tk=tk, unroll=unroll)
    o_ref[0] = finalize(res, extra).astype(BF16)
    ok = None
    for l, _ in res:
        ok_r = (jnp.min(l) >= SHIFT_OK_LO) & (jnp.max(l) <= SHIFT_OK_HI)
        ok = ok_r if ok is None else ok & ok_r

    @pl.when(jnp.logical_not(ok))
    def _():
        o_ref[0] = finalize(_sweep_running_max(qs, kv_refs, tk=tk), extra).astype(BF16)


def _finalize_plain(res, extra):
    (l, acc), = res
    return acc / l


def _finalize_diff(res, extra, *, lam_init):
    lq1_ref, lk1_ref, lq2_ref, lk2_ref, g_ref = extra
    (l1, a1), (l2, a2) = res
    lam = (jnp.exp(jnp.sum(lq1_ref[...] * lk1_ref[...], axis=-1, keepdims=True))
           - jnp.exp(jnp.sum(lq2_ref[...] * lk2_ref[...], axis=-1, keepdims=True)) + lam_init)
    o = a1 / l1 - lam * (a2 / l2)
    return o * lax.rsqrt(jnp.mean(o * o, axis=0, keepdims=True) + RMS_EPS) * g_ref[...] * (1.0 - lam_init)


def _attention(qt, kv, extra=(), *, dv, n_sub, finalize, name):
    b, hw, n = qt.shape
    heads = hw // HEAD_SLOT
    tq = _tile(n, ATTN_TQ)
    tk = 256 if all(k.shape[1] % 256 == 0 for k, _ in kv) else 128
    full = lambda a: pl.BlockSpec(a.shape, lambda bi, hi, qi: (0,) * a.ndim)
    kv_specs, kv_args = [], []
    for k, vt in kv:
        t = k.shape[1]
        kv_specs += [pl.BlockSpec((1, t, HEAD_SLOT), lambda bi, hi, qi: (bi, 0, hi)),
                     pl.BlockSpec((1, dv, t), lambda bi, hi, qi: (bi, hi, 0))]
        kv_args += [k, vt]
    return pl.pallas_call(
        functools.partial(_attn_kernel, tk=tk, unroll=ATTN_UNROLL, n_sub=n_sub, n_src=len(kv), finalize=finalize),
        out_shape=jax.ShapeDtypeStruct((b, heads * dv, n), BF16),
        grid=(b, heads, n // tq),
        in_specs=[pl.BlockSpec((1, HEAD_SLOT, tq), lambda bi, hi, qi: (bi, hi, qi))] + kv_specs
        + [full(a) for a in extra],
        out_specs=pl.BlockSpec((1, dv, tq), lambda bi, hi, qi: (bi, hi, qi)),
        scratch_shapes=[pltpu.VMEM((8, HEAD_SLOT), F32), pltpu.VMEM((2, n_sub, tk, tq), F32)],
        compiler_params=_cparams(("parallel", "parallel", "arbitrary")),
        name=name,
    )(qt, *kv_args, *extra)


def _out_kernel(*refs, n_rowmajor):
    (x_ref, g_ref, lng_ref, lnb_ref, scf_ref, shf_ref, rt_ref), rest = refs[:7], refs[7:]
    x1_ref, aff_ref, hp_ref = rest[-3:]
    ops = rest[:-3]
    m = None
    for i in range(len(ops) // 2):
        a = ops[2 * i][0]
        w = ops[2 * i + 1][...]
        part = _dot(a, w) if i < n_rowmajor else _dot_tn(a, w)
        m = part if m is None else m + part
    x1 = _ln(DEEPNORM_ALPHA * x_ref[0] + (1.0 + g_ref[0]) * m, lng_ref[...], lnb_ref[...])
    x1_ref[0] = x1
    hf = (x1 * (1.0 + scf_ref[0]) + shf_ref[0]).astype(BF16)
    lg = _dot_nt(rt_ref[...], hf)
    e = jnp.exp(lg - jnp.max(lg, axis=0, keepdims=True))
    aff_ref[0] = e / jnp.sum(e, axis=0, keepdims=True)
    dh = hf.shape[1] // 2
    lo = pltpu.bitcast(hf[:, :dh].astype(F32), jnp.uint32)
    hi = pltpu.bitcast(hf[:, dh:].astype(F32), jnp.uint32)
    hp_ref[0] = _to_tiled(lax.shift_right_logical(lo, jnp.uint32(16)) | (hi & jnp.uint32(0xFFFF0000)))


def _mixer_out(x, g, lng, lnb, scf, shf, router_t, rowmajor_ops, chanmajor_ops):
    b, n, d = x.shape
    e = router_t.shape[0]
    tm = _tile(n, 512)
    vec = pl.BlockSpec((1, 1, d), lambda bi, ti: (bi, 0, 0))
    full = lambda a: pl.BlockSpec(a.shape, lambda bi, ti: (0,) * a.ndim)
    args = [x, g, lng, lnb, scf, shf, router_t]
    specs = [pl.BlockSpec((1, tm, d), lambda bi, ti: (bi, ti, 0)), vec, full(lng), full(lnb), vec, vec,
             full(router_t)]
    for a, w in rowmajor_ops:
        args += [a, w]
        specs += [pl.BlockSpec((1, tm, a.shape[2]), lambda bi, ti: (bi, ti, 0)), full(w)]
    for a, w in chanmajor_ops:
        args += [a, w]
        specs += [pl.BlockSpec((1, a.shape[1], tm), lambda bi, ti: (bi, 0, ti)), full(w)]
    return pl.pallas_call(
        functools.partial(_out_kernel, n_rowmajor=len(rowmajor_ops)),
        out_shape=(jax.ShapeDtypeStruct((b, n, d), F32), jax.ShapeDtypeStruct((b, e, n), F32),
                   jax.ShapeDtypeStruct((b, n * (d // 2) // LANES, LANES), jnp.uint32)),
        grid=(b, n // tm),
        in_specs=specs,
        out_specs=(pl.BlockSpec((1, tm, d), lambda bi, ti: (bi, ti, 0)),
                   pl.BlockSpec((1, e, tm), lambda bi, ti: (bi, 0, ti)),
                   pl.BlockSpec((1, tm * (d // 2) // LANES, LANES), lambda bi, ti: (bi, ti, 0))),
        compiler_params=_cparams(("parallel", "parallel")),
        name="mixer_out",
    )(*args)


ROW_UNROLL = 8
FFN_ROWS = 512
TOKEN_SPLIT = 4096


def _ffn_kernel(src_ref, hp_ref, g_ref, wg_ref, wu_ref, wd_ref, y_ref, rows_ref):
    cap = y_ref.shape[2]
    nt = rows_ref.shape[0] // cap
    dh = nt * LANES

    def body(jb, carry):
        for u in range(ROW_UNROLL):
            rows_ref[pl.ds(jb * (ROW_UNROLL * nt) + u, nt, stride=SUBLANES), :] = (
                hp_ref[0, pl.ds(src_ref[0, 0, jb * ROW_UNROLL + u], nt, stride=SUBLANES), :])
        return carry

    lax.fori_loop(0, cap // ROW_UNROLL, body, 0)
    blk = min(cap, FFN_ROWS)
    for c0 in range(0, cap, blk):
        w = _from_tiled(rows_ref[c0 * nt:(c0 + blk) * nt], dh)
        lo = pltpu.bitcast(lax.shift_left(w, jnp.uint32(16)), F32)
        hi = pltpu.bitcast(w & jnp.uint32(0xFFFF0000), F32)
        xs = jnp.concatenate([lo, hi], axis=1).astype(BF16)
        gate = _dot(xs, wg_ref[0])
        hid = gate / (1.0 + jnp.exp(-gate)) * _dot(xs, wu_ref[0])
        y_ref[0, 0, c0:c0 + blk] = (_dot(hid.astype(BF16), wd_ref[0]) * g_ref[0, c0:c0 + blk]).astype(BF16)


def _moe_ffn(hp, src3, gates3, wg, wu, wd, cap):
    b, hp_rows, _ = hp.shape
    n_exp, d, ff = wg.shape
    nt = d // 2 // LANES
    sample_major = hp_rows * LANES * 4 > 3 * d * ff * 2
    if sample_major:
        grid, be = (b, n_exp), (lambda g0, g1: (g0, g1))
        hp_spec = pl.BlockSpec((1, hp_rows, LANES), lambda g0, g1: (g0, 0, 0), pipeline_mode=pl.Buffered(1))
    else:
        grid, be = (n_exp, b), (lambda g0, g1: (g1, g0))
        hp_spec = pl.BlockSpec((1, hp_rows, LANES), lambda g0, g1: (g1, 0, 0))
    slot = lambda g0, g1: (be(g0, g1)[0] * n_exp + be(g0, g1)[1], 0, 0)
    weight = lambda g0, g1: (be(g0, g1)[1], 0, 0)
    return pl.pallas_call(
        _ffn_kernel,
        out_shape=jax.ShapeDtypeStruct((b, n_exp, cap, d), BF16),
        grid=grid,
        in_specs=[pl.BlockSpec((1, 1, cap), slot, memory_space=pltpu.SMEM),
                  hp_spec,
                  pl.BlockSpec((1, cap, 1), slot),
                  pl.BlockSpec((1, d, ff), weight),
                  pl.BlockSpec((1, d, ff), weight),
                  pl.BlockSpec((1, ff, d), weight)],
        out_specs=pl.BlockSpec((1, 1, cap, d), lambda g0, g1: be(g0, g1) + (0, 0)),
        scratch_shapes=[pltpu.VMEM((cap * nt, LANES), jnp.uint32)],
        compiler_params=_cparams(("parallel", "arbitrary")),
        name="moe_ffn",
    )(src3, hp, gates3, wg, wu, wd)


def _scatter_kernel(dst_ref, bnd_ref, y_ref, f_ref, rows_ref):
    d = y_ref.shape[3]
    nt = d // LANES

    @pl.when(pl.program_id(2) == 0)
    def _():
        f_ref[...] = jnp.zeros_like(f_ref)

    rows_ref[...] = _to_tiled(y_ref[0, 0].astype(F32))
    lo = bnd_ref[0, 0, pl.program_id(1)]
    hi = bnd_ref[0, 0, pl.program_id(1) + 1]

    def row(ref, start):
        return ref[pl.ds(start, nt, stride=SUBLANES), :]

    def one(j, carry):
        dst = dst_ref[0, 0, j]
        f_ref[0, pl.ds(dst, nt, stride=SUBLANES), :] = row(f_ref.at[0], dst) + row(rows_ref, _tiled_row(j, d))
        return carry

    def batch(jb, carry):
        dsts = [dst_ref[0, 0, jb * ROW_UNROLL + u] for u in range(ROW_UNROLL)]
        sums = [row(f_ref.at[0], dsts[u]) + row(rows_ref, jb * (ROW_UNROLL * nt) + u) for u in range(ROW_UNROLL)]
        for u in range(ROW_UNROLL):
            f_ref[0, pl.ds(dsts[u], nt, stride=SUBLANES), :] = sums[u]
        return carry

    head_end = jnp.minimum((lo + ROW_UNROLL - 1) // ROW_UNROLL * ROW_UNROLL, hi)
    tail_start = jnp.maximum(head_end, hi // ROW_UNROLL * ROW_UNROLL)
    lax.fori_loop(lo, head_end, one, 0)
    lax.fori_loop(head_end // ROW_UNROLL, tail_start // ROW_UNROLL, batch, 0)
    lax.fori_loop(tail_start, hi, one, 0)


def _moe_scatter(y, dst3, bounds3, n):
    b, n_exp, cap, d = y.shape
    nt = _tile(n, TOKEN_SPLIT)
    lt = d // LANES
    return pl.pallas_call(
        _scatter_kernel,
        out_shape=jax.ShapeDtypeStruct((b, n * lt, LANES), F32),
        grid=(b, n // nt, n_exp),
        in_specs=[pl.BlockSpec((1, 1, cap), lambda bi, ki, ei: (bi * n_exp + ei, 0, 0), memory_space=pltpu.SMEM),
                  pl.BlockSpec((1, 1, n // nt + 1), lambda bi, ki, ei: (bi * n_exp + ei, 0, 0),
                               memory_space=pltpu.SMEM),
                  pl.BlockSpec((1, 1, cap, d), lambda bi, ki, ei: (bi, ei, 0, 0))],
        out_specs=pl.BlockSpec((1, nt * lt, LANES), lambda bi, ki, ei: (bi, ki, 0)),
        scratch_shapes=[pltpu.VMEM((cap * lt, LANES), F32)],
        compiler_params=_cparams(("parallel", "parallel", "arbitrary")),
        name="moe_scatter",
    )(dst3, bounds3, y)


def _post_kernel(x_ref, f_ref, g_ref, lng_ref, lnb_ref, o_ref):
    f = _from_tiled(f_ref[0], x_ref.shape[2])
    o_ref[0] = _ln(DEEPNORM_ALPHA * x_ref[0] + (1.0 + g_ref[0]) * f, lng_ref[...], lnb_ref[...])


def _post(x1, f_tiled, g, lng, lnb):
    b, n, d = x1.shape
    tm = _tile(n, 512)
    blk = pl.BlockSpec((1, tm, d), lambda bi, ti: (bi, ti, 0))
    f_blk = pl.BlockSpec((1, tm * d // LANES, LANES), lambda bi, ti: (bi, ti, 0))
    full = lambda a: pl.BlockSpec(a.shape, lambda bi, ti: (0,) * a.ndim)
    return pl.pallas_call(
        _post_kernel,
        out_shape=jax.ShapeDtypeStruct((b, n, d), F32),
        grid=(b, n // tm),
        in_specs=[blk, f_blk, pl.BlockSpec((1, 1, d), lambda bi, ti: (bi, 0, 0)), full(lng), full(lnb)],
        out_specs=blk,
        compiler_params=_cparams(("parallel", "parallel")),
        name="ffn_post",
    )(x1, f_tiled, g, lng, lnb)


def _select_kernel(aff_ref, idx_ref, gate_ref, *, cap, n_exp):
    a = aff_ref[0]
    er, lanes = a.shape
    r_n = er // n_exp
    e_pad = 16
    bits = pltpu.bitcast(a, jnp.int32)

    member = (lax.broadcasted_iota(jnp.int32, (e_pad, er), 1) // r_n
              == lax.broadcasted_iota(jnp.int32, (e_pad, er), 0))
    gs = jnp.where(member, 1.0, 0.0).astype(BF16)
    member_t = (lax.broadcasted_iota(jnp.int32, (er, e_pad), 0) // r_n
                == lax.broadcasted_iota(jnp.int32, (er, e_pad), 1))
    gst = jnp.where(member_t, 1.0, 0.0).astype(BF16)
    ri = lax.broadcasted_iota(jnp.int32, (er, er), 0)
    rj = lax.broadcasted_iota(jnp.int32, (er, er), 1)
    rows_before = jnp.where((ri // r_n == rj // r_n) & (rj < ri), 1.0, 0.0).astype(BF16)
    li = lax.broadcasted_iota(jnp.int32, (lanes, lanes), 0)
    lj = lax.broadcasted_iota(jnp.int32, (lanes, lanes), 1)
    lanes_upto = jnp.where(li <= lj, 1.0, 0.0).astype(BF16)

    def expert_total(m):
        per_lane = _dot(gs, m.astype(BF16))
        return jnp.broadcast_to(jnp.sum(per_lane, axis=1, keepdims=True), per_lane.shape)

    def to_rows(ev):
        if r_n % SUBLANES == 0 and n_exp == e_pad:
            return jnp.broadcast_to(ev[:, None, :], (e_pad, r_n, lanes)).reshape(er, lanes)
        hi = jnp.floor(ev * (1.0 / 128.0))
        return 128.0 * _dot(gst, hi.astype(BF16)) + _dot(gst, (ev - 128.0 * hi).astype(BF16))

    def search(i, t):
        cand = t | lax.shift_left(jnp.int32(1), 30 - i)
        cnt = expert_total(jnp.where(bits >= cand, 1.0, 0.0))
        return jnp.where(to_rows(jnp.where(cnt >= cap, 1.0, 0.0)) > 0.5, cand, t)

    t = lax.fori_loop(0, 31, search, jnp.zeros((er, lanes), jnp.int32))
    gt = jnp.where(bits > t, 1.0, 0.0)
    eq = jnp.where(bits == t, 1.0, 0.0)

    def prefix(m):
        rc = _dot(m.astype(BF16), lanes_upto)
        off = _dot(rows_before, jnp.broadcast_to(rc[:, lanes - 1:lanes], m.shape).astype(BF16))
        return rc, off

    rc_eq, off_eq = prefix(eq)
    need = cap - to_rows(expert_total(gt))
    sel = jnp.maximum(gt, eq * jnp.where(off_eq + rc_eq - eq < need, 1.0, 0.0))
    rc, off = prefix(sel)
    row_end = off + jnp.broadcast_to(rc[:, lanes - 1:lanes], rc.shape)

    slot = lax.broadcasted_iota(jnp.int32, (cap, lanes), 0).astype(F32) + 1.0
    lane_c = lax.broadcasted_iota(jnp.int32, (cap, lanes), 1)
    lane_cf = lane_c.astype(F32)
    diag = lax.broadcasted_iota(jnp.int32, (r_n, lanes), 0) == lax.broadcasted_iota(jnp.int32, (r_n, lanes), 1)
    lane_r = lax.broadcasted_iota(jnp.int32, (1, lanes), 1)
    pad = jnp.zeros((lanes - r_n, lanes), F32)

    def table(m):
        return (jnp.concatenate([m, pad], axis=0) if r_n < lanes else m).astype(BF16)

    all_ones = jnp.ones((lanes, lanes), BF16)
    idx_all = jnp.zeros((cap, lanes), jnp.int32)
    gate_all = jnp.zeros((cap, lanes), F32)
    for e in range(n_exp):
        r0 = e * r_n
        end_lane = jnp.sum(jnp.where(diag, row_end[r0:r0 + r_n], 0.0), axis=0, keepdims=True)
        end_lane = jnp.where(lane_r < r_n, end_lane, 3.0e38)
        row_j = jnp.sum(jnp.where(end_lane < slot, 1.0, 0.0), axis=1, keepdims=True)
        oh = jnp.where(lane_cf == row_j, 1.0, 0.0).astype(BF16)
        cum_e = off[r0:r0 + r_n] + rc[r0:r0 + r_n]
        cum_hi = jnp.floor(cum_e * (1.0 / 128.0))
        cum_2 = _dot(oh, jnp.concatenate([table(cum_hi), table(cum_e - 128.0 * cum_hi)], axis=1))
        cum_j = 128.0 * cum_2[:, :lanes] + cum_2[:, lanes:]
        lane_j = _dot(jnp.where(cum_j < slot, 1.0, 0.0).astype(BF16), all_ones)
        a_e = a[r0:r0 + r_n]
        a1 = a_e.astype(BF16).astype(F32)
        a2 = (a_e - a1).astype(BF16).astype(F32)
        a3 = a_e - a1 - a2
        aff_2 = _dot(oh, jnp.concatenate([table(a1), table(a2)], axis=1))
        aff_j = aff_2[:, :lanes] + aff_2[:, lanes:] + _dot(oh, table(a3))
        gate_j = jnp.sum(jnp.where(lane_cf == lane_j, aff_j, 0.0), axis=1, keepdims=True)
        idx_j = (row_j * float(lanes) + lane_j).astype(jnp.int32)
        idx_all = jnp.where(lane_c == e, idx_j, idx_all)
        gate_all = jnp.where(lane_c == e, gate_j, gate_all)
    idx_ref[0] = idx_all
    gate_ref[0] = gate_all


def _moe_select(aff_t, cap):
    b, n_exp, n = aff_t.shape
    lanes = HEAD_SLOT
    er = n_exp * (n // lanes)
    idx, gates = pl.pallas_call(
        functools.partial(_select_kernel, cap=cap, n_exp=n_exp),
        out_shape=(jax.ShapeDtypeStruct((b, cap, lanes), jnp.int32), jax.ShapeDtypeStruct((b, cap, lanes), F32)),
        grid=(b,),
        in_specs=[pl.BlockSpec((1, er, lanes), lambda bi: (bi, 0, 0))],
        out_specs=(pl.BlockSpec((1, cap, lanes), lambda bi: (bi, 0, 0)),
                   pl.BlockSpec((1, cap, lanes), lambda bi: (bi, 0, 0))),
        compiler_params=_cparams(("parallel",)),
        name="moe_select",
    )(aff_t.reshape(b, er, lanes))
    to_expert_major = lambda m: jnp.swapaxes(m[:, :, :n_exp], 1, 2)
    return to_expert_major(gates), to_expert_major(idx)


def _moe(x1, aff_t, hp, gf, wg, wu, wd, lng, lnb):
    b, n, d = x1.shape
    n_exp = aff_t.shape[1]
    cap = CAPACITY_FACTOR * n // n_exp
    gates, idx = _moe_select(aff_t, cap)
    gates3 = gates.reshape(b * n_exp, cap, 1)
    nt = _tile(n, TOKEN_SPLIT)
    edges = jnp.arange(0, n + 1, nt, dtype=jnp.int32)
    bounds3 = jnp.sum(idx[..., None] < edges, axis=2, dtype=jnp.int32).reshape(b * n_exp, 1, n // nt + 1)
    src3 = _tiled_row(idx, d // 2).reshape(b * n_exp, 1, cap)
    dst3 = _tiled_row(idx % nt, d).reshape(b * n_exp, 1, cap)
    y = _moe_ffn(hp, src3, gates3, wg, wu, wd, cap)
    f_tiled = _moe_scatter(y, dst3, bounds3, n)
    return _post(x1, f_tiled, gf, lng, lnb)


def _rope_angles(n_tokens, dim):
    n_rows = n_tokens // GRID_W
    row = jnp.repeat(jnp.arange(n_rows, dtype=F32), GRID_W)
    col = jnp.tile(jnp.arange(GRID_W, dtype=F32), n_rows)
    n_freq = dim // 4
    inv_freq = ROPE_BASE ** (-jnp.arange(n_freq, dtype=F32) / n_freq)
    ang = jnp.concatenate([row[:, None] * inv_freq, col[:, None] * inv_freq], -1)
    return jnp.cos(ang), jnp.sin(ang)


def _rope_tables(n_tokens, dim, lane_offset, identity):
    half = dim // 2
    if identity:
        cos = jnp.ones((n_tokens, half), F32)
        sin = jnp.zeros((n_tokens, half), F32)
    else:
        cos, sin = _rope_angles(n_tokens, dim)
    c2 = jnp.concatenate([cos, cos], -1)
    s2 = jnp.concatenate([-sin, sin], -1)
    reps = (HEAD_SLOT - lane_offset) // dim if lane_offset == 0 else 1
    ck = jnp.zeros((n_tokens, HEAD_SLOT), F32)
    sk = jnp.zeros((n_tokens, HEAD_SLOT), F32)
    for r in range(reps):
        ck = ck.at[:, lane_offset + r * dim:lane_offset + (r + 1) * dim].set(c2)
        sk = sk.at[:, lane_offset + r * dim:lane_offset + (r + 1) * dim].set(s2)
    return ck, sk, c2.T, s2.T


def _deinterleave(w, axis):
    return jnp.concatenate([jnp.take(w, jnp.arange(0, w.shape[axis], 2), axis=axis),
                            jnp.take(w, jnp.arange(1, w.shape[axis], 2), axis=axis)], axis=axis)


def _prep_layer0(w_in, ln_g, ln_b, ws, bs, q_norm, w_uq, kv_norm, w_ukv):
    d = w_in.shape[0]
    w = GMLP_WIDTH
    o_kr = 2 * w + MLA_Q_RANK + MLA_KV_RANK
    kr = _deinterleave(w_in[:, o_kr:o_kr + MLA_ROPE], 1)
    kr_sw = jnp.concatenate([kr[:, MLA_ROPE // 2:], kr[:, :MLA_ROPE // 2]], 1)
    slot = lambda m: jnp.zeros((d, HEAD_SLOT), F32).at[:, MLA_NOPE:MLA_NOPE + MLA_ROPE].set(m)
    win = jnp.concatenate([w_in[:, :o_kr], slot(kr), slot(kr_sw)], 1).astype(BF16)

    qd = MLA_NOPE + MLA_ROPE
    wq = w_uq.reshape(MLA_Q_RANK, MLA_HEADS, qd)
    wq = jnp.concatenate([wq[..., :MLA_NOPE], _deinterleave(wq[..., MLA_NOPE:], 2),
                          jnp.zeros((MLA_Q_RANK, MLA_HEADS, HEAD_SLOT - qd), F32)], -1)
    wq_t = wq.reshape(MLA_Q_RANK, MLA_HEADS * HEAD_SLOT).T.astype(BF16)

    wkv = w_ukv.reshape(MLA_KV_RANK, MLA_HEADS, MLA_NOPE + MLA_V)
    wk = jnp.concatenate([wkv[..., :MLA_NOPE], jnp.zeros((MLA_KV_RANK, MLA_HEADS, HEAD_SLOT - MLA_NOPE), F32)], -1)
    wk = wk.reshape(MLA_KV_RANK, MLA_HEADS * HEAD_SLOT).astype(BF16)
    wv_t = wkv[..., MLA_NOPE:].reshape(MLA_KV_RANK, MLA_HEADS * MLA_V).T.astype(BF16)

    bsb = jnp.broadcast_to(bs[:, :, None], (GMLP_GROUPS, CHUNK, GMLP_GROUP_CH))
    return (win, ln_g.reshape(1, -1), ln_b.reshape(1, -1), ws.astype(BF16), bsb,
            q_norm.reshape(1, -1), wq_t, kv_norm.reshape(1, -1), wk, wv_t)


def _prep_layer1(w_in):
    d = w_in.shape[0]
    dw = 2 * DIFF_HEADS * DIFF_HEAD_DIM
    perm = lambda m: _deinterleave(m.reshape(d, 2 * DIFF_HEADS, DIFF_HEAD_DIM), 2).reshape(d, dw)
    wq_t = perm(w_in[:, :dw]).T.astype(BF16)
    wk = perm(w_in[:, dw:2 * dw]).astype(BF16)
    wv_t = w_in[:, 2 * dw:].T.astype(BF16)
    return wq_t, wk, wv_t


def _split6(m_row):
    return [v[:, None, :] for v in jnp.split(m_row, 6, axis=-1)]


def kernel(x, c, ctx, c_ctx, w_mod_0, b_mod_0, w_in_0, gmlp_ln_g_0, gmlp_ln_b_0, gmlp_ws_0, gmlp_bs_0, mla_q_norm_0, mla_w_uq_0, mla_kv_norm_0, mla_w_ukv_0, w_out_0, ln_mix_g_0, ln_mix_b_0, router_0, w_gate_0, w_up_0, w_down_0, ln_ffn_g_0, ln_ffn_b_0, w_mod_1, b_mod_1, w_in_1, lambda_q1_1, lambda_k1_1, lambda_q2_1, lambda_k2_1, subln_g_1, w_out_1, ln_mix_g_1, ln_mix_b_1, router_1, w_gate_1, w_up_1, w_down_1, ln_ffn_g_1, ln_ffn_b_1):
    b, n, d = x.shape
    m_ctx = ctx.shape[1]
    row = lambda v: v.reshape(1, -1)

    cond = jnp.concatenate([c, c_ctx[None, :], jnp.zeros((-(b + 1) % 8, d), F32)], 0)
    mod0 = _modulation(cond, w_mod_0, b_mod_0)
    mod1 = _modulation(cond, w_mod_1, b_mod_1)
    sh_a0, sc_a0, g_a0, sh_f0, sc_f0, g_f0 = _split6(mod0[:b])
    csh_a0, csc_a0, cg_a0, csh_f0, csc_f0, cg_f0 = [jnp.broadcast_to(v, (b, 1, d)) for v in _split6(mod0[b:b + 1])]
    sh_a1, sc_a1, g_a1, sh_f1, sc_f1, g_f1 = _split6(mod1[:b])
    csh_a1, csc_a1 = [jnp.broadcast_to(v, (b, 1, d)) for v in _split6(mod1[b:b + 1])[:2]]

    wts0 = _prep_layer0(w_in_0, gmlp_ln_g_0, gmlp_ln_b_0, gmlp_ws_0, gmlp_bs_0,
                        mla_q_norm_0, mla_w_uq_0, mla_kv_norm_0, mla_w_ukv_0)
    tabs_l = _rope_tables(n, MLA_ROPE, MLA_NOPE, identity=False)
    tabs_c = _rope_tables(m_ctx, MLA_ROPE, MLA_NOPE, identity=True)
    a_l, q_l, k_l, v_l = _proj0(x, sc_a0, sh_a0, wts0, tabs_l)
    a_c, q_c, k_c, v_c = _proj0(ctx, csc_a0, csh_a0, wts0, tabs_c)
    kv_cat = [(jnp.concatenate([k_l, k_c], 1), jnp.concatenate([v_l, v_c], 2))]
    o_l = _attention(q_l, kv_cat, dv=MLA_V, n_sub=1, finalize=_finalize_plain, name="mla_attn")
    o_c = _attention(q_c, [(k_c, v_c)], dv=MLA_V, n_sub=1, finalize=_finalize_plain, name="mla_attn_ctx")

    w_out_a = w_out_0[:GMLP_WIDTH].astype(BF16)
    w_out_o = w_out_0[GMLP_WIDTH:].astype(BF16)
    router_t0 = router_0.T.astype(BF16)
    wg0, wu0, wd0 = w_gate_0.astype(BF16), w_up_0.astype(BF16), w_down_0.astype(BF16)
    lng, lnb = row(ln_mix_g_0), row(ln_mix_b_0)
    x1, aff, hp = _mixer_out(x, g_a0, lng, lnb, sc_f0, sh_f0, router_t0, [(a_l, w_out_a)], [(o_l, w_out_o)])
    x_lat = _moe(x1, aff, hp, g_f0, wg0, wu0, wd0, row(ln_ffn_g_0), row(ln_ffn_b_0))
    c1, caff, chp = _mixer_out(ctx, cg_a0, lng, lnb, csc_f0, csh_f0, router_t0, [(a_c, w_out_a)], [(o_c, w_out_o)])
    x_ctx = _moe(c1, caff, chp, cg_f0, wg0, wu0, wd0, row(ln_ffn_g_0), row(ln_ffn_b_0))

    lam_init = 0.8 - 0.6 * math.exp(-0.3 * 1)
    wts1 = _prep_layer1(w_in_1)
    tabs_l = _rope_tables(n, DIFF_HEAD_DIM, 0, identity=False)
    tabs_c = _rope_tables(m_ctx, DIFF_HEAD_DIM, 0, identity=True)
    q_l, k_l, v_l = _proj1(x_lat, sc_a1, sh_a1, wts1, tabs_l)
    _, k_c, v_c = _proj1(x_ctx, csc_a1, csh_a1, wts1, tabs_c)
    extra = (row(lambda_q1_1), row(lambda_k1_1), row(lambda_q2_1), row(lambda_k2_1), subln_g_1.reshape(-1, 1))
    kv_cat = [(jnp.concatenate([k_l, k_c], 1), jnp.concatenate([v_l, v_c], 2))]
    o_l = _attention(q_l, kv_cat, extra, dv=2 * DIFF_HEAD_DIM, n_sub=2,
                     finalize=functools.partial(_finalize_diff, lam_init=lam_init), name="diff_attn")
    x1, aff, hp = _mixer_out(x_lat, g_a1, row(ln_mix_g_1), row(ln_mix_b_1), sc_f1, sh_f1, router_1.T.astype(BF16),
                             [], [(o_l, w_out_1.astype(BF16))])
    return _moe(x1, aff, hp, g_f1, w_gate_1.astype(BF16), w_up_1.astype(BF16), w_down_1.astype(BF16),
                row(ln_ffn_g_1), row(ln_ffn_b_1))
```

```python
import functools
import math

import jax
import jax.numpy as jnp
from jax import lax
from jax.experimental import pallas as pl
from jax.experimental.pallas import tpu as pltpu

F32 = jnp.float32
BF16 = jnp.bfloat16

DEPTH = 2
GRID_W = 64
ROPE_BASE = 10000.0
CHUNK = 128
GMLP_GROUPS = 4
GMLP_GROUP_CH = 128
GMLP_WIDTH = GMLP_GROUPS * GMLP_GROUP_CH
MLA_HEADS = 8
MLA_NOPE = 64
MLA_ROPE = 32
MLA_V = 64
MLA_Q_RANK = 256
MLA_KV_RANK = 128
DIFF_HEADS = 8
DIFF_HEAD_DIM = 64
CAPACITY_FACTOR = 2
DEEPNORM_ALPHA = (2 * DEPTH) ** 0.25
LN_EPS = 1e-5
RMS_EPS = 1e-6
LOG2E = 1.4426950408889634

LANES = 128
SUBLANES = 8
HEAD_SLOT = LANES
NEG_BIG = -1e30
VMEM_LIMIT = 56 * 1024 * 1024


def _cparams(sem):
    return pltpu.CompilerParams(dimension_semantics=sem, vmem_limit_bytes=VMEM_LIMIT)


def _tile(n, pref):
    return pref if n % pref == 0 else n


def _ln(x, g, b):
    mu = jnp.mean(x, axis=-1, keepdims=True)
    xc = x - mu
    var = jnp.mean(xc * xc, axis=-1, keepdims=True)
    return xc * lax.rsqrt(var + LN_EPS) * g + b


def _rms(x, g):
    return x * lax.rsqrt(jnp.mean(x * x, axis=-1, keepdims=True) + RMS_EPS) * g


def _gelu(x):
    return 0.5 * x * (1.0 + lax.erf(x * (2.0 ** -0.5)))


def _to_tiled(v):
    m, w = v.shape
    nt = w // LANES
    tiles = [v[:, c * LANES:(c + 1) * LANES].reshape(m // SUBLANES, SUBLANES, LANES) for c in range(nt)]
    return jnp.stack(tiles, axis=1).reshape(m * nt, LANES)


def _from_tiled(t, w):
    nt = w // LANES
    m = t.shape[0] // nt
    t4 = t.reshape(m // SUBLANES, nt, SUBLANES, LANES)
    return jnp.concatenate([t4[:, c].reshape(m, LANES) for c in range(nt)], axis=1)


def _tiled_row(i, w):
    return (i // SUBLANES) * (SUBLANES * (w // LANES)) + i % SUBLANES


def _dot(a, b):
    return jnp.dot(a, b, preferred_element_type=F32)


def _dot_nt(a, b):
    return lax.dot_general(a, b, (((1,), (1,)), ((), ())), preferred_element_type=F32)


def _dot_tn(a, b):
    return lax.dot_general(a, b, (((0,), (0,)), ((), ())), preferred_element_type=F32)


def _mod_kernel(c_ref, w_ref, b_ref, o_ref):
    c = c_ref[...]
    s = c / (1.0 + jnp.exp(-c))
    o_ref[...] = _dot(s.astype(BF16), w_ref[...].astype(BF16)) + b_ref[...]


def _modulation(cond, w_mod, b_mod):
    r, d = cond.shape
    n = w_mod.shape[1]
    tn = _tile(n, 1024)
    return pl.pallas_call(
        _mod_kernel,
        out_shape=jax.ShapeDtypeStruct((r, n), F32),
        grid=(n // tn,),
        in_specs=[pl.BlockSpec((r, d), lambda j: (0, 0)),
                  pl.BlockSpec((d, tn), lambda j: (0, j)),
                  pl.BlockSpec((1, tn), lambda j: (0, j))],
        out_specs=pl.BlockSpec((r, tn), lambda j: (0, j)),
        compiler_params=_cparams(("parallel",)),
        name="modulation",
    )(cond, w_mod, b_mod.reshape(1, n))


def _proj0_kernel(x_ref, sc_ref, sh_ref, win_ref, lng_ref, lnb_ref, ws_ref, bs_ref,
                  qn_ref, wq_ref, kvn_ref, wk_ref, wv_ref, ck_ref, sk_ref, cq_ref, sq_ref,
                  a_ref, q_ref, k_ref, v_ref, *, q_scale):
    tm = x_ref.shape[1]
    h = (x_ref[0] * (1.0 + sc_ref[0]) + sh_ref[0]).astype(BF16)
    y = _dot(h, win_ref[...])
    w = GMLP_WIDTH
    u = _gelu(y[:, :w])
    vn = _ln(_gelu(y[:, w:2 * w]), lng_ref[...], lnb_ref[...]).astype(BF16)
    for ci in range(tm // CHUNK):
        r0 = ci * CHUNK
        for g in range(GMLP_GROUPS):
            c0 = g * GMLP_GROUP_CH
            mixed = _dot(ws_ref[g], vn[r0:r0 + CHUNK, c0:c0 + GMLP_GROUP_CH]) + bs_ref[g]
            a_ref[0, r0:r0 + CHUNK, c0:c0 + GMLP_GROUP_CH] = (
                u[r0:r0 + CHUNK, c0:c0 + GMLP_GROUP_CH] * mixed).astype(BF16)

    o = 2 * w
    cq = _rms(y[:, o:o + MLA_Q_RANK], qn_ref[...]).astype(BF16)
    qt = _dot_nt(wq_ref[...], cq) * q_scale
    q_ref[0] = qt.astype(BF16)
    hr = MLA_ROPE // 2
    cq_t = cq_ref[...]
    sq_t = sq_ref[...]
    for hd in range(MLA_HEADS):
        r0 = hd * HEAD_SLOT + MLA_NOPE
        x1 = qt[r0:r0 + hr]
        x2 = qt[r0 + hr:r0 + 2 * hr]
        blk = jnp.concatenate([x1, x2], axis=0)
        swp = jnp.concatenate([x2, x1], axis=0)
        q_ref[0, r0:r0 + 2 * hr, :] = (blk * cq_t + swp * sq_t).astype(BF16)

    o += MLA_Q_RANK
    ckv = _rms(y[:, o:o + MLA_KV_RANK], kvn_ref[...]).astype(BF16)
    o += MLA_KV_RANK
    kr = y[:, o:o + HEAD_SLOT] * ck_ref[...] + y[:, o + HEAD_SLOT:o + 2 * HEAD_SLOT] * sk_ref[...]
    k_ref[0] = (_dot(ckv, wk_ref[...]) + jnp.tile(kr, (1, MLA_HEADS))).astype(BF16)
    v_ref[0] = _dot_nt(wv_ref[...], ckv).astype(BF16)


def _proj0(x, sc, sh, wts, tabs):
    b, n, d = x.shape
    tm = _tile(n, 512)
    ck, sk, cq, sq = tabs
    win, lng, lnb, ws, bsb, qn, wq, kvn, wk, wv = wts
    hw = MLA_HEADS * HEAD_SLOT
    vw = MLA_HEADS * MLA_V
    full = lambda a: pl.BlockSpec(a.shape, lambda bi, ti: (0,) * a.ndim)
    q_scale = (MLA_NOPE + MLA_ROPE) ** -0.5 * LOG2E
    return pl.pallas_call(
        functools.partial(_proj0_kernel, q_scale=q_scale),
        out_shape=(jax.ShapeDtypeStruct((b, n, GMLP_WIDTH), BF16),
                   jax.ShapeDtypeStruct((b, hw, n), BF16),
                   jax.ShapeDtypeStruct((b, n, hw), BF16),
                   jax.ShapeDtypeStruct((b, vw, n), BF16)),
        grid=(b, n // tm),
        in_specs=[pl.BlockSpec((1, tm, d), lambda bi, ti: (bi, ti, 0)),
                  pl.BlockSpec((1, 1, d), lambda bi, ti: (bi, 0, 0)),
                  pl.BlockSpec((1, 1, d), lambda bi, ti: (bi, 0, 0)),
                  full(win), full(lng), full(lnb), full(ws), full(bsb),
                  full(qn), full(wq), full(kvn), full(wk), full(wv),
                  pl.BlockSpec((tm, HEAD_SLOT), lambda bi, ti: (ti, 0)),
                  pl.BlockSpec((tm, HEAD_SLOT), lambda bi, ti: (ti, 0)),
                  pl.BlockSpec((MLA_ROPE, tm), lambda bi, ti: (0, ti)),
                  pl.BlockSpec((MLA_ROPE, tm), lambda bi, ti: (0, ti))],
        out_specs=(pl.BlockSpec((1, tm, GMLP_WIDTH), lambda bi, ti: (bi, ti, 0)),
                   pl.BlockSpec((1, hw, tm), lambda bi, ti: (bi, 0, ti)),
                   pl.BlockSpec((1, tm, hw), lambda bi, ti: (bi, ti, 0)),
                   pl.BlockSpec((1, vw, tm), lambda bi, ti: (bi, 0, ti))),
        compiler_params=_cparams(("parallel", "parallel")),
        name="proj0",
    )(x, sc, sh, win, lng, lnb, ws, bsb, qn, wq, kvn, wk, wv, ck, sk, cq, sq)


def _proj1_kernel(x_ref, sc_ref, sh_ref, wq_ref, wk_ref, wv_ref, ck_ref, sk_ref, cq_ref, sq_ref,
                  q_ref, k_ref, v_ref, *, q_scale):
    h = (x_ref[0] * (1.0 + sc_ref[0]) + sh_ref[0]).astype(BF16)
    n_sub = 2 * DIFF_HEADS
    hd = DIFF_HEAD_DIM
    hr = hd // 2
    qt = _dot_nt(wq_ref[...], h) * q_scale
    cq_t = cq_ref[...]
    sq_t = sq_ref[...]
    for s in range(n_sub):
        r0 = s * hd
        x1 = qt[r0:r0 + hr]
        x2 = qt[r0 + hr:r0 + hd]
        swp = jnp.concatenate([x2, x1], axis=0)
        q_ref[0, r0:r0 + hd, :] = (qt[r0:r0 + hd] * cq_t + swp * sq_t).astype(BF16)
    k = _dot(h, wk_ref[...])
    width = k.shape[1]
    lane = lax.broadcasted_iota(jnp.int32, k.shape, 1)
    partner = jnp.where((lane % hd) < hr, pltpu.roll(k, width - hr, 1), pltpu.roll(k, hr, 1))
    reps = width // HEAD_SLOT
    k_ref[0] = (k * jnp.tile(ck_ref[...], (1, reps)) + partner * jnp.tile(sk_ref[...], (1, reps))).astype(BF16)
    v_ref[0] = _dot_nt(wv_ref[...], h).astype(BF16)


def _proj1(x, sc, sh, wts, tabs):
    b, n, d = x.shape
    tm = _tile(n, 512)
    wq, wk, wv = wts
    ck, sk, cq, sq = tabs
    dw = wq.shape[0]
    full = lambda a: pl.BlockSpec(a.shape, lambda bi, ti: (0,) * a.ndim)
    q_scale = DIFF_HEAD_DIM ** -0.5 * LOG2E
    return pl.pallas_call(
        functools.partial(_proj1_kernel, q_scale=q_scale),
        out_shape=(jax.ShapeDtypeStruct((b, dw, n), BF16),
                   jax.ShapeDtypeStruct((b, n, dw), BF16),
                   jax.ShapeDtypeStruct((b, dw, n), BF16)),
        grid=(b, n // tm),
        in_specs=[pl.BlockSpec((1, tm, d), lambda bi, ti: (bi, ti, 0)),
                  pl.BlockSpec((1, 1, d), lambda bi, ti: (bi, 0, 0)),
                  pl.BlockSpec((1, 1, d), lambda bi, ti: (bi, 0, 0)),
                  full(wq), full(wk), full(wv),
                  pl.BlockSpec((tm, HEAD_SLOT), lambda bi, ti: (ti, 0)),
                  pl.BlockSpec((tm, HEAD_SLOT), lambda bi, ti: (ti, 0)),
                  pl.BlockSpec((DIFF_HEAD_DIM, tm), lambda bi, ti: (0, ti)),
                  pl.BlockSpec((DIFF_HEAD_DIM, tm), lambda bi, ti: (0, ti))],
        out_specs=(pl.BlockSpec((1, dw, tm), lambda bi, ti: (bi, 0, ti)),
                   pl.BlockSpec((1, tm, dw), lambda bi, ti: (bi, ti, 0)),
                   pl.BlockSpec((1, dw, tm), lambda bi, ti: (bi, 0, ti))),
        compiler_params=_cparams(("parallel", "parallel")),
        name="proj1",
    )(x, sc, sh, wq, wk, wv, ck, sk, cq, sq)


SHIFT_OK_LO = 2.0 ** -85
SHIFT_OK_HI = 2.0 ** 100
ATTN_TQ = 1024
ATTN_UNROLL = 16
ATTN_STATIC_CHUNKS = 8


def _key_norm_max(kv_refs, kmax_ref, *, tk, n_sub):
    sub_w = HEAD_SLOT // n_sub
    row = lax.broadcasted_iota(jnp.int32, (8, HEAD_SLOT), 0)
    lane = lax.broadcasted_iota(jnp.int32, (8, HEAD_SLOT), 1)
    sel = jnp.where(lane // sub_w == row, 1.0, 0.0).astype(BF16)
    best = None
    for k_ref, _ in kv_refs:
        nchunk = k_ref.shape[1] // tk
        group = max(g for g in range(1, 12) if nchunk % g == 0)
        rows = group * tk

        def body(c, mx, k_ref=k_ref, rows=rows):
            off = pl.multiple_of(c * rows, rows)
            kc = k_ref[0, pl.ds(off, rows), :].astype(F32)
            return jnp.maximum(mx, _dot_nt(sel, (kc * kc).astype(BF16)))

        mx = lax.fori_loop(0, nchunk // group, body, jnp.zeros((8, rows), F32))
        mx = jnp.max(mx, axis=1, keepdims=True)
        best = mx if best is None else jnp.maximum(best, mx)
    kmax_ref[...] = jnp.broadcast_to(best, kmax_ref.shape)


def _sweep_bounded(qs, shifts, kv_refs, s_ref, *, tk, unroll):
    tq = qs[0].shape[1]
    dv = kv_refs[0][1].shape[1]
    chunks = [(si, c * tk) for si, (k_ref, _) in enumerate(kv_refs) for c in range(k_ref.shape[1] // tk)]

    def aligned(off):
        return off if isinstance(off, int) else pl.multiple_of(off, tk)

    def scores(si, off, buf):
        kc = kv_refs[si][0][0, pl.ds(aligned(off), tk), :]
        for r, q in enumerate(qs):
            s_ref[buf, r] = _dot(kc, q)

    def consume(si, off, buf, carry):
        vc = kv_refs[si][1][0, :, pl.ds(aligned(off), tk)]
        nxt = []
        for r, ((l8, acc), m) in enumerate(zip(carry, shifts)):
            p = jnp.exp2(s_ref[buf, r] - m)
            l8 = l8 + jnp.sum(p.reshape(tk // 8, 8, tq), axis=0)
            nxt.append((l8, acc + _dot(vc, p.astype(BF16))))
        return tuple(nxt)

    def body(i, carry):
        for u in range(unroll):
            off = pl.multiple_of((i * unroll + u) * tk, tk)
            scores(0, off + tk, (u + 1) % 2)
            carry = consume(0, off, u % 2, carry)
        return carry

    carry = tuple((jnp.zeros((8, tq), F32), jnp.zeros((dv, tq), F32)) for _ in qs)
    scores(*chunks[0], 0)
    n_first = kv_refs[0][0].shape[1] // tk
    n_loop = 0 if len(chunks) <= ATTN_STATIC_CHUNKS else (n_first - 1) // unroll * unroll
    if n_loop:
        carry = lax.fori_loop(0, n_loop // unroll, body, carry)
    for c in range(n_loop, len(chunks)):
        if c + 1 < len(chunks):
            scores(*chunks[c + 1], (c + 1) % 2)
        carry = consume(*chunks[c], c % 2, carry)
    return [(jnp.sum(l8, axis=0, keepdims=True), acc) for l8, acc in carry]


def _sweep_running_max(qs, kv_refs, *, tk):
    tq = qs[0].shape[1]
    dv = kv_refs[0][1].shape[1]
    carry = tuple((jnp.full((1, tq), NEG_BIG, F32), jnp.zeros((1, tq), F32), jnp.zeros((dv, tq), F32)) for _ in qs)
    for k_ref, v_ref in kv_refs:

        def body(c, carry, k_ref=k_ref, v_ref=v_ref):
            off = pl.multiple_of(c * tk, tk)
            kc = k_ref[0, pl.ds(off, tk), :]
            vc = v_ref[0, :, pl.ds(off, tk)]
            nxt = []
            for (m, l, acc), q in zip(carry, qs):
                s = _dot(kc, q)
                mn = jnp.maximum(m, jnp.max(s, axis=0, keepdims=True))
                alpha = jnp.exp2(m - mn)
                p = jnp.exp2(s - mn)
                nxt.append((mn, alpha * l + jnp.sum(p, axis=0, keepdims=True),
                            alpha * acc + _dot(vc, p.astype(BF16))))
            return tuple(nxt)

        carry = lax.fori_loop(0, k_ref.shape[1] // tk, body, carry)
    return [(l, acc) for _, l, acc in carry]


def _attn_kernel(*refs, tk, unroll, n_sub, n_src, finalize):
    q_ref = refs[0]
    kv_refs = [(refs[1 + 2 * i], refs[2 + 2 * i]) for i in range(n_src)]
    extra = refs[1 + 2 * n_src:-3]
    o_ref, kmax_ref, s_ref = refs[-3:]

    @pl.when(pl.program_id(2) == 0)
    def _():
        _key_norm_max(kv_refs, kmax_ref, tk=tk, n_sub=n_sub)

    q = q_ref[0]
    sub_w = HEAD_SLOT // n_sub
    row = lax.broadcasted_iota(jnp.int32, q.shape, 0)
    qf = q.astype(F32)
    qs, shifts = [], []
    for r in range(n_sub):
        mine = (row // sub_w) == r
        qs.append(jnp.where(mine, q, jnp.zeros_like(q)) if n_sub > 1 else q)
        qn2 = jnp.sum(jnp.where(mine, qf * qf, 0.0), axis=0, keepdims=True)
        shifts.append(jnp.sqrt(qn2 * kmax_ref[r:r + 1, 0:1]))

    res = _sweep_bounded(qs, shifts, kv_refs, s_ref, tk=tk, unroll=unroll)
    o_ref[0] = finalize(res, extra).astype(BF16)
    ok = None
    for l, _ in res:
        ok_r = (jnp.min(l) >= SHIFT_OK_LO) & (jnp.max(l) <= SHIFT_OK_HI)
        ok = ok_r if ok is None else ok & ok_r

    @pl.when(jnp.logical_not(ok))
    def _():
        o_ref[0] = finalize(_sweep_running_max(qs, kv_refs, tk=tk), extra).astype(BF16)


def _finalize_plain(res, extra):
    (l, acc), = res
    return acc / l


def _finalize_diff(res, extra, *, lam_init):
    lq1_ref, lk1_ref, lq2_ref, lk2_ref, g_ref = extra
    (l1, a1), (l2, a2) = res
    lam = (jnp.exp(jnp.sum(lq1_ref[...] * lk1_ref[...], axis=-1, keepdims=True))
           - jnp.exp(jnp.sum(lq2_ref[...] * lk2_ref[...], axis=-1, keepdims=True)) + lam_init)
    o = a1 / l1 - lam * (a2 / l2)
    return o * lax.rsqrt(jnp.mean(o * o, axis=0, keepdims=True) + RMS_EPS) * g_ref[...] * (1.0 - lam_init)


def _attention(qt, kv, extra=(), *, dv, n_sub, finalize, name):
    b, hw, n = qt.shape
    heads = hw // HEAD_SLOT
    tq = _tile(n, ATTN_TQ)
    tk = 256 if all(k.shape[1] % 256 == 0 for k, _ in kv) else 128
    full = lambda a: pl.BlockSpec(a.shape, lambda bi, hi, qi: (0,) * a.ndim)
    kv_specs, kv_args = [], []
    for k, vt in kv:
        t = k.shape[1]
        kv_specs += [pl.BlockSpec((1, t, HEAD_SLOT), lambda bi, hi, qi: (bi, 0, hi)),
                     pl.BlockSpec((1, dv, t), lambda bi, hi, qi: (bi, hi, 0))]
        kv_args += [k, vt]
    return pl.pallas_call(
        functools.partial(_attn_kernel, tk=tk, unroll=ATTN_UNROLL, n_sub=n_sub, n_src=len(kv), finalize=finalize),
        out_shape=jax.ShapeDtypeStruct((b, heads * dv, n), BF16),
        grid=(b, heads, n // tq),
        in_specs=[pl.BlockSpec((1, HEAD_SLOT, tq), lambda bi, hi, qi: (bi, hi, qi))] + kv_specs
        + [full(a) for a in extra],
        out_specs=pl.BlockSpec((1, dv, tq), lambda bi, hi, qi: (bi, hi, qi)),
        scratch_shapes=[pltpu.VMEM((8, HEAD_SLOT), F32), pltpu.VMEM((2, n_sub, tk, tq), F32)],
        compiler_params=_cparams(("parallel", "parallel", "arbitrary")),
        name=name,
    )(qt, *kv_args, *extra)


def _out_kernel(*refs, n_rowmajor):
    (x_ref, g_ref, lng_ref, lnb_ref, scf_ref, shf_ref, rt_ref), rest = refs[:7], refs[7:]
    x1_ref, aff_ref, hp_ref = rest[-3:]
    ops = rest[:-3]
    m = None
    for i in range(len(ops) // 2):
        a = ops[2 * i][0]
        w = ops[2 * i + 1][...]
        part = _dot(a, w) if i < n_rowmajor else _dot_tn(a, w)
        m = part if m is None else m + part
    x1 = _ln(DEEPNORM_ALPHA * x_ref[0] + (1.0 + g_ref[0]) * m, lng_ref[...], lnb_ref[...])
    x1_ref[0] = x1
    hf = (x1 * (1.0 + scf_ref[0]) + shf_ref[0]).astype(BF16)
    lg = _dot_nt(rt_ref[...], hf)
    e = jnp.exp(lg - jnp.max(lg, axis=0, keepdims=True))
    aff_ref[0] = e / jnp.sum(e, axis=0, keepdims=True)
    dh = hf.shape[1] // 2
    lo = pltpu.bitcast(hf[:, :dh].astype(F32), jnp.uint32)
    hi = pltpu.bitcast(hf[:, dh:].astype(F32), jnp.uint32)
    hp_ref[0] = _to_tiled(lax.shift_right_logical(lo, jnp.uint32(16)) | (hi & jnp.uint32(0xFFFF0000)))


def _mixer_out(x, g, lng, lnb, scf, shf, router_t, rowmajor_ops, chanmajor_ops):
    b, n, d = x.shape
    e = router_t.shape[0]
    tm = _tile(n, 512)
    vec = pl.BlockSpec((1, 1, d), lambda bi, ti: (bi, 0, 0))
    full = lambda a: pl.BlockSpec(a.shape, lambda bi, ti: (0,) * a.ndim)
    args = [x, g, lng, lnb, scf, shf, router_t]
    specs = [pl.BlockSpec((1, tm, d), lambda bi, ti: (bi, ti, 0)), vec, full(lng), full(lnb), vec, vec,
             full(router_t)]
    for a, w in rowmajor_ops:
        args += [a, w]
        specs += [pl.BlockSpec((1, tm, a.shape[2]), lambda bi, ti: (bi, ti, 0)), full(w)]
    for a, w in chanmajor_ops:
        args += [a, w]
        specs += [pl.BlockSpec((1, a.shape[1], tm), lambda bi, ti: (bi, 0, ti)), full(w)]
    return pl.pallas_call(
        functools.partial(_out_kernel, n_rowmajor=len(rowmajor_ops)),
        out_shape=(jax.ShapeDtypeStruct((b, n, d), F32), jax.ShapeDtypeStruct((b, e, n), F32),
                   jax.ShapeDtypeStruct((b, n * (d // 2) // LANES, LANES), jnp.uint32)),
        grid=(b, n // tm),
        in_specs=specs,
        out_specs=(pl.BlockSpec((1, tm, d), lambda bi, ti: (bi, ti, 0)),
                   pl.BlockSpec((1, e, tm), lambda bi, ti: (bi, 0, ti)),
                   pl.BlockSpec((1, tm * (d // 2) // LANES, LANES), lambda bi, ti: (bi, ti, 0))),
        compiler_params=_cparams(("parallel", "parallel")),
        name="mixer_out",
    )(*args)


ROW_UNROLL = 8
FFN_ROWS = 512
TOKEN_SPLIT = 4096


def _ffn_kernel(src_ref, nxt_ref, hp_ref, g_ref, wg_ref, wu_ref, wd_ref, y_ref, rows_ref, xs_ref, *, prefetch):
    cap = y_ref.shape[2]
    nt = rows_ref.shape[0] // cap
    dh = nt * LANES

    def gather(idx_ref, j, dst):
        rows_ref[pl.ds(dst, nt, stride=SUBLANES), :] = hp_ref[0, pl.ds(idx_ref[0, 0, j], nt, stride=SUBLANES), :]

    def gather_loop():
        def body(jb, carry):
            for u in range(ROW_UNROLL):
                gather(src_ref, jb * ROW_UNROLL + u, jb * (ROW_UNROLL * nt) + u)
            return carry

        lax.fori_loop(0, cap // ROW_UNROLL, body, 0)

    if prefetch:
        pl.when(pl.program_id(1) == 0)(gather_loop)
    else:
        gather_loop()

    blk = min(cap, FFN_ROWS)
    for c0 in range(0, cap, blk):
        w = _from_tiled(rows_ref[c0 * nt:(c0 + blk) * nt], dh)
        lo = pltpu.bitcast(lax.shift_left(w, jnp.uint32(16)), F32)
        hi = pltpu.bitcast(w & jnp.uint32(0xFFFF0000), F32)
        xs_ref[c0:c0 + blk] = jnp.concatenate([lo, hi], axis=1).astype(BF16)

    if prefetch:
        for j in range(cap):
            gather(nxt_ref, j, _tiled_row(j, dh))

    for c0 in range(0, cap, blk):
        xs = xs_ref[c0:c0 + blk]
        gate = _dot(xs, wg_ref[0])
        hid = gate / (1.0 + jnp.exp(-gate)) * _dot(xs, wu_ref[0])
        y_ref[0, 0, c0:c0 + blk] = (_dot(hid.astype(BF16), wd_ref[0]) * g_ref[0, c0:c0 + blk]).astype(BF16)


def _activations_outweigh(activation_bytes, weight_bytes):
    return activation_bytes > weight_bytes


def _moe_ffn(hp, src3, gates3, wg, wu, wd, cap):
    b, hp_rows, _ = hp.shape
    n_exp, d, ff = wg.shape
    nt = d // 2 // LANES
    sample_major = _activations_outweigh(hp_rows * LANES * 4, 3 * d * ff * 2)
    if sample_major:
        grid, be = (b, n_exp), (lambda g0, g1: (g0, g1))
        hp_spec = pl.BlockSpec((1, hp_rows, LANES), lambda g0, g1: (g0, 0, 0), pipeline_mode=pl.Buffered(1))
    else:
        grid, be = (n_exp, b), (lambda g0, g1: (g1, g0))
        hp_spec = pl.BlockSpec((1, hp_rows, LANES), lambda g0, g1: (g1, 0, 0))
    slot = lambda g0, g1: (be(g0, g1)[0] * n_exp + be(g0, g1)[1], 0, 0)
    weight = lambda g0, g1: (be(g0, g1)[1], 0, 0)
    nxt = lambda g0, g1: (be(g0, g1)[0] * n_exp + jnp.minimum(be(g0, g1)[1] + 1, n_exp - 1), 0, 0)
    return pl.pallas_call(
        functools.partial(_ffn_kernel, prefetch=sample_major),
        out_shape=jax.ShapeDtypeStruct((b, n_exp, cap, d), BF16),
        grid=grid,
        in_specs=[pl.BlockSpec((1, 1, cap), slot, memory_space=pltpu.SMEM),
                  pl.BlockSpec((1, 1, cap), nxt, memory_space=pltpu.SMEM),
                  hp_spec,
                  pl.BlockSpec((1, cap, 1), slot),
                  pl.BlockSpec((1, d, ff), weight),
                  pl.BlockSpec((1, d, ff), weight),
                  pl.BlockSpec((1, ff, d), weight)],
        out_specs=pl.BlockSpec((1, 1, cap, d), lambda g0, g1: be(g0, g1) + (0, 0)),
        scratch_shapes=[pltpu.VMEM((cap * nt, LANES), jnp.uint32), pltpu.VMEM((cap, d), BF16)],
        compiler_params=_cparams(("parallel", "arbitrary")),
        name="moe_ffn",
    )(src3, src3, hp, gates3, wg, wu, wd)


def _scatter_kernel(dst_ref, bnd_ref, y_ref, f_ref, rows_ref):
    d = y_ref.shape[3]
    nt = d // LANES

    @pl.when(pl.program_id(2) == 0)
    def _():
        f_ref[...] = jnp.zeros_like(f_ref)

    rows_ref[...] = _to_tiled(y_ref[0, 0].astype(F32))
    lo = bnd_ref[0, 0, pl.program_id(1)]
    hi = bnd_ref[0, 0, pl.program_id(1) + 1]

    def row(ref, start):
        return ref[pl.ds(start, nt, stride=SUBLANES), :]

    def one(j, carry):
        dst = dst_ref[0, 0, j]
        f_ref[0, pl.ds(dst, nt, stride=SUBLANES), :] = row(f_ref.at[0], dst) + row(rows_ref, _tiled_row(j, d))
        return carry

    def batch(jb, carry):
        dsts = [dst_ref[0, 0, jb * ROW_UNROLL + u] for u in range(ROW_UNROLL)]
        sums = [row(f_ref.at[0], dsts[u]) + row(rows_ref, jb * (ROW_UNROLL * nt) + u) for u in range(ROW_UNROLL)]
        for u in range(ROW_UNROLL):
            f_ref[0, pl.ds(dsts[u], nt, stride=SUBLANES), :] = sums[u]
        return carry

    head_end = jnp.minimum((lo + ROW_UNROLL - 1) // ROW_UNROLL * ROW_UNROLL, hi)
    tail_start = jnp.maximum(head_end, hi // ROW_UNROLL * ROW_UNROLL)
    lax.fori_loop(lo, head_end, one, 0)
    lax.fori_loop(head_end // ROW_UNROLL, tail_start // ROW_UNROLL, batch, 0)
    lax.fori_loop(tail_start, hi, one, 0)


def _moe_scatter(y, dst3, bounds3, n):
    b, n_exp, cap, d = y.shape
    nt = _tile(n, TOKEN_SPLIT)
    lt = d // LANES
    return pl.pallas_call(
        _scatter_kernel,
        out_shape=jax.ShapeDtypeStruct((b, n * lt, LANES), F32),
        grid=(b, n // nt, n_exp),
        in_specs=[pl.BlockSpec((1, 1, cap), lambda bi, ki, ei: (bi * n_exp + ei, 0, 0), memory_space=pltpu.SMEM),
                  pl.BlockSpec((1, 1, n // nt + 1), lambda bi, ki, ei: (bi * n_exp + ei, 0, 0),
                               memory_space=pltpu.SMEM),
                  pl.BlockSpec((1, 1, cap, d), lambda bi, ki, ei: (bi, ei, 0, 0))],
        out_specs=pl.BlockSpec((1, nt * lt, LANES), lambda bi, ki, ei: (bi, ki, 0)),
        scratch_shapes=[pltpu.VMEM((cap * lt, LANES), F32)],
        compiler_params=_cparams(("parallel", "parallel", "arbitrary")),
        name="moe_scatter",
    )(dst3, bounds3, y)


def _post_kernel(x_ref, f_ref, g_ref, lng_ref, lnb_ref, o_ref):
    f = _from_tiled(f_ref[0], x_ref.shape[2])
    o_ref[0] = _ln(DEEPNORM_ALPHA * x_ref[0] + (1.0 + g_ref[0]) * f, lng_ref[...], lnb_ref[...])


def _post(x1, f_tiled, g, lng, lnb):
    b, n, d = x1.shape
    tm = _tile(n, 512)
    blk = pl.BlockSpec((1, tm, d), lambda bi, ti: (bi, ti, 0))
    f_blk = pl.BlockSpec((1, tm * d // LANES, LANES), lambda bi, ti: (bi, ti, 0))
    full = lambda a: pl.BlockSpec(a.shape, lambda bi, ti: (0,) * a.ndim)
    return pl.pallas_call(
        _post_kernel,
        out_shape=jax.ShapeDtypeStruct((b, n, d), F32),
        grid=(b, n // tm),
        in_specs=[blk, f_blk, pl.BlockSpec((1, 1, d), lambda bi, ti: (bi, 0, 0)), full(lng), full(lnb)],
        out_specs=blk,
        compiler_params=_cparams(("parallel", "parallel")),
        name="ffn_post",
    )(x1, f_tiled, g, lng, lnb)


def _select_kernel(aff_ref, idx_ref, gate_ref, *, cap, n_exp):
    a = aff_ref[0]
    er, lanes = a.shape
    r_n = er // n_exp
    e_pad = 16
    bits = pltpu.bitcast(a, jnp.int32)

    member = (lax.broadcasted_iota(jnp.int32, (e_pad, er), 1) // r_n
              == lax.broadcasted_iota(jnp.int32, (e_pad, er), 0))
    gs = jnp.where(member, 1.0, 0.0).astype(BF16)
    member_t = (lax.broadcasted_iota(jnp.int32, (er, e_pad), 0) // r_n
                == lax.broadcasted_iota(jnp.int32, (er, e_pad), 1))
    gst = jnp.where(member_t, 1.0, 0.0).astype(BF16)
    ri = lax.broadcasted_iota(jnp.int32, (er, er), 0)
    rj = lax.broadcasted_iota(jnp.int32, (er, er), 1)
    rows_before = jnp.where((ri // r_n == rj // r_n) & (rj < ri), 1.0, 0.0).astype(BF16)
    li = lax.broadcasted_iota(jnp.int32, (lanes, lanes), 0)
    lj = lax.broadcasted_iota(jnp.int32, (lanes, lanes), 1)
    lanes_upto = jnp.where(li <= lj, 1.0, 0.0).astype(BF16)

    def expert_total(m):
        per_lane = _dot(gs, m.astype(BF16))
        return jnp.broadcast_to(jnp.sum(per_lane, axis=1, keepdims=True), per_lane.shape)

    def to_rows(ev):
        if r_n % SUBLANES == 0 and n_exp == e_pad:
            return jnp.broadcast_to(ev[:, None, :], (e_pad, r_n, lanes)).reshape(er, lanes)
        hi = jnp.floor(ev * (1.0 / 128.0))
        return 128.0 * _dot(gst, hi.astype(BF16)) + _dot(gst, (ev - 128.0 * hi).astype(BF16))

    def search(i, t):
        cand = t | lax.shift_left(jnp.int32(1), 30 - i)
        cnt = expert_total(jnp.where(bits >= cand, 1.0, 0.0))
        return jnp.where(to_rows(jnp.where(cnt >= cap, 1.0, 0.0)) > 0.5, cand, t)

    t = lax.fori_loop(0, 31, search, jnp.zeros((er, lanes), jnp.int32))
    gt = jnp.where(bits > t, 1.0, 0.0)
    eq = jnp.where(bits == t, 1.0, 0.0)

    def prefix(m):
        rc = _dot(m.astype(BF16), lanes_upto)
        off = _dot(rows_before, jnp.broadcast_to(rc[:, lanes - 1:lanes], m.shape).astype(BF16))
        return rc, off

    rc_eq, off_eq = prefix(eq)
    need = cap - to_rows(expert_total(gt))
    sel = jnp.maximum(gt, eq * jnp.where(off_eq + rc_eq - eq < need, 1.0, 0.0))
    rc, off = prefix(sel)
    row_end = off + jnp.broadcast_to(rc[:, lanes - 1:lanes], rc.shape)

    slot = lax.broadcasted_iota(jnp.int32, (cap, lanes), 0).astype(F32) + 1.0
    lane_c = lax.broadcasted_iota(jnp.int32, (cap, lanes), 1)
    lane_cf = lane_c.astype(F32)
    diag = lax.broadcasted_iota(jnp.int32, (r_n, lanes), 0) == lax.broadcasted_iota(jnp.int32, (r_n, lanes), 1)
    lane_r = lax.broadcasted_iota(jnp.int32, (1, lanes), 1)
    pad = jnp.zeros((lanes - r_n, lanes), F32)

    def table(m):
        return (jnp.concatenate([m, pad], axis=0) if r_n < lanes else m).astype(BF16)

    all_ones = jnp.ones((lanes, lanes), BF16)
    idx_all = jnp.zeros((cap, lanes), jnp.int32)
    gate_all = jnp.zeros((cap, lanes), F32)
    for e in range(n_exp):
        r0 = e * r_n
        end_lane = jnp.sum(jnp.where(diag, row_end[r0:r0 + r_n], 0.0), axis=0, keepdims=True)
        end_lane = jnp.where(lane_r < r_n, end_lane, 3.0e38)
        row_j = jnp.sum(jnp.where(end_lane < slot, 1.0, 0.0), axis=1, keepdims=True)
        oh = jnp.where(lane_cf == row_j, 1.0, 0.0).astype(BF16)
        cum_e = off[r0:r0 + r_n] + rc[r0:r0 + r_n]
        cum_hi = jnp.floor(cum_e * (1.0 / 128.0))
        cum_2 = _dot(oh, jnp.concatenate([table(cum_hi), table(cum_e - 128.0 * cum_hi)], axis=1))
        cum_j = 128.0 * cum_2[:, :lanes] + cum_2[:, lanes:]
        lane_j = _dot(jnp.where(cum_j < slot, 1.0, 0.0).astype(BF16), all_ones)
        a_e = a[r0:r0 + r_n]
        a1 = a_e.astype(BF16).astype(F32)
        a2 = (a_e - a1).astype(BF16).astype(F32)
        a3 = a_e - a1 - a2
        aff_2 = _dot(oh, jnp.concatenate([table(a1), table(a2)], axis=1))
        aff_j = aff_2[:, :lanes] + aff_2[:, lanes:] + _dot(oh, table(a3))
        gate_j = jnp.sum(jnp.where(lane_cf == lane_j, aff_j, 0.0), axis=1, keepdims=True)
        idx_j = (row_j * float(lanes) + lane_j).astype(jnp.int32)
        idx_all = jnp.where(lane_c == e, idx_j, idx_all)
        gate_all = jnp.where(lane_c == e, gate_j, gate_all)
    idx_ref[0] = idx_all
    gate_ref[0] = gate_all


def _moe_select(aff_t, cap):
    b, n_exp, n = aff_t.shape
    lanes = HEAD_SLOT
    er = n_exp * (n // lanes)
    idx, gates = pl.pallas_call(
        functools.partial(_select_kernel, cap=cap, n_exp=n_exp),
        out_shape=(jax.ShapeDtypeStruct((b, cap, lanes), jnp.int32), jax.ShapeDtypeStruct((b, cap, lanes), F32)),
        grid=(b,),
        in_specs=[pl.BlockSpec((1, er, lanes), lambda bi: (bi, 0, 0))],
        out_specs=(pl.BlockSpec((1, cap, lanes), lambda bi: (bi, 0, 0)),
                   pl.BlockSpec((1, cap, lanes), lambda bi: (bi, 0, 0))),
        compiler_params=_cparams(("parallel",)),
        name="moe_select",
    )(aff_t.reshape(b, er, lanes))
    to_expert_major = lambda m: jnp.swapaxes(m[:, :, :n_exp], 1, 2)
    return to_expert_major(gates), to_expert_major(idx)


def _moe(x1, aff_t, hp, gf, wg, wu, wd, lng, lnb):
    b, n, d = x1.shape
    n_exp = aff_t.shape[1]
    cap = CAPACITY_FACTOR * n // n_exp
    gates, idx = _moe_select(aff_t, cap)
    gates3 = gates.reshape(b * n_exp, cap, 1)
    nt = _tile(n, TOKEN_SPLIT)
    edges = jnp.arange(0, n + 1, nt, dtype=jnp.int32)
    bounds3 = jnp.sum(idx[..., None] < edges, axis=2, dtype=jnp.int32).reshape(b * n_exp, 1, n // nt + 1)
    src3 = _tiled_row(idx, d // 2).reshape(b * n_exp, 1, cap)
    dst3 = _tiled_row(idx % nt, d).reshape(b * n_exp, 1, cap)
    y = _moe_ffn(hp, src3, gates3, wg, wu, wd, cap)
    f_tiled = _moe_scatter(y, dst3, bounds3, n)
    return _post(x1, f_tiled, gf, lng, lnb)


def _rope_angles(n_tokens, dim):
    n_rows = n_tokens // GRID_W
    row = jnp.repeat(jnp.arange(n_rows, dtype=F32), GRID_W)
    col = jnp.tile(jnp.arange(GRID_W, dtype=F32), n_rows)
    n_freq = dim // 4
    inv_freq = ROPE_BASE ** (-jnp.arange(n_freq, dtype=F32) / n_freq)
    ang = jnp.concatenate([row[:, None] * inv_freq, col[:, None] * inv_freq], -1)
    return jnp.cos(ang), jnp.sin(ang)


def _rope_tables(n_tokens, dim, lane_offset, identity):
    half = dim // 2
    if identity:
        cos = jnp.ones((n_tokens, half), F32)
        sin = jnp.zeros((n_tokens, half), F32)
    else:
        cos, sin = _rope_angles(n_tokens, dim)
    c2 = jnp.concatenate([cos, cos], -1)
    s2 = jnp.concatenate([-sin, sin], -1)
    reps = (HEAD_SLOT - lane_offset) // dim if lane_offset == 0 else 1
    ck = jnp.zeros((n_tokens, HEAD_SLOT), F32)
    sk = jnp.zeros((n_tokens, HEAD_SLOT), F32)
    for r in range(reps):
        ck = ck.at[:, lane_offset + r * dim:lane_offset + (r + 1) * dim].set(c2)
        sk = sk.at[:, lane_offset + r * dim:lane_offset + (r + 1) * dim].set(s2)
    return ck, sk, c2.T, s2.T


def _deinterleave(w, axis):
    return jnp.concatenate([jnp.take(w, jnp.arange(0, w.shape[axis], 2), axis=axis),
                            jnp.take(w, jnp.arange(1, w.shape[axis], 2), axis=axis)], axis=axis)


def _prep_layer0(w_in, ln_g, ln_b, ws, bs, q_norm, w_uq, kv_norm, w_ukv):
    d = w_in.shape[0]
    w = GMLP_WIDTH
    o_kr = 2 * w + MLA_Q_RANK + MLA_KV_RANK
    kr = _deinterleave(w_in[:, o_kr:o_kr + MLA_ROPE], 1)
    kr_sw = jnp.concatenate([kr[:, MLA_ROPE // 2:], kr[:, :MLA_ROPE // 2]], 1)
    slot = lambda m: jnp.zeros((d, HEAD_SLOT), F32).at[:, MLA_NOPE:MLA_NOPE + MLA_ROPE].set(m)
    win = jnp.concatenate([w_in[:, :o_kr], slot(kr), slot(kr_sw)], 1).astype(BF16)

    qd = MLA_NOPE + MLA_ROPE
    wq = w_uq.reshape(MLA_Q_RANK, MLA_HEADS, qd)
    wq = jnp.concatenate([wq[..., :MLA_NOPE], _deinterleave(wq[..., MLA_NOPE:], 2),
                          jnp.zeros((MLA_Q_RANK, MLA_HEADS, HEAD_SLOT - qd), F32)], -1)
    wq_t = wq.reshape(MLA_Q_RANK, MLA_HEADS * HEAD_SLOT).T.astype(BF16)

    wkv = w_ukv.reshape(MLA_KV_RANK, MLA_HEADS, MLA_NOPE + MLA_V)
    wk = jnp.concatenate([wkv[..., :MLA_NOPE], jnp.zeros((MLA_KV_RANK, MLA_HEADS, HEAD_SLOT - MLA_NOPE), F32)], -1)
    wk = wk.reshape(MLA_KV_RANK, MLA_HEADS * HEAD_SLOT).astype(BF16)
    wv_t = wkv[..., MLA_NOPE:].reshape(MLA_KV_RANK, MLA_HEADS * MLA_V).T.astype(BF16)

    bsb = jnp.broadcast_to(bs[:, :, None], (GMLP_GROUPS, CHUNK, GMLP_GROUP_CH))
    return (win, ln_g.reshape(1, -1), ln_b.reshape(1, -1), ws.astype(BF16), bsb,
            q_norm.reshape(1, -1), wq_t, kv_norm.reshape(1, -1), wk, wv_t)


def _prep_layer1(w_in):
    d = w_in.shape[0]
    dw = 2 * DIFF_HEADS * DIFF_HEAD_DIM
    perm = lambda m: _deinterleave(m.reshape(d, 2 * DIFF_HEADS, DIFF_HEAD_DIM), 2).reshape(d, dw)
    wq_t = perm(w_in[:, :dw]).T.astype(BF16)
    wk = perm(w_in[:, dw:2 * dw]).astype(BF16)
    wv_t = w_in[:, 2 * dw:].T.astype(BF16)
    return wq_t, wk, wv_t


def _split6(m_row):
    return [v[:, None, :] for v in jnp.split(m_row, 6, axis=-1)]


def kernel(x, c, ctx, c_ctx, w_mod_0, b_mod_0, w_in_0, gmlp_ln_g_0, gmlp_ln_b_0, gmlp_ws_0, gmlp_bs_0, mla_q_norm_0, mla_w_uq_0, mla_kv_norm_0, mla_w_ukv_0, w_out_0, ln_mix_g_0, ln_mix_b_0, router_0, w_gate_0, w_up_0, w_down_0, ln_ffn_g_0, ln_ffn_b_0, w_mod_1, b_mod_1, w_in_1, lambda_q1_1, lambda_k1_1, lambda_q2_1, lambda_k2_1, subln_g_1, w_out_1, ln_mix_g_1, ln_mix_b_1, router_1, w_gate_1, w_up_1, w_down_1, ln_ffn_g_1, ln_ffn_b_1):
    b, n, d = x.shape
    m_ctx = ctx.shape[1]
    row = lambda v: v.reshape(1, -1)

    cond = jnp.concatenate([c, c_ctx[None, :], jnp.zeros((-(b + 1) % 8, d), F32)], 0)
    mod0 = _modulation(cond, w_mod_0, b_mod_0)
    mod1 = _modulation(cond, w_mod_1, b_mod_1)
    sh_a0, sc_a0, g_a0, sh_f0, sc_f0, g_f0 = _split6(mod0[:b])
    csh_a0, csc_a0, cg_a0, csh_f0, csc_f0, cg_f0 = [jnp.broadcast_to(v, (b, 1, d)) for v in _split6(mod0[b:b + 1])]
    sh_a1, sc_a1, g_a1, sh_f1, sc_f1, g_f1 = _split6(mod1[:b])
    csh_a1, csc_a1 = [jnp.broadcast_to(v, (b, 1, d)) for v in _split6(mod1[b:b + 1])[:2]]

    wts0 = _prep_layer0(w_in_0, gmlp_ln_g_0, gmlp_ln_b_0, gmlp_ws_0, gmlp_bs_0,
                        mla_q_norm_0, mla_w_uq_0, mla_kv_norm_0, mla_w_ukv_0)
    tabs_l = _rope_tables(n, MLA_ROPE, MLA_NOPE, identity=False)
    tabs_c = _rope_tables(m_ctx, MLA_ROPE, MLA_NOPE, identity=True)
    a_l, q_l, k_l, v_l = _proj0(x, sc_a0, sh_a0, wts0, tabs_l)
    a_c, q_c, k_c, v_c = _proj0(ctx, csc_a0, csh_a0, wts0, tabs_c)
    kv_cat = [(jnp.concatenate([k_l, k_c], 1), jnp.concatenate([v_l, v_c], 2))]
    o_l = _attention(q_l, kv_cat, dv=MLA_V, n_sub=1, finalize=_finalize_plain, name="mla_attn")
    o_c = _attention(q_c, [(k_c, v_c)], dv=MLA_V, n_sub=1, finalize=_finalize_plain, name="mla_attn_ctx")

    w_out_a = w_out_0[:GMLP_WIDTH].astype(BF16)
    w_out_o = w_out_0[GMLP_WIDTH:].astype(BF16)
    router_t0 = router_0.T.astype(BF16)
    wg0, wu0, wd0 = w_gate_0.astype(BF16), w_up_0.astype(BF16), w_down_0.astype(BF16)
    lng, lnb = row(ln_mix_g_0), row(ln_mix_b_0)
    x1, aff, hp = _mixer_out(x, g_a0, lng, lnb, sc_f0, sh_f0, router_t0, [(a_l, w_out_a)], [(o_l, w_out_o)])
    x_lat = _moe(x1, aff, hp, g_f0, wg0, wu0, wd0, row(ln_ffn_g_0), row(ln_ffn_b_0))
    c1, caff, chp = _mixer_out(ctx, cg_a0, lng, lnb, csc_f0, csh_f0, router_t0, [(a_c, w_out_a)], [(o_c, w_out_o)])
    x_ctx = _moe(c1, caff, chp, cg_f0, wg0, wu0, wd0, row(ln_ffn_g_0), row(ln_ffn_b_0))

    lam_init = 0.8 - 0.6 * math.exp(-0.3 * 1)
    wts1 = _prep_layer1(w_in_1)
    tabs_l = _rope_tables(n, DIFF_HEAD_DIM, 0, identity=False)
    tabs_c = _rope_tables(m_ctx, DIFF_HEAD_DIM, 0, identity=True)
    q_l, k_l, v_l = _proj1(x_lat, sc_a1, sh_a1, wts1, tabs_l)
    _, k_c, v_c = _proj1(x_ctx, csc_a1, csh_a1, wts1, tabs_c)
    extra = (row(lambda_q1_1), row(lambda_k1_1), row(lambda_q2_1), row(lambda_k2_1), subln_g_1.reshape(-1, 1))
    kv_cat = [(jnp.concatenate([k_l, k_c], 1), jnp.concatenate([v_l, v_c], 2))]
    o_l = _attention(q_l, kv_cat, extra, dv=2 * DIFF_HEAD_DIM, n_sub=2,
                     finalize=functools.partial(_finalize_diff, lam_init=lam_init), name="diff_attn")
    x1, aff, hp = _mixer_out(x_lat, g_a1, row(ln_mix_g_1), row(ln_mix_b_1), sc_f1, sh_f1, router_1.T.astype(BF16),
                             [], [(o_l, w_out_1.astype(BF16))])
    return _moe(x1, aff, hp, g_f1, w_gate_1.astype(BF16), w_up_1.astype(BF16), w_down_1.astype(BF16),
                row(ln_ffn_g_1), row(ln_ffn_b_1))
```

```python
import functools
import math

import jax
import jax.numpy as jnp
from jax import lax
from jax.experimental import pallas as pl
from jax.experimental.pallas import tpu as pltpu

F32 = jnp.float32
BF16 = jnp.bfloat16

DEPTH = 2
GRID_W = 64
ROPE_BASE = 10000.0
CHUNK = 128
GMLP_GROUPS = 4
GMLP_GROUP_CH = 128
GMLP_WIDTH = GMLP_GROUPS * GMLP_GROUP_CH
MLA_HEADS = 8
MLA_NOPE = 64
MLA_ROPE = 32
MLA_V = 64
MLA_Q_RANK = 256
MLA_KV_RANK = 128
DIFF_HEADS = 8
DIFF_HEAD_DIM = 64
CAPACITY_FACTOR = 2
DEEPNORM_ALPHA = (2 * DEPTH) ** 0.25
LN_EPS = 1e-5
RMS_EPS = 1e-6
LOG2E = 1.4426950408889634

LANES = 128
SUBLANES = 8
HEAD_SLOT = LANES
NEG_BIG = -1e30
VMEM_LIMIT = 56 * 1024 * 1024


def _cparams(sem):
    return pltpu.CompilerParams(dimension_semantics=sem, vmem_limit_bytes=VMEM_LIMIT)


def _tile(n, pref):
    return pref if n % pref == 0 else n


def _ln(x, g, b):
    mu = jnp.mean(x, axis=-1, keepdims=True)
    xc = x - mu
    var = jnp.mean(xc * xc, axis=-1, keepdims=True)
    return xc * lax.rsqrt(var + LN_EPS) * g + b


def _rms(x, g):
    return x * lax.rsqrt(jnp.mean(x * x, axis=-1, keepdims=True) + RMS_EPS) * g


def _gelu(x):
    return 0.5 * x * (1.0 + lax.erf(x * (2.0 ** -0.5)))


def _to_tiled(v):
    m, w = v.shape
    nt = w // LANES
    tiles = [v[:, c * LANES:(c + 1) * LANES].reshape(m // SUBLANES, SUBLANES, LANES) for c in range(nt)]
    return jnp.stack(tiles, axis=1).reshape(m * nt, LANES)


def _from_tiled(t, w):
    nt = w // LANES
    m = t.shape[0] // nt
    t4 = t.reshape(m // SUBLANES, nt, SUBLANES, LANES)
    return jnp.concatenate([t4[:, c].reshape(m, LANES) for c in range(nt)], axis=1)


def _tiled_row(i, w):
    return (i // SUBLANES) * (SUBLANES * (w // LANES)) + i % SUBLANES


def _dot(a, b):
    return jnp.dot(a, b, preferred_element_type=F32)


def _dot_nt(a, b):
    return lax.dot_general(a, b, (((1,), (1,)), ((), ())), preferred_element_type=F32)


def _dot_tn(a, b):
    return lax.dot_general(a, b, (((0,), (0,)), ((), ())), preferred_element_type=F32)


def _mod_kernel(c_ref, w_ref, b_ref, o_ref):
    c = c_ref[...]
    s = c / (1.0 + jnp.exp(-c))
    o_ref[...] = _dot(s.astype(BF16), w_ref[...].astype(BF16)) + b_ref[...]


def _modulation(cond, w_mod, b_mod):
    r, d = cond.shape
    n = w_mod.shape[1]
    tn = _tile(n, 1024)
    return pl.pallas_call(
        _mod_kernel,
        out_shape=jax.ShapeDtypeStruct((r, n), F32),
        grid=(n // tn,),
        in_specs=[pl.BlockSpec((r, d), lambda j: (0, 0)),
                  pl.BlockSpec((d, tn), lambda j: (0, j)),
                  pl.BlockSpec((1, tn), lambda j: (0, j))],
        out_specs=pl.BlockSpec((r, tn), lambda j: (0, j)),
        compiler_params=_cparams(("parallel",)),
        name="modulation",
    )(cond, w_mod, b_mod.reshape(1, n))


def _proj0_kernel(x_ref, sc_ref, sh_ref, win_ref, lng_ref, lnb_ref, ws_ref, bs_ref,
                  qn_ref, wq_ref, kvn_ref, wk_ref, wv_ref, ck_ref, sk_ref, cq_ref, sq_ref,
                  a_ref, q_ref, k_ref, v_ref, *, q_scale):
    tm = x_ref.shape[1]
    h = (x_ref[0] * (1.0 + sc_ref[0]) + sh_ref[0]).astype(BF16)
    y = _dot(h, win_ref[...])
    w = GMLP_WIDTH
    u = _gelu(y[:, :w])
    vn = _ln(_gelu(y[:, w:2 * w]), lng_ref[...], lnb_ref[...]).astype(BF16)
    for ci in range(tm // CHUNK):
        r0 = ci * CHUNK
        for g in range(GMLP_GROUPS):
            c0 = g * GMLP_GROUP_CH
            mixed = _dot(ws_ref[g], vn[r0:r0 + CHUNK, c0:c0 + GMLP_GROUP_CH]) + bs_ref[g]
            a_ref[0, r0:r0 + CHUNK, c0:c0 + GMLP_GROUP_CH] = (
                u[r0:r0 + CHUNK, c0:c0 + GMLP_GROUP_CH] * mixed).astype(BF16)

    o = 2 * w
    cq = _rms(y[:, o:o + MLA_Q_RANK], qn_ref[...]).astype(BF16)
    qt = _dot_nt(wq_ref[...], cq) * q_scale
    q_ref[0] = qt.astype(BF16)
    hr = MLA_ROPE // 2
    cq_t = cq_ref[...]
    sq_t = sq_ref[...]
    for hd in range(MLA_HEADS):
        r0 = hd * HEAD_SLOT + MLA_NOPE
        x1 = qt[r0:r0 + hr]
        x2 = qt[r0 + hr:r0 + 2 * hr]
        blk = jnp.concatenate([x1, x2], axis=0)
        swp = jnp.concatenate([x2, x1], axis=0)
        q_ref[0, r0:r0 + 2 * hr, :] = (blk * cq_t + swp * sq_t).astype(BF16)

    o += MLA_Q_RANK
    ckv = _rms(y[:, o:o + MLA_KV_RANK], kvn_ref[...]).astype(BF16)
    o += MLA_KV_RANK
    kr = y[:, o:o + HEAD_SLOT] * ck_ref[...] + y[:, o + HEAD_SLOT:o + 2 * HEAD_SLOT] * sk_ref[...]
    k_ref[0] = (_dot(ckv, wk_ref[...]) + jnp.tile(kr, (1, MLA_HEADS))).astype(BF16)
    v_ref[0] = _dot_nt(wv_ref[...], ckv).astype(BF16)


def _proj0(x, sc, sh, wts, tabs):
    b, n, d = x.shape
    tm = _tile(n, 512)
    ck, sk, cq, sq = tabs
    win, lng, lnb, ws, bsb, qn, wq, kvn, wk, wv = wts
    hw = MLA_HEADS * HEAD_SLOT
    vw = MLA_HEADS * MLA_V
    full = lambda a: pl.BlockSpec(a.shape, lambda bi, ti: (0,) * a.ndim)
    q_scale = (MLA_NOPE + MLA_ROPE) ** -0.5 * LOG2E
    return pl.pallas_call(
        functools.partial(_proj0_kernel, q_scale=q_scale),
        out_shape=(jax.ShapeDtypeStruct((b, n, GMLP_WIDTH), BF16),
                   jax.ShapeDtypeStruct((b, hw, n), BF16),
                   jax.ShapeDtypeStruct((b, n, hw), BF16),
                   jax.ShapeDtypeStruct((b, vw, n), BF16)),
        grid=(b, n // tm),
        in_specs=[pl.BlockSpec((1, tm, d), lambda bi, ti: (bi, ti, 0)),
                  pl.BlockSpec((1, 1, d), lambda bi, ti: (bi, 0, 0)),
                  pl.BlockSpec((1, 1, d), lambda bi, ti: (bi, 0, 0)),
                  full(win), full(lng), full(lnb), full(ws), full(bsb),
                  full(qn), full(wq), full(kvn), full(wk), full(wv),
                  pl.BlockSpec((tm, HEAD_SLOT), lambda bi, ti: (ti, 0)),
                  pl.BlockSpec((tm, HEAD_SLOT), lambda bi, ti: (ti, 0)),
                  pl.BlockSpec((MLA_ROPE, tm), lambda bi, ti: (0, ti)),
                  pl.BlockSpec((MLA_ROPE, tm), lambda bi, ti: (0, ti))],
        out_specs=(pl.BlockSpec((1, tm, GMLP_WIDTH), lambda bi, ti: (bi, ti, 0)),
                   pl.BlockSpec((1, hw, tm), lambda bi, ti: (bi, 0, ti)),
                   pl.BlockSpec((1, tm, hw), lambda bi, ti: (bi, ti, 0)),
                   pl.BlockSpec((1, vw, tm), lambda bi, ti: (bi, 0, ti))),
        compiler_params=_cparams(("parallel", "parallel")),
        name="proj0",
    )(x, sc, sh, win, lng, lnb, ws, bsb, qn, wq, kvn, wk, wv, ck, sk, cq, sq)


def _proj1_kernel(x_ref, sc_ref, sh_ref, wq_ref, wk_ref, wv_ref, ck_ref, sk_ref, cq_ref, sq_ref,
                  q_ref, k_ref, v_ref, *, q_scale):
    h = (x_ref[0] * (1.0 + sc_ref[0]) + sh_ref[0]).astype(BF16)
    n_sub = 2 * DIFF_HEADS
    hd = DIFF_HEAD_DIM
    hr = hd // 2
    qt = _dot_nt(wq_ref[...], h) * q_scale
    cq_t = cq_ref[...]
    sq_t = sq_ref[...]
    for s in range(n_sub):
        r0 = s * hd
        x1 = qt[r0:r0 + hr]
        x2 = qt[r0 + hr:r0 + hd]
        swp = jnp.concatenate([x2, x1], axis=0)
        q_ref[0, r0:r0 + hd, :] = (qt[r0:r0 + hd] * cq_t + swp * sq_t).astype(BF16)
    k = _dot(h, wk_ref[...])
    width = k.shape[1]
    lane = lax.broadcasted_iota(jnp.int32, k.shape, 1)
    partner = jnp.where((lane % hd) < hr, pltpu.roll(k, width - hr, 1), pltpu.roll(k, hr, 1))
    reps = width // HEAD_SLOT
    k_ref[0] = (k * jnp.tile(ck_ref[...], (1, reps)) + partner * jnp.tile(sk_ref[...], (1, reps))).astype(BF16)
    v_ref[0] = _dot_nt(wv_ref[...], h).astype(BF16)


def _proj1(x, sc, sh, wts, tabs):
    b, n, d = x.shape
    tm = _tile(n, 512)
    wq, wk, wv = wts
    ck, sk, cq, sq = tabs
    dw = wq.shape[0]
    full = lambda a: pl.BlockSpec(a.shape, lambda bi, ti: (0,) * a.ndim)
    q_scale = DIFF_HEAD_DIM ** -0.5 * LOG2E
    return pl.pallas_call(
        functools.partial(_proj1_kernel, q_scale=q_scale),
        out_shape=(jax.ShapeDtypeStruct((b, dw, n), BF16),
                   jax.ShapeDtypeStruct((b, n, dw), BF16),
                   jax.ShapeDtypeStruct((b, dw, n), BF16)),
        grid=(b, n // tm),
        in_specs=[pl.BlockSpec((1, tm, d), lambda bi, ti: (bi, ti, 0)),
                  pl.BlockSpec((1, 1, d), lambda bi, ti: (bi, 0, 0)),
                  pl.BlockSpec((1, 1, d), lambda bi, ti: (bi, 0, 0)),
                  full(wq), full(wk), full(wv),
                  pl.BlockSpec((tm, HEAD_SLOT), lambda bi, ti: (ti, 0)),
                  pl.BlockSpec((tm, HEAD_SLOT), lambda bi, ti: (ti, 0)),
                  pl.BlockSpec((DIFF_HEAD_DIM, tm), lambda bi, ti: (0, ti)),
                  pl.BlockSpec((DIFF_HEAD_DIM, tm), lambda bi, ti: (0, ti))],
        out_specs=(pl.BlockSpec((1, dw, tm), lambda bi, ti: (bi, 0, ti)),
                   pl.BlockSpec((1, tm, dw), lambda bi, ti: (bi, ti, 0)),
                   pl.BlockSpec((1, dw, tm), lambda bi, ti: (bi, 0, ti))),
        compiler_params=_cparams(("parallel", "parallel")),
        name="proj1",
    )(x, sc, sh, wq, wk, wv, ck, sk, cq, sq)


SHIFT_OK_LO = 2.0 ** -85
SHIFT_OK_HI = 2.0 ** 100
ATTN_TQ = 1024
ATTN_UNROLL = 16
ATTN_STATIC_CHUNKS = 8


def _key_norm_max(kv_refs, kmax_ref, *, tk, n_sub):
    sub_w = HEAD_SLOT // n_sub
    row = lax.broadcasted_iota(jnp.int32, (8, HEAD_SLOT), 0)
    lane = lax.broadcasted_iota(jnp.int32, (8, HEAD_SLOT), 1)
    sel = jnp.where(lane // sub_w == row, 1.0, 0.0).astype(BF16)
    best = None
    for k_ref, _ in kv_refs:
        nchunk = k_ref.shape[1] // tk
        group = max(g for g in range(1, 12) if nchunk % g == 0)
        rows = group * tk

        def body(c, mx, k_ref=k_ref, rows=rows):
            off = pl.multiple_of(c * rows, rows)
            kc = k_ref[0, pl.ds(off, rows), :].astype(F32)
            return jnp.maximum(mx, _dot_nt(sel, (kc * kc).astype(BF16)))

        mx = lax.fori_loop(0, nchunk // group, body, jnp.zeros((8, rows), F32))
        mx = jnp.max(mx, axis=1, keepdims=True)
        best = mx if best is None else jnp.maximum(best, mx)
    kmax_ref[...] = jnp.broadcast_to(best, kmax_ref.shape)


def _sweep_bounded(qs, shifts, kv_refs, s_ref, *, tk, unroll):
    tq = qs[0].shape[1]
    dv = kv_refs[0][1].shape[1]
    chunks = [(si, c * tk) for si, (k_ref, _) in enumerate(kv_refs) for c in range(k_ref.shape[1] // tk)]

    def aligned(off):
        return off if isinstance(off, int) else pl.multiple_of(off, tk)

    def scores(si, off, buf):
        kc = kv_refs[si][0][0, pl.ds(aligned(off), tk), :]
        for r, q in enumerate(qs):
            s_ref[buf, r] = _dot(kc, q)

    def consume(si, off, buf, carry):
        vc = kv_refs[si][1][0, :, pl.ds(aligned(off), tk)]
        nxt = []
        for r, ((l8, acc), m) in enumerate(zip(carry, shifts)):
            p = jnp.exp2(s_ref[buf, r] - m)
            l8 = l8 + jnp.sum(p.reshape(tk // 8, 8, tq), axis=0)
            nxt.append((l8, acc + _dot(vc, p.astype(BF16))))
        return tuple(nxt)

    def body(i, carry):
        for u in range(unroll):
            off = pl.multiple_of((i * unroll + u) * tk, tk)
            scores(0, off + tk, (u + 1) % 2)
            carry = consume(0, off, u % 2, carry)
        return carry

    carry = tuple((jnp.zeros((8, tq), F32), jnp.zeros((dv, tq), F32)) for _ in qs)
    scores(*chunks[0], 0)
    n_first = kv_refs[0][0].shape[1] // tk
    n_loop = 0 if len(chunks) <= ATTN_STATIC_CHUNKS else (n_first - 1) // unroll * unroll
    if n_loop:
        carry = lax.fori_loop(0, n_loop // unroll, body, carry)
    for c in range(n_loop, len(chunks)):
        if c + 1 < len(chunks):
            scores(*chunks[c + 1], (c + 1) % 2)
        carry = consume(*chunks[c], c % 2, carry)
    return [(jnp.sum(l8, axis=0, keepdims=True), acc) for l8, acc in carry]


def _sweep_running_max(qs, kv_refs, *, tk):
    tq = qs[0].shape[1]
    dv = kv_refs[0][1].shape[1]
    carry = tuple((jnp.full((1, tq), NEG_BIG, F32), jnp.zeros((1, tq), F32), jnp.zeros((dv, tq), F32)) for _ in qs)
    for k_ref, v_ref in kv_refs:

        def body(c, carry, k_ref=k_ref, v_ref=v_ref):
            off = pl.multiple_of(c * tk, tk)
            kc = k_ref[0, pl.ds(off, tk), :]
            vc = v_ref[0, :, pl.ds(off, tk)]
            nxt = []
            for (m, l, acc), q in zip(carry, qs):
                s = _dot(kc, q)
                mn = jnp.maximum(m, jnp.max(s, axis=0, keepdims=True))
                alpha = jnp.exp2(m - mn)
                p = jnp.exp2(s - mn)
                nxt.append((mn, alpha * l + jnp.sum(p, axis=0, keepdims=True),
                            alpha * acc + _dot(vc, p.astype(BF16))))
            return tuple(nxt)

        carry = lax.fori_loop(0, k_ref.shape[1] // tk, body, carry)
    return [(l, acc) for _, l, acc in carry]


def _attn_kernel(*refs, tk, unroll, n_sub, n_src, finalize):
    q_ref = refs[0]
    kv_refs = [(refs[1 + 2 * i], refs[2 + 2 * i]) for i in range(n_src)]
    extra = refs[1 + 2 * n_src:-3]
    o_ref, kmax_ref, s_ref = refs[-3:]

    @pl.when(pl.program_id(2) == 0)
    def _():
        _key_norm_max(kv_refs, kmax_ref, tk=tk, n_sub=n_sub)

    q = q_ref[0]
    sub_w = HEAD_SLOT // n_sub
    row = lax.broadcasted_iota(jnp.int32, q.shape, 0)
    qf = q.astype(F32)
    qs, shifts = [], []
    for r in range(n_sub):
        mine = (row // sub_w) == r
        qs.append(jnp.where(mine, q, jnp.zeros_like(q)) if n_sub > 1 else q)
        qn2 = jnp.sum(jnp.where(mine, qf * qf, 0.0), axis=0, keepdims=True)
        shifts.append(jnp.sqrt(qn2 * kmax_ref[r:r + 1, 0:1]))

    res = _sweep_bounded(qs, shifts, kv_refs, s_ref, tk=tk, unroll=unroll)
    o_ref[0] = finalize(res, extra).astype(BF16)
    ok = None
    for l, _ in res:
        ok_r = (jnp.min(l) >= SHIFT_OK_LO) & (jnp.max(l) <= SHIFT_OK_HI)
        ok = ok_r if ok is None else ok & ok_r

    @pl.when(jnp.logical_not(ok))
    def _():
        o_ref[0] = finalize(_sweep_running_max(qs, kv_refs, tk=tk), extra).astype(BF16)


def _finalize_plain(res, extra):
    (l, acc), = res
    return acc / l


def _finalize_diff(res, extra, *, lam_init):
    lq1_ref, lk1_ref, lq2_ref, lk2_ref, g_ref = extra
    (l1, a1), (l2, a2) = res
    lam = (jnp.exp(jnp.sum(lq1_ref[...] * lk1_ref[...], axis=-1, keepdims=True))
           - jnp.exp(jnp.sum(lq2_ref[...] * lk2_ref[...], axis=-1, keepdims=True)) + lam_init)
    o = a1 / l1 - lam * (a2 / l2)
    return o * lax.rsqrt(jnp.mean(o * o, axis=0, keepdims=True) + RMS_EPS) * g_ref[...] * (1.0 - lam_init)


def _attention(qt, kv, extra=(), *, dv, n_sub, finalize, name):
    b, hw, n = qt.shape
    heads = hw // HEAD_SLOT
    tq = _tile(n, ATTN_TQ)
    tk = 256 if all(k.shape[1] % 256 == 0 for k, _ in kv) else 128
    full = lambda a: pl.BlockSpec(a.shape, lambda bi, hi, qi: (0,) * a.ndim)
    kv_specs, kv_args = [], []
    for k, vt in kv:
        t = k.shape[1]
        kv_specs += [pl.BlockSpec((1, t, HEAD_SLOT), lambda bi, hi, qi: (bi, 0, hi)),
                     pl.BlockSpec((1, dv, t), lambda bi, hi, qi: (bi, hi, 0))]
        kv_args += [k, vt]
    return pl.pallas_call(
        functools.partial(_attn_kernel, tk=tk, unroll=ATTN_UNROLL, n_sub=n_sub, n_src=len(kv), finalize=finalize),
        out_shape=jax.ShapeDtypeStruct((b, heads * dv, n), BF16),
        grid=(b, heads, n // tq),
        in_specs=[pl.BlockSpec((1, HEAD_SLOT, tq), lambda bi, hi, qi: (bi, hi, qi))] + kv_specs
        + [full(a) for a in extra],
        out_specs=pl.BlockSpec((1, dv, tq), lambda bi, hi, qi: (bi, hi, qi)),
        scratch_shapes=[pltpu.VMEM((8, HEAD_SLOT), F32), pltpu.VMEM((2, n_sub, tk, tq), F32)],
        compiler_params=_cparams(("parallel", "parallel", "arbitrary")),
        name=name,
    )(qt, *kv_args, *extra)


def _out_kernel(*refs, n_rowmajor):
    (x_ref, g_ref, lng_ref, lnb_ref, scf_ref, shf_ref, rt_ref), rest = refs[:7], refs[7:]
    x1_ref, aff_ref, hp_ref = rest[-3:]
    ops = rest[:-3]
    m = None
    for i in range(len(ops) // 2):
        a = ops[2 * i][0]
        w = ops[2 * i + 1][...]
        part = _dot(a, w) if i < n_rowmajor else _dot_tn(a, w)
        m = part if m is None else m + part
    x1 = _ln(DEEPNORM_ALPHA * x_ref[0] + (1.0 + g_ref[0]) * m, lng_ref[...], lnb_ref[...])
    x1_ref[0] = x1
    hf = (x1 * (1.0 + scf_ref[0]) + shf_ref[0]).astype(BF16)
    lg = _dot_nt(rt_ref[...], hf)
    e = jnp.exp(lg - jnp.max(lg, axis=0, keepdims=True))
    aff_ref[0] = e / jnp.sum(e, axis=0, keepdims=True)
    dh = hf.shape[1] // 2
    lo = pltpu.bitcast(hf[:, :dh].astype(F32), jnp.uint32)
    hi = pltpu.bitcast(hf[:, dh:].astype(F32), jnp.uint32)
    hp_ref[0] = _to_tiled(lax.shift_right_logical(lo, jnp.uint32(16)) | (hi & jnp.uint32(0xFFFF0000)))


def _mixer_out(x, g, lng, lnb, scf, shf, router_t, rowmajor_ops, chanmajor_ops):
    b, n, d = x.shape
    e = router_t.shape[0]
    tm = _tile(n, 512)
    vec = pl.BlockSpec((1, 1, d), lambda bi, ti: (bi, 0, 0))
    full = lambda a: pl.BlockSpec(a.shape, lambda bi, ti: (0,) * a.ndim)
    args = [x, g, lng, lnb, scf, shf, router_t]
    specs = [pl.BlockSpec((1, tm, d), lambda bi, ti: (bi, ti, 0)), vec, full(lng), full(lnb), vec, vec,
             full(router_t)]
    for a, w in rowmajor_ops:
        args += [a, w]
        specs += [pl.BlockSpec((1, tm, a.shape[2]), lambda bi, ti: (bi, ti, 0)), full(w)]
    for a, w in chanmajor_ops:
        args += [a, w]
        specs += [pl.BlockSpec((1, a.shape[1], tm), lambda bi, ti: (bi, 0, ti)), full(w)]
    return pl.pallas_call(
        functools.partial(_out_kernel, n_rowmajor=len(rowmajor_ops)),
        out_shape=(jax.ShapeDtypeStruct((b, n, d), F32), jax.ShapeDtypeStruct((b, e, n), F32),
                   jax.ShapeDtypeStruct((b, n * (d // 2) // LANES, LANES), jnp.uint32)),
        grid=(b, n // tm),
        in_specs=specs,
        out_specs=(pl.BlockSpec((1, tm, d), lambda bi, ti: (bi, ti, 0)),
                   pl.BlockSpec((1, e, tm), lambda bi, ti: (bi, 0, ti)),
                   pl.BlockSpec((1, tm * (d // 2) // LANES, LANES), lambda bi, ti: (bi, ti, 0))),
        compiler_params=_cparams(("parallel", "parallel")),
        name="mixer_out",
    )(*args)


ROW_UNROLL = 8
FFN_ROWS = 512
TOKEN_SPLIT = 4096


def _ffn_kernel(src_ref, nxt_ref, hp_ref, g_ref, wg_ref, wu_ref, wd_ref, y_ref, rows_ref, xs_ref, *, prefetch):
    cap = xs_ref.shape[0]
    nt = rows_ref.shape[0] // cap
    dh = nt * LANES
    lt = xs_ref.shape[1] // LANES

    def gather(idx_ref, j, dst):
        rows_ref[pl.ds(dst, nt, stride=SUBLANES), :] = hp_ref[0, pl.ds(idx_ref[0, 0, j], nt, stride=SUBLANES), :]

    def gather_loop():
        def body(jb, carry):
            for u in range(ROW_UNROLL):
                gather(src_ref, jb * ROW_UNROLL + u, jb * (ROW_UNROLL * nt) + u)
            return carry

        lax.fori_loop(0, cap // ROW_UNROLL, body, 0)

    if prefetch:
        pl.when(pl.program_id(1) == 0)(gather_loop)
    else:
        gather_loop()

    blk = min(cap, FFN_ROWS)
    for c0 in range(0, cap, blk):
        w = _from_tiled(rows_ref[c0 * nt:(c0 + blk) * nt], dh)
        lo = pltpu.bitcast(lax.shift_left(w, jnp.uint32(16)), F32)
        hi = pltpu.bitcast(w & jnp.uint32(0xFFFF0000), F32)
        xs_ref[c0:c0 + blk] = jnp.concatenate([lo, hi], axis=1).astype(BF16)

    if prefetch:
        for j in range(cap):
            gather(nxt_ref, j, _tiled_row(j, dh))

    for c0 in range(0, cap, blk):
        xs = xs_ref[c0:c0 + blk]
        gate = _dot(xs, wg_ref[0])
        hid = gate / (1.0 + jnp.exp(-gate)) * _dot(xs, wu_ref[0])
        y = _dot(hid.astype(BF16), wd_ref[0]) * g_ref[0, c0:c0 + blk]
        y_ref[0, 0, c0 * lt:(c0 + blk) * lt] = _to_tiled(y)


def _activations_outweigh(activation_bytes, weight_bytes):
    return activation_bytes > weight_bytes


def _moe_ffn(hp, src3, gates3, wg, wu, wd, cap):
    b, hp_rows, _ = hp.shape
    n_exp, d, ff = wg.shape
    nt = d // 2 // LANES
    sample_major = _activations_outweigh(hp_rows * LANES * 4, 3 * d * ff * 2)
    if sample_major:
        grid, be = (b, n_exp), (lambda g0, g1: (g0, g1))
        hp_spec = pl.BlockSpec((1, hp_rows, LANES), lambda g0, g1: (g0, 0, 0), pipeline_mode=pl.Buffered(1))
    else:
        grid, be = (n_exp, b), (lambda g0, g1: (g1, g0))
        hp_spec = pl.BlockSpec((1, hp_rows, LANES), lambda g0, g1: (g1, 0, 0))
    slot = lambda g0, g1: (be(g0, g1)[0] * n_exp + be(g0, g1)[1], 0, 0)
    weight = lambda g0, g1: (be(g0, g1)[1], 0, 0)
    nxt = lambda g0, g1: (be(g0, g1)[0] * n_exp + jnp.minimum(be(g0, g1)[1] + 1, n_exp - 1), 0, 0)
    return pl.pallas_call(
        functools.partial(_ffn_kernel, prefetch=sample_major),
        out_shape=jax.ShapeDtypeStruct((b, n_exp, cap * d // LANES, LANES), F32),
        grid=grid,
        in_specs=[pl.BlockSpec((1, 1, cap), slot, memory_space=pltpu.SMEM),
                  pl.BlockSpec((1, 1, cap), nxt, memory_space=pltpu.SMEM),
                  hp_spec,
                  pl.BlockSpec((1, cap, 1), slot),
                  pl.BlockSpec((1, d, ff), weight),
                  pl.BlockSpec((1, d, ff), weight),
                  pl.BlockSpec((1, ff, d), weight)],
        out_specs=pl.BlockSpec((1, 1, cap * d // LANES, LANES), lambda g0, g1: be(g0, g1) + (0, 0)),
        scratch_shapes=[pltpu.VMEM((cap * nt, LANES), jnp.uint32), pltpu.VMEM((cap, d), BF16)],
        compiler_params=_cparams(("parallel", "arbitrary")),
        name="moe_ffn",
    )(src3, src3, hp, gates3, wg, wu, wd)


def _scatter_kernel(dst_ref, bnd_ref, y_ref, f_ref, *, d):
    nt = d // LANES
    rows_ref = y_ref.at[0, 0]

    @pl.when(pl.program_id(2) == 0)
    def _():
        f_ref[...] = jnp.zeros_like(f_ref)

    lo = bnd_ref[0, 0, pl.program_id(1)]
    hi = bnd_ref[0, 0, pl.program_id(1) + 1]

    def row(ref, start):
        return ref[pl.ds(start, nt, stride=SUBLANES), :]

    def one(j, carry):
        dst = dst_ref[0, 0, j]
        f_ref[0, pl.ds(dst, nt, stride=SUBLANES), :] = row(f_ref.at[0], dst) + row(rows_ref, _tiled_row(j, d))
        return carry

    def batch(jb, carry):
        dsts = [dst_ref[0, 0, jb * ROW_UNROLL + u] for u in range(ROW_UNROLL)]
        sums = [row(f_ref.at[0], dsts[u]) + row(rows_ref, jb * (ROW_UNROLL * nt) + u) for u in range(ROW_UNROLL)]
        for u in range(ROW_UNROLL):
            f_ref[0, pl.ds(dsts[u], nt, stride=SUBLANES), :] = sums[u]
        return carry

    head_end = jnp.minimum((lo + ROW_UNROLL - 1) // ROW_UNROLL * ROW_UNROLL, hi)
    tail_start = jnp.maximum(head_end, hi // ROW_UNROLL * ROW_UNROLL)
    lax.fori_loop(lo, head_end, one, 0)
    lax.fori_loop(head_end // ROW_UNROLL, tail_start // ROW_UNROLL, batch, 0)
    lax.fori_loop(tail_start, hi, one, 0)


def _moe_scatter(y, dst3, bounds3, n, d):
    b, n_exp, y_rows, _ = y.shape
    lt = d // LANES
    cap = y_rows // lt
    nt = _tile(n, TOKEN_SPLIT)
    return pl.pallas_call(
        functools.partial(_scatter_kernel, d=d),
        out_shape=jax.ShapeDtypeStruct((b, n * lt, LANES), F32),
        grid=(b, n // nt, n_exp),
        in_specs=[pl.BlockSpec((1, 1, cap), lambda bi, ki, ei: (bi * n_exp + ei, 0, 0), memory_space=pltpu.SMEM),
                  pl.BlockSpec((1, 1, n // nt + 1), lambda bi, ki, ei: (bi * n_exp + ei, 0, 0),
                               memory_space=pltpu.SMEM),
                  pl.BlockSpec((1, 1, y_rows, LANES), lambda bi, ki, ei: (bi, ei, 0, 0))],
        out_specs=pl.BlockSpec((1, nt * lt, LANES), lambda bi, ki, ei: (bi, ki, 0)),
        compiler_params=_cparams(("parallel", "parallel", "arbitrary")),
        name="moe_scatter",
    )(dst3, bounds3, y)


def _post_kernel(x_ref, f_ref, g_ref, lng_ref, lnb_ref, o_ref):
    f = _from_tiled(f_ref[0], x_ref.shape[2])
    o_ref[0] = _ln(DEEPNORM_ALPHA * x_ref[0] + (1.0 + g_ref[0]) * f, lng_ref[...], lnb_ref[...])


def _post(x1, f_tiled, g, lng, lnb):
    b, n, d = x1.shape
    tm = _tile(n, 512)
    blk = pl.BlockSpec((1, tm, d), lambda bi, ti: (bi, ti, 0))
    f_blk = pl.BlockSpec((1, tm * d // LANES, LANES), lambda bi, ti: (bi, ti, 0))
    full = lambda a: pl.BlockSpec(a.shape, lambda bi, ti: (0,) * a.ndim)
    return pl.pallas_call(
        _post_kernel,
        out_shape=jax.ShapeDtypeStruct((b, n, d), F32),
        grid=(b, n // tm),
        in_specs=[blk, f_blk, pl.BlockSpec((1, 1, d), lambda bi, ti: (bi, 0, 0)), full(lng), full(lnb)],
        out_specs=blk,
        compiler_params=_cparams(("parallel", "parallel")),
        name="ffn_post",
    )(x1, f_tiled, g, lng, lnb)


def _select_kernel(aff_ref, idx_ref, gate_ref, *, cap, n_exp):
    a = aff_ref[0]
    er, lanes = a.shape
    r_n = er // n_exp
    e_pad = 16
    bits = pltpu.bitcast(a, jnp.int32)

    member = (lax.broadcasted_iota(jnp.int32, (e_pad, er), 1) // r_n
              == lax.broadcasted_iota(jnp.int32, (e_pad, er), 0))
    gs = jnp.where(member, 1.0, 0.0).astype(BF16)
    member_t = (lax.broadcasted_iota(jnp.int32, (er, e_pad), 0) // r_n
                == lax.broadcasted_iota(jnp.int32, (er, e_pad), 1))
    gst = jnp.where(member_t, 1.0, 0.0).astype(BF16)
    ri = lax.broadcasted_iota(jnp.int32, (er, er), 0)
    rj = lax.broadcasted_iota(jnp.int32, (er, er), 1)
    rows_before = jnp.where((ri // r_n == rj // r_n) & (rj < ri), 1.0, 0.0).astype(BF16)
    li = lax.broadcasted_iota(jnp.int32, (lanes, lanes), 0)
    lj = lax.broadcasted_iota(jnp.int32, (lanes, lanes), 1)
    lanes_upto = jnp.where(li <= lj, 1.0, 0.0).astype(BF16)

    def expert_total(m):
        per_lane = _dot(gs, m.astype(BF16))
        return jnp.broadcast_to(jnp.sum(per_lane, axis=1, keepdims=True), per_lane.shape)

    def to_rows(ev):
        if r_n % SUBLANES == 0 and n_exp == e_pad:
            return jnp.broadcast_to(ev[:, None, :], (e_pad, r_n, lanes)).reshape(er, lanes)
        hi = jnp.floor(ev * (1.0 / 128.0))
        return 128.0 * _dot(gst, hi.astype(BF16)) + _dot(gst, (ev - 128.0 * hi).astype(BF16))

    def search(i, t):
        cand = t | lax.shift_left(jnp.int32(1), 30 - i)
        cnt = expert_total(jnp.where(bits >= cand, 1.0, 0.0))
        return jnp.where(to_rows(jnp.where(cnt >= cap, 1.0, 0.0)) > 0.5, cand, t)

    t = lax.fori_loop(0, 31, search, jnp.zeros((er, lanes), jnp.int32))
    gt = jnp.where(bits > t, 1.0, 0.0)
    eq = jnp.where(bits == t, 1.0, 0.0)

    def prefix(m):
        rc = _dot(m.astype(BF16), lanes_upto)
        off = _dot(rows_before, jnp.broadcast_to(rc[:, lanes - 1:lanes], m.shape).astype(BF16))
        return rc, off

    rc_eq, off_eq = prefix(eq)
    need = cap - to_rows(expert_total(gt))
    sel = jnp.maximum(gt, eq * jnp.where(off_eq + rc_eq - eq < need, 1.0, 0.0))
    rc, off = prefix(sel)
    row_end = off + jnp.broadcast_to(rc[:, lanes - 1:lanes], rc.shape)

    slot = lax.broadcasted_iota(jnp.int32, (cap, lanes), 0).astype(F32) + 1.0
    lane_c = lax.broadcasted_iota(jnp.int32, (cap, lanes), 1)
    lane_cf = lane_c.astype(F32)
    diag = lax.broadcasted_iota(jnp.int32, (r_n, lanes), 0) == lax.broadcasted_iota(jnp.int32, (r_n, lanes), 1)
    lane_r = lax.broadcasted_iota(jnp.int32, (1, lanes), 1)
    pad = jnp.zeros((lanes - r_n, lanes), F32)

    def table(m):
        return (jnp.concatenate([m, pad], axis=0) if r_n < lanes else m).astype(BF16)

    all_ones = jnp.ones((lanes, lanes), BF16)
    idx_all = jnp.zeros((cap, lanes), jnp.int32)
    gate_all = jnp.zeros((cap, lanes), F32)
    for e in range(n_exp):
        r0 = e * r_n
        end_lane = jnp.sum(jnp.where(diag, row_end[r0:r0 + r_n], 0.0), axis=0, keepdims=True)
        end_lane = jnp.where(lane_r < r_n, end_lane, 3.0e38)
        row_j = jnp.sum(jnp.where(end_lane < slot, 1.0, 0.0), axis=1, keepdims=True)
        oh = jnp.where(lane_cf == row_j, 1.0, 0.0).astype(BF16)
        cum_e = off[r0:r0 + r_n] + rc[r0:r0 + r_n]
        cum_hi = jnp.floor(cum_e * (1.0 / 128.0))
        cum_2 = _dot(oh, jnp.concatenate([table(cum_hi), table(cum_e - 128.0 * cum_hi)], axis=1))
        cum_j = 128.0 * cum_2[:, :lanes] + cum_2[:, lanes:]
        lane_j = _dot(jnp.where(cum_j < slot, 1.0, 0.0).astype(BF16), all_ones)
        a_e = a[r0:r0 + r_n]
        a1 = a_e.astype(BF16).astype(F32)
        a2 = (a_e - a1).astype(BF16).astype(F32)
        a3 = a_e - a1 - a2
        aff_2 = _dot(oh, jnp.concatenate([table(a1), table(a2)], axis=1))
        aff_j = aff_2[:, :lanes] + aff_2[:, lanes:] + _dot(oh, table(a3))
        gate_j = jnp.sum(jnp.where(lane_cf == lane_j, aff_j, 0.0), axis=1, keepdims=True)
        idx_j = (row_j * float(lanes) + lane_j).astype(jnp.int32)
        idx_all = jnp.where(lane_c == e, idx_j, idx_all)
        gate_all = jnp.where(lane_c == e, gate_j, gate_all)
    idx_ref[0] = idx_all
    gate_ref[0] = gate_all


def _moe_select(aff_t, cap):
    b, n_exp, n = aff_t.shape
    lanes = HEAD_SLOT
    er = n_exp * (n // lanes)
    idx, gates = pl.pallas_call(
        functools.partial(_select_kernel, cap=cap, n_exp=n_exp),
        out_shape=(jax.ShapeDtypeStruct((b, cap, lanes), jnp.int32), jax.ShapeDtypeStruct((b, cap, lanes), F32)),
        grid=(b,),
        in_specs=[pl.BlockSpec((1, er, lanes), lambda bi: (bi, 0, 0))],
        out_specs=(pl.BlockSpec((1, cap, lanes), lambda bi: (bi, 0, 0)),
                   pl.BlockSpec((1, cap, lanes), lambda bi: (bi, 0, 0))),
        compiler_params=_cparams(("parallel",)),
        name="moe_select",
    )(aff_t.reshape(b, er, lanes))
    to_expert_major = lambda m: jnp.swapaxes(m[:, :, :n_exp], 1, 2)
    return to_expert_major(gates), to_expert_major(idx)


def _moe(x1, aff_t, hp, gf, wg, wu, wd, lng, lnb):
    b, n, d = x1.shape
    n_exp = aff_t.shape[1]
    cap = CAPACITY_FACTOR * n // n_exp
    gates, idx = _moe_select(aff_t, cap)
    gates3 = gates.reshape(b * n_exp, cap, 1)
    nt = _tile(n, TOKEN_SPLIT)
    edges = jnp.arange(0, n + 1, nt, dtype=jnp.int32)
    bounds3 = jnp.sum(idx[..., None] < edges, axis=2, dtype=jnp.int32).reshape(b * n_exp, 1, n // nt + 1)
    src3 = _tiled_row(idx, d // 2).reshape(b * n_exp, 1, cap)
    dst3 = _tiled_row(idx % nt, d).reshape(b * n_exp, 1, cap)
    y = _moe_ffn(hp, src3, gates3, wg, wu, wd, cap)
    f_tiled = _moe_scatter(y, dst3, bounds3, n, d)
    return _post(x1, f_tiled, gf, lng, lnb)


def _rope_angles(n_tokens, dim):
    n_rows = n_tokens // GRID_W
    row = jnp.repeat(jnp.arange(n_rows, dtype=F32), GRID_W)
    col = jnp.tile(jnp.arange(GRID_W, dtype=F32), n_rows)
    n_freq = dim // 4
    inv_freq = ROPE_BASE ** (-jnp.arange(n_freq, dtype=F32) / n_freq)
    ang = jnp.concatenate([row[:, None] * inv_freq, col[:, None] * inv_freq], -1)
    return jnp.cos(ang), jnp.sin(ang)


def _rope_tables(n_tokens, dim, lane_offset, identity):
    half = dim // 2
    if identity:
        cos = jnp.ones((n_tokens, half), F32)
        sin = jnp.zeros((n_tokens, half), F32)
    else:
        cos, sin = _rope_angles(n_tokens, dim)
    c2 = jnp.concatenate([cos, cos], -1)
    s2 = jnp.concatenate([-sin, sin], -1)
    reps = (HEAD_SLOT - lane_offset) // dim if lane_offset == 0 else 1
    ck = jnp.zeros((n_tokens, HEAD_SLOT), F32)
    sk = jnp.zeros((n_tokens, HEAD_SLOT), F32)
    for r in range(reps):
        ck = ck.at[:, lane_offset + r * dim:lane_offset + (r + 1) * dim].set(c2)
        sk = sk.at[:, lane_offset + r * dim:lane_offset + (r + 1) * dim].set(s2)
    return ck, sk, c2.T, s2.T


def _deinterleave(w, axis):
    return jnp.concatenate([jnp.take(w, jnp.arange(0, w.shape[axis], 2), axis=axis),
                            jnp.take(w, jnp.arange(1, w.shape[axis], 2), axis=axis)], axis=axis)


def _prep_layer0(w_in, ln_g, ln_b, ws, bs, q_norm, w_uq, kv_norm, w_ukv):
    d = w_in.shape[0]
    w = GMLP_WIDTH
    o_kr = 2 * w + MLA_Q_RANK + MLA_KV_RANK
    kr = _deinterleave(w_in[:, o_kr:o_kr + MLA_ROPE], 1)
    kr_sw = jnp.concatenate([kr[:, MLA_ROPE // 2:], kr[:, :MLA_ROPE // 2]], 1)
    slot = lambda m: jnp.zeros((d, HEAD_SLOT), F32).at[:, MLA_NOPE:MLA_NOPE + MLA_ROPE].set(m)
    win = jnp.concatenate([w_in[:, :o_kr], slot(kr), slot(kr_sw)], 1).astype(BF16)

    qd = MLA_NOPE + MLA_ROPE
    wq = w_uq.reshape(MLA_Q_RANK, MLA_HEADS, qd)
    wq = jnp.concatenate([wq[..., :MLA_NOPE], _deinterleave(wq[..., MLA_NOPE:], 2),
                          jnp.zeros((MLA_Q_RANK, MLA_HEADS, HEAD_SLOT - qd), F32)], -1)
    wq_t = wq.reshape(MLA_Q_RANK, MLA_HEADS * HEAD_SLOT).T.astype(BF16)

    wkv = w_ukv.reshape(MLA_KV_RANK, MLA_HEADS, MLA_NOPE + MLA_V)
    wk = jnp.concatenate([wkv[..., :MLA_NOPE], jnp.zeros((MLA_KV_RANK, MLA_HEADS, HEAD_SLOT - MLA_NOPE), F32)], -1)
    wk = wk.reshape(MLA_KV_RANK, MLA_HEADS * HEAD_SLOT).astype(BF16)
    wv_t = wkv[..., MLA_NOPE:].reshape(MLA_KV_RANK, MLA_HEADS * MLA_V).T.astype(BF16)

    bsb = jnp.broadcast_to(bs[:, :, None], (GMLP_GROUPS, CHUNK, GMLP_GROUP_CH))
    return (win, ln_g.reshape(1, -1), ln_b.reshape(1, -1), ws.astype(BF16), bsb,
            q_norm.reshape(1, -1), wq_t, kv_norm.reshape(1, -1), wk, wv_t)


def _prep_layer1(w_in):
    d = w_in.shape[0]
    dw = 2 * DIFF_HEADS * DIFF_HEAD_DIM
    perm = lambda m: _deinterleave(m.reshape(d, 2 * DIFF_HEADS, DIFF_HEAD_DIM), 2).reshape(d, dw)
    wq_t = perm(w_in[:, :dw]).T.astype(BF16)
    wk = perm(w_in[:, dw:2 * dw]).astype(BF16)
    wv_t = w_in[:, 2 * dw:].T.astype(BF16)
    return wq_t, wk, wv_t


def _split6(m_row):
    return [v[:, None, :] for v in jnp.split(m_row, 6, axis=-1)]


def kernel(x, c, ctx, c_ctx, w_mod_0, b_mod_0, w_in_0, gmlp_ln_g_0, gmlp_ln_b_0, gmlp_ws_0, gmlp_bs_0, mla_q_norm_0, mla_w_uq_0, mla_kv_norm_0, mla_w_ukv_0, w_out_0, ln_mix_g_0, ln_mix_b_0, router_0, w_gate_0, w_up_0, w_down_0, ln_ffn_g_0, ln_ffn_b_0, w_mod_1, b_mod_1, w_in_1, lambda_q1_1, lambda_k1_1, lambda_q2_1, lambda_k2_1, subln_g_1, w_out_1, ln_mix_g_1, ln_mix_b_1, router_1, w_gate_1, w_up_1, w_down_1, ln_ffn_g_1, ln_ffn_b_1):
    b, n, d = x.shape
    m_ctx = ctx.shape[1]
    row = lambda v: v.reshape(1, -1)

    cond = jnp.concatenate([c, c_ctx[None, :], jnp.zeros((-(b + 1) % 8, d), F32)], 0)
    mod0 = _modulation(cond, w_mod_0, b_mod_0)
    mod1 = _modulation(cond, w_mod_1, b_mod_1)
    sh_a0, sc_a0, g_a0, sh_f0, sc_f0, g_f0 = _split6(mod0[:b])
    csh_a0, csc_a0, cg_a0, csh_f0, csc_f0, cg_f0 = [jnp.broadcast_to(v, (b, 1, d)) for v in _split6(mod0[b:b + 1])]
    sh_a1, sc_a1, g_a1, sh_f1, sc_f1, g_f1 = _split6(mod1[:b])
    csh_a1, csc_a1 = [jnp.broadcast_to(v, (b, 1, d)) for v in _split6(mod1[b:b + 1])[:2]]

    wts0 = _prep_layer0(w_in_0, gmlp_ln_g_0, gmlp_ln_b_0, gmlp_ws_0, gmlp_bs_0,
                        mla_q_norm_0, mla_w_uq_0, mla_kv_norm_0, mla_w_ukv_0)
    tabs_l = _rope_tables(n, MLA_ROPE, MLA_NOPE, identity=False)
    tabs_c = _rope_tables(m_ctx, MLA_ROPE, MLA_NOPE, identity=True)
    a_l, q_l, k_l, v_l = _proj0(x, sc_a0, sh_a0, wts0, tabs_l)
    a_c, q_c, k_c, v_c = _proj0(ctx, csc_a0, csh_a0, wts0, tabs_c)
    kv_cat = [(jnp.concatenate([k_l, k_c], 1), jnp.concatenate([v_l, v_c], 2))]
    o_l = _attention(q_l, kv_cat, dv=MLA_V, n_sub=1, finalize=_finalize_plain, name="mla_attn")
    o_c = _attention(q_c, [(k_c, v_c)], dv=MLA_V, n_sub=1, finalize=_finalize_plain, name="mla_attn_ctx")

    w_out_a = w_out_0[:GMLP_WIDTH].astype(BF16)
    w_out_o = w_out_0[GMLP_WIDTH:].astype(BF16)
    router_t0 = router_0.T.astype(BF16)
    wg0, wu0, wd0 = w_gate_0.astype(BF16), w_up_0.astype(BF16), w_down_0.astype(BF16)
    lng, lnb = row(ln_mix_g_0), row(ln_mix_b_0)
    x1, aff, hp = _mixer_out(x, g_a0, lng, lnb, sc_f0, sh_f0, router_t0, [(a_l, w_out_a)], [(o_l, w_out_o)])
    x_lat = _moe(x1, aff, hp, g_f0, wg0, wu0, wd0, row(ln_ffn_g_0), row(ln_ffn_b_0))
    c1, caff, chp = _mixer_out(ctx, cg_a0, lng, lnb, csc_f0, csh_f0, router_t0, [(a_c, w_out_a)], [(o_c, w_out_o)])
    x_ctx = _moe(c1, caff, chp, cg_f0, wg0, wu0, wd0, row(ln_ffn_g_0), row(ln_ffn_b_0))

    lam_init = 0.8 - 0.6 * math.exp(-0.3 * 1)
    wts1 = _prep_layer1(w_in_1)
    tabs_l = _rope_tables(n, DIFF_HEAD_DIM, 0, identity=False)
    tabs_c = _rope_tables(m_ctx, DIFF_HEAD_DIM, 0, identity=True)
    q_l, k_l, v_l = _proj1(x_lat, sc_a1, sh_a1, wts1, tabs_l)
    _, k_c, v_c = _proj1(x_ctx, csc_a1, csh_a1, wts1, tabs_c)
    extra = (row(lambda_q1_1), row(lambda_k1_1), row(lambda_q2_1), row(lambda_k2_1), subln_g_1.reshape(-1, 1))
    kv_cat = [(jnp.concatenate([k_l, k_c], 1), jnp.concatenate([v_l, v_c], 2))]
    o_l = _attention(q_l, kv_cat, extra, dv=2 * DIFF_HEAD_DIM, n_sub=2,
                     finalize=functools.partial(_finalize_diff, lam_init=lam_init), name="diff_attn")
    x1, aff, hp = _mixer_out(x_lat, g_a1, row(ln_mix_g_1), row(ln_mix_b_1), sc_f1, sh_f1, router_1.T.astype(BF16),
                             [], [(o_l, w_out_1.astype(BF16))])
    return _moe(x1, aff, hp, g_f1, w_gate_1.astype(BF16), w_up_1.astype(BF16), w_down_1.astype(BF16),
                row(ln_ffn_g_1), row(ln_ffn_b_1))
```

```python
import functools
import math

import jax
import jax.numpy as jnp
from jax import lax
from jax.experimental import pallas as pl
from jax.experimental.pallas import tpu as pltpu

F32 = jnp.float32
BF16 = jnp.bfloat16

DEPTH = 2
GRID_W = 64
ROPE_BASE = 10000.0
CHUNK = 128
GMLP_GROUPS = 4
GMLP_GROUP_CH = 128
GMLP_WIDTH = GMLP_GROUPS * GMLP_GROUP_CH
MLA_HEADS = 8
MLA_NOPE = 64
MLA_ROPE = 32
MLA_V = 64
MLA_Q_RANK = 256
MLA_KV_RANK = 128
DIFF_HEADS = 8
DIFF_HEAD_DIM = 64
CAPACITY_FACTOR = 2
DEEPNORM_ALPHA = (2 * DEPTH) ** 0.25
LN_EPS = 1e-5
RMS_EPS = 1e-6
LOG2E = 1.4426950408889634

LANES = 128
SUBLANES = 8
HEAD_SLOT = LANES
NEG_BIG = -1e30
VMEM_LIMIT = 56 * 1024 * 1024


def _cparams(sem):
    return pltpu.CompilerParams(dimension_semantics=sem, vmem_limit_bytes=VMEM_LIMIT)


def _tile(n, pref):
    return pref if n % pref == 0 else n


def _ln(x, g, b):
    mu = jnp.mean(x, axis=-1, keepdims=True)
    xc = x - mu
    var = jnp.mean(xc * xc, axis=-1, keepdims=True)
    return xc * lax.rsqrt(var + LN_EPS) * g + b


def _rms(x, g):
    return x * lax.rsqrt(jnp.mean(x * x, axis=-1, keepdims=True) + RMS_EPS) * g


def _gelu(x):
    return 0.5 * x * (1.0 + lax.erf(x * (2.0 ** -0.5)))


def _to_tiled(v):
    m, w = v.shape
    nt = w // LANES
    tiles = [v[:, c * LANES:(c + 1) * LANES].reshape(m // SUBLANES, SUBLANES, LANES) for c in range(nt)]
    return jnp.stack(tiles, axis=1).reshape(m * nt, LANES)


def _from_tiled(t, w):
    nt = w // LANES
    m = t.shape[0] // nt
    t4 = t.reshape(m // SUBLANES, nt, SUBLANES, LANES)
    return jnp.concatenate([t4[:, c].reshape(m, LANES) for c in range(nt)], axis=1)


def _tiled_row(i, w):
    return (i // SUBLANES) * (SUBLANES * (w // LANES)) + i % SUBLANES


def _dot(a, b):
    return jnp.dot(a, b, preferred_element_type=F32)


def _dot_nt(a, b):
    return lax.dot_general(a, b, (((1,), (1,)), ((), ())), preferred_element_type=F32)


def _dot_tn(a, b):
    return lax.dot_general(a, b, (((0,), (0,)), ((), ())), preferred_element_type=F32)


def _mod_kernel(c_ref, w_ref, b_ref, o_ref):
    c = c_ref[...]
    s = c / (1.0 + jnp.exp(-c))
    o_ref[...] = _dot(s.astype(BF16), w_ref[...].astype(BF16)) + b_ref[...]


def _modulation(cond, w_mod, b_mod):
    r, d = cond.shape
    n = w_mod.shape[1]
    tn = _tile(n, 1024)
    return pl.pallas_call(
        _mod_kernel,
        out_shape=jax.ShapeDtypeStruct((r, n), F32),
        grid=(n // tn,),
        in_specs=[pl.BlockSpec((r, d), lambda j: (0, 0)),
                  pl.BlockSpec((d, tn), lambda j: (0, j)),
                  pl.BlockSpec((1, tn), lambda j: (0, j))],
        out_specs=pl.BlockSpec((r, tn), lambda j: (0, j)),
        compiler_params=_cparams(("parallel",)),
        name="modulation",
    )(cond, w_mod, b_mod.reshape(1, n))


def _proj0_kernel(x_ref, sc_ref, sh_ref, win_ref, lng_ref, lnb_ref, ws_ref, bs_ref,
                  qn_ref, wq_ref, kvn_ref, wk_ref, wv_ref, ck_ref, sk_ref, cq_ref, sq_ref,
                  a_ref, q_ref, k_ref, v_ref, *, q_scale):
    tm = x_ref.shape[1]
    h = (x_ref[0] * (1.0 + sc_ref[0]) + sh_ref[0]).astype(BF16)
    y = _dot(h, win_ref[...])
    w = GMLP_WIDTH
    u = _gelu(y[:, :w])
    vn = _ln(_gelu(y[:, w:2 * w]), lng_ref[...], lnb_ref[...]).astype(BF16)
    for ci in range(tm // CHUNK):
        r0 = ci * CHUNK
        for g in range(GMLP_GROUPS):
            c0 = g * GMLP_GROUP_CH
            mixed = _dot(ws_ref[g], vn[r0:r0 + CHUNK, c0:c0 + GMLP_GROUP_CH]) + bs_ref[g]
            a_ref[0, r0:r0 + CHUNK, c0:c0 + GMLP_GROUP_CH] = (
                u[r0:r0 + CHUNK, c0:c0 + GMLP_GROUP_CH] * mixed).astype(BF16)

    o = 2 * w
    cq = _rms(y[:, o:o + MLA_Q_RANK], qn_ref[...]).astype(BF16)
    qt = _dot_nt(wq_ref[...], cq) * q_scale
    q_ref[0] = qt.astype(BF16)
    hr = MLA_ROPE // 2
    cq_t = cq_ref[...]
    sq_t = sq_ref[...]
    for hd in range(MLA_HEADS):
        r0 = hd * HEAD_SLOT + MLA_NOPE
        x1 = qt[r0:r0 + hr]
        x2 = qt[r0 + hr:r0 + 2 * hr]
        blk = jnp.concatenate([x1, x2], axis=0)
        swp = jnp.concatenate([x2, x1], axis=0)
        q_ref[0, r0:r0 + 2 * hr, :] = (blk * cq_t + swp * sq_t).astype(BF16)

    o += MLA_Q_RANK
    ckv = _rms(y[:, o:o + MLA_KV_RANK], kvn_ref[...]).astype(BF16)
    o += MLA_KV_RANK
    kr = y[:, o:o + HEAD_SLOT] * ck_ref[...] + y[:, o + HEAD_SLOT:o + 2 * HEAD_SLOT] * sk_ref[...]
    k_ref[0] = (_dot(ckv, wk_ref[...]) + jnp.tile(kr, (1, MLA_HEADS))).astype(BF16)
    v_ref[0] = _dot_nt(wv_ref[...], ckv).astype(BF16)


def _proj0(x, sc, sh, wts, tabs):
    b, n, d = x.shape
    tm = _tile(n, 512)
    ck, sk, cq, sq = tabs
    win, lng, lnb, ws, bsb, qn, wq, kvn, wk, wv = wts
    hw = MLA_HEADS * HEAD_SLOT
    vw = MLA_HEADS * MLA_V
    full = lambda a: pl.BlockSpec(a.shape, lambda bi, ti: (0,) * a.ndim)
    q_scale = (MLA_NOPE + MLA_ROPE) ** -0.5 * LOG2E
    return pl.pallas_call(
        functools.partial(_proj0_kernel, q_scale=q_scale),
        out_shape=(jax.ShapeDtypeStruct((b, n, GMLP_WIDTH), BF16),
                   jax.ShapeDtypeStruct((b, hw, n), BF16),
                   jax.ShapeDtypeStruct((b, n, hw), BF16),
                   jax.ShapeDtypeStruct((b, vw, n), BF16)),
        grid=(b, n // tm),
        in_specs=[pl.BlockSpec((1, tm, d), lambda bi, ti: (bi, ti, 0)),
                  pl.BlockSpec((1, 1, d), lambda bi, ti: (bi, 0, 0)),
                  pl.BlockSpec((1, 1, d), lambda bi, ti: (bi, 0, 0)),
                  full(win), full(lng), full(lnb), full(ws), full(bsb),
                  full(qn), full(wq), full(kvn), full(wk), full(wv),
                  pl.BlockSpec((tm, HEAD_SLOT), lambda bi, ti: (ti, 0)),
                  pl.BlockSpec((tm, HEAD_SLOT), lambda bi, ti: (ti, 0)),
                  pl.BlockSpec((MLA_ROPE, tm), lambda bi, ti: (0, ti)),
                  pl.BlockSpec((MLA_ROPE, tm), lambda bi, ti: (0, ti))],
        out_specs=(pl.BlockSpec((1, tm, GMLP_WIDTH), lambda bi, ti: (bi, ti, 0)),
                   pl.BlockSpec((1, hw, tm), lambda bi, ti: (bi, 0, ti)),
                   pl.BlockSpec((1, tm, hw), lambda bi, ti: (bi, ti, 0)),
                   pl.BlockSpec((1, vw, tm), lambda bi, ti: (bi, 0, ti))),
        compiler_params=_cparams(("parallel", "parallel")),
        name="proj0",
    )(x, sc, sh, win, lng, lnb, ws, bsb, qn, wq, kvn, wk, wv, ck, sk, cq, sq)


def _proj1_kernel(x_ref, sc_ref, sh_ref, wq_ref, wk_ref, wv_ref, ck_ref, sk_ref, cq_ref, sq_ref,
                  q_ref, k_ref, v_ref, *, q_scale):
    h = (x_ref[0] * (1.0 + sc_ref[0]) + sh_ref[0]).astype(BF16)
    n_sub = 2 * DIFF_HEADS
    hd = DIFF_HEAD_DIM
    hr = hd // 2
    qt = _dot_nt(wq_ref[...], h) * q_scale
    cq_t = cq_ref[...]
    sq_t = sq_ref[...]
    for s in range(n_sub):
        r0 = s * hd
        x1 = qt[r0:r0 + hr]
        x2 = qt[r0 + hr:r0 + hd]
        swp = jnp.concatenate([x2, x1], axis=0)
        q_ref[0, r0:r0 + hd, :] = (qt[r0:r0 + hd] * cq_t + swp * sq_t).astype(BF16)
    k = _dot(h, wk_ref[...])
    width = k.shape[1]
    lane = lax.broadcasted_iota(jnp.int32, k.shape, 1)
    partner = jnp.where((lane % hd) < hr, pltpu.roll(k, width - hr, 1), pltpu.roll(k, hr, 1))
    reps = width // HEAD_SLOT
    k_ref[0] = (k * jnp.tile(ck_ref[...], (1, reps)) + partner * jnp.tile(sk_ref[...], (1, reps))).astype(BF16)
    v_ref[0] = _dot_nt(wv_ref[...], h).astype(BF16)


def _proj1(x, sc, sh, wts, tabs):
    b, n, d = x.shape
    tm = _tile(n, 512)
    wq, wk, wv = wts
    ck, sk, cq, sq = tabs
    dw = wq.shape[0]
    full = lambda a: pl.BlockSpec(a.shape, lambda bi, ti: (0,) * a.ndim)
    q_scale = DIFF_HEAD_DIM ** -0.5 * LOG2E
    return pl.pallas_call(
        functools.partial(_proj1_kernel, q_scale=q_scale),
        out_shape=(jax.ShapeDtypeStruct((b, dw, n), BF16),
                   jax.ShapeDtypeStruct((b, n, dw), BF16),
                   jax.ShapeDtypeStruct((b, dw, n), BF16)),
        grid=(b, n // tm),
        in_specs=[pl.BlockSpec((1, tm, d), lambda bi, ti: (bi, ti, 0)),
                  pl.BlockSpec((1, 1, d), lambda bi, ti: (bi, 0, 0)),
                  pl.BlockSpec((1, 1, d), lambda bi, ti: (bi, 0, 0)),
                  full(wq), full(wk), full(wv),
                  pl.BlockSpec((tm, HEAD_SLOT), lambda bi, ti: (ti, 0)),
                  pl.BlockSpec((tm, HEAD_SLOT), lambda bi, ti: (ti, 0)),
                  pl.BlockSpec((DIFF_HEAD_DIM, tm), lambda bi, ti: (0, ti)),
                  pl.BlockSpec((DIFF_HEAD_DIM, tm), lambda bi, ti: (0, ti))],
        out_specs=(pl.BlockSpec((1, dw, tm), lambda bi, ti: (bi, 0, ti)),
                   pl.BlockSpec((1, tm, dw), lambda bi, ti: (bi, ti, 0)),
                   pl.BlockSpec((1, dw, tm), lambda bi, ti: (bi, 0, ti))),
        compiler_params=_cparams(("parallel", "parallel")),
        name="proj1",
    )(x, sc, sh, wq, wk, wv, ck, sk, cq, sq)


SHIFT_OK_LO = 2.0 ** -85
SHIFT_OK_HI = 2.0 ** 100
ATTN_TQ = 1024
ATTN_UNROLL = 16
ATTN_STATIC_CHUNKS = 8


def _key_norm_max(kv_refs, kmax_ref, *, tk, n_sub):
    sub_w = HEAD_SLOT // n_sub
    row = lax.broadcasted_iota(jnp.int32, (8, HEAD_SLOT), 0)
    lane = lax.broadcasted_iota(jnp.int32, (8, HEAD_SLOT), 1)
    sel = jnp.where(lane // sub_w == row, 1.0, 0.0).astype(BF16)
    best = None
    for k_ref, _ in kv_refs:
        nchunk = k_ref.shape[1] // tk
        group = max(g for g in range(1, 12) if nchunk % g == 0)
        rows = group * tk

        def body(c, mx, k_ref=k_ref, rows=rows):
            off = pl.multiple_of(c * rows, rows)
            kc = k_ref[0, pl.ds(off, rows), :].astype(F32)
            return jnp.maximum(mx, _dot_nt(sel, (kc * kc).astype(BF16)))

        mx = lax.fori_loop(0, nchunk // group, body, jnp.zeros((8, rows), F32))
        mx = jnp.max(mx, axis=1, keepdims=True)
        best = mx if best is None else jnp.maximum(best, mx)
    kmax_ref[...] = jnp.broadcast_to(best, kmax_ref.shape)


def _sweep_bounded(qs, shifts, kv_refs, s_ref, *, tk, unroll):
    tq = qs[0].shape[1]
    dv = kv_refs[0][1].shape[1]
    chunks = [(si, c * tk) for si, (k_ref, _) in enumerate(kv_refs) for c in range(k_ref.shape[1] // tk)]

    def aligned(off):
        return off if isinstance(off, int) else pl.multiple_of(off, tk)

    def scores(si, off, buf):
        kc = kv_refs[si][0][0, pl.ds(aligned(off), tk), :]
        for r, q in enumerate(qs):
            s_ref[buf, r] = _dot(kc, q)

    def consume(si, off, buf, carry):
        vc = kv_refs[si][1][0, :, pl.ds(aligned(off), tk)]
        nxt = []
        for r, ((l8, acc), m) in enumerate(zip(carry, shifts)):
            p = jnp.exp2(s_ref[buf, r] - m)
            l8 = l8 + jnp.sum(p.reshape(tk // 8, 8, tq), axis=0)
            nxt.append((l8, acc + _dot(vc, p.astype(BF16))))
        return tuple(nxt)

    def body(i, carry):
        for u in range(unroll):
            off = pl.multiple_of((i * unroll + u) * tk, tk)
            scores(0, off + tk, (u + 1) % 2)
            carry = consume(0, off, u % 2, carry)
        return carry

    carry = tuple((jnp.zeros((8, tq), F32), jnp.zeros((dv, tq), F32)) for _ in qs)
    scores(*chunks[0], 0)
    n_first = kv_refs[0][0].shape[1] // tk
    n_loop = 0 if len(chunks) <= ATTN_STATIC_CHUNKS else (n_first - 1) // unroll * unroll
    if n_loop:
        carry = lax.fori_loop(0, n_loop // unroll, body, carry)
    for c in range(n_loop, len(chunks)):
        if c + 1 < len(chunks):
            scores(*chunks[c + 1], (c + 1) % 2)
        carry = consume(*chunks[c], c % 2, carry)
    return [(jnp.sum(l8, axis=0, keepdims=True), acc) for l8, acc in carry]


def _sweep_running_max(qs, kv_refs, *, tk):
    tq = qs[0].shape[1]
    dv = kv_refs[0][1].shape[1]
    carry = tuple((jnp.full((1, tq), NEG_BIG, F32), jnp.zeros((1, tq), F32), jnp.zeros((dv, tq), F32)) for _ in qs)
    for k_ref, v_ref in kv_refs:

        def body(c, carry, k_ref=k_ref, v_ref=v_ref):
            off = pl.multiple_of(c * tk, tk)
            kc = k_ref[0, pl.ds(off, tk), :]
            vc = v_ref[0, :, pl.ds(off, tk)]
            nxt = []
            for (m, l, acc), q in zip(carry, qs):
                s = _dot(kc, q)
                mn = jnp.maximum(m, jnp.max(s, axis=0, keepdims=True))
                alpha = jnp.exp2(m - mn)
                p = jnp.exp2(s - mn)
                nxt.append((mn, alpha * l + jnp.sum(p, axis=0, keepdims=True),
                            alpha * acc + _dot(vc, p.astype(BF16))))
            return tuple(nxt)

        carry = lax.fori_loop(0, k_ref.shape[1] // tk, body, carry)
    return [(l, acc) for _, l, acc in carry]


def _attn_kernel(*refs, tk, unroll, n_sub, n_src, finalize):
    q_ref = refs[0]
    srcs = [(refs[1 + 2 * i], refs[2 + 2 * i]) for i in range(n_src)]
    n_scratch = 3 if n_src == 1 else 5
    extra = refs[1 + 2 * n_src:-n_scratch]
    o_ref, kmax_ref, s_ref = refs[-n_scratch:][:3]
    kv_refs = srcs if n_src == 1 else [tuple(refs[-2:])]

    @pl.when(pl.program_id(2) == 0)
    def _():
        if n_src > 1:
            kcat_ref, vcat_ref = kv_refs[0]
            off = 0
            for k_ref, v_ref in srcs:
                t = k_ref.shape[1]
                kcat_ref[0, off:off + t] = k_ref[0]
                vcat_ref[0, :, off:off + t] = v_ref[0]
                off += t
        _key_norm_max(kv_refs, kmax_ref, tk=tk, n_sub=n_sub)

    q = q_ref[0]
    sub_w = HEAD_SLOT // n_sub
    row = lax.broadcasted_iota(jnp.int32, q.shape, 0)
    qf = q.astype(F32)
    qs, shifts = [], []
    for r in range(n_sub):
        mine = (row // sub_w) == r
        qs.append(jnp.where(mine, q, jnp.zeros_like(q)) if n_sub > 1 else q)
        qn2 = jnp.sum(jnp.where(mine, qf * qf, 0.0), axis=0, keepdims=True)
        shifts.append(jnp.sqrt(qn2 * kmax_ref[r:r + 1, 0:1]))

    res = _sweep_bounded(qs, shifts, kv_refs, s_ref, tk=tk, unroll=unroll)
    o_ref[0] = finalize(res, extra).astype(BF16)
    ok = None
    for l, _ in res:
        ok_r = (jnp.min(l) >= SHIFT_OK_LO) & (jnp.max(l) <= SHIFT_OK_HI)
        ok = ok_r if ok is None else ok & ok_r

    @pl.when(jnp.logical_not(ok))
    def _():
        o_ref[0] = finalize(_sweep_running_max(qs, kv_refs, tk=tk), extra).astype(BF16)


def _finalize_plain(res, extra):
    (l, acc), = res
    return acc / l


def _finalize_diff(res, extra, *, lam_init):
    lq1_ref, lk1_ref, lq2_ref, lk2_ref, g_ref = extra
    (l1, a1), (l2, a2) = res
    lam = (jnp.exp(jnp.sum(lq1_ref[...] * lk1_ref[...], axis=-1, keepdims=True))
           - jnp.exp(jnp.sum(lq2_ref[...] * lk2_ref[...], axis=-1, keepdims=True)) + lam_init)
    o = a1 / l1 - lam * (a2 / l2)
    return o * lax.rsqrt(jnp.mean(o * o, axis=0, keepdims=True) + RMS_EPS) * g_ref[...] * (1.0 - lam_init)


def _attention(qt, kv, extra=(), *, dv, n_sub, finalize, name):
    b, hw, n = qt.shape
    heads = hw // HEAD_SLOT
    tq = _tile(n, ATTN_TQ)
    tk = 256 if all(k.shape[1] % 256 == 0 for k, _ in kv) else 128
    full = lambda a: pl.BlockSpec(a.shape, lambda bi, hi, qi: (0,) * a.ndim)
    kv_specs, kv_args = [], []
    for k, vt in kv:
        t = k.shape[1]
        kv_specs += [pl.BlockSpec((1, t, HEAD_SLOT), lambda bi, hi, qi: (bi, 0, hi)),
                     pl.BlockSpec((1, dv, t), lambda bi, hi, qi: (bi, hi, 0))]
        kv_args += [k, vt]
    scratch = [pltpu.VMEM((8, HEAD_SLOT), F32), pltpu.VMEM((2, n_sub, tk, tq), F32)]
    if len(kv) > 1:
        t_all = sum(k.shape[1] for k, _ in kv)
        scratch += [pltpu.VMEM((1, t_all, HEAD_SLOT), BF16), pltpu.VMEM((1, dv, t_all), BF16)]
    return pl.pallas_call(
        functools.partial(_attn_kernel, tk=tk, unroll=ATTN_UNROLL, n_sub=n_sub, n_src=len(kv), finalize=finalize),
        out_shape=jax.ShapeDtypeStruct((b, heads * dv, n), BF16),
        grid=(b, heads, n // tq),
        in_specs=[pl.BlockSpec((1, HEAD_SLOT, tq), lambda bi, hi, qi: (bi, hi, qi))] + kv_specs
        + [full(a) for a in extra],
        out_specs=pl.BlockSpec((1, dv, tq), lambda bi, hi, qi: (bi, hi, qi)),
        scratch_shapes=scratch,
        compiler_params=_cparams(("parallel", "parallel", "arbitrary")),
        name=name,
    )(qt, *kv_args, *extra)


def _out_kernel(*refs, n_rowmajor):
    (x_ref, g_ref, lng_ref, lnb_ref, scf_ref, shf_ref, rt_ref), rest = refs[:7], refs[7:]
    x1_ref, aff_ref, hp_ref = rest[-3:]
    ops = rest[:-3]
    m = None
    for i in range(len(ops) // 2):
        a = ops[2 * i][0]
        w = ops[2 * i + 1][...]
        part = _dot(a, w) if i < n_rowmajor else _dot_tn(a, w)
        m = part if m is None else m + part
    x1 = _ln(DEEPNORM_ALPHA * x_ref[0] + (1.0 + g_ref[0]) * m, lng_ref[...], lnb_ref[...])
    x1_ref[0] = x1
    hf = (x1 * (1.0 + scf_ref[0]) + shf_ref[0]).astype(BF16)
    lg = _dot_nt(rt_ref[...], hf)
    e = jnp.exp(lg - jnp.max(lg, axis=0, keepdims=True))
    aff_ref[0] = e / jnp.sum(e, axis=0, keepdims=True)
    dh = hf.shape[1] // 2
    lo = pltpu.bitcast(hf[:, :dh].astype(F32), jnp.uint32)
    hi = pltpu.bitcast(hf[:, dh:].astype(F32), jnp.uint32)
    hp_ref[0] = _to_tiled(lax.shift_right_logical(lo, jnp.uint32(16)) | (hi & jnp.uint32(0xFFFF0000)))


def _mixer_out(x, g, lng, lnb, scf, shf, router_t, rowmajor_ops, chanmajor_ops):
    b, n, d = x.shape
    e = router_t.shape[0]
    tm = _tile(n, 512)
    vec = pl.BlockSpec((1, 1, d), lambda bi, ti: (bi, 0, 0))
    full = lambda a: pl.BlockSpec(a.shape, lambda bi, ti: (0,) * a.ndim)
    args = [x, g, lng, lnb, scf, shf, router_t]
    specs = [pl.BlockSpec((1, tm, d), lambda bi, ti: (bi, ti, 0)), vec, full(lng), full(lnb), vec, vec,
             full(router_t)]
    for a, w in rowmajor_ops:
        args += [a, w]
        specs += [pl.BlockSpec((1, tm, a.shape[2]), lambda bi, ti: (bi, ti, 0)), full(w)]
    for a, w in chanmajor_ops:
        args += [a, w]
        specs += [pl.BlockSpec((1, a.shape[1], tm), lambda bi, ti: (bi, 0, ti)), full(w)]
    return pl.pallas_call(
        functools.partial(_out_kernel, n_rowmajor=len(rowmajor_ops)),
        out_shape=(jax.ShapeDtypeStruct((b, n, d), F32), jax.ShapeDtypeStruct((b, e, n), F32),
                   jax.ShapeDtypeStruct((b, n * (d // 2) // LANES, LANES), jnp.uint32)),
        grid=(b, n // tm),
        in_specs=specs,
        out_specs=(pl.BlockSpec((1, tm, d), lambda bi, ti: (bi, ti, 0)),
                   pl.BlockSpec((1, e, tm), lambda bi, ti: (bi, 0, ti)),
                   pl.BlockSpec((1, tm * (d // 2) // LANES, LANES), lambda bi, ti: (bi, ti, 0))),
        compiler_params=_cparams(("parallel", "parallel")),
        name="mixer_out",
    )(*args)


ROW_UNROLL = 8
FFN_ROWS = 512
TOKEN_SPLIT = 4096


def _ffn_kernel(src_ref, nxt_ref, hp_ref, g_ref, wg_ref, wu_ref, wd_ref, y_ref, rows_ref, xs_ref, *, prefetch):
    cap = y_ref.shape[2]
    nt = rows_ref.shape[0] // cap
    dh = nt * LANES

    def gather(idx_ref, j, dst):
        rows_ref[pl.ds(dst, nt, stride=SUBLANES), :] = hp_ref[0, pl.ds(idx_ref[0, 0, j], nt, stride=SUBLANES), :]

    def gather_loop():
        def body(jb, carry):
            for u in range(ROW_UNROLL):
                gather(src_ref, jb * ROW_UNROLL + u, jb * (ROW_UNROLL * nt) + u)
            return carry

        lax.fori_loop(0, cap // ROW_UNROLL, body, 0)

    if prefetch:
        pl.when(pl.program_id(1) == 0)(gather_loop)
    else:
        gather_loop()

    blk = min(cap, FFN_ROWS)
    for c0 in range(0, cap, blk):
        w = _from_tiled(rows_ref[c0 * nt:(c0 + blk) * nt], dh)
        lo = pltpu.bitcast(lax.shift_left(w, jnp.uint32(16)), F32)
        hi = pltpu.bitcast(w & jnp.uint32(0xFFFF0000), F32)
        xs_ref[c0:c0 + blk] = jnp.concatenate([lo, hi], axis=1).astype(BF16)

    if prefetch:
        for j in range(cap):
            gather(nxt_ref, j, _tiled_row(j, dh))

    for c0 in range(0, cap, blk):
        xs = xs_ref[c0:c0 + blk]
        gate = _dot(xs, wg_ref[0])
        hid = gate / (1.0 + jnp.exp(-gate)) * _dot(xs, wu_ref[0])
        y_ref[0, 0, c0:c0 + blk] = (_dot(hid.astype(BF16), wd_ref[0]) * g_ref[0, c0:c0 + blk]).astype(BF16)


def _activations_outweigh(activation_bytes, weight_bytes):
    return activation_bytes > weight_bytes


def _moe_ffn(hp, src3, gates3, wg, wu, wd, cap):
    b, hp_rows, _ = hp.shape
    n_exp, d, ff = wg.shape
    nt = d // 2 // LANES
    sample_major = _activations_outweigh(hp_rows * LANES * 4, 3 * d * ff * 2)
    if sample_major:
        grid, be = (b, n_exp), (lambda g0, g1: (g0, g1))
        hp_spec = pl.BlockSpec((1, hp_rows, LANES), lambda g0, g1: (g0, 0, 0), pipeline_mode=pl.Buffered(1))
    else:
        grid, be = (n_exp, b), (lambda g0, g1: (g1, g0))
        hp_spec = pl.BlockSpec((1, hp_rows, LANES), lambda g0, g1: (g1, 0, 0))
    slot = lambda g0, g1: (be(g0, g1)[0] * n_exp + be(g0, g1)[1], 0, 0)
    weight = lambda g0, g1: (be(g0, g1)[1], 0, 0)
    nxt = lambda g0, g1: (be(g0, g1)[0] * n_exp + jnp.minimum(be(g0, g1)[1] + 1, n_exp - 1), 0, 0)
    return pl.pallas_call(
        functools.partial(_ffn_kernel, prefetch=sample_major),
        out_shape=jax.ShapeDtypeStruct((b, n_exp, cap, d), BF16),
        grid=grid,
        in_specs=[pl.BlockSpec((1, 1, cap), slot, memory_space=pltpu.SMEM),
                  pl.BlockSpec((1, 1, cap), nxt, memory_space=pltpu.SMEM),
                  hp_spec,
                  pl.BlockSpec((1, cap, 1), slot),
                  pl.BlockSpec((1, d, ff), weight),
                  pl.BlockSpec((1, d, ff), weight),
                  pl.BlockSpec((1, ff, d), weight)],
        out_specs=pl.BlockSpec((1, 1, cap, d), lambda g0, g1: be(g0, g1) + (0, 0)),
        scratch_shapes=[pltpu.VMEM((cap * nt, LANES), jnp.uint32), pltpu.VMEM((cap, d), BF16)],
        compiler_params=_cparams(("parallel", "arbitrary")),
        name="moe_ffn",
    )(src3, src3, hp, gates3, wg, wu, wd)


def _scatter_kernel(dst_ref, bnd_ref, y_ref, f_ref, rows_ref):
    d = y_ref.shape[3]
    nt = d // LANES

    @pl.when(pl.program_id(2) == 0)
    def _():
        f_ref[...] = jnp.zeros_like(f_ref)

    rows_ref[...] = _to_tiled(y_ref[0, 0].astype(F32))
    lo = bnd_ref[0, 0, pl.program_id(1)]
    hi = bnd_ref[0, 0, pl.program_id(1) + 1]

    def row(ref, start):
        return ref[pl.ds(start, nt, stride=SUBLANES), :]

    def one(j, carry):
        dst = dst_ref[0, 0, j]
        f_ref[0, pl.ds(dst, nt, stride=SUBLANES), :] = row(f_ref.at[0], dst) + row(rows_ref, _tiled_row(j, d))
        return carry

    def batch(jb, carry):
        dsts = [dst_ref[0, 0, jb * ROW_UNROLL + u] for u in range(ROW_UNROLL)]
        sums = [row(f_ref.at[0], dsts[u]) + row(rows_ref, jb * (ROW_UNROLL * nt) + u) for u in range(ROW_UNROLL)]
        for u in range(ROW_UNROLL):
            f_ref[0, pl.ds(dsts[u], nt, stride=SUBLANES), :] = sums[u]
        return carry

    head_end = jnp.minimum((lo + ROW_UNROLL - 1) // ROW_UNROLL * ROW_UNROLL, hi)
    tail_start = jnp.maximum(head_end, hi // ROW_UNROLL * ROW_UNROLL)
    lax.fori_loop(lo, head_end, one, 0)
    lax.fori_loop(head_end // ROW_UNROLL, tail_start // ROW_UNROLL, batch, 0)
    lax.fori_loop(tail_start, hi, one, 0)


def _moe_scatter(y, dst3, bounds3, n):
    b, n_exp, cap, d = y.shape
    nt = _tile(n, TOKEN_SPLIT)
    lt = d // LANES
    return pl.pallas_call(
        _scatter_kernel,
        out_shape=jax.ShapeDtypeStruct((b, n * lt, LANES), F32),
        grid=(b, n // nt, n_exp),
        in_specs=[pl.BlockSpec((1, 1, cap), lambda bi, ki, ei: (bi * n_exp + ei, 0, 0), memory_space=pltpu.SMEM),
                  pl.BlockSpec((1, 1, n // nt + 1), lambda bi, ki, ei: (bi * n_exp + ei, 0, 0),
                               memory_space=pltpu.SMEM),
                  pl.BlockSpec((1, 1, cap, d), lambda bi, ki, ei: (bi, ei, 0, 0))],
        out_specs=pl.BlockSpec((1, nt * lt, LANES), lambda bi, ki, ei: (bi, ki, 0)),
        scratch_shapes=[pltpu.VMEM((cap * lt, LANES), F32)],
        compiler_params=_cparams(("parallel", "parallel", "arbitrary")),
        name="moe_scatter",
    )(dst3, bounds3, y)


def _post_kernel(x_ref, f_ref, g_ref, lng_ref, lnb_ref, o_ref):
    f = _from_tiled(f_ref[0], x_ref.shape[2])
    o_ref[0] = _ln(DEEPNORM_ALPHA * x_ref[0] + (1.0 + g_ref[0]) * f, lng_ref[...], lnb_ref[...])


def _post(x1, f_tiled, g, lng, lnb):
    b, n, d = x1.shape
    tm = _tile(n, 512)
    blk = pl.BlockSpec((1, tm, d), lambda bi, ti: (bi, ti, 0))
    f_blk = pl.BlockSpec((1, tm * d // LANES, LANES), lambda bi, ti: (bi, ti, 0))
    full = lambda a: pl.BlockSpec(a.shape, lambda bi, ti: (0,) * a.ndim)
    return pl.pallas_call(
        _post_kernel,
        out_shape=jax.ShapeDtypeStruct((b, n, d), F32),
        grid=(b, n // tm),
        in_specs=[blk, f_blk, pl.BlockSpec((1, 1, d), lambda bi, ti: (bi, 0, 0)), full(lng), full(lnb)],
        out_specs=blk,
        compiler_params=_cparams(("parallel", "parallel")),
        name="ffn_post",
    )(x1, f_tiled, g, lng, lnb)


def _select_kernel(aff_ref, idx_ref, gate_ref, *, cap, n_exp):
    a = aff_ref[0]
    er, lanes = a.shape
    r_n = er // n_exp
    e_pad = 16
    bits = pltpu.bitcast(a, jnp.int32)

    member = (lax.broadcasted_iota(jnp.int32, (e_pad, er), 1) // r_n
              == lax.broadcasted_iota(jnp.int32, (e_pad, er), 0))
    gs = jnp.where(member, 1.0, 0.0).astype(BF16)
    member_t = (lax.broadcasted_iota(jnp.int32, (er, e_pad), 0) // r_n
                == lax.broadcasted_iota(jnp.int32, (er, e_pad), 1))
    gst = jnp.where(member_t, 1.0, 0.0).astype(BF16)
    ri = lax.broadcasted_iota(jnp.int32, (er, er), 0)
    rj = lax.broadcasted_iota(jnp.int32, (er, er), 1)
    rows_before = jnp.where((ri // r_n == rj // r_n) & (rj < ri), 1.0, 0.0).astype(BF16)
    li = lax.broadcasted_iota(jnp.int32, (lanes, lanes), 0)
    lj = lax.broadcasted_iota(jnp.int32, (lanes, lanes), 1)
    lanes_upto = jnp.where(li <= lj, 1.0, 0.0).astype(BF16)

    def expert_total(m):
        per_lane = _dot(gs, m.astype(BF16))
        return jnp.broadcast_to(jnp.sum(per_lane, axis=1, keepdims=True), per_lane.shape)

    def to_rows(ev):
        if r_n % SUBLANES == 0 and n_exp == e_pad:
            return jnp.broadcast_to(ev[:, None, :], (e_pad, r_n, lanes)).reshape(er, lanes)
        hi = jnp.floor(ev * (1.0 / 128.0))
        return 128.0 * _dot(gst, hi.astype(BF16)) + _dot(gst, (ev - 128.0 * hi).astype(BF16))

    def search(i, t):
        cand = t | lax.shift_left(jnp.int32(1), 30 - i)
        cnt = expert_total(jnp.where(bits >= cand, 1.0, 0.0))
        return jnp.where(to_rows(jnp.where(cnt >= cap, 1.0, 0.0)) > 0.5, cand, t)

    t = lax.fori_loop(0, 31, search, jnp.zeros((er, lanes), jnp.int32))
    gt = jnp.where(bits > t, 1.0, 0.0)
    eq = jnp.where(bits == t, 1.0, 0.0)

    def prefix(m):
        rc = _dot(m.astype(BF16), lanes_upto)
        off = _dot(rows_before, jnp.broadcast_to(rc[:, lanes - 1:lanes], m.shape).astype(BF16))
        return rc, off

    rc_eq, off_eq = prefix(eq)
    need = cap - to_rows(expert_total(gt))
    sel = jnp.maximum(gt, eq * jnp.where(off_eq + rc_eq - eq < need, 1.0, 0.0))
    rc, off = prefix(sel)
    row_end = off + jnp.broadcast_to(rc[:, lanes - 1:lanes], rc.shape)

    slot = lax.broadcasted_iota(jnp.int32, (cap, lanes), 0).astype(F32) + 1.0
    lane_c = lax.broadcasted_iota(jnp.int32, (cap, lanes), 1)
    lane_cf = lane_c.astype(F32)
    diag = lax.broadcasted_iota(jnp.int32, (r_n, lanes), 0) == lax.broadcasted_iota(jnp.int32, (r_n, lanes), 1)
    lane_r = lax.broadcasted_iota(jnp.int32, (1, lanes), 1)
    pad = jnp.zeros((lanes - r_n, lanes), F32)

    def table(m):
        return (jnp.concatenate([m, pad], axis=0) if r_n < lanes else m).astype(BF16)

    all_ones = jnp.ones((lanes, lanes), BF16)
    idx_all = jnp.zeros((cap, lanes), jnp.int32)
    gate_all = jnp.zeros((cap, lanes), F32)
    for e in range(n_exp):
        r0 = e * r_n
        end_lane = jnp.sum(jnp.where(diag, row_end[r0:r0 + r_n], 0.0), axis=0, keepdims=True)
        end_lane = jnp.where(lane_r < r_n, end_lane, 3.0e38)
        row_j = jnp.sum(jnp.where(end_lane < slot, 1.0, 0.0), axis=1, keepdims=True)
        oh = jnp.where(lane_cf == row_j, 1.0, 0.0).astype(BF16)
        cum_e = off[r0:r0 + r_n] + rc[r0:r0 + r_n]
        cum_hi = jnp.floor(cum_e * (1.0 / 128.0))
        cum_2 = _dot(oh, jnp.concatenate([table(cum_hi), table(cum_e - 128.0 * cum_hi)], axis=1))
        cum_j = 128.0 * cum_2[:, :lanes] + cum_2[:, lanes:]
        lane_j = _dot(jnp.where(cum_j < slot, 1.0, 0.0).astype(BF16), all_ones)
        a_e = a[r0:r0 + r_n]
        a1 = a_e.astype(BF16).astype(F32)
        a2 = (a_e - a1).astype(BF16).astype(F32)
        a3 = a_e - a1 - a2
        aff_2 = _dot(oh, jnp.concatenate([table(a1), table(a2)], axis=1))
        aff_j = aff_2[:, :lanes] + aff_2[:, lanes:] + _dot(oh, table(a3))
        gate_j = jnp.sum(jnp.where(lane_cf == lane_j, aff_j, 0.0), axis=1, keepdims=True)
        idx_j = (row_j * float(lanes) + lane_j).astype(jnp.int32)
        idx_all = jnp.where(lane_c == e, idx_j, idx_all)
        gate_all = jnp.where(lane_c == e, gate_j, gate_all)
    idx_ref[0] = idx_all
    gate_ref[0] = gate_all


def _moe_select(aff_t, cap):
    b, n_exp, n = aff_t.shape
    lanes = HEAD_SLOT
    er = n_exp * (n // lanes)
    idx, gates = pl.pallas_call(
        functools.partial(_select_kernel, cap=cap, n_exp=n_exp),
        out_shape=(jax.ShapeDtypeStruct((b, cap, lanes), jnp.int32), jax.ShapeDtypeStruct((b, cap, lanes), F32)),
        grid=(b,),
        in_specs=[pl.BlockSpec((1, er, lanes), lambda bi: (bi, 0, 0))],
        out_specs=(pl.BlockSpec((1, cap, lanes), lambda bi: (bi, 0, 0)),
                   pl.BlockSpec((1, cap, lanes), lambda bi: (bi, 0, 0))),
        compiler_params=_cparams(("parallel",)),
        name="moe_select",
    )(aff_t.reshape(b, er, lanes))
    to_expert_major = lambda m: jnp.swapaxes(m[:, :, :n_exp], 1, 2)
    return to_expert_major(gates), to_expert_major(idx)


def _moe(x1, aff_t, hp, gf, wg, wu, wd, lng, lnb):
    b, n, d = x1.shape
    n_exp = aff_t.shape[1]
    cap = CAPACITY_FACTOR * n // n_exp
    gates, idx = _moe_select(aff_t, cap)
    gates3 = gates.reshape(b * n_exp, cap, 1)
    nt = _tile(n, TOKEN_SPLIT)
    edges = jnp.arange(0, n + 1, nt, dtype=jnp.int32)
    bounds3 = jnp.sum(idx[..., None] < edges, axis=2, dtype=jnp.int32).reshape(b * n_exp, 1, n // nt + 1)
    src3 = _tiled_row(idx, d // 2).reshape(b * n_exp, 1, cap)
    dst3 = _tiled_row(idx % nt, d).reshape(b * n_exp, 1, cap)
    y = _moe_ffn(hp, src3, gates3, wg, wu, wd, cap)
    f_tiled = _moe_scatter(y, dst3, bounds3, n)
    return _post(x1, f_tiled, gf, lng, lnb)


def _rope_angles(n_tokens, dim):
    n_rows = n_tokens // GRID_W
    row = jnp.repeat(jnp.arange(n_rows, dtype=F32), GRID_W)
    col = jnp.tile(jnp.arange(GRID_W, dtype=F32), n_rows)
    n_freq = dim // 4
    inv_freq = ROPE_BASE ** (-jnp.arange(n_freq, dtype=F32) / n_freq)
    ang = jnp.concatenate([row[:, None] * inv_freq, col[:, None] * inv_freq], -1)
    return jnp.cos(ang), jnp.sin(ang)


def _rope_tables(n_tokens, dim, lane_offset, identity):
    half = dim // 2
    if identity:
        cos = jnp.ones((n_tokens, half), F32)
        sin = jnp.zeros((n_tokens, half), F32)
    else:
        cos, sin = _rope_angles(n_tokens, dim)
    c2 = jnp.concatenate([cos, cos], -1)
    s2 = jnp.concatenate([-sin, sin], -1)
    reps = (HEAD_SLOT - lane_offset) // dim if lane_offset == 0 else 1
    ck = jnp.zeros((n_tokens, HEAD_SLOT), F32)
    sk = jnp.zeros((n_tokens, HEAD_SLOT), F32)
    for r in range(reps):
        ck = ck.at[:, lane_offset + r * dim:lane_offset + (r + 1) * dim].set(c2)
        sk = sk.at[:, lane_offset + r * dim:lane_offset + (r + 1) * dim].set(s2)
    return ck, sk, c2.T, s2.T


def _deinterleave(w, axis):
    return jnp.concatenate([jnp.take(w, jnp.arange(0, w.shape[axis], 2), axis=axis),
                            jnp.take(w, jnp.arange(1, w.shape[axis], 2), axis=axis)], axis=axis)


def _prep_layer0(w_in, ln_g, ln_b, ws, bs, q_norm, w_uq, kv_norm, w_ukv):
    d = w_in.shape[0]
    w = GMLP_WIDTH
    o_kr = 2 * w + MLA_Q_RANK + MLA_KV_RANK
    kr = _deinterleave(w_in[:, o_kr:o_kr + MLA_ROPE], 1)
    kr_sw = jnp.concatenate([kr[:, MLA_ROPE // 2:], kr[:, :MLA_ROPE // 2]], 1)
    slot = lambda m: jnp.zeros((d, HEAD_SLOT), F32).at[:, MLA_NOPE:MLA_NOPE + MLA_ROPE].set(m)
    win = jnp.concatenate([w_in[:, :o_kr], slot(kr), slot(kr_sw)], 1).astype(BF16)

    qd = MLA_NOPE + MLA_ROPE
    wq = w_uq.reshape(MLA_Q_RANK, MLA_HEADS, qd)
    wq = jnp.concatenate([wq[..., :MLA_NOPE], _deinterleave(wq[..., MLA_NOPE:], 2),
                          jnp.zeros((MLA_Q_RANK, MLA_HEADS, HEAD_SLOT - qd), F32)], -1)
    wq_t = wq.reshape(MLA_Q_RANK, MLA_HEADS * HEAD_SLOT).T.astype(BF16)

    wkv = w_ukv.reshape(MLA_KV_RANK, MLA_HEADS, MLA_NOPE + MLA_V)
    wk = jnp.concatenate([wkv[..., :MLA_NOPE], jnp.zeros((MLA_KV_RANK, MLA_HEADS, HEAD_SLOT - MLA_NOPE), F32)], -1)
    wk = wk.reshape(MLA_KV_RANK, MLA_HEADS * HEAD_SLOT).astype(BF16)
    wv_t = wkv[..., MLA_NOPE:].reshape(MLA_KV_RANK, MLA_HEADS * MLA_V).T.astype(BF16)

    bsb = jnp.broadcast_to(bs[:, :, None], (GMLP_GROUPS, CHUNK, GMLP_GROUP_CH))
    return (win, ln_g.reshape(1, -1), ln_b.reshape(1, -1), ws.astype(BF16), bsb,
            q_norm.reshape(1, -1), wq_t, kv_norm.reshape(1, -1), wk, wv_t)


def _prep_layer1(w_in):
    d = w_in.shape[0]
    dw = 2 * DIFF_HEADS * DIFF_HEAD_DIM
    perm = lambda m: _deinterleave(m.reshape(d, 2 * DIFF_HEADS, DIFF_HEAD_DIM), 2).reshape(d, dw)
    wq_t = perm(w_in[:, :dw]).T.astype(BF16)
    wk = perm(w_in[:, dw:2 * dw]).astype(BF16)
    wv_t = w_in[:, 2 * dw:].T.astype(BF16)
    return wq_t, wk, wv_t


def _split6(m_row):
    return [v[:, None, :] for v in jnp.split(m_row, 6, axis=-1)]


def kernel(x, c, ctx, c_ctx, w_mod_0, b_mod_0, w_in_0, gmlp_ln_g_0, gmlp_ln_b_0, gmlp_ws_0, gmlp_bs_0, mla_q_norm_0, mla_w_uq_0, mla_kv_norm_0, mla_w_ukv_0, w_out_0, ln_mix_g_0, ln_mix_b_0, router_0, w_gate_0, w_up_0, w_down_0, ln_ffn_g_0, ln_ffn_b_0, w_mod_1, b_mod_1, w_in_1, lambda_q1_1, lambda_k1_1, lambda_q2_1, lambda_k2_1, subln_g_1, w_out_1, ln_mix_g_1, ln_mix_b_1, router_1, w_gate_1, w_up_1, w_down_1, ln_ffn_g_1, ln_ffn_b_1):
    b, n, d = x.shape
    m_ctx = ctx.shape[1]
    row = lambda v: v.reshape(1, -1)

    cond = jnp.concatenate([c, c_ctx[None, :], jnp.zeros((-(b + 1) % 8, d), F32)], 0)
    mod0 = _modulation(cond, w_mod_0, b_mod_0)
    mod1 = _modulation(cond, w_mod_1, b_mod_1)
    sh_a0, sc_a0, g_a0, sh_f0, sc_f0, g_f0 = _split6(mod0[:b])
    csh_a0, csc_a0, cg_a0, csh_f0, csc_f0, cg_f0 = [jnp.broadcast_to(v, (b, 1, d)) for v in _split6(mod0[b:b + 1])]
    sh_a1, sc_a1, g_a1, sh_f1, sc_f1, g_f1 = _split6(mod1[:b])
    csh_a1, csc_a1 = [jnp.broadcast_to(v, (b, 1, d)) for v in _split6(mod1[b:b + 1])[:2]]

    wts0 = _prep_layer0(w_in_0, gmlp_ln_g_0, gmlp_ln_b_0, gmlp_ws_0, gmlp_bs_0,
                        mla_q_norm_0, mla_w_uq_0, mla_kv_norm_0, mla_w_ukv_0)
    tabs_l = _rope_tables(n, MLA_ROPE, MLA_NOPE, identity=False)
    tabs_c = _rope_tables(m_ctx, MLA_ROPE, MLA_NOPE, identity=True)
    a_l, q_l, k_l, v_l = _proj0(x, sc_a0, sh_a0, wts0, tabs_l)
    a_c, q_c, k_c, v_c = _proj0(ctx, csc_a0, csh_a0, wts0, tabs_c)
    o_l = _attention(q_l, [(k_l, v_l), (k_c, v_c)], dv=MLA_V, n_sub=1, finalize=_finalize_plain, name="mla_attn")
    o_c = _attention(q_c, [(k_c, v_c)], dv=MLA_V, n_sub=1, finalize=_finalize_plain, name="mla_attn_ctx")

    w_out_a = w_out_0[:GMLP_WIDTH].astype(BF16)
    w_out_o = w_out_0[GMLP_WIDTH:].astype(BF16)
    router_t0 = router_0.T.astype(BF16)
    wg0, wu0, wd0 = w_gate_0.astype(BF16), w_up_0.astype(BF16), w_down_0.astype(BF16)
    lng, lnb = row(ln_mix_g_0), row(ln_mix_b_0)
    x1, aff, hp = _mixer_out(x, g_a0, lng, lnb, sc_f0, sh_f0, router_t0, [(a_l, w_out_a)], [(o_l, w_out_o)])
    x_lat = _moe(x1, aff, hp, g_f0, wg0, wu0, wd0, row(ln_ffn_g_0), row(ln_ffn_b_0))
    c1, caff, chp = _mixer_out(ctx, cg_a0, lng, lnb, csc_f0, csh_f0, router_t0, [(a_c, w_out_a)], [(o_c, w_out_o)])
    x_ctx = _moe(c1, caff, chp, cg_f0, wg0, wu0, wd0, row(ln_ffn_g_0), row(ln_ffn_b_0))

    lam_init = 0.8 - 0.6 * math.exp(-0.3 * 1)
    wts1 = _prep_layer1(w_in_1)
    tabs_l = _rope_tables(n, DIFF_HEAD_DIM, 0, identity=False)
    tabs_c = _rope_tables(m_ctx, DIFF_HEAD_DIM, 0, identity=True)
    q_l, k_l, v_l = _proj1(x_lat, sc_a1, sh_a1, wts1, tabs_l)
    _, k_c, v_c = _proj1(x_ctx, csc_a1, csh_a1, wts1, tabs_c)
    extra = (row(lambda_q1_1), row(lambda_k1_1), row(lambda_q2_1), row(lambda_k2_1), subln_g_1.reshape(-1, 1))
    o_l = _attention(q_l, [(k_l, v_l), (k_c, v_c)], extra, dv=2 * DIFF_HEAD_DIM, n_sub=2,
                     finalize=functools.partial(_finalize_diff, lam_init=lam_init), name="diff_attn")
    x1, aff, hp = _mixer_out(x_lat, g_a1, row(ln_mix_g_1), row(ln_mix_b_1), sc_f1, sh_f1, router_1.T.astype(BF16),
                             [], [(o_l, w_out_1.astype(BF16))])
    return _moe(x1, aff, hp, g_f1, w_gate_1.astype(BF16), w_up_1.astype(BF16), w_down_1.astype(BF16),
                row(ln_ffn_g_1), row(ln_ffn_b_1))
```

```python
import functools
import math

import jax
import jax.numpy as jnp
from jax import lax
from jax.experimental import pallas as pl
from jax.experimental.pallas import tpu as pltpu

F32 = jnp.float32
BF16 = jnp.bfloat16

DEPTH = 2
GRID_W = 64
ROPE_BASE = 10000.0
CHUNK = 128
GMLP_GROUPS = 4
GMLP_GROUP_CH = 128
GMLP_WIDTH = GMLP_GROUPS * GMLP_GROUP_CH
MLA_HEADS = 8
MLA_NOPE = 64
MLA_ROPE = 32
MLA_V = 64
MLA_Q_RANK = 256
MLA_KV_RANK = 128
DIFF_HEADS = 8
DIFF_HEAD_DIM = 64
CAPACITY_FACTOR = 2
DEEPNORM_ALPHA = (2 * DEPTH) ** 0.25
LN_EPS = 1e-5
RMS_EPS = 1e-6
LOG2E = 1.4426950408889634

LANES = 128
SUBLANES = 8
HEAD_SLOT = LANES
NEG_BIG = -1e30
VMEM_LIMIT = 56 * 1024 * 1024
TOKEN_TILE = 512
MOD_TILE = 1024


def _cparams(sem):
    return pltpu.CompilerParams(dimension_semantics=sem, vmem_limit_bytes=VMEM_LIMIT)


def _tile(n, pref):
    return pref if n % pref == 0 else n


def _ln(x, g, b):
    mu = jnp.mean(x, axis=-1, keepdims=True)
    xc = x - mu
    var = jnp.mean(xc * xc, axis=-1, keepdims=True)
    return xc * lax.rsqrt(var + LN_EPS) * g + b


def _rms(x, g):
    return x * lax.rsqrt(jnp.mean(x * x, axis=-1, keepdims=True) + RMS_EPS) * g


def _gelu(x):
    return 0.5 * x * (1.0 + lax.erf(x * (2.0 ** -0.5)))


def _to_tiled(v):
    m, w = v.shape
    nt = w // LANES
    tiles = [v[:, c * LANES:(c + 1) * LANES].reshape(m // SUBLANES, SUBLANES, LANES) for c in range(nt)]
    return jnp.stack(tiles, axis=1).reshape(m * nt, LANES)


def _from_tiled(t, w):
    nt = w // LANES
    m = t.shape[0] // nt
    t4 = t.reshape(m // SUBLANES, nt, SUBLANES, LANES)
    return jnp.concatenate([t4[:, c].reshape(m, LANES) for c in range(nt)], axis=1)


def _tiled_row(i, w):
    return (i // SUBLANES) * (SUBLANES * (w // LANES)) + i % SUBLANES


def _dot(a, b):
    return jnp.dot(a, b, preferred_element_type=F32)


def _dot_nt(a, b):
    return lax.dot_general(a, b, (((1,), (1,)), ((), ())), preferred_element_type=F32)


def _dot_tn(a, b):
    return lax.dot_general(a, b, (((0,), (0,)), ((), ())), preferred_element_type=F32)


def _mod_kernel(c_ref, w_ref, b_ref, o_ref):
    c = c_ref[...]
    s = c / (1.0 + jnp.exp(-c))
    o_ref[...] = _dot(s.astype(BF16), w_ref[...].astype(BF16)) + b_ref[...]


def _modulation(cond, w_mod, b_mod):
    r, d = cond.shape
    n = w_mod.shape[1]
    tn = _tile(n, MOD_TILE)
    return pl.pallas_call(
        _mod_kernel,
        out_shape=jax.ShapeDtypeStruct((r, n), F32),
        grid=(n // tn,),
        in_specs=[pl.BlockSpec((r, d), lambda j: (0, 0)),
                  pl.BlockSpec((d, tn), lambda j: (0, j)),
                  pl.BlockSpec((1, tn), lambda j: (0, j))],
        out_specs=pl.BlockSpec((r, tn), lambda j: (0, j)),
        compiler_params=_cparams(("parallel",)),
        name="modulation",
    )(cond, w_mod, b_mod.reshape(1, n))


def _proj0_kernel(x_ref, sc_ref, sh_ref, win_ref, lng_ref, lnb_ref, ws_ref, bs_ref,
                  qn_ref, wq_ref, kvn_ref, wk_ref, wv_ref, ck_ref, sk_ref, cq_ref, sq_ref,
                  a_ref, q_ref, k_ref, v_ref, *, q_scale):
    tm = x_ref.shape[1]
    h = (x_ref[0] * (1.0 + sc_ref[0]) + sh_ref[0]).astype(BF16)
    y = _dot(h, win_ref[...])
    w = GMLP_WIDTH
    u = _gelu(y[:, :w])
    vn = _ln(_gelu(y[:, w:2 * w]), lng_ref[...], lnb_ref[...]).astype(BF16)
    for ci in range(tm // CHUNK):
        r0 = ci * CHUNK
        for g in range(GMLP_GROUPS):
            c0 = g * GMLP_GROUP_CH
            mixed = _dot(ws_ref[g], vn[r0:r0 + CHUNK, c0:c0 + GMLP_GROUP_CH]) + bs_ref[g]
            a_ref[0, r0:r0 + CHUNK, c0:c0 + GMLP_GROUP_CH] = (
                u[r0:r0 + CHUNK, c0:c0 + GMLP_GROUP_CH] * mixed).astype(BF16)

    o = 2 * w
    cq = _rms(y[:, o:o + MLA_Q_RANK], qn_ref[...]).astype(BF16)
    qt = _dot_nt(wq_ref[...], cq) * q_scale
    q_ref[0] = qt.astype(BF16)
    hr = MLA_ROPE // 2
    cq_t = cq_ref[...]
    sq_t = sq_ref[...]
    for hd in range(MLA_HEADS):
        r0 = hd * HEAD_SLOT + MLA_NOPE
        x1 = qt[r0:r0 + hr]
        x2 = qt[r0 + hr:r0 + 2 * hr]
        blk = jnp.concatenate([x1, x2], axis=0)
        swp = jnp.concatenate([x2, x1], axis=0)
        q_ref[0, r0:r0 + 2 * hr, :] = (blk * cq_t + swp * sq_t).astype(BF16)

    o += MLA_Q_RANK
    ckv = _rms(y[:, o:o + MLA_KV_RANK], kvn_ref[...]).astype(BF16)
    o += MLA_KV_RANK
    kr = y[:, o:o + HEAD_SLOT] * ck_ref[...] + y[:, o + HEAD_SLOT:o + 2 * HEAD_SLOT] * sk_ref[...]
    k_ref[0] = (_dot(ckv, wk_ref[...]) + jnp.tile(kr, (1, MLA_HEADS))).astype(BF16)
    v_ref[0] = _dot_nt(wv_ref[...], ckv).astype(BF16)


def _proj0(x, sc, sh, wts, tabs):
    b, n, d = x.shape
    tm = _tile(n, TOKEN_TILE)
    ck, sk, cq, sq = tabs
    win, lng, lnb, ws, bsb, qn, wq, kvn, wk, wv = wts
    hw = MLA_HEADS * HEAD_SLOT
    vw = MLA_HEADS * MLA_V
    full = lambda a: pl.BlockSpec(a.shape, lambda bi, ti: (0,) * a.ndim)
    q_scale = (MLA_NOPE + MLA_ROPE) ** -0.5 * LOG2E
    return pl.pallas_call(
        functools.partial(_proj0_kernel, q_scale=q_scale),
        out_shape=(jax.ShapeDtypeStruct((b, n, GMLP_WIDTH), BF16),
                   jax.ShapeDtypeStruct((b, hw, n), BF16),
                   jax.ShapeDtypeStruct((b, n, hw), BF16),
                   jax.ShapeDtypeStruct((b, vw, n), BF16)),
        grid=(b, n // tm),
        in_specs=[pl.BlockSpec((1, tm, d), lambda bi, ti: (bi, ti, 0)),
                  pl.BlockSpec((1, 1, d), lambda bi, ti: (bi, 0, 0)),
                  pl.BlockSpec((1, 1, d), lambda bi, ti: (bi, 0, 0)),
                  full(win), full(lng), full(lnb), full(ws), full(bsb),
                  full(qn), full(wq), full(kvn), full(wk), full(wv),
                  pl.BlockSpec((tm, HEAD_SLOT), lambda bi, ti: (ti, 0)),
                  pl.BlockSpec((tm, HEAD_SLOT), lambda bi, ti: (ti, 0)),
                  pl.BlockSpec((MLA_ROPE, tm), lambda bi, ti: (0, ti)),
                  pl.BlockSpec((MLA_ROPE, tm), lambda bi, ti: (0, ti))],
        out_specs=(pl.BlockSpec((1, tm, GMLP_WIDTH), lambda bi, ti: (bi, ti, 0)),
                   pl.BlockSpec((1, hw, tm), lambda bi, ti: (bi, 0, ti)),
                   pl.BlockSpec((1, tm, hw), lambda bi, ti: (bi, ti, 0)),
                   pl.BlockSpec((1, vw, tm), lambda bi, ti: (bi, 0, ti))),
        compiler_params=_cparams(("parallel", "parallel")),
        name="proj0",
    )(x, sc, sh, win, lng, lnb, ws, bsb, qn, wq, kvn, wk, wv, ck, sk, cq, sq)


def _proj1_kernel(x_ref, sc_ref, sh_ref, wq_ref, wk_ref, wv_ref, ck_ref, sk_ref, cq_ref, sq_ref,
                  q_ref, k_ref, v_ref, *, q_scale):
    h = (x_ref[0] * (1.0 + sc_ref[0]) + sh_ref[0]).astype(BF16)
    n_sub = 2 * DIFF_HEADS
    hd = DIFF_HEAD_DIM
    hr = hd // 2
    qt = _dot_nt(wq_ref[...], h) * q_scale
    cq_t = cq_ref[...]
    sq_t = sq_ref[...]
    for s in range(n_sub):
        r0 = s * hd
        x1 = qt[r0:r0 + hr]
        x2 = qt[r0 + hr:r0 + hd]
        swp = jnp.concatenate([x2, x1], axis=0)
        q_ref[0, r0:r0 + hd, :] = (qt[r0:r0 + hd] * cq_t + swp * sq_t).astype(BF16)
    k = _dot(h, wk_ref[...])
    width = k.shape[1]
    lane = lax.broadcasted_iota(jnp.int32, k.shape, 1)
    partner = jnp.where((lane % hd) < hr, pltpu.roll(k, width - hr, 1), pltpu.roll(k, hr, 1))
    reps = width // HEAD_SLOT
    k_ref[0] = (k * jnp.tile(ck_ref[...], (1, reps)) + partner * jnp.tile(sk_ref[...], (1, reps))).astype(BF16)
    v_ref[0] = _dot_nt(wv_ref[...], h).astype(BF16)


def _proj1(x, sc, sh, wts, tabs):
    b, n, d = x.shape
    tm = _tile(n, TOKEN_TILE)
    wq, wk, wv = wts
    ck, sk, cq, sq = tabs
    dw = wq.shape[0]
    full = lambda a: pl.BlockSpec(a.shape, lambda bi, ti: (0,) * a.ndim)
    q_scale = DIFF_HEAD_DIM ** -0.5 * LOG2E
    return pl.pallas_call(
        functools.partial(_proj1_kernel, q_scale=q_scale),
        out_shape=(jax.ShapeDtypeStruct((b, dw, n), BF16),
                   jax.ShapeDtypeStruct((b, n, dw), BF16),
                   jax.ShapeDtypeStruct((b, dw, n), BF16)),
        grid=(b, n // tm),
        in_specs=[pl.BlockSpec((1, tm, d), lambda bi, ti: (bi, ti, 0)),
                  pl.BlockSpec((1, 1, d), lambda bi, ti: (bi, 0, 0)),
                  pl.BlockSpec((1, 1, d), lambda bi, ti: (bi, 0, 0)),
                  full(wq), full(wk), full(wv),
                  pl.BlockSpec((tm, HEAD_SLOT), lambda bi, ti: (ti, 0)),
                  pl.BlockSpec((tm, HEAD_SLOT), lambda bi, ti: (ti, 0)),
                  pl.BlockSpec((DIFF_HEAD_DIM, tm), lambda bi, ti: (0, ti)),
                  pl.BlockSpec((DIFF_HEAD_DIM, tm), lambda bi, ti: (0, ti))],
        out_specs=(pl.BlockSpec((1, dw, tm), lambda bi, ti: (bi, 0, ti)),
                   pl.BlockSpec((1, tm, dw), lambda bi, ti: (bi, ti, 0)),
                   pl.BlockSpec((1, dw, tm), lambda bi, ti: (bi, 0, ti))),
        compiler_params=_cparams(("parallel", "parallel")),
        name="proj1",
    )(x, sc, sh, wq, wk, wv, ck, sk, cq, sq)


SHIFT_OK_LO = 2.0 ** -85
SHIFT_OK_HI = 2.0 ** 100
ATTN_TQ = 1024
ATTN_TK = 256
ATTN_UNROLL = 16


def _key_norm_max(k_ref, kmax_ref, *, tk, n_sub):
    nchunk = k_ref.shape[1] // tk
    sub_w = HEAD_SLOT // n_sub
    row = lax.broadcasted_iota(jnp.int32, (SUBLANES, HEAD_SLOT), 0)
    lane = lax.broadcasted_iota(jnp.int32, (SUBLANES, HEAD_SLOT), 1)
    sel = jnp.where(lane // sub_w == row, 1.0, 0.0).astype(BF16)
    group = max(g for g in range(1, 12) if nchunk % g == 0)
    rows = group * tk

    def body(c, mx):
        off = pl.multiple_of(c * rows, rows)
        kc = k_ref[0, pl.ds(off, rows), :].astype(F32)
        return jnp.maximum(mx, _dot_nt(sel, (kc * kc).astype(BF16)))

    mx = lax.fori_loop(0, nchunk // group, body, jnp.zeros((SUBLANES, rows), F32))
    kmax_ref[...] = jnp.broadcast_to(jnp.max(mx, axis=1, keepdims=True), kmax_ref.shape)


def _sweep_bounded(qs, shifts, k_ref, v_ref, s_ref, *, tk, unroll):
    tq = qs[0].shape[1]
    dv = v_ref.shape[1]
    nchunk = k_ref.shape[1] // tk

    def aligned(off):
        return off if isinstance(off, int) else pl.multiple_of(off, tk)

    def scores(off, buf):
        kc = k_ref[0, pl.ds(aligned(off), tk), :]
        for r, q in enumerate(qs):
            s_ref[buf, r] = _dot(kc, q)

    def consume(off, buf, carry):
        vc = v_ref[0, :, pl.ds(aligned(off), tk)]
        nxt = []
        for r, ((l8, acc), m) in enumerate(zip(carry, shifts)):
            p = jnp.exp2(s_ref[buf, r] - m)
            l8 = l8 + jnp.sum(p.reshape(tk // SUBLANES, SUBLANES, tq), axis=0)
            nxt.append((l8, acc + _dot(vc, p.astype(BF16))))
        return tuple(nxt)

    def body(i, carry):
        for u in range(unroll):
            off = pl.multiple_of((i * unroll + u) * tk, tk)
            scores(off + tk, (u + 1) % 2)
            carry = consume(off, u % 2, carry)
        return carry

    carry = tuple((jnp.zeros((SUBLANES, tq), F32), jnp.zeros((dv, tq), F32)) for _ in qs)
    scores(0, 0)
    n_loop = (nchunk - 1) // unroll * unroll
    if n_loop:
        carry = lax.fori_loop(0, n_loop // unroll, body, carry)
    for c in range(n_loop, nchunk):
        if c + 1 < nchunk:
            scores((c + 1) * tk, (c + 1) % 2)
        carry = consume(c * tk, c % 2, carry)
    return [(jnp.sum(l8, axis=0, keepdims=True), acc) for l8, acc in carry]


def _sweep_running_max(qs, k_ref, v_ref, *, tk):
    tq = qs[0].shape[1]
    dv = v_ref.shape[1]

    def body(c, carry):
        off = pl.multiple_of(c * tk, tk)
        kc = k_ref[0, pl.ds(off, tk), :]
        vc = v_ref[0, :, pl.ds(off, tk)]
        nxt = []
        for (m, l, acc), q in zip(carry, qs):
            s = _dot(kc, q)
            mn = jnp.maximum(m, jnp.max(s, axis=0, keepdims=True))
            alpha = jnp.exp2(m - mn)
            p = jnp.exp2(s - mn)
            nxt.append((mn, alpha * l + jnp.sum(p, axis=0, keepdims=True), alpha * acc + _dot(vc, p.astype(BF16))))
        return tuple(nxt)

    init = tuple((jnp.full((1, tq), NEG_BIG, F32), jnp.zeros((1, tq), F32), jnp.zeros((dv, tq), F32)) for _ in qs)
    res = lax.fori_loop(0, k_ref.shape[1] // tk, body, init)
    return [(l, acc) for _, l, acc in res]


def _attn_kernel(*refs, tk, unroll, n_sub, n_src, finalize):
    q_ref = refs[0]
    srcs = [(refs[1 + 2 * i], refs[2 + 2 * i]) for i in range(n_src)]
    n_scratch = 3 if n_src == 1 else 5
    extra = refs[1 + 2 * n_src:-n_scratch]
    o_ref, kmax_ref, s_ref = refs[-n_scratch:][:3]
    k_ref, v_ref = srcs[0] if n_src == 1 else refs[-2:]

    @pl.when(pl.program_id(2) == 0)
    def _():
        if n_src > 1:
            off = 0
            for ks_ref, vs_ref in srcs:
                t = ks_ref.shape[1]
                k_ref[0, off:off + t] = ks_ref[0]
                v_ref[0, :, off:off + t] = vs_ref[0]
                off += t
        _key_norm_max(k_ref, kmax_ref, tk=tk, n_sub=n_sub)

    q = q_ref[0]
    sub_w = HEAD_SLOT // n_sub
    row = lax.broadcasted_iota(jnp.int32, q.shape, 0)
    qf = q.astype(F32)
    qs, shifts = [], []
    for r in range(n_sub):
        mine = (row // sub_w) == r
        qs.append(jnp.where(mine, q, jnp.zeros_like(q)) if n_sub > 1 else q)
        qn2 = jnp.sum(jnp.where(mine, qf * qf, 0.0), axis=0, keepdims=True)
        shifts.append(jnp.sqrt(qn2 * kmax_ref[r:r + 1, 0:1]))

    res = _sweep_bounded(qs, shifts, k_ref, v_ref, s_ref, tk=tk, unroll=unroll)
    o_ref[0] = finalize(res, extra).astype(BF16)
    ok = None
    for l, _ in res:
        ok_r = (jnp.min(l) >= SHIFT_OK_LO) & (jnp.max(l) <= SHIFT_OK_HI)
        ok = ok_r if ok is None else ok & ok_r

    @pl.when(jnp.logical_not(ok))
    def _():
        o_ref[0] = finalize(_sweep_running_max(qs, k_ref, v_ref, tk=tk), extra).astype(BF16)


def _finalize_plain(res, extra):
    (l, acc), = res
    return acc / l


def _finalize_diff(res, extra, *, lam_init):
    lq1_ref, lk1_ref, lq2_ref, lk2_ref, g_ref = extra
    (l1, a1), (l2, a2) = res
    lam = (jnp.exp(jnp.sum(lq1_ref[...] * lk1_ref[...], axis=-1, keepdims=True))
           - jnp.exp(jnp.sum(lq2_ref[...] * lk2_ref[...], axis=-1, keepdims=True)) + lam_init)
    o = a1 / l1 - lam * (a2 / l2)
    return o * lax.rsqrt(jnp.mean(o * o, axis=0, keepdims=True) + RMS_EPS) * g_ref[...] * (1.0 - lam_init)


def _attention(qt, kv, extra=(), *, dv, n_sub, finalize, name):
    b, hw, n = qt.shape
    heads = hw // HEAD_SLOT
    tq = _tile(n, ATTN_TQ)
    tk = ATTN_TK if all(k.shape[1] % ATTN_TK == 0 for k, _ in kv) else LANES
    full = lambda a: pl.BlockSpec(a.shape, lambda bi, hi, qi: (0,) * a.ndim)
    kv_specs, kv_args = [], []
    for k, vt in kv:
        t = k.shape[1]
        kv_specs += [pl.BlockSpec((1, t, HEAD_SLOT), lambda bi, hi, qi: (bi, 0, hi)),
                     pl.BlockSpec((1, dv, t), lambda bi, hi, qi: (bi, hi, 0))]
        kv_args += [k, vt]
    scratch = [pltpu.VMEM((8, HEAD_SLOT), F32), pltpu.VMEM((2, n_sub, tk, tq), F32)]
    if len(kv) > 1:
        t_all = sum(k.shape[1] for k, _ in kv)
        scratch += [pltpu.VMEM((1, t_all, HEAD_SLOT), BF16), pltpu.VMEM((1, dv, t_all), BF16)]
    return pl.pallas_call(
        functools.partial(_attn_kernel, tk=tk, unroll=ATTN_UNROLL, n_sub=n_sub, n_src=len(kv), finalize=finalize),
        out_shape=jax.ShapeDtypeStruct((b, heads * dv, n), BF16),
        grid=(b, heads, n // tq),
        in_specs=[pl.BlockSpec((1, HEAD_SLOT, tq), lambda bi, hi, qi: (bi, hi, qi))] + kv_specs
        + [full(a) for a in extra],
        out_specs=pl.BlockSpec((1, dv, tq), lambda bi, hi, qi: (bi, hi, qi)),
        scratch_shapes=scratch,
        compiler_params=_cparams(("parallel", "parallel", "arbitrary")),
        name=name,
    )(qt, *kv_args, *extra)


def _out_kernel(*refs, n_rowmajor):
    (x_ref, g_ref, lng_ref, lnb_ref, scf_ref, shf_ref, rt_ref), rest = refs[:7], refs[7:]
    x1_ref, aff_ref, hp_ref = rest[-3:]
    ops = rest[:-3]
    m = None
    for i in range(len(ops) // 2):
        a = ops[2 * i][0]
        w = ops[2 * i + 1][...]
        part = _dot(a, w) if i < n_rowmajor else _dot_tn(a, w)
        m = part if m is None else m + part
    x1 = _ln(DEEPNORM_ALPHA * x_ref[0] + (1.0 + g_ref[0]) * m, lng_ref[...], lnb_ref[...])
    x1_ref[0] = x1
    hf = (x1 * (1.0 + scf_ref[0]) + shf_ref[0]).astype(BF16)
    lg = _dot_nt(rt_ref[...], hf)
    e = jnp.exp(lg - jnp.max(lg, axis=0, keepdims=True))
    aff_ref[0] = e / jnp.sum(e, axis=0, keepdims=True)
    dh = hf.shape[1] // 2
    lo = pltpu.bitcast(hf[:, :dh].astype(F32), jnp.uint32)
    hi = pltpu.bitcast(hf[:, dh:].astype(F32), jnp.uint32)
    hp_ref[0] = _to_tiled(lax.shift_right_logical(lo, jnp.uint32(16)) | (hi & jnp.uint32(0xFFFF0000)))


def _mixer_out(x, g, lng, lnb, scf, shf, router_t, rowmajor_ops, chanmajor_ops):
    b, n, d = x.shape
    e = router_t.shape[0]
    tm = _tile(n, TOKEN_TILE)
    vec = pl.BlockSpec((1, 1, d), lambda bi, ti: (bi, 0, 0))
    full = lambda a: pl.BlockSpec(a.shape, lambda bi, ti: (0,) * a.ndim)
    args = [x, g, lng, lnb, scf, shf, router_t]
    specs = [pl.BlockSpec((1, tm, d), lambda bi, ti: (bi, ti, 0)), vec, full(lng), full(lnb), vec, vec,
             full(router_t)]
    for a, w in rowmajor_ops:
        args += [a, w]
        specs += [pl.BlockSpec((1, tm, a.shape[2]), lambda bi, ti: (bi, ti, 0)), full(w)]
    for a, w in chanmajor_ops:
        args += [a, w]
        specs += [pl.BlockSpec((1, a.shape[1], tm), lambda bi, ti: (bi, 0, ti)), full(w)]
    return pl.pallas_call(
        functools.partial(_out_kernel, n_rowmajor=len(rowmajor_ops)),
        out_shape=(jax.ShapeDtypeStruct((b, n, d), F32), jax.ShapeDtypeStruct((b, e, n), F32),
                   jax.ShapeDtypeStruct((b, n * (d // 2) // LANES, LANES), jnp.uint32)),
        grid=(b, n // tm),
        in_specs=specs,
        out_specs=(pl.BlockSpec((1, tm, d), lambda bi, ti: (bi, ti, 0)),
                   pl.BlockSpec((1, e, tm), lambda bi, ti: (bi, 0, ti)),
                   pl.BlockSpec((1, tm * (d // 2) // LANES, LANES), lambda bi, ti: (bi, ti, 0))),
        compiler_params=_cparams(("parallel", "parallel")),
        name="mixer_out",
    )(*args)


ROW_UNROLL = 8
FFN_ROWS = 512
TOKEN_SPLIT = 4096


def _ffn_kernel(src_ref, nxt_ref, hp_ref, g_ref, wg_ref, wu_ref, wd_ref, y_ref, rows_ref, xs_ref, *, prefetch):
    cap = y_ref.shape[2]
    nt = rows_ref.shape[0] // cap
    dh = nt * LANES

    def gather(idx_ref, j, dst):
        rows_ref[pl.ds(dst, nt, stride=SUBLANES), :] = hp_ref[0, pl.ds(idx_ref[0, 0, j], nt, stride=SUBLANES), :]

    def gather_loop():
        def body(jb, carry):
            for u in range(ROW_UNROLL):
                gather(src_ref, jb * ROW_UNROLL + u, jb * (ROW_UNROLL * nt) + u)
            return carry

        lax.fori_loop(0, cap // ROW_UNROLL, body, 0)

    if prefetch:
        pl.when(pl.program_id(1) == 0)(gather_loop)
    else:
        gather_loop()

    blk = min(cap, FFN_ROWS)
    for c0 in range(0, cap, blk):
        w = _from_tiled(rows_ref[c0 * nt:(c0 + blk) * nt], dh)
        lo = pltpu.bitcast(lax.shift_left(w, jnp.uint32(16)), F32)
        hi = pltpu.bitcast(w & jnp.uint32(0xFFFF0000), F32)
        xs_ref[c0:c0 + blk] = jnp.concatenate([lo, hi], axis=1).astype(BF16)

    if prefetch:
        for j in range(cap):
            gather(nxt_ref, j, _tiled_row(j, dh))

    for c0 in range(0, cap, blk):
        xs = xs_ref[c0:c0 + blk]
        gate = _dot(xs, wg_ref[0])
        hid = gate / (1.0 + jnp.exp(-gate)) * _dot(xs, wu_ref[0])
        y_ref[0, 0, c0:c0 + blk] = (_dot(hid.astype(BF16), wd_ref[0]) * g_ref[0, c0:c0 + blk]).astype(BF16)


def _activations_outweigh(activation_bytes, weight_bytes):
    return activation_bytes > weight_bytes


def _moe_ffn(hp, src3, gates3, wg, wu, wd, cap):
    b, hp_rows, _ = hp.shape
    n_exp, d, ff = wg.shape
    nt = d // 2 // LANES
    sample_major = _activations_outweigh(hp_rows * LANES * 4, 3 * d * ff * 2)
    if sample_major:
        grid, be = (b, n_exp), (lambda g0, g1: (g0, g1))
        hp_spec = pl.BlockSpec((1, hp_rows, LANES), lambda g0, g1: (g0, 0, 0), pipeline_mode=pl.Buffered(1))
    else:
        grid, be = (n_exp, b), (lambda g0, g1: (g1, g0))
        hp_spec = pl.BlockSpec((1, hp_rows, LANES), lambda g0, g1: (g1, 0, 0))
    slot = lambda g0, g1: (be(g0, g1)[0] * n_exp + be(g0, g1)[1], 0, 0)
    weight = lambda g0, g1: (be(g0, g1)[1], 0, 0)
    nxt = lambda g0, g1: (be(g0, g1)[0] * n_exp + jnp.minimum(be(g0, g1)[1] + 1, n_exp - 1), 0, 0)
    return pl.pallas_call(
        functools.partial(_ffn_kernel, prefetch=sample_major),
        out_shape=jax.ShapeDtypeStruct((b, n_exp, cap, d), BF16),
        grid=grid,
        in_specs=[pl.BlockSpec((1, 1, cap), slot, memory_space=pltpu.SMEM),
                  pl.BlockSpec((1, 1, cap), nxt, memory_space=pltpu.SMEM),
                  hp_spec,
                  pl.BlockSpec((1, cap, 1), slot),
                  pl.BlockSpec((1, d, ff), weight),
                  pl.BlockSpec((1, d, ff), weight),
                  pl.BlockSpec((1, ff, d), weight)],
        out_specs=pl.BlockSpec((1, 1, cap, d), lambda g0, g1: be(g0, g1) + (0, 0)),
        scratch_shapes=[pltpu.VMEM((cap * nt, LANES), jnp.uint32), pltpu.VMEM((cap, d), BF16)],
        compiler_params=_cparams(("parallel", "arbitrary")),
        name="moe_ffn",
    )(src3, src3, hp, gates3, wg, wu, wd)


def _scatter_kernel(dst_ref, bnd_ref, y_ref, f_ref, rows_ref):
    d = y_ref.shape[3]
    nt = d // LANES

    @pl.when(pl.program_id(2) == 0)
    def _():
        f_ref[...] = jnp.zeros_like(f_ref)

    rows_ref[...] = _to_tiled(y_ref[0, 0].astype(F32))
    lo = bnd_ref[0, 0, pl.program_id(1)]
    hi = bnd_ref[0, 0, pl.program_id(1) + 1]

    def row(ref, start):
        return ref[pl.ds(start, nt, stride=SUBLANES), :]

    def one(j, carry):
        dst = dst_ref[0, 0, j]
        f_ref[0, pl.ds(dst, nt, stride=SUBLANES), :] = row(f_ref.at[0], dst) + row(rows_ref, _tiled_row(j, d))
        return carry

    def batch(jb, carry):
        dsts = [dst_ref[0, 0, jb * ROW_UNROLL + u] for u in range(ROW_UNROLL)]
        sums = [row(f_ref.at[0], dsts[u]) + row(rows_ref, jb * (ROW_UNROLL * nt) + u) for u in range(ROW_UNROLL)]
        for u in range(ROW_UNROLL):
            f_ref[0, pl.ds(dsts[u], nt, stride=SUBLANES), :] = sums[u]
        return carry

    head_end = jnp.minimum((lo + ROW_UNROLL - 1) // ROW_UNROLL * ROW_UNROLL, hi)
    tail_start = jnp.maximum(head_end, hi // ROW_UNROLL * ROW_UNROLL)
    lax.fori_loop(lo, head_end, one, 0)
    lax.fori_loop(head_end // ROW_UNROLL, tail_start // ROW_UNROLL, batch, 0)
    lax.fori_loop(tail_start, hi, one, 0)


def _moe_scatter(y, dst3, bounds3, n):
    b, n_exp, cap, d = y.shape
    nt = _tile(n, TOKEN_SPLIT)
    lt = d // LANES
    return pl.pallas_call(
        _scatter_kernel,
        out_shape=jax.ShapeDtypeStruct((b, n * lt, LANES), F32),
        grid=(b, n // nt, n_exp),
        in_specs=[pl.BlockSpec((1, 1, cap), lambda bi, ki, ei: (bi * n_exp + ei, 0, 0), memory_space=pltpu.SMEM),
                  pl.BlockSpec((1, 1, n // nt + 1), lambda bi, ki, ei: (bi * n_exp + ei, 0, 0),
                               memory_space=pltpu.SMEM),
                  pl.BlockSpec((1, 1, cap, d), lambda bi, ki, ei: (bi, ei, 0, 0))],
        out_specs=pl.BlockSpec((1, nt * lt, LANES), lambda bi, ki, ei: (bi, ki, 0)),
        scratch_shapes=[pltpu.VMEM((cap * lt, LANES), F32)],
        compiler_params=_cparams(("parallel", "parallel", "arbitrary")),
        name="moe_scatter",
    )(dst3, bounds3, y)


def _post_kernel(x_ref, f_ref, g_ref, lng_ref, lnb_ref, o_ref):
    f = _from_tiled(f_ref[0], x_ref.shape[2])
    o_ref[0] = _ln(DEEPNORM_ALPHA * x_ref[0] + (1.0 + g_ref[0]) * f, lng_ref[...], lnb_ref[...])


def _post(x1, f_tiled, g, lng, lnb):
    b, n, d = x1.shape
    tm = _tile(n, TOKEN_TILE)
    blk = pl.BlockSpec((1, tm, d), lambda bi, ti: (bi, ti, 0))
    f_blk = pl.BlockSpec((1, tm * d // LANES, LANES), lambda bi, ti: (bi, ti, 0))
    full = lambda a: pl.BlockSpec(a.shape, lambda bi, ti: (0,) * a.ndim)
    return pl.pallas_call(
        _post_kernel,
        out_shape=jax.ShapeDtypeStruct((b, n, d), F32),
        grid=(b, n // tm),
        in_specs=[blk, f_blk, pl.BlockSpec((1, 1, d), lambda bi, ti: (bi, 0, 0)), full(lng), full(lnb)],
        out_specs=blk,
        compiler_params=_cparams(("parallel", "parallel")),
        name="ffn_post",
    )(x1, f_tiled, g, lng, lnb)


def _select_kernel(aff_ref, idx_ref, gate_ref, *, cap, n_exp):
    a = aff_ref[0]
    er, lanes = a.shape
    r_n = er // n_exp
    e_pad = 16
    bits = pltpu.bitcast(a, jnp.int32)

    member = (lax.broadcasted_iota(jnp.int32, (e_pad, er), 1) // r_n
              == lax.broadcasted_iota(jnp.int32, (e_pad, er), 0))
    gs = jnp.where(member, 1.0, 0.0).astype(BF16)
    member_t = (lax.broadcasted_iota(jnp.int32, (er, e_pad), 0) // r_n
                == lax.broadcasted_iota(jnp.int32, (er, e_pad), 1))
    gst = jnp.where(member_t, 1.0, 0.0).astype(BF16)
    ri = lax.broadcasted_iota(jnp.int32, (er, er), 0)
    rj = lax.broadcasted_iota(jnp.int32, (er, er), 1)
    rows_before = jnp.where((ri // r_n == rj // r_n) & (rj < ri), 1.0, 0.0).astype(BF16)
    li = lax.broadcasted_iota(jnp.int32, (lanes, lanes), 0)
    lj = lax.broadcasted_iota(jnp.int32, (lanes, lanes), 1)
    lanes_upto = jnp.where(li <= lj, 1.0, 0.0).astype(BF16)

    def expert_total(m):
        per_lane = _dot(gs, m.astype(BF16))
        return jnp.broadcast_to(jnp.sum(per_lane, axis=1, keepdims=True), per_lane.shape)

    def to_rows(ev):
        if r_n % SUBLANES == 0 and n_exp == e_pad:
            return jnp.broadcast_to(ev[:, None, :], (e_pad, r_n, lanes)).reshape(er, lanes)
        hi = jnp.floor(ev * (1.0 / 128.0))
        return 128.0 * _dot(gst, hi.astype(BF16)) + _dot(gst, (ev - 128.0 * hi).astype(BF16))

    def search(i, t):
        cand = t | lax.shift_left(jnp.int32(1), 30 - i)
        cnt = expert_total(jnp.where(bits >= cand, 1.0, 0.0))
        return jnp.where(to_rows(jnp.where(cnt >= cap, 1.0, 0.0)) > 0.5, cand, t)

    t = lax.fori_loop(0, 31, search, jnp.zeros((er, lanes), jnp.int32))
    gt = jnp.where(bits > t, 1.0, 0.0)
    eq = jnp.where(bits == t, 1.0, 0.0)

    def prefix(m):
        rc = _dot(m.astype(BF16), lanes_upto)
        off = _dot(rows_before, jnp.broadcast_to(rc[:, lanes - 1:lanes], m.shape).astype(BF16))
        return rc, off

    rc_eq, off_eq = prefix(eq)
    need = cap - to_rows(expert_total(gt))
    sel = jnp.maximum(gt, eq * jnp.where(off_eq + rc_eq - eq < need, 1.0, 0.0))
    rc, off = prefix(sel)
    row_end = off + jnp.broadcast_to(rc[:, lanes - 1:lanes], rc.shape)

    slot = lax.broadcasted_iota(jnp.int32, (cap, lanes), 0).astype(F32) + 1.0
    lane_c = lax.broadcasted_iota(jnp.int32, (cap, lanes), 1)
    lane_cf = lane_c.astype(F32)
    diag = lax.broadcasted_iota(jnp.int32, (r_n, lanes), 0) == lax.broadcasted_iota(jnp.int32, (r_n, lanes), 1)
    lane_r = lax.broadcasted_iota(jnp.int32, (1, lanes), 1)
    pad = jnp.zeros((lanes - r_n, lanes), F32)

    def table(m):
        return (jnp.concatenate([m, pad], axis=0) if r_n < lanes else m).astype(BF16)

    all_ones = jnp.ones((lanes, lanes), BF16)
    idx_all = jnp.zeros((cap, lanes), jnp.int32)
    gate_all = jnp.zeros((cap, lanes), F32)
    for e in range(n_exp):
        r0 = e * r_n
        end_lane = jnp.sum(jnp.where(diag, row_end[r0:r0 + r_n], 0.0), axis=0, keepdims=True)
        end_lane = jnp.where(lane_r < r_n, end_lane, 3.0e38)
        row_j = jnp.sum(jnp.where(end_lane < slot, 1.0, 0.0), axis=1, keepdims=True)
        oh = jnp.where(lane_cf == row_j, 1.0, 0.0).astype(BF16)
        cum_e = off[r0:r0 + r_n] + rc[r0:r0 + r_n]
        cum_hi = jnp.floor(cum_e * (1.0 / 128.0))
        cum_2 = _dot(oh, jnp.concatenate([table(cum_hi), table(cum_e - 128.0 * cum_hi)], axis=1))
        cum_j = 128.0 * cum_2[:, :lanes] + cum_2[:, lanes:]
        lane_j = _dot(jnp.where(cum_j < slot, 1.0, 0.0).astype(BF16), all_ones)
        a_e = a[r0:r0 + r_n]
        a1 = a_e.astype(BF16).astype(F32)
        a2 = (a_e - a1).astype(BF16).astype(F32)
        a3 = a_e - a1 - a2
        aff_2 = _dot(oh, jnp.concatenate([table(a1), table(a2)], axis=1))
        aff_j = aff_2[:, :lanes] + aff_2[:, lanes:] + _dot(oh, table(a3))
        gate_j = jnp.sum(jnp.where(lane_cf == lane_j, aff_j, 0.0), axis=1, keepdims=True)
        idx_j = (row_j * float(lanes) + lane_j).astype(jnp.int32)
        idx_all = jnp.where(lane_c == e, idx_j, idx_all)
        gate_all = jnp.where(lane_c == e, gate_j, gate_all)
    idx_ref[0] = idx_all
    gate_ref[0] = gate_all


def _moe_select(aff_t, cap):
    b, n_exp, n = aff_t.shape
    lanes = HEAD_SLOT
    er = n_exp * (n // lanes)
    idx, gates = pl.pallas_call(
        functools.partial(_select_kernel, cap=cap, n_exp=n_exp),
        out_shape=(jax.ShapeDtypeStruct((b, cap, lanes), jnp.int32), jax.ShapeDtypeStruct((b, cap, lanes), F32)),
        grid=(b,),
        in_specs=[pl.BlockSpec((1, er, lanes), lambda bi: (bi, 0, 0))],
        out_specs=(pl.BlockSpec((1, cap, lanes), lambda bi: (bi, 0, 0)),
                   pl.BlockSpec((1, cap, lanes), lambda bi: (bi, 0, 0))),
        compiler_params=_cparams(("parallel",)),
        name="moe_select",
    )(aff_t.reshape(b, er, lanes))
    to_expert_major = lambda m: jnp.swapaxes(m[:, :, :n_exp], 1, 2)
    return to_expert_major(gates), to_expert_major(idx)


def _moe(x1, aff_t, hp, gf, wg, wu, wd, lng, lnb):
    b, n, d = x1.shape
    n_exp = aff_t.shape[1]
    cap = CAPACITY_FACTOR * n // n_exp
    gates, idx = _moe_select(aff_t, cap)
    gates3 = gates.reshape(b * n_exp, cap, 1)
    nt = _tile(n, TOKEN_SPLIT)
    edges = jnp.arange(0, n + 1, nt, dtype=jnp.int32)
    bounds3 = jnp.sum(idx[..., None] < edges, axis=2, dtype=jnp.int32).reshape(b * n_exp, 1, n // nt + 1)
    src3 = _tiled_row(idx, d // 2).reshape(b * n_exp, 1, cap)
    dst3 = _tiled_row(idx % nt, d).reshape(b * n_exp, 1, cap)
    y = _moe_ffn(hp, src3, gates3, wg, wu, wd, cap)
    f_tiled = _moe_scatter(y, dst3, bounds3, n)
    return _post(x1, f_tiled, gf, lng, lnb)


def _rope_angles(n_tokens, dim):
    n_rows = n_tokens // GRID_W
    row = jnp.repeat(jnp.arange(n_rows, dtype=F32), GRID_W)
    col = jnp.tile(jnp.arange(GRID_W, dtype=F32), n_rows)
    n_freq = dim // 4
    inv_freq = ROPE_BASE ** (-jnp.arange(n_freq, dtype=F32) / n_freq)
    ang = jnp.concatenate([row[:, None] * inv_freq, col[:, None] * inv_freq], -1)
    return jnp.cos(ang), jnp.sin(ang)


def _rope_tables(n_tokens, dim, lane_offset, identity):
    half = dim // 2
    if identity:
        cos = jnp.ones((n_tokens, half), F32)
        sin = jnp.zeros((n_tokens, half), F32)
    else:
        cos, sin = _rope_angles(n_tokens, dim)
    c2 = jnp.concatenate([cos, cos], -1)
    s2 = jnp.concatenate([-sin, sin], -1)
    reps = (HEAD_SLOT - lane_offset) // dim if lane_offset == 0 else 1
    ck = jnp.zeros((n_tokens, HEAD_SLOT), F32)
    sk = jnp.zeros((n_tokens, HEAD_SLOT), F32)
    for r in range(reps):
        ck = ck.at[:, lane_offset + r * dim:lane_offset + (r + 1) * dim].set(c2)
        sk = sk.at[:, lane_offset + r * dim:lane_offset + (r + 1) * dim].set(s2)
    return ck, sk, c2.T, s2.T


def _deinterleave(w, axis):
    return jnp.concatenate([jnp.take(w, jnp.arange(0, w.shape[axis], 2), axis=axis),
                            jnp.take(w, jnp.arange(1, w.shape[axis], 2), axis=axis)], axis=axis)


def _prep_layer0(w_in, ln_g, ln_b, ws, bs, q_norm, w_uq, kv_norm, w_ukv):
    d = w_in.shape[0]
    w = GMLP_WIDTH
    o_kr = 2 * w + MLA_Q_RANK + MLA_KV_RANK
    kr = _deinterleave(w_in[:, o_kr:o_kr + MLA_ROPE], 1)
    kr_sw = jnp.concatenate([kr[:, MLA_ROPE // 2:], kr[:, :MLA_ROPE // 2]], 1)
    slot = lambda m: jnp.zeros((d, HEAD_SLOT), F32).at[:, MLA_NOPE:MLA_NOPE + MLA_ROPE].set(m)
    win = jnp.concatenate([w_in[:, :o_kr], slot(kr), slot(kr_sw)], 1).astype(BF16)

    qd = MLA_NOPE + MLA_ROPE
    wq = w_uq.reshape(MLA_Q_RANK, MLA_HEADS, qd)
    wq = jnp.concatenate([wq[..., :MLA_NOPE], _deinterleave(wq[..., MLA_NOPE:], 2),
                          jnp.zeros((MLA_Q_RANK, MLA_HEADS, HEAD_SLOT - qd), F32)], -1)
    wq_t = wq.reshape(MLA_Q_RANK, MLA_HEADS * HEAD_SLOT).T.astype(BF16)

    wkv = w_ukv.reshape(MLA_KV_RANK, MLA_HEADS, MLA_NOPE + MLA_V)
    wk = jnp.concatenate([wkv[..., :MLA_NOPE], jnp.zeros((MLA_KV_RANK, MLA_HEADS, HEAD_SLOT - MLA_NOPE), F32)], -1)
    wk = wk.reshape(MLA_KV_RANK, MLA_HEADS * HEAD_SLOT).astype(BF16)
    wv_t = wkv[..., MLA_NOPE:].reshape(MLA_KV_RANK, MLA_HEADS * MLA_V).T.astype(BF16)

    bsb = jnp.broadcast_to(bs[:, :, None], (GMLP_GROUPS, CHUNK, GMLP_GROUP_CH))
    return (win, ln_g.reshape(1, -1), ln_b.reshape(1, -1), ws.astype(BF16), bsb,
            q_norm.reshape(1, -1), wq_t, kv_norm.reshape(1, -1), wk, wv_t)


def _prep_layer1(w_in):
    d = w_in.shape[0]
    dw = 2 * DIFF_HEADS * DIFF_HEAD_DIM
    perm = lambda m: _deinterleave(m.reshape(d, 2 * DIFF_HEADS, DIFF_HEAD_DIM), 2).reshape(d, dw)
    wq_t = perm(w_in[:, :dw]).T.astype(BF16)
    wk = perm(w_in[:, dw:2 * dw]).astype(BF16)
    wv_t = w_in[:, 2 * dw:].T.astype(BF16)
    return wq_t, wk, wv_t


def _split6(m_row):
    return [v[:, None, :] for v in jnp.split(m_row, 6, axis=-1)]


def kernel(x, c, ctx, c_ctx, w_mod_0, b_mod_0, w_in_0, gmlp_ln_g_0, gmlp_ln_b_0, gmlp_ws_0, gmlp_bs_0, mla_q_norm_0, mla_w_uq_0, mla_kv_norm_0, mla_w_ukv_0, w_out_0, ln_mix_g_0, ln_mix_b_0, router_0, w_gate_0, w_up_0, w_down_0, ln_ffn_g_0, ln_ffn_b_0, w_mod_1, b_mod_1, w_in_1, lambda_q1_1, lambda_k1_1, lambda_q2_1, lambda_k2_1, subln_g_1, w_out_1, ln_mix_g_1, ln_mix_b_1, router_1, w_gate_1, w_up_1, w_down_1, ln_ffn_g_1, ln_ffn_b_1):
    b, n, d = x.shape
    m_ctx = ctx.shape[1]
    row = lambda v: v.reshape(1, -1)

    cond = jnp.concatenate([c, c_ctx[None, :], jnp.zeros((-(b + 1) % 8, d), F32)], 0)
    mod0 = _modulation(cond, w_mod_0, b_mod_0)
    mod1 = _modulation(cond, w_mod_1, b_mod_1)
    sh_a0, sc_a0, g_a0, sh_f0, sc_f0, g_f0 = _split6(mod0[:b])
    csh_a0, csc_a0, cg_a0, csh_f0, csc_f0, cg_f0 = [jnp.broadcast_to(v, (b, 1, d)) for v in _split6(mod0[b:b + 1])]
    sh_a1, sc_a1, g_a1, sh_f1, sc_f1, g_f1 = _split6(mod1[:b])
    csh_a1, csc_a1 = [jnp.broadcast_to(v, (b, 1, d)) for v in _split6(mod1[b:b + 1])[:2]]

    wts0 = _prep_layer0(w_in_0, gmlp_ln_g_0, gmlp_ln_b_0, gmlp_ws_0, gmlp_bs_0,
                        mla_q_norm_0, mla_w_uq_0, mla_kv_norm_0, mla_w_ukv_0)
    tabs_l = _rope_tables(n, MLA_ROPE, MLA_NOPE, identity=False)
    tabs_c = _rope_tables(m_ctx, MLA_ROPE, MLA_NOPE, identity=True)
    a_l, q_l, k_l, v_l = _proj0(x, sc_a0, sh_a0, wts0, tabs_l)
    a_c, q_c, k_c, v_c = _proj0(ctx, csc_a0, csh_a0, wts0, tabs_c)
    o_l = _attention(q_l, [(k_l, v_l), (k_c, v_c)], dv=MLA_V, n_sub=1, finalize=_finalize_plain, name="mla_attn")
    o_c = _attention(q_c, [(k_c, v_c)], dv=MLA_V, n_sub=1, finalize=_finalize_plain, name="mla_attn_ctx")

    w_out_a = w_out_0[:GMLP_WIDTH].astype(BF16)
    w_out_o = w_out_0[GMLP_WIDTH:].astype(BF16)
    router_t0 = router_0.T.astype(BF16)
    wg0, wu0, wd0 = w_gate_0.astype(BF16), w_up_0.astype(BF16), w_down_0.astype(BF16)
    lng, lnb = row(ln_mix_g_0), row(ln_mix_b_0)
    x1, aff, hp = _mixer_out(x, g_a0, lng, lnb, sc_f0, sh_f0, router_t0, [(a_l, w_out_a)], [(o_l, w_out_o)])
    x_lat = _moe(x1, aff, hp, g_f0, wg0, wu0, wd0, row(ln_ffn_g_0), row(ln_ffn_b_0))
    c1, caff, chp = _mixer_out(ctx, cg_a0, lng, lnb, csc_f0, csh_f0, router_t0, [(a_c, w_out_a)], [(o_c, w_out_o)])
    x_ctx = _moe(c1, caff, chp, cg_f0, wg0, wu0, wd0, row(ln_ffn_g_0), row(ln_ffn_b_0))

    lam_init = 0.8 - 0.6 * math.exp(-0.3 * 1)
    wts1 = _prep_layer1(w_in_1)
    tabs_l = _rope_tables(n, DIFF_HEAD_DIM, 0, identity=False)
    tabs_c = _rope_tables(m_ctx, DIFF_HEAD_DIM, 0, identity=True)
    q_l, k_l, v_l = _proj1(x_lat, sc_a1, sh_a1, wts1, tabs_l)
    _, k_c, v_c = _proj1(x_ctx, csc_a1, csh_a1, wts1, tabs_c)
    extra = (row(lambda_q1_1), row(lambda_k1_1), row(lambda_q2_1), row(lambda_k2_1), subln_g_1.reshape(-1, 1))
    o_l = _attention(q_l, [(k_l, v_l), (k_c, v_c)], extra, dv=2 * DIFF_HEAD_DIM, n_sub=2,
                     finalize=functools.partial(_finalize_diff, lam_init=lam_init), name="diff_attn")
    x1, aff, hp = _mixer_out(x_lat, g_a1, row(ln_mix_g_1), row(ln_mix_b_1), sc_f1, sh_f1, router_1.T.astype(BF16),
                             [], [(o_l, w_out_1.astype(BF16))])
    return _moe(x1, aff, hp, g_f1, w_gate_1.astype(BF16), w_up_1.astype(BF16), w_down_1.astype(BF16),
                row(ln_ffn_g_1), row(ln_ffn_b_1))
```

```python
import functools
import math

import jax
import jax.numpy as jnp
from jax import lax
from jax.experimental import pallas as pl
from jax.experimental.pallas import tpu as pltpu

F32 = jnp.float32
BF16 = jnp.bfloat16

DEPTH = 2
GRID_W = 64
ROPE_BASE = 10000.0
CHUNK = 128
GMLP_GROUPS = 4
GMLP_GROUP_CH = 128
GMLP_WIDTH = GMLP_GROUPS * GMLP_GROUP_CH
MLA_HEADS = 8
MLA_NOPE = 64
MLA_ROPE = 32
MLA_V = 64
MLA_Q_RANK = 256
MLA_KV_RANK = 128
DIFF_HEADS = 8
DIFF_HEAD_DIM = 64
CAPACITY_FACTOR = 2
DEEPNORM_ALPHA = (2 * DEPTH) ** 0.25
LN_EPS = 1e-5
RMS_EPS = 1e-6
LOG2E = 1.4426950408889634

LANES = 128
SUBLANES = 8
HEAD_SLOT = LANES
NEG_BIG = -1e30
VMEM_LIMIT = 56 * 1024 * 1024
TOKEN_TILE = 512
MOD_TILE = 1024


def _cparams(sem):
    return pltpu.CompilerParams(dimension_semantics=sem, vmem_limit_bytes=VMEM_LIMIT)


def _tile(n, pref):
    return pref if n % pref == 0 else n


def _ln(x, g, b):
    mu = jnp.mean(x, axis=-1, keepdims=True)
    xc = x - mu
    var = jnp.mean(xc * xc, axis=-1, keepdims=True)
    return xc * lax.rsqrt(var + LN_EPS) * g + b


def _rms(x, g):
    return x * lax.rsqrt(jnp.mean(x * x, axis=-1, keepdims=True) + RMS_EPS) * g


def _gelu(x):
    return 0.5 * x * (1.0 + lax.erf(x * (2.0 ** -0.5)))


def _to_tiled(v):
    m, w = v.shape
    nt = w // LANES
    tiles = [v[:, c * LANES:(c + 1) * LANES].reshape(m // SUBLANES, SUBLANES, LANES) for c in range(nt)]
    return jnp.stack(tiles, axis=1).reshape(m * nt, LANES)


def _from_tiled(t, w):
    nt = w // LANES
    m = t.shape[0] // nt
    t4 = t.reshape(m // SUBLANES, nt, SUBLANES, LANES)
    return jnp.concatenate([t4[:, c].reshape(m, LANES) for c in range(nt)], axis=1)


def _tiled_row(i, w):
    return (i // SUBLANES) * (SUBLANES * (w // LANES)) + i % SUBLANES


def _dot(a, b):
    return jnp.dot(a, b, preferred_element_type=F32)


def _dot_nt(a, b):
    return lax.dot_general(a, b, (((1,), (1,)), ((), ())), preferred_element_type=F32)


def _dot_tn(a, b):
    return lax.dot_general(a, b, (((0,), (0,)), ((), ())), preferred_element_type=F32)


def _mod_kernel(c_ref, w_ref, b_ref, o_ref):
    c = c_ref[...]
    s = c / (1.0 + jnp.exp(-c))
    o_ref[...] = _dot(s.astype(BF16), w_ref[...].astype(BF16)) + b_ref[...]


def _modulation(cond, w_mod, b_mod):
    r, d = cond.shape
    n = w_mod.shape[1]
    tn = _tile(n, MOD_TILE)
    return pl.pallas_call(
        _mod_kernel,
        out_shape=jax.ShapeDtypeStruct((r, n), F32),
        grid=(n // tn,),
        in_specs=[pl.BlockSpec((r, d), lambda j: (0, 0)),
                  pl.BlockSpec((d, tn), lambda j: (0, j)),
                  pl.BlockSpec((1, tn), lambda j: (0, j))],
        out_specs=pl.BlockSpec((r, tn), lambda j: (0, j)),
        compiler_params=_cparams(("parallel",)),
        name="modulation",
    )(cond, w_mod, b_mod.reshape(1, n))


def _proj0_kernel(x_ref, sc_ref, sh_ref, win_ref, lng_ref, lnb_ref, ws_ref, bs_ref,
                  qn_ref, wq_ref, kvn_ref, wk_ref, wv_ref, ck_ref, sk_ref, cq_ref, sq_ref,
                  a_ref, q_ref, k_ref, v_ref, *, q_scale):
    tm = x_ref.shape[1]
    h = (x_ref[0] * (1.0 + sc_ref[0]) + sh_ref[0]).astype(BF16)
    y = _dot(h, win_ref[...])
    w = GMLP_WIDTH
    u = _gelu(y[:, :w])
    vn = _ln(_gelu(y[:, w:2 * w]), lng_ref[...], lnb_ref[...]).astype(BF16)
    for ci in range(tm // CHUNK):
        r0 = ci * CHUNK
        for g in range(GMLP_GROUPS):
            c0 = g * GMLP_GROUP_CH
            mixed = _dot(ws_ref[g], vn[r0:r0 + CHUNK, c0:c0 + GMLP_GROUP_CH]) + bs_ref[g]
            a_ref[0, r0:r0 + CHUNK, c0:c0 + GMLP_GROUP_CH] = (
                u[r0:r0 + CHUNK, c0:c0 + GMLP_GROUP_CH] * mixed).astype(BF16)

    o = 2 * w
    cq = _rms(y[:, o:o + MLA_Q_RANK], qn_ref[...]).astype(BF16)
    qt = _dot_nt(wq_ref[...], cq) * q_scale
    q_ref[0] = qt.astype(BF16)
    hr = MLA_ROPE // 2
    cq_t = cq_ref[...]
    sq_t = sq_ref[...]
    for hd in range(MLA_HEADS):
        r0 = hd * HEAD_SLOT + MLA_NOPE
        x1 = qt[r0:r0 + hr]
        x2 = qt[r0 + hr:r0 + 2 * hr]
        blk = jnp.concatenate([x1, x2], axis=0)
        swp = jnp.concatenate([x2, x1], axis=0)
        q_ref[0, r0:r0 + 2 * hr, :] = (blk * cq_t + swp * sq_t).astype(BF16)

    o += MLA_Q_RANK
    ckv = _rms(y[:, o:o + MLA_KV_RANK], kvn_ref[...]).astype(BF16)
    o += MLA_KV_RANK
    kr = y[:, o:o + HEAD_SLOT] * ck_ref[...] + y[:, o + HEAD_SLOT:o + 2 * HEAD_SLOT] * sk_ref[...]
    k_ref[0] = (_dot(ckv, wk_ref[...]) + jnp.tile(kr, (1, MLA_HEADS))).astype(BF16)
    v_ref[0] = _dot_nt(wv_ref[...], ckv).astype(BF16)


def _proj0(x, sc, sh, wts, tabs):
    b, n, d = x.shape
    tm = _tile(n, TOKEN_TILE)
    ck, sk, cq, sq = tabs
    win, lng, lnb, ws, bsb, qn, wq, kvn, wk, wv = wts
    hw = MLA_HEADS * HEAD_SLOT
    vw = MLA_HEADS * MLA_V
    full = lambda a: pl.BlockSpec(a.shape, lambda bi, ti: (0,) * a.ndim)
    q_scale = (MLA_NOPE + MLA_ROPE) ** -0.5 * LOG2E
    return pl.pallas_call(
        functools.partial(_proj0_kernel, q_scale=q_scale),
        out_shape=(jax.ShapeDtypeStruct((b, n, GMLP_WIDTH), BF16),
                   jax.ShapeDtypeStruct((b, hw, n), BF16),
                   jax.ShapeDtypeStruct((b, n, hw), BF16),
                   jax.ShapeDtypeStruct((b, vw, n), BF16)),
        grid=(b, n // tm),
        in_specs=[pl.BlockSpec((1, tm, d), lambda bi, ti: (bi, ti, 0)),
                  pl.BlockSpec((1, 1, d), lambda bi, ti: (bi, 0, 0)),
                  pl.BlockSpec((1, 1, d), lambda bi, ti: (bi, 0, 0)),
                  full(win), full(lng), full(lnb), full(ws), full(bsb),
                  full(qn), full(wq), full(kvn), full(wk), full(wv),
                  pl.BlockSpec((tm, HEAD_SLOT), lambda bi, ti: (ti, 0)),
                  pl.BlockSpec((tm, HEAD_SLOT), lambda bi, ti: (ti, 0)),
                  pl.BlockSpec((MLA_ROPE, tm), lambda bi, ti: (0, ti)),
                  pl.BlockSpec((MLA_ROPE, tm), lambda bi, ti: (0, ti))],
        out_specs=(pl.BlockSpec((1, tm, GMLP_WIDTH), lambda bi, ti: (bi, ti, 0)),
                   pl.BlockSpec((1, hw, tm), lambda bi, ti: (bi, 0, ti)),
                   pl.BlockSpec((1, tm, hw), lambda bi, ti: (bi, ti, 0)),
                   pl.BlockSpec((1, vw, tm), lambda bi, ti: (bi, 0, ti))),
        compiler_params=_cparams(("parallel", "parallel")),
        name="proj0",
    )(x, sc, sh, win, lng, lnb, ws, bsb, qn, wq, kvn, wk, wv, ck, sk, cq, sq)


def _proj1_kernel(x_ref, sc_ref, sh_ref, wq_ref, wk_ref, wv_ref, ck_ref, sk_ref, cq_ref, sq_ref,
                  q_ref, k_ref, v_ref, *, q_scale):
    h = (x_ref[0] * (1.0 + sc_ref[0]) + sh_ref[0]).astype(BF16)
    n_sub = 2 * DIFF_HEADS
    hd = DIFF_HEAD_DIM
    hr = hd // 2
    qt = _dot_nt(wq_ref[...], h) * q_scale
    cq_t = cq_ref[...]
    sq_t = sq_ref[...]
    for s in range(n_sub):
        r0 = s * hd
        x1 = qt[r0:r0 + hr]
        x2 = qt[r0 + hr:r0 + hd]
        swp = jnp.concatenate([x2, x1], axis=0)
        q_ref[0, r0:r0 + hd, :] = (qt[r0:r0 + hd] * cq_t + swp * sq_t).astype(BF16)
    k = _dot(h, wk_ref[...])
    width = k.shape[1]
    lane = lax.broadcasted_iota(jnp.int32, k.shape, 1)
    partner = jnp.where((lane % hd) < hr, pltpu.roll(k, width - hr, 1), pltpu.roll(k, hr, 1))
    reps = width // HEAD_SLOT
    k_ref[0] = (k * jnp.tile(ck_ref[...], (1, reps)) + partner * jnp.tile(sk_ref[...], (1, reps))).astype(BF16)
    v_ref[0] = _dot_nt(wv_ref[...], h).astype(BF16)


def _proj1(x, sc, sh, wts, tabs):
    b, n, d = x.shape
    tm = _tile(n, TOKEN_TILE)
    wq, wk, wv = wts
    ck, sk, cq, sq = tabs
    dw = wq.shape[0]
    full = lambda a: pl.BlockSpec(a.shape, lambda bi, ti: (0,) * a.ndim)
    q_scale = DIFF_HEAD_DIM ** -0.5 * LOG2E
    return pl.pallas_call(
        functools.partial(_proj1_kernel, q_scale=q_scale),
        out_shape=(jax.ShapeDtypeStruct((b, dw, n), BF16),
                   jax.ShapeDtypeStruct((b, n, dw), BF16),
                   jax.ShapeDtypeStruct((b, dw, n), BF16)),
        grid=(b, n // tm),
        in_specs=[pl.BlockSpec((1, tm, d), lambda bi, ti: (bi, ti, 0)),
                  pl.BlockSpec((1, 1, d), lambda bi, ti: (bi, 0, 0)),
                  pl.BlockSpec((1, 1, d), lambda bi, ti: (bi, 0, 0)),
                  full(wq), full(wk), full(wv),
                  pl.BlockSpec((tm, HEAD_SLOT), lambda bi, ti: (ti, 0)),
                  pl.BlockSpec((tm, HEAD_SLOT), lambda bi, ti: (ti, 0)),
                  pl.BlockSpec((DIFF_HEAD_DIM, tm), lambda bi, ti: (0, ti)),
                  pl.BlockSpec((DIFF_HEAD_DIM, tm), lambda bi, ti: (0, ti))],
        out_specs=(pl.BlockSpec((1, dw, tm), lambda bi, ti: (bi, 0, ti)),
                   pl.BlockSpec((1, tm, dw), lambda bi, ti: (bi, ti, 0)),
                   pl.BlockSpec((1, dw, tm), lambda bi, ti: (bi, 0, ti))),
        compiler_params=_cparams(("parallel", "parallel")),
        name="proj1",
    )(x, sc, sh, wq, wk, wv, ck, sk, cq, sq)


SHIFT_OK_LO = 2.0 ** -85
SHIFT_OK_HI = 2.0 ** 100
ATTN_TQ = 1024
ATTN_TK = 256
KMAX_ROWS = 32
ATTN_UNROLL = 16


def _key_norm_max(k_ref, kmax_ref, *, tk, n_sub):
    nchunk = k_ref.shape[1] // tk
    sub_w = HEAD_SLOT // n_sub
    row = lax.broadcasted_iota(jnp.int32, (SUBLANES, HEAD_SLOT), 0)
    lane = lax.broadcasted_iota(jnp.int32, (SUBLANES, HEAD_SLOT), 1)
    sel = jnp.where(lane // sub_w == row, 1.0, 0.0).astype(BF16)
    group = max(g for g in range(1, 12) if nchunk % g == 0)
    rows = group * tk

    def body(c, mx):
        off = pl.multiple_of(c * rows, rows)
        kc = k_ref[0, pl.ds(off, rows), :].astype(F32)
        return jnp.maximum(mx, _dot_nt(sel, (kc * kc).astype(BF16)))

    mx = lax.fori_loop(0, nchunk // group, body, jnp.zeros((SUBLANES, rows), F32))
    tile8 = jnp.broadcast_to(jnp.max(mx, axis=1, keepdims=True), (SUBLANES, HEAD_SLOT))
    kmax_ref[...] = jnp.tile(tile8, (kmax_ref.shape[0] // SUBLANES, 1))


def _sweep_bounded(qs, shifts, k_ref, v_ref, s_ref, *, tk, unroll):
    tq = qs[0].shape[1]
    dv = v_ref.shape[1]
    nchunk = k_ref.shape[1] // tk

    def aligned(off):
        return off if isinstance(off, int) else pl.multiple_of(off, tk)

    def scores(off, buf):
        kc = k_ref[0, pl.ds(aligned(off), tk), :]
        for r, q in enumerate(qs):
            s_ref[buf, r] = _dot(kc, q)

    def consume(off, buf, carry):
        vc = v_ref[0, :, pl.ds(aligned(off), tk)]
        nxt = []
        for r, ((l8, acc), m) in enumerate(zip(carry, shifts)):
            p = jnp.exp2(s_ref[buf, r] - m)
            l8 = l8 + jnp.sum(p.reshape(tk // SUBLANES, SUBLANES, tq), axis=0)
            nxt.append((l8, acc + _dot(vc, p.astype(BF16))))
        return tuple(nxt)

    def body(i, carry):
        for u in range(unroll):
            off = pl.multiple_of((i * unroll + u) * tk, tk)
            scores(off + tk, (u + 1) % 2)
            carry = consume(off, u % 2, carry)
        return carry

    carry = tuple((jnp.zeros((SUBLANES, tq), F32), jnp.zeros((dv, tq), F32)) for _ in qs)
    scores(0, 0)
    n_loop = (nchunk - 1) // unroll * unroll
    if n_loop:
        carry = lax.fori_loop(0, n_loop // unroll, body, carry)
    for c in range(n_loop, nchunk):
        if c + 1 < nchunk:
            scores((c + 1) * tk, (c + 1) % 2)
        carry = consume(c * tk, c % 2, carry)
    return [(jnp.sum(l8, axis=0, keepdims=True), acc) for l8, acc in carry]


def _sweep_running_max(qs, k_ref, v_ref, *, tk):
    tq = qs[0].shape[1]
    dv = v_ref.shape[1]

    def body(c, carry):
        off = pl.multiple_of(c * tk, tk)
        kc = k_ref[0, pl.ds(off, tk), :]
        vc = v_ref[0, :, pl.ds(off, tk)]
        nxt = []
        for (m, l, acc), q in zip(carry, qs):
            s = _dot(kc, q)
            mn = jnp.maximum(m, jnp.max(s, axis=0, keepdims=True))
            alpha = jnp.exp2(m - mn)
            p = jnp.exp2(s - mn)
            nxt.append((mn, alpha * l + jnp.sum(p, axis=0, keepdims=True), alpha * acc + _dot(vc, p.astype(BF16))))
        return tuple(nxt)

    init = tuple((jnp.full((1, tq), NEG_BIG, F32), jnp.zeros((1, tq), F32), jnp.zeros((dv, tq), F32)) for _ in qs)
    res = lax.fori_loop(0, k_ref.shape[1] // tk, body, init)
    return [(l, acc) for _, l, acc in res]


def _attn_kernel(*refs, tk, unroll, n_sub, n_src, finalize):
    q_ref = refs[0]
    srcs = [(refs[1 + 2 * i], refs[2 + 2 * i]) for i in range(n_src)]
    n_scratch = 3 if n_src == 1 else 5
    extra = refs[1 + 2 * n_src:-n_scratch]
    o_ref, kmax_ref, s_ref = refs[-n_scratch:][:3]
    k_ref, v_ref = srcs[0] if n_src == 1 else refs[-2:]

    @pl.when(pl.program_id(2) == 0)
    def _():
        if n_src > 1:
            off = 0
            for ks_ref, vs_ref in srcs:
                t = ks_ref.shape[1]
                k_ref[0, off:off + t] = ks_ref[0]
                v_ref[0, :, off:off + t] = vs_ref[0]
                off += t
        _key_norm_max(k_ref, kmax_ref, tk=tk, n_sub=n_sub)

    q = q_ref[0]
    sub_w = HEAD_SLOT // n_sub
    row = lax.broadcasted_iota(jnp.int32, q.shape, 0)
    qf = q.astype(F32)
    qs, shifts = [], []
    for r in range(n_sub):
        mine = (row // sub_w) == r
        qs.append(jnp.where(mine, q, jnp.zeros_like(q)) if n_sub > 1 else q)
        qn2 = jnp.sum(jnp.where(mine, qf * qf, 0.0), axis=0, keepdims=True)
        shifts.append(jnp.sqrt(qn2 * kmax_ref[r:r + 1, 0:1]))

    res = _sweep_bounded(qs, shifts, k_ref, v_ref, s_ref, tk=tk, unroll=unroll)
    o_ref[0] = finalize(res, extra).astype(BF16)
    ok = None
    for l, _ in res:
        ok_r = (jnp.min(l) >= SHIFT_OK_LO) & (jnp.max(l) <= SHIFT_OK_HI)
        ok = ok_r if ok is None else ok & ok_r

    @pl.when(jnp.logical_not(ok))
    def _():
        o_ref[0] = finalize(_sweep_running_max(qs, k_ref, v_ref, tk=tk), extra).astype(BF16)


def _finalize_plain(res, extra):
    (l, acc), = res
    return acc / l


def _finalize_diff(res, extra, *, lam_init):
    lq1_ref, lk1_ref, lq2_ref, lk2_ref, g_ref = extra
    (l1, a1), (l2, a2) = res
    lam = (jnp.exp(jnp.sum(lq1_ref[...] * lk1_ref[...], axis=-1, keepdims=True))
           - jnp.exp(jnp.sum(lq2_ref[...] * lk2_ref[...], axis=-1, keepdims=True)) + lam_init)
    o = a1 / l1 - lam * (a2 / l2)
    return o * lax.rsqrt(jnp.mean(o * o, axis=0, keepdims=True) + RMS_EPS) * g_ref[...] * (1.0 - lam_init)


def _attention(qt, kv, extra=(), *, dv, n_sub, finalize, name):
    b, hw, n = qt.shape
    heads = hw // HEAD_SLOT
    tq = _tile(n, ATTN_TQ)
    tk = ATTN_TK if all(k.shape[1] % ATTN_TK == 0 for k, _ in kv) else LANES
    full = lambda a: pl.BlockSpec(a.shape, lambda bi, hi, qi: (0,) * a.ndim)
    kv_specs, kv_args = [], []
    for k, vt in kv:
        t = k.shape[1]
        kv_specs += [pl.BlockSpec((1, t, HEAD_SLOT), lambda bi, hi, qi: (bi, 0, hi)),
                     pl.BlockSpec((1, dv, t), lambda bi, hi, qi: (bi, hi, 0))]
        kv_args += [k, vt]
    scratch = [pltpu.VMEM((KMAX_ROWS, HEAD_SLOT), F32), pltpu.VMEM((2, n_sub, tk, tq), F32)]
    if len(kv) > 1:
        t_all = sum(k.shape[1] for k, _ in kv)
        scratch += [pltpu.VMEM((1, t_all, HEAD_SLOT), BF16), pltpu.VMEM((1, dv, t_all), BF16)]
    return pl.pallas_call(
        functools.partial(_attn_kernel, tk=tk, unroll=ATTN_UNROLL, n_sub=n_sub, n_src=len(kv), finalize=finalize),
        out_shape=jax.ShapeDtypeStruct((b, heads * dv, n), BF16),
        grid=(b, heads, n // tq),
        in_specs=[pl.BlockSpec((1, HEAD_SLOT, tq), lambda bi, hi, qi: (bi, hi, qi))] + kv_specs
        + [full(a) for a in extra],
        out_specs=pl.BlockSpec((1, dv, tq), lambda bi, hi, qi: (bi, hi, qi)),
        scratch_shapes=scratch,
        compiler_params=_cparams(("parallel", "parallel", "arbitrary")),
        name=name,
    )(qt, *kv_args, *extra)


def _out_kernel(*refs, n_rowmajor):
    (x_ref, g_ref, lng_ref, lnb_ref, scf_ref, shf_ref, rt_ref), rest = refs[:7], refs[7:]
    x1_ref, aff_ref, hp_ref = rest[-3:]
    ops = rest[:-3]
    m = None
    for i in range(len(ops) // 2):
        a = ops[2 * i][0]
        w = ops[2 * i + 1][...]
        part = _dot(a, w) if i < n_rowmajor else _dot_tn(a, w)
        m = part if m is None else m + part
    x1 = _ln(DEEPNORM_ALPHA * x_ref[0] + (1.0 + g_ref[0]) * m, lng_ref[...], lnb_ref[...])
    x1_ref[0] = x1
    hf = (x1 * (1.0 + scf_ref[0]) + shf_ref[0]).astype(BF16)
    lg = _dot_nt(rt_ref[...], hf)
    e = jnp.exp(lg - jnp.max(lg, axis=0, keepdims=True))
    aff_ref[0] = e / jnp.sum(e, axis=0, keepdims=True)
    dh = hf.shape[1] // 2
    lo = pltpu.bitcast(hf[:, :dh].astype(F32), jnp.uint32)
    hi = pltpu.bitcast(hf[:, dh:].astype(F32), jnp.uint32)
    hp_ref[0] = _to_tiled(lax.shift_right_logical(lo, jnp.uint32(16)) | (hi & jnp.uint32(0xFFFF0000)))


def _mixer_out(x, g, lng, lnb, scf, shf, router_t, rowmajor_ops, chanmajor_ops):
    b, n, d = x.shape
    e = router_t.shape[0]
    tm = _tile(n, TOKEN_TILE)
    vec = pl.BlockSpec((1, 1, d), lambda bi, ti: (bi, 0, 0))
    full = lambda a: pl.BlockSpec(a.shape, lambda bi, ti: (0,) * a.ndim)
    args = [x, g, lng, lnb, scf, shf, router_t]
    specs = [pl.BlockSpec((1, tm, d), lambda bi, ti: (bi, ti, 0)), vec, full(lng), full(lnb), vec, vec,
             full(router_t)]
    for a, w in rowmajor_ops:
        args += [a, w]
        specs += [pl.BlockSpec((1, tm, a.shape[2]), lambda bi, ti: (bi, ti, 0)), full(w)]
    for a, w in chanmajor_ops:
        args += [a, w]
        specs += [pl.BlockSpec((1, a.shape[1], tm), lambda bi, ti: (bi, 0, ti)), full(w)]
    return pl.pallas_call(
        functools.partial(_out_kernel, n_rowmajor=len(rowmajor_ops)),
        out_shape=(jax.ShapeDtypeStruct((b, n, d), F32), jax.ShapeDtypeStruct((b, e, n), F32),
                   jax.ShapeDtypeStruct((b, n * (d // 2) // LANES, LANES), jnp.uint32)),
        grid=(b, n // tm),
        in_specs=specs,
        out_specs=(pl.BlockSpec((1, tm, d), lambda bi, ti: (bi, ti, 0)),
                   pl.BlockSpec((1, e, tm), lambda bi, ti: (bi, 0, ti)),
                   pl.BlockSpec((1, tm * (d // 2) // LANES, LANES), lambda bi, ti: (bi, ti, 0))),
        compiler_params=_cparams(("parallel", "parallel")),
        name="mixer_out",
    )(*args)


ROW_UNROLL = 8
FFN_ROWS = 512
TOKEN_SPLIT = 4096


def _ffn_kernel(src_ref, nxt_ref, hp_ref, g_ref, wg_ref, wu_ref, wd_ref, y_ref, rows_ref, xs_ref, *, prefetch):
    cap = y_ref.shape[2]
    nt = rows_ref.shape[0] // cap
    dh = nt * LANES

    def gather(idx_ref, j, dst):
        rows_ref[pl.ds(dst, nt, stride=SUBLANES), :] = hp_ref[0, pl.ds(idx_ref[0, 0, j], nt, stride=SUBLANES), :]

    def gather_loop():
        def body(jb, carry):
            for u in range(ROW_UNROLL):
                gather(src_ref, jb * ROW_UNROLL + u, jb * (ROW_UNROLL * nt) + u)
            return carry

        lax.fori_loop(0, cap // ROW_UNROLL, body, 0)

    if prefetch:
        pl.when(pl.program_id(1) == 0)(gather_loop)
    else:
        gather_loop()

    blk = min(cap, FFN_ROWS)
    for c0 in range(0, cap, blk):
        w = _from_tiled(rows_ref[c0 * nt:(c0 + blk) * nt], dh)
        lo = pltpu.bitcast(lax.shift_left(w, jnp.uint32(16)), F32)
        hi = pltpu.bitcast(w & jnp.uint32(0xFFFF0000), F32)
        xs_ref[c0:c0 + blk] = jnp.concatenate([lo, hi], axis=1).astype(BF16)

    if prefetch:
        for j in range(cap):
            gather(nxt_ref, j, _tiled_row(j, dh))

    for c0 in range(0, cap, blk):
        xs = xs_ref[c0:c0 + blk]
        gate = _dot(xs, wg_ref[0])
        hid = gate / (1.0 + jnp.exp(-gate)) * _dot(xs, wu_ref[0])
        y_ref[0, 0, c0:c0 + blk] = (_dot(hid.astype(BF16), wd_ref[0]) * g_ref[0, c0:c0 + blk]).astype(BF16)


def _activations_outweigh(activation_bytes, weight_bytes):
    return activation_bytes > weight_bytes


def _moe_ffn(hp, src3, gates3, wg, wu, wd, cap):
    b, hp_rows, _ = hp.shape
    n_exp, d, ff = wg.shape
    nt = d // 2 // LANES
    sample_major = _activations_outweigh(hp_rows * LANES * 4, 3 * d * ff * 2)
    if sample_major:
        grid, be = (b, n_exp), (lambda g0, g1: (g0, g1))
        hp_spec = pl.BlockSpec((1, hp_rows, LANES), lambda g0, g1: (g0, 0, 0), pipeline_mode=pl.Buffered(1))
    else:
        grid, be = (n_exp, b), (lambda g0, g1: (g1, g0))
        hp_spec = pl.BlockSpec((1, hp_rows, LANES), lambda g0, g1: (g1, 0, 0))
    slot = lambda g0, g1: (be(g0, g1)[0] * n_exp + be(g0, g1)[1], 0, 0)
    weight = lambda g0, g1: (be(g0, g1)[1], 0, 0)
    nxt = lambda g0, g1: (be(g0, g1)[0] * n_exp + jnp.minimum(be(g0, g1)[1] + 1, n_exp - 1), 0, 0)
    return pl.pallas_call(
        functools.partial(_ffn_kernel, prefetch=sample_major),
        out_shape=jax.ShapeDtypeStruct((b, n_exp, cap, d), BF16),
        grid=grid,
        in_specs=[pl.BlockSpec((1, 1, cap), slot, memory_space=pltpu.SMEM),
                  pl.BlockSpec((1, 1, cap), nxt, memory_space=pltpu.SMEM),
                  hp_spec,
                  pl.BlockSpec((1, cap, 1), slot),
                  pl.BlockSpec((1, d, ff), weight),
                  pl.BlockSpec((1, d, ff), weight),
                  pl.BlockSpec((1, ff, d), weight)],
        out_specs=pl.BlockSpec((1, 1, cap, d), lambda g0, g1: be(g0, g1) + (0, 0)),
        scratch_shapes=[pltpu.VMEM((cap * nt, LANES), jnp.uint32), pltpu.VMEM((cap, d), BF16)],
        compiler_params=_cparams(("parallel", "arbitrary")),
        name="moe_ffn",
    )(src3, src3, hp, gates3, wg, wu, wd)


def _scatter_kernel(dst_ref, bnd_ref, y_ref, f_ref, rows_ref):
    d = y_ref.shape[3]
    nt = d // LANES

    @pl.when(pl.program_id(2) == 0)
    def _():
        f_ref[...] = jnp.zeros_like(f_ref)

    rows_ref[...] = _to_tiled(y_ref[0, 0].astype(F32))
    lo = bnd_ref[0, 0, pl.program_id(1)]
    hi = bnd_ref[0, 0, pl.program_id(1) + 1]

    def row(ref, start):
        return ref[pl.ds(start, nt, stride=SUBLANES), :]

    def one(j, carry):
        dst = dst_ref[0, 0, j]
        f_ref[0, pl.ds(dst, nt, stride=SUBLANES), :] = row(f_ref.at[0], dst) + row(rows_ref, _tiled_row(j, d))
        return carry

    def batch(jb, carry):
        dsts = [dst_ref[0, 0, jb * ROW_UNROLL + u] for u in range(ROW_UNROLL)]
        sums = [row(f_ref.at[0], dsts[u]) + row(rows_ref, jb * (ROW_UNROLL * nt) + u) for u in range(ROW_UNROLL)]
        for u in range(ROW_UNROLL):
            f_ref[0, pl.ds(dsts[u], nt, stride=SUBLANES), :] = sums[u]
        return carry

    head_end = jnp.minimum((lo + ROW_UNROLL - 1) // ROW_UNROLL * ROW_UNROLL, hi)
    tail_start = jnp.maximum(head_end, hi // ROW_UNROLL * ROW_UNROLL)
    lax.fori_loop(lo, head_end, one, 0)
    lax.fori_loop(head_end // ROW_UNROLL, tail_start // ROW_UNROLL, batch, 0)
    lax.fori_loop(tail_start, hi, one, 0)


def _moe_scatter(y, dst3, bounds3, n):
    b, n_exp, cap, d = y.shape
    nt = _tile(n, TOKEN_SPLIT)
    lt = d // LANES
    return pl.pallas_call(
        _scatter_kernel,
        out_shape=jax.ShapeDtypeStruct((b, n * lt, LANES), F32),
        grid=(b, n // nt, n_exp),
        in_specs=[pl.BlockSpec((1, 1, cap), lambda bi, ki, ei: (bi * n_exp + ei, 0, 0), memory_space=pltpu.SMEM),
                  pl.BlockSpec((1, 1, n // nt + 1), lambda bi, ki, ei: (bi * n_exp + ei, 0, 0),
                               memory_space=pltpu.SMEM),
                  pl.BlockSpec((1, 1, cap, d), lambda bi, ki, ei: (bi, ei, 0, 0))],
        out_specs=pl.BlockSpec((1, nt * lt, LANES), lambda bi, ki, ei: (bi, ki, 0)),
        scratch_shapes=[pltpu.VMEM((cap * lt, LANES), F32)],
        compiler_params=_cparams(("parallel", "parallel", "arbitrary")),
        name="moe_scatter",
    )(dst3, bounds3, y)


def _post_kernel(x_ref, f_ref, g_ref, lng_ref, lnb_ref, o_ref):
    f = _from_tiled(f_ref[0], x_ref.shape[2])
    o_ref[0] = _ln(DEEPNORM_ALPHA * x_ref[0] + (1.0 + g_ref[0]) * f, lng_ref[...], lnb_ref[...])


def _post(x1, f_tiled, g, lng, lnb):
    b, n, d = x1.shape
    tm = _tile(n, TOKEN_TILE)
    blk = pl.BlockSpec((1, tm, d), lambda bi, ti: (bi, ti, 0))
    f_blk = pl.BlockSpec((1, tm * d // LANES, LANES), lambda bi, ti: (bi, ti, 0))
    full = lambda a: pl.BlockSpec(a.shape, lambda bi, ti: (0,) * a.ndim)
    return pl.pallas_call(
        _post_kernel,
        out_shape=jax.ShapeDtypeStruct((b, n, d), F32),
        grid=(b, n // tm),
        in_specs=[blk, f_blk, pl.BlockSpec((1, 1, d), lambda bi, ti: (bi, 0, 0)), full(lng), full(lnb)],
        out_specs=blk,
        compiler_params=_cparams(("parallel", "parallel")),
        name="ffn_post",
    )(x1, f_tiled, g, lng, lnb)


def _select_kernel(aff_ref, idx_ref, gate_ref, *, cap, n_exp):
    a = aff_ref[0]
    er, lanes = a.shape
    r_n = er // n_exp
    e_pad = 16
    bits = pltpu.bitcast(a, jnp.int32)

    member = (lax.broadcasted_iota(jnp.int32, (e_pad, er), 1) // r_n
              == lax.broadcasted_iota(jnp.int32, (e_pad, er), 0))
    gs = jnp.where(member, 1.0, 0.0).astype(BF16)
    member_t = (lax.broadcasted_iota(jnp.int32, (er, e_pad), 0) // r_n
                == lax.broadcasted_iota(jnp.int32, (er, e_pad), 1))
    gst = jnp.where(member_t, 1.0, 0.0).astype(BF16)
    ri = lax.broadcasted_iota(jnp.int32, (er, er), 0)
    rj = lax.broadcasted_iota(jnp.int32, (er, er), 1)
    rows_before = jnp.where((ri // r_n == rj // r_n) & (rj < ri), 1.0, 0.0).astype(BF16)
    li = lax.broadcasted_iota(jnp.int32, (lanes, lanes), 0)
    lj = lax.broadcasted_iota(jnp.int32, (lanes, lanes), 1)
    lanes_upto = jnp.where(li <= lj, 1.0, 0.0).astype(BF16)

    def expert_total(m):
        per_lane = _dot(gs, m.astype(BF16))
        return jnp.broadcast_to(jnp.sum(per_lane, axis=1, keepdims=True), per_lane.shape)

    def to_rows(ev):
        if r_n % SUBLANES == 0 and n_exp == e_pad:
            return jnp.broadcast_to(ev[:, None, :], (e_pad, r_n, lanes)).reshape(er, lanes)
        hi = jnp.floor(ev * (1.0 / 128.0))
        return 128.0 * _dot(gst, hi.astype(BF16)) + _dot(gst, (ev - 128.0 * hi).astype(BF16))

    def search(i, t):
        cand = t | lax.shift_left(jnp.int32(1), 30 - i)
        cnt = expert_total(jnp.where(bits >= cand, 1.0, 0.0))
        return jnp.where(to_rows(jnp.where(cnt >= cap, 1.0, 0.0)) > 0.5, cand, t)

    t = lax.fori_loop(0, 31, search, jnp.zeros((er, lanes), jnp.int32))
    gt = jnp.where(bits > t, 1.0, 0.0)
    eq = jnp.where(bits == t, 1.0, 0.0)

    def prefix(m):
        rc = _dot(m.astype(BF16), lanes_upto)
        off = _dot(rows_before, jnp.broadcast_to(rc[:, lanes - 1:lanes], m.shape).astype(BF16))
        return rc, off

    rc_eq, off_eq = prefix(eq)
    need = cap - to_rows(expert_total(gt))
    sel = jnp.maximum(gt, eq * jnp.where(off_eq + rc_eq - eq < need, 1.0, 0.0))
    rc, off = prefix(sel)
    row_end = off + jnp.broadcast_to(rc[:, lanes - 1:lanes], rc.shape)

    slot = lax.broadcasted_iota(jnp.int32, (cap, lanes), 0).astype(F32) + 1.0
    lane_c = lax.broadcasted_iota(jnp.int32, (cap, lanes), 1)
    lane_cf = lane_c.astype(F32)
    diag = lax.broadcasted_iota(jnp.int32, (r_n, lanes), 0) == lax.broadcasted_iota(jnp.int32, (r_n, lanes), 1)
    lane_r = lax.broadcasted_iota(jnp.int32, (1, lanes), 1)
    pad = jnp.zeros((lanes - r_n, lanes), F32)

    def table(m):
        return (jnp.concatenate([m, pad], axis=0) if r_n < lanes else m).astype(BF16)

    all_ones = jnp.ones((lanes, lanes), BF16)
    idx_all = jnp.zeros((cap, lanes), jnp.int32)
    gate_all = jnp.zeros((cap, lanes), F32)
    for e in range(n_exp):
        r0 = e * r_n
        end_lane = jnp.sum(jnp.where(diag, row_end[r0:r0 + r_n], 0.0), axis=0, keepdims=True)
        end_lane = jnp.where(lane_r < r_n, end_lane, 3.0e38)
        row_j = jnp.sum(jnp.where(end_lane < slot, 1.0, 0.0), axis=1, keepdims=True)
        oh = jnp.where(lane_cf == row_j, 1.0, 0.0).astype(BF16)
        cum_e = off[r0:r0 + r_n] + rc[r0:r0 + r_n]
        cum_hi = jnp.floor(cum_e * (1.0 / 128.0))
        cum_2 = _dot(oh, jnp.concatenate([table(cum_hi), table(cum_e - 128.0 * cum_hi)], axis=1))
        cum_j = 128.0 * cum_2[:, :lanes] + cum_2[:, lanes:]
        lane_j = _dot(jnp.where(cum_j < slot, 1.0, 0.0).astype(BF16), all_ones)
        a_e = a[r0:r0 + r_n]
        a1 = a_e.astype(BF16).astype(F32)
        a2 = (a_e - a1).astype(BF16).astype(F32)
        a3 = a_e - a1 - a2
        aff_2 = _dot(oh, jnp.concatenate([table(a1), table(a2)], axis=1))
        aff_j = aff_2[:, :lanes] + aff_2[:, lanes:] + _dot(oh, table(a3))
        gate_j = jnp.sum(jnp.where(lane_cf == lane_j, aff_j, 0.0), axis=1, keepdims=True)
        idx_j = (row_j * float(lanes) + lane_j).astype(jnp.int32)
        idx_all = jnp.where(lane_c == e, idx_j, idx_all)
        gate_all = jnp.where(lane_c == e, gate_j, gate_all)
    idx_ref[0] = idx_all
    gate_ref[0] = gate_all


def _moe_select(aff_t, cap):
    b, n_exp, n = aff_t.shape
    lanes = HEAD_SLOT
    er = n_exp * (n // lanes)
    idx, gates = pl.pallas_call(
        functools.partial(_select_kernel, cap=cap, n_exp=n_exp),
        out_shape=(jax.ShapeDtypeStruct((b, cap, lanes), jnp.int32), jax.ShapeDtypeStruct((b, cap, lanes), F32)),
        grid=(b,),
        in_specs=[pl.BlockSpec((1, er, lanes), lambda bi: (bi, 0, 0))],
        out_specs=(pl.BlockSpec((1, cap, lanes), lambda bi: (bi, 0, 0)),
                   pl.BlockSpec((1, cap, lanes), lambda bi: (bi, 0, 0))),
        compiler_params=_cparams(("parallel",)),
        name="moe_select",
    )(aff_t.reshape(b, er, lanes))
    to_expert_major = lambda m: jnp.swapaxes(m[:, :, :n_exp], 1, 2)
    return to_expert_major(gates), to_expert_major(idx)


def _moe(x1, aff_t, hp, gf, wg, wu, wd, lng, lnb):
    b, n, d = x1.shape
    n_exp = aff_t.shape[1]
    cap = CAPACITY_FACTOR * n // n_exp
    gates, idx = _moe_select(aff_t, cap)
    gates3 = gates.reshape(b * n_exp, cap, 1)
    nt = _tile(n, TOKEN_SPLIT)
    edges = jnp.arange(0, n + 1, nt, dtype=jnp.int32)
    bounds3 = jnp.sum(idx[..., None] < edges, axis=2, dtype=jnp.int32).reshape(b * n_exp, 1, n // nt + 1)
    src3 = _tiled_row(idx, d // 2).reshape(b * n_exp, 1, cap)
    dst3 = _tiled_row(idx % nt, d).reshape(b * n_exp, 1, cap)
    y = _moe_ffn(hp, src3, gates3, wg, wu, wd, cap)
    f_tiled = _moe_scatter(y, dst3, bounds3, n)
    return _post(x1, f_tiled, gf, lng, lnb)


def _rope_angles(n_tokens, dim):
    n_rows = n_tokens // GRID_W
    row = jnp.repeat(jnp.arange(n_rows, dtype=F32), GRID_W)
    col = jnp.tile(jnp.arange(GRID_W, dtype=F32), n_rows)
    n_freq = dim // 4
    inv_freq = ROPE_BASE ** (-jnp.arange(n_freq, dtype=F32) / n_freq)
    ang = jnp.concatenate([row[:, None] * inv_freq, col[:, None] * inv_freq], -1)
    return jnp.cos(ang), jnp.sin(ang)


def _rope_tables(n_tokens, dim, lane_offset, identity):
    half = dim // 2
    if identity:
        cos = jnp.ones((n_tokens, half), F32)
        sin = jnp.zeros((n_tokens, half), F32)
    else:
        cos, sin = _rope_angles(n_tokens, dim)
    c2 = jnp.concatenate([cos, cos], -1)
    s2 = jnp.concatenate([-sin, sin], -1)
    reps = (HEAD_SLOT - lane_offset) // dim if lane_offset == 0 else 1
    ck = jnp.zeros((n_tokens, HEAD_SLOT), F32)
    sk = jnp.zeros((n_tokens, HEAD_SLOT), F32)
    for r in range(reps):
        ck = ck.at[:, lane_offset + r * dim:lane_offset + (r + 1) * dim].set(c2)
        sk = sk.at[:, lane_offset + r * dim:lane_offset + (r + 1) * dim].set(s2)
    return ck, sk, c2.T, s2.T


def _deinterleave(w, axis):
    return jnp.concatenate([jnp.take(w, jnp.arange(0, w.shape[axis], 2), axis=axis),
                            jnp.take(w, jnp.arange(1, w.shape[axis], 2), axis=axis)], axis=axis)


def _prep_layer0(w_in, ln_g, ln_b, ws, bs, q_norm, w_uq, kv_norm, w_ukv):
    d = w_in.shape[0]
    w = GMLP_WIDTH
    o_kr = 2 * w + MLA_Q_RANK + MLA_KV_RANK
    kr = _deinterleave(w_in[:, o_kr:o_kr + MLA_ROPE], 1)
    kr_sw = jnp.concatenate([kr[:, MLA_ROPE // 2:], kr[:, :MLA_ROPE // 2]], 1)
    slot = lambda m: jnp.zeros((d, HEAD_SLOT), F32).at[:, MLA_NOPE:MLA_NOPE + MLA_ROPE].set(m)
    win = jnp.concatenate([w_in[:, :o_kr], slot(kr), slot(kr_sw)], 1).astype(BF16)

    qd = MLA_NOPE + MLA_ROPE
    wq = w_uq.reshape(MLA_Q_RANK, MLA_HEADS, qd)
    wq = jnp.concatenate([wq[..., :MLA_NOPE], _deinterleave(wq[..., MLA_NOPE:], 2),
                          jnp.zeros((MLA_Q_RANK, MLA_HEADS, HEAD_SLOT - qd), F32)], -1)
    wq_t = wq.reshape(MLA_Q_RANK, MLA_HEADS * HEAD_SLOT).T.astype(BF16)

    wkv = w_ukv.reshape(MLA_KV_RANK, MLA_HEADS, MLA_NOPE + MLA_V)
    wk = jnp.concatenate([wkv[..., :MLA_NOPE], jnp.zeros((MLA_KV_RANK, MLA_HEADS, HEAD_SLOT - MLA_NOPE), F32)], -1)
    wk = wk.reshape(MLA_KV_RANK, MLA_HEADS * HEAD_SLOT).astype(BF16)
    wv_t = wkv[..., MLA_NOPE:].reshape(MLA_KV_RANK, MLA_HEADS * MLA_V).T.astype(BF16)

    bsb = jnp.broadcast_to(bs[:, :, None], (GMLP_GROUPS, CHUNK, GMLP_GROUP_CH))
    return (win, ln_g.reshape(1, -1), ln_b.reshape(1, -1), ws.astype(BF16), bsb,
            q_norm.reshape(1, -1), wq_t, kv_norm.reshape(1, -1), wk, wv_t)


def _prep_layer1(w_in):
    d = w_in.shape[0]
    dw = 2 * DIFF_HEADS * DIFF_HEAD_DIM
    perm = lambda m: _deinterleave(m.reshape(d, 2 * DIFF_HEADS, DIFF_HEAD_DIM), 2).reshape(d, dw)
    wq_t = perm(w_in[:, :dw]).T.astype(BF16)
    wk = perm(w_in[:, dw:2 * dw]).astype(BF16)
    wv_t = w_in[:, 2 * dw:].T.astype(BF16)
    return wq_t, wk, wv_t


def _split6(m_row):
    return [v[:, None, :] for v in jnp.split(m_row, 6, axis=-1)]


def kernel(x, c, ctx, c_ctx, w_mod_0, b_mod_0, w_in_0, gmlp_ln_g_0, gmlp_ln_b_0, gmlp_ws_0, gmlp_bs_0, mla_q_norm_0, mla_w_uq_0, mla_kv_norm_0, mla_w_ukv_0, w_out_0, ln_mix_g_0, ln_mix_b_0, router_0, w_gate_0, w_up_0, w_down_0, ln_ffn_g_0, ln_ffn_b_0, w_mod_1, b_mod_1, w_in_1, lambda_q1_1, lambda_k1_1, lambda_q2_1, lambda_k2_1, subln_g_1, w_out_1, ln_mix_g_1, ln_mix_b_1, router_1, w_gate_1, w_up_1, w_down_1, ln_ffn_g_1, ln_ffn_b_1):
    b, n, d = x.shape
    m_ctx = ctx.shape[1]
    row = lambda v: v.reshape(1, -1)

    cond = jnp.concatenate([c, c_ctx[None, :], jnp.zeros((-(b + 1) % 8, d), F32)], 0)
    mod0 = _modulation(cond, w_mod_0, b_mod_0)
    mod1 = _modulation(cond, w_mod_1, b_mod_1)
    sh_a0, sc_a0, g_a0, sh_f0, sc_f0, g_f0 = _split6(mod0[:b])
    csh_a0, csc_a0, cg_a0, csh_f0, csc_f0, cg_f0 = [jnp.broadcast_to(v, (b, 1, d)) for v in _split6(mod0[b:b + 1])]
    sh_a1, sc_a1, g_a1, sh_f1, sc_f1, g_f1 = _split6(mod1[:b])
    csh_a1, csc_a1 = [jnp.broadcast_to(v, (b, 1, d)) for v in _split6(mod1[b:b + 1])[:2]]

    wts0 = _prep_layer0(w_in_0, gmlp_ln_g_0, gmlp_ln_b_0, gmlp_ws_0, gmlp_bs_0,
                        mla_q_norm_0, mla_w_uq_0, mla_kv_norm_0, mla_w_ukv_0)
    tabs_l = _rope_tables(n, MLA_ROPE, MLA_NOPE, identity=False)
    tabs_c = _rope_tables(m_ctx, MLA_ROPE, MLA_NOPE, identity=True)
    a_l, q_l, k_l, v_l = _proj0(x, sc_a0, sh_a0, wts0, tabs_l)
    a_c, q_c, k_c, v_c = _proj0(ctx, csc_a0, csh_a0, wts0, tabs_c)
    o_l = _attention(q_l, [(k_l, v_l), (k_c, v_c)], dv=MLA_V, n_sub=1, finalize=_finalize_plain, name="mla_attn")
    o_c = _attention(q_c, [(k_c, v_c)], dv=MLA_V, n_sub=1, finalize=_finalize_plain, name="mla_attn_ctx")

    w_out_a = w_out_0[:GMLP_WIDTH].astype(BF16)
    w_out_o = w_out_0[GMLP_WIDTH:].astype(BF16)
    router_t0 = router_0.T.astype(BF16)
    wg0, wu0, wd0 = w_gate_0.astype(BF16), w_up_0.astype(BF16), w_down_0.astype(BF16)
    lng, lnb = row(ln_mix_g_0), row(ln_mix_b_0)
    x1, aff, hp = _mixer_out(x, g_a0, lng, lnb, sc_f0, sh_f0, router_t0, [(a_l, w_out_a)], [(o_l, w_out_o)])
    x_lat = _moe(x1, aff, hp, g_f0, wg0, wu0, wd0, row(ln_ffn_g_0), row(ln_ffn_b_0))
    c1, caff, chp = _mixer_out(ctx, cg_a0, lng, lnb, csc_f0, csh_f0, router_t0, [(a_c, w_out_a)], [(o_c, w_out_o)])
    x_ctx = _moe(c1, caff, chp, cg_f0, wg0, wu0, wd0, row(ln_ffn_g_0), row(ln_ffn_b_0))

    lam_init = 0.8 - 0.6 * math.exp(-0.3 * 1)
    wts1 = _prep_layer1(w_in_1)
    tabs_l = _rope_tables(n, DIFF_HEAD_DIM, 0, identity=False)
    tabs_c = _rope_tables(m_ctx, DIFF_HEAD_DIM, 0, identity=True)
    q_l, k_l, v_l = _proj1(x_lat, sc_a1, sh_a1, wts1, tabs_l)
    _, k_c, v_c = _proj1(x_ctx, csc_a1, csh_a1, wts1, tabs_c)
    extra = (row(lambda_q1_1), row(lambda_k1_1), row(lambda_q2_1), row(lambda_k2_1), subln_g_1.reshape(-1, 1))
    o_l = _attention(q_l, [(k_l, v_l), (k_c, v_c)], extra, dv=2 * DIFF_HEAD_DIM, n_sub=2,
                     finalize=functools.partial(_finalize_diff, lam_init=lam_init), name="diff_attn")
    x1, aff, hp = _mixer_out(x_lat, g_a1, row(ln_mix_g_1), row(ln_mix_b_1), sc_f1, sh_f1, router_1.T.astype(BF16),
                             [], [(o_l, w_out_1.astype(BF16))])
    return _moe(x1, aff, hp, g_f1, w_gate_1.astype(BF16), w_up_1.astype(BF16), w_down_1.astype(BF16),
                row(ln_ffn_g_1), row(ln_ffn_b_1))
```

```python
import functools
import math

import jax
import jax.numpy as jnp
from jax import lax
from jax.experimental import pallas as pl
from jax.experimental.pallas import tpu as pltpu

F32 = jnp.float32
BF16 = jnp.bfloat16

DEPTH = 2
GRID_W = 64
ROPE_BASE = 10000.0
CHUNK = 128
GMLP_GROUPS = 4
GMLP_GROUP_CH = 128
GMLP_WIDTH = GMLP_GROUPS * GMLP_GROUP_CH
MLA_HEADS = 8
MLA_NOPE = 64
MLA_ROPE = 32
MLA_V = 64
MLA_Q_RANK = 256
MLA_KV_RANK = 128
DIFF_HEADS = 8
DIFF_HEAD_DIM = 64
CAPACITY_FACTOR = 2
DEEPNORM_ALPHA = (2 * DEPTH) ** 0.25
LN_EPS = 1e-5
RMS_EPS = 1e-6
LOG2E = 1.4426950408889634

LANES = 128
SUBLANES = 8
HEAD_SLOT = LANES
NEG_BIG = -1e30
VMEM_LIMIT = 56 * 1024 * 1024
TOKEN_TILE = 512
MOD_TILE = 1024


def _cparams(sem):
    return pltpu.CompilerParams(dimension_semantics=sem, vmem_limit_bytes=VMEM_LIMIT)


def _tile(n, pref):
    return pref if n % pref == 0 else n


def _ln(x, g, b):
    mu = jnp.mean(x, axis=-1, keepdims=True)
    xc = x - mu
    var = jnp.mean(xc * xc, axis=-1, keepdims=True)
    return xc * lax.rsqrt(var + LN_EPS) * g + b


def _rms(x, g):
    return x * lax.rsqrt(jnp.mean(x * x, axis=-1, keepdims=True) + RMS_EPS) * g


def _gelu(x):
    return 0.5 * x * (1.0 + lax.erf(x * (2.0 ** -0.5)))


def _to_tiled(v):
    m, w = v.shape
    nt = w // LANES
    tiles = [v[:, c * LANES:(c + 1) * LANES].reshape(m // SUBLANES, SUBLANES, LANES) for c in range(nt)]
    return jnp.stack(tiles, axis=1).reshape(m * nt, LANES)


def _from_tiled(t, w):
    nt = w // LANES
    m = t.shape[0] // nt
    t4 = t.reshape(m // SUBLANES, nt, SUBLANES, LANES)
    return jnp.concatenate([t4[:, c].reshape(m, LANES) for c in range(nt)], axis=1)


def _tiled_row(i, w):
    return (i // SUBLANES) * (SUBLANES * (w // LANES)) + i % SUBLANES


def _dot(a, b):
    return jnp.dot(a, b, preferred_element_type=F32)


def _dot_nt(a, b):
    return lax.dot_general(a, b, (((1,), (1,)), ((), ())), preferred_element_type=F32)


def _dot_tn(a, b):
    return lax.dot_general(a, b, (((0,), (0,)), ((), ())), preferred_element_type=F32)


def _mod_kernel(c_ref, w_ref, b_ref, o_ref):
    c = c_ref[...]
    s = c / (1.0 + jnp.exp(-c))
    o_ref[...] = _dot(s.astype(BF16), w_ref[...].astype(BF16)) + b_ref[...]


def _modulation(cond, w_mod, b_mod):
    r, d = cond.shape
    n = w_mod.shape[1]
    tn = _tile(n, MOD_TILE)
    return pl.pallas_call(
        _mod_kernel,
        out_shape=jax.ShapeDtypeStruct((r, n), F32),
        grid=(n // tn,),
        in_specs=[pl.BlockSpec((r, d), lambda j: (0, 0)),
                  pl.BlockSpec((d, tn), lambda j: (0, j)),
                  pl.BlockSpec((1, tn), lambda j: (0, j))],
        out_specs=pl.BlockSpec((r, tn), lambda j: (0, j)),
        compiler_params=_cparams(("parallel",)),
        name="modulation",
    )(cond, w_mod, b_mod.reshape(1, n))


def _proj0_kernel(x_ref, sc_ref, sh_ref, win_ref, lng_ref, lnb_ref, ws_ref, bs_ref,
                  qn_ref, wq_ref, kvn_ref, wk_ref, wv_ref, ck_ref, sk_ref, cq_ref, sq_ref,
                  a_ref, q_ref, k_ref, v_ref, *, q_scale):
    tm = x_ref.shape[1]
    h = (x_ref[0] * (1.0 + sc_ref[0]) + sh_ref[0]).astype(BF16)
    y = _dot(h, win_ref[...])
    w = GMLP_WIDTH
    u = _gelu(y[:, :w])
    vn = _ln(_gelu(y[:, w:2 * w]), lng_ref[...], lnb_ref[...]).astype(BF16)
    for ci in range(tm // CHUNK):
        r0 = ci * CHUNK
        for g in range(GMLP_GROUPS):
            c0 = g * GMLP_GROUP_CH
            mixed = _dot(ws_ref[g], vn[r0:r0 + CHUNK, c0:c0 + GMLP_GROUP_CH]) + bs_ref[g]
            a_ref[0, r0:r0 + CHUNK, c0:c0 + GMLP_GROUP_CH] = (
                u[r0:r0 + CHUNK, c0:c0 + GMLP_GROUP_CH] * mixed).astype(BF16)

    o = 2 * w
    cq = _rms(y[:, o:o + MLA_Q_RANK], qn_ref[...]).astype(BF16)
    qt = _dot_nt(wq_ref[...], cq) * q_scale
    q_ref[0] = qt.astype(BF16)
    hr = MLA_ROPE // 2
    cq_t = cq_ref[...]
    sq_t = sq_ref[...]
    for hd in range(MLA_HEADS):
        r0 = hd * HEAD_SLOT + MLA_NOPE
        x1 = qt[r0:r0 + hr]
        x2 = qt[r0 + hr:r0 + 2 * hr]
        blk = jnp.concatenate([x1, x2], axis=0)
        swp = jnp.concatenate([x2, x1], axis=0)
        q_ref[0, r0:r0 + 2 * hr, :] = (blk * cq_t + swp * sq_t).astype(BF16)

    o += MLA_Q_RANK
    ckv = _rms(y[:, o:o + MLA_KV_RANK], kvn_ref[...]).astype(BF16)
    o += MLA_KV_RANK
    kr = y[:, o:o + HEAD_SLOT] * ck_ref[...] + y[:, o + HEAD_SLOT:o + 2 * HEAD_SLOT] * sk_ref[...]
    k_ref[0] = (_dot(ckv, wk_ref[...]) + jnp.tile(kr, (1, MLA_HEADS))).astype(BF16)
    v_ref[0] = _dot_nt(wv_ref[...], ckv).astype(BF16)


def _proj0(x, sc, sh, wts, tabs):
    b, n, d = x.shape
    tm = _tile(n, TOKEN_TILE)
    ck, sk, cq, sq = tabs
    win, lng, lnb, ws, bsb, qn, wq, kvn, wk, wv = wts
    hw = MLA_HEADS * HEAD_SLOT
    vw = MLA_HEADS * MLA_V
    full = lambda a: pl.BlockSpec(a.shape, lambda bi, ti: (0,) * a.ndim)
    q_scale = (MLA_NOPE + MLA_ROPE) ** -0.5 * LOG2E
    return pl.pallas_call(
        functools.partial(_proj0_kernel, q_scale=q_scale),
        out_shape=(jax.ShapeDtypeStruct((b, n, GMLP_WIDTH), BF16),
                   jax.ShapeDtypeStruct((b, hw, n), BF16),
                   jax.ShapeDtypeStruct((b, n, hw), BF16),
                   jax.ShapeDtypeStruct((b, vw, n), BF16)),
        grid=(b, n // tm),
        in_specs=[pl.BlockSpec((1, tm, d), lambda bi, ti: (bi, ti, 0)),
                  pl.BlockSpec((1, 1, d), lambda bi, ti: (bi, 0, 0)),
                  pl.BlockSpec((1, 1, d), lambda bi, ti: (bi, 0, 0)),
                  full(win), full(lng), full(lnb), full(ws), full(bsb),
                  full(qn), full(wq), full(kvn), full(wk), full(wv),
                  pl.BlockSpec((tm, HEAD_SLOT), lambda bi, ti: (ti, 0)),
                  pl.BlockSpec((tm, HEAD_SLOT), lambda bi, ti: (ti, 0)),
                  pl.BlockSpec((MLA_ROPE, tm), lambda bi, ti: (0, ti)),
                  pl.BlockSpec((MLA_ROPE, tm), lambda bi, ti: (0, ti))],
        out_specs=(pl.BlockSpec((1, tm, GMLP_WIDTH), lambda bi, ti: (bi, ti, 0)),
                   pl.BlockSpec((1, hw, tm), lambda bi, ti: (bi, 0, ti)),
                   pl.BlockSpec((1, tm, hw), lambda bi, ti: (bi, ti, 0)),
                   pl.BlockSpec((1, vw, tm), lambda bi, ti: (bi, 0, ti))),
        compiler_params=_cparams(("parallel", "parallel")),
        name="proj0",
    )(x, sc, sh, win, lng, lnb, ws, bsb, qn, wq, kvn, wk, wv, ck, sk, cq, sq)


def _proj1_kernel(x_ref, sc_ref, sh_ref, wq_ref, wk_ref, wv_ref, ck_ref, sk_ref, cq_ref, sq_ref,
                  q_ref, k_ref, v_ref, *, q_scale):
    h = (x_ref[0] * (1.0 + sc_ref[0]) + sh_ref[0]).astype(BF16)
    n_sub = 2 * DIFF_HEADS
    hd = DIFF_HEAD_DIM
    hr = hd // 2
    qt = _dot_nt(wq_ref[...], h) * q_scale
    cq_t = cq_ref[...]
    sq_t = sq_ref[...]
    for s in range(n_sub):
        r0 = s * hd
        x1 = qt[r0:r0 + hr]
        x2 = qt[r0 + hr:r0 + hd]
        swp = jnp.concatenate([x2, x1], axis=0)
        q_ref[0, r0:r0 + hd, :] = (qt[r0:r0 + hd] * cq_t + swp * sq_t).astype(BF16)
    k = _dot(h, wk_ref[...])
    width = k.shape[1]
    lane = lax.broadcasted_iota(jnp.int32, k.shape, 1)
    partner = jnp.where((lane % hd) < hr, pltpu.roll(k, width - hr, 1), pltpu.roll(k, hr, 1))
    reps = width // HEAD_SLOT
    k_ref[0] = (k * jnp.tile(ck_ref[...], (1, reps)) + partner * jnp.tile(sk_ref[...], (1, reps))).astype(BF16)
    v_ref[0] = _dot_nt(wv_ref[...], h).astype(BF16)


def _proj1(x, sc, sh, wts, tabs):
    b, n, d = x.shape
    tm = _tile(n, TOKEN_TILE)
    wq, wk, wv = wts
    ck, sk, cq, sq = tabs
    dw = wq.shape[0]
    full = lambda a: pl.BlockSpec(a.shape, lambda bi, ti: (0,) * a.ndim)
    q_scale = DIFF_HEAD_DIM ** -0.5 * LOG2E
    return pl.pallas_call(
        functools.partial(_proj1_kernel, q_scale=q_scale),
        out_shape=(jax.ShapeDtypeStruct((b, dw, n), BF16),
                   jax.ShapeDtypeStruct((b, n, dw), BF16),
                   jax.ShapeDtypeStruct((b, dw, n), BF16)),
        grid=(b, n // tm),
        in_specs=[pl.BlockSpec((1, tm, d), lambda bi, ti: (bi, ti, 0)),
                  pl.BlockSpec((1, 1, d), lambda bi, ti: (bi, 0, 0)),
                  pl.BlockSpec((1, 1, d), lambda bi, ti: (bi, 0, 0)),
                  full(wq), full(wk), full(wv),
                  pl.BlockSpec((tm, HEAD_SLOT), lambda bi, ti: (ti, 0)),
                  pl.BlockSpec((tm, HEAD_SLOT), lambda bi, ti: (ti, 0)),
                  pl.BlockSpec((DIFF_HEAD_DIM, tm), lambda bi, ti: (0, ti)),
                  pl.BlockSpec((DIFF_HEAD_DIM, tm), lambda bi, ti: (0, ti))],
        out_specs=(pl.BlockSpec((1, dw, tm), lambda bi, ti: (bi, 0, ti)),
                   pl.BlockSpec((1, tm, dw), lambda bi, ti: (bi, ti, 0)),
                   pl.BlockSpec((1, dw, tm), lambda bi, ti: (bi, 0, ti))),
        compiler_params=_cparams(("parallel", "parallel")),
        name="proj1",
    )(x, sc, sh, wq, wk, wv, ck, sk, cq, sq)


SHIFT_OK_LO = 2.0 ** -85
SHIFT_OK_HI = 2.0 ** 100
ATTN_TQ = 1024
ATTN_TK = 256
KMAX_ROWS = 32
ATTN_UNROLL = 16


def _key_norm_max(k_ref, kmax_ref, *, tk, n_sub):
    nchunk = k_ref.shape[1] // tk
    sub_w = HEAD_SLOT // n_sub
    row = lax.broadcasted_iota(jnp.int32, (SUBLANES, HEAD_SLOT), 0)
    lane = lax.broadcasted_iota(jnp.int32, (SUBLANES, HEAD_SLOT), 1)
    sel = jnp.where(lane // sub_w == row, 1.0, 0.0).astype(BF16)
    group = max(g for g in range(1, 12) if nchunk % g == 0)
    rows = group * tk

    def body(c, mx):
        off = pl.multiple_of(c * rows, rows)
        kc = k_ref[0, pl.ds(off, rows), :].astype(F32)
        return jnp.maximum(mx, _dot_nt(sel, (kc * kc).astype(BF16)))

    mx = lax.fori_loop(0, nchunk // group, body, jnp.zeros((SUBLANES, rows), F32))
    tile8 = jnp.broadcast_to(jnp.max(mx, axis=1, keepdims=True), (SUBLANES, HEAD_SLOT))
    kmax_ref[...] = jnp.tile(tile8, (kmax_ref.shape[0] // SUBLANES, 1))


def _sweep_bounded(qs, shifts, k_ref, v_ref, s_ref, *, tk, unroll):
    tq = qs[0].shape[1]
    dv = v_ref.shape[1]
    nchunk = k_ref.shape[1] // tk

    def aligned(off):
        return off if isinstance(off, int) else pl.multiple_of(off, tk)

    def scores(off, buf):
        kc = k_ref[0, pl.ds(aligned(off), tk), :]
        for r, q in enumerate(qs):
            s_ref[buf, r] = _dot(kc, q)

    def consume(off, buf, carry):
        vc = v_ref[0, :, pl.ds(aligned(off), tk)]
        nxt = []
        for r, ((l8, acc), m) in enumerate(zip(carry, shifts)):
            p = jnp.exp2(s_ref[buf, r] - m)
            l8 = l8 + jnp.sum(p.reshape(tk // SUBLANES, SUBLANES, tq), axis=0)
            nxt.append((l8, acc + _dot(vc, p.astype(BF16))))
        return tuple(nxt)

    def body(i, carry):
        for u in range(unroll):
            off = pl.multiple_of((i * unroll + u) * tk, tk)
            scores(off + tk, (u + 1) % 2)
            carry = consume(off, u % 2, carry)
        return carry

    carry = tuple((jnp.zeros((SUBLANES, tq), F32), jnp.zeros((dv, tq), F32)) for _ in qs)
    scores(0, 0)
    n_loop = (nchunk - 1) // unroll * unroll
    if n_loop:
        carry = lax.fori_loop(0, n_loop // unroll, body, carry)
    for c in range(n_loop, nchunk):
        if c + 1 < nchunk:
            scores((c + 1) * tk, (c + 1) % 2)
        carry = consume(c * tk, c % 2, carry)
    return [(jnp.sum(l8, axis=0, keepdims=True), acc) for l8, acc in carry]


def _sweep_running_max(qs, k_ref, v_ref, *, tk):
    tq = qs[0].shape[1]
    dv = v_ref.shape[1]

    def body(c, carry):
        off = pl.multiple_of(c * tk, tk)
        kc = k_ref[0, pl.ds(off, tk), :]
        vc = v_ref[0, :, pl.ds(off, tk)]
        nxt = []
        for (m, l, acc), q in zip(carry, qs):
            s = _dot(kc, q)
            mn = jnp.maximum(m, jnp.max(s, axis=0, keepdims=True))
            alpha = jnp.exp2(m - mn)
            p = jnp.exp2(s - mn)
            nxt.append((mn, alpha * l + jnp.sum(p, axis=0, keepdims=True), alpha * acc + _dot(vc, p.astype(BF16))))
        return tuple(nxt)

    init = tuple((jnp.full((1, tq), NEG_BIG, F32), jnp.zeros((1, tq), F32), jnp.zeros((dv, tq), F32)) for _ in qs)
    res = lax.fori_loop(0, k_ref.shape[1] // tk, body, init)
    return [(l, acc) for _, l, acc in res]


def _attn_kernel(*refs, tk, unroll, n_sub, n_src, finalize):
    q_ref = refs[0]
    srcs = [(refs[1 + 2 * i], refs[2 + 2 * i]) for i in range(n_src)]
    n_scratch = 3 if n_src == 1 else 5
    extra = refs[1 + 2 * n_src:-n_scratch]
    o_ref, kmax_ref, s_ref = refs[-n_scratch:][:3]
    k_ref, v_ref = srcs[0] if n_src == 1 else refs[-2:]

    @pl.when(pl.program_id(2) == 0)
    def _():
        if n_src > 1:
            off = 0
            for ks_ref, vs_ref in srcs:
                t = ks_ref.shape[1]
                k_ref[0, off:off + t] = ks_ref[0]
                v_ref[0, :, off:off + t] = vs_ref[0]
                off += t
        _key_norm_max(k_ref, kmax_ref, tk=tk, n_sub=n_sub)

    q = q_ref[0]
    sub_w = HEAD_SLOT // n_sub
    row = lax.broadcasted_iota(jnp.int32, q.shape, 0)
    qf = q.astype(F32)
    qs, shifts = [], []
    for r in range(n_sub):
        mine = (row // sub_w) == r
        qs.append(jnp.where(mine, q, jnp.zeros_like(q)) if n_sub > 1 else q)
        qn2 = jnp.sum(jnp.where(mine, qf * qf, 0.0), axis=0, keepdims=True)
        shifts.append(jnp.sqrt(qn2 * kmax_ref[r:r + 1, 0:1]))

    res = _sweep_bounded(qs, shifts, k_ref, v_ref, s_ref, tk=tk, unroll=unroll)
    o_ref[0] = finalize(res, extra).astype(BF16)
    ok = None
    for l, _ in res:
        ok_r = (jnp.min(l) >= SHIFT_OK_LO) & (jnp.max(l) <= SHIFT_OK_HI)
        ok = ok_r if ok is None else ok & ok_r

    @pl.when(jnp.logical_not(ok))
    def _():
        o_ref[0] = finalize(_sweep_running_max(qs, k_ref, v_ref, tk=tk), extra).astype(BF16)


def _finalize_plain(res, extra):
    (l, acc), = res
    return acc / l


def _finalize_diff(res, extra, *, lam_init):
    lam_ref, g_ref = extra
    (l1, a1), (l2, a2) = res
    lam = (jnp.exp(jnp.sum(lam_ref[0:1] * lam_ref[1:2], axis=-1, keepdims=True))
           - jnp.exp(jnp.sum(lam_ref[2:3] * lam_ref[3:4], axis=-1, keepdims=True)) + lam_init)
    o = a1 / l1 - lam * (a2 / l2)
    return o * lax.rsqrt(jnp.mean(o * o, axis=0, keepdims=True) + RMS_EPS) * g_ref[...] * (1.0 - lam_init)


def _attention(qt, kv, extra=(), *, dv, n_sub, finalize, name):
    b, hw, n = qt.shape
    heads = hw // HEAD_SLOT
    tq = _tile(n, ATTN_TQ)
    tk = ATTN_TK if all(k.shape[1] % ATTN_TK == 0 for k, _ in kv) else LANES
    full = lambda a: pl.BlockSpec(a.shape, lambda bi, hi, qi: (0,) * a.ndim)
    kv_specs, kv_args = [], []
    for k, vt in kv:
        t = k.shape[1]
        kv_specs += [pl.BlockSpec((1, t, HEAD_SLOT), lambda bi, hi, qi: (bi, 0, hi)),
                     pl.BlockSpec((1, dv, t), lambda bi, hi, qi: (bi, hi, 0))]
        kv_args += [k, vt]
    scratch = [pltpu.VMEM((KMAX_ROWS, HEAD_SLOT), F32), pltpu.VMEM((2, n_sub, tk, tq), F32)]
    if len(kv) > 1:
        t_all = sum(k.shape[1] for k, _ in kv)
        scratch += [pltpu.VMEM((1, t_all, HEAD_SLOT), BF16), pltpu.VMEM((1, dv, t_all), BF16)]
    return pl.pallas_call(
        functools.partial(_attn_kernel, tk=tk, unroll=ATTN_UNROLL, n_sub=n_sub, n_src=len(kv), finalize=finalize),
        out_shape=jax.ShapeDtypeStruct((b, heads * dv, n), BF16),
        grid=(b, heads, n // tq),
        in_specs=[pl.BlockSpec((1, HEAD_SLOT, tq), lambda bi, hi, qi: (bi, hi, qi))] + kv_specs
        + [full(a) for a in extra],
        out_specs=pl.BlockSpec((1, dv, tq), lambda bi, hi, qi: (bi, hi, qi)),
        scratch_shapes=scratch,
        compiler_params=_cparams(("parallel", "parallel", "arbitrary")),
        name=name,
    )(qt, *kv_args, *extra)


def _out_kernel(*refs, n_rowmajor):
    (x_ref, g_ref, lng_ref, lnb_ref, scf_ref, shf_ref, rt_ref), rest = refs[:7], refs[7:]
    x1_ref, aff_ref, hp_ref = rest[-3:]
    ops = rest[:-3]
    m = None
    for i in range(len(ops) // 2):
        a = ops[2 * i][0]
        w = ops[2 * i + 1][...]
        part = _dot(a, w) if i < n_rowmajor else _dot_tn(a, w)
        m = part if m is None else m + part
    x1 = _ln(DEEPNORM_ALPHA * x_ref[0] + (1.0 + g_ref[0]) * m, lng_ref[...], lnb_ref[...])
    x1_ref[0] = x1
    hf = (x1 * (1.0 + scf_ref[0]) + shf_ref[0]).astype(BF16)
    lg = _dot_nt(rt_ref[...], hf)
    e = jnp.exp(lg - jnp.max(lg, axis=0, keepdims=True))
    aff_ref[0] = e / jnp.sum(e, axis=0, keepdims=True)
    dh = hf.shape[1] // 2
    lo = pltpu.bitcast(hf[:, :dh].astype(F32), jnp.uint32)
    hi = pltpu.bitcast(hf[:, dh:].astype(F32), jnp.uint32)
    hp_ref[0] = _to_tiled(lax.shift_right_logical(lo, jnp.uint32(16)) | (hi & jnp.uint32(0xFFFF0000)))


def _mixer_out(x, g, lng, lnb, scf, shf, router_t, rowmajor_ops, chanmajor_ops):
    b, n, d = x.shape
    e = router_t.shape[0]
    tm = _tile(n, TOKEN_TILE)
    vec = pl.BlockSpec((1, 1, d), lambda bi, ti: (bi, 0, 0))
    full = lambda a: pl.BlockSpec(a.shape, lambda bi, ti: (0,) * a.ndim)
    args = [x, g, lng, lnb, scf, shf, router_t]
    specs = [pl.BlockSpec((1, tm, d), lambda bi, ti: (bi, ti, 0)), vec, full(lng), full(lnb), vec, vec,
             full(router_t)]
    for a, w in rowmajor_ops:
        args += [a, w]
        specs += [pl.BlockSpec((1, tm, a.shape[2]), lambda bi, ti: (bi, ti, 0)), full(w)]
    for a, w in chanmajor_ops:
        args += [a, w]
        specs += [pl.BlockSpec((1, a.shape[1], tm), lambda bi, ti: (bi, 0, ti)), full(w)]
    return pl.pallas_call(
        functools.partial(_out_kernel, n_rowmajor=len(rowmajor_ops)),
        out_shape=(jax.ShapeDtypeStruct((b, n, d), F32), jax.ShapeDtypeStruct((b, e, n), F32),
                   jax.ShapeDtypeStruct((b, n * (d // 2) // LANES, LANES), jnp.uint32)),
        grid=(b, n // tm),
        in_specs=specs,
        out_specs=(pl.BlockSpec((1, tm, d), lambda bi, ti: (bi, ti, 0)),
                   pl.BlockSpec((1, e, tm), lambda bi, ti: (bi, 0, ti)),
                   pl.BlockSpec((1, tm * (d // 2) // LANES, LANES), lambda bi, ti: (bi, ti, 0))),
        compiler_params=_cparams(("parallel", "parallel")),
        name="mixer_out",
    )(*args)


ROW_UNROLL = 8
FFN_ROWS = 512
TOKEN_SPLIT = 4096


def _ffn_kernel(src_ref, nxt_ref, hp_ref, g_ref, wg_ref, wu_ref, wd_ref, y_ref, rows_ref, xs_ref, *, prefetch):
    cap = y_ref.shape[2]
    nt = rows_ref.shape[0] // cap
    dh = nt * LANES

    def gather(idx_ref, j, dst):
        rows_ref[pl.ds(dst, nt, stride=SUBLANES), :] = hp_ref[0, pl.ds(idx_ref[0, 0, j], nt, stride=SUBLANES), :]

    def gather_loop():
        def body(jb, carry):
            for u in range(ROW_UNROLL):
                gather(src_ref, jb * ROW_UNROLL + u, jb * (ROW_UNROLL * nt) + u)
            return carry

        lax.fori_loop(0, cap // ROW_UNROLL, body, 0)

    if prefetch:
        pl.when(pl.program_id(1) == 0)(gather_loop)
    else:
        gather_loop()

    blk = min(cap, FFN_ROWS)
    for c0 in range(0, cap, blk):
        w = _from_tiled(rows_ref[c0 * nt:(c0 + blk) * nt], dh)
        lo = pltpu.bitcast(lax.shift_left(w, jnp.uint32(16)), F32)
        hi = pltpu.bitcast(w & jnp.uint32(0xFFFF0000), F32)
        xs_ref[c0:c0 + blk] = jnp.concatenate([lo, hi], axis=1).astype(BF16)

    if prefetch:
        for j in range(cap):
            gather(nxt_ref, j, _tiled_row(j, dh))

    expert = pl.program_id(1 if prefetch else 0)
    mine = lax.broadcasted_iota(jnp.int32, (blk, LANES), 1) == expert
    for c0 in range(0, cap, blk):
        xs = xs_ref[c0:c0 + blk]
        gate = _dot(xs, wg_ref[0])
        hid = gate / (1.0 + jnp.exp(-gate)) * _dot(xs, wu_ref[0])
        g_col = jnp.sum(jnp.where(mine, g_ref[0, c0:c0 + blk], 0.0), axis=1, keepdims=True)
        y_ref[0, 0, c0:c0 + blk] = (_dot(hid.astype(BF16), wd_ref[0]) * g_col).astype(BF16)


def _activations_outweigh(activation_bytes, weight_bytes):
    return activation_bytes > weight_bytes


def _moe_ffn(hp, src3, gates, wg, wu, wd, cap):
    b, hp_rows, _ = hp.shape
    n_exp, d, ff = wg.shape
    nt = d // 2 // LANES
    sample_major = _activations_outweigh(hp_rows * LANES * 4, 3 * d * ff * 2)
    if sample_major:
        grid, be = (b, n_exp), (lambda g0, g1: (g0, g1))
        hp_spec = pl.BlockSpec((1, hp_rows, LANES), lambda g0, g1: (g0, 0, 0), pipeline_mode=pl.Buffered(1))
    else:
        grid, be = (n_exp, b), (lambda g0, g1: (g1, g0))
        hp_spec = pl.BlockSpec((1, hp_rows, LANES), lambda g0, g1: (g1, 0, 0))
    slot = lambda g0, g1: (be(g0, g1)[0] * n_exp + be(g0, g1)[1], 0, 0)
    weight = lambda g0, g1: (be(g0, g1)[1], 0, 0)
    nxt = lambda g0, g1: (be(g0, g1)[0] * n_exp + jnp.minimum(be(g0, g1)[1] + 1, n_exp - 1), 0, 0)
    return pl.pallas_call(
        functools.partial(_ffn_kernel, prefetch=sample_major),
        out_shape=jax.ShapeDtypeStruct((b, n_exp, cap, d), BF16),
        grid=grid,
        in_specs=[pl.BlockSpec((1, 1, cap), slot, memory_space=pltpu.SMEM),
                  pl.BlockSpec((1, 1, cap), nxt, memory_space=pltpu.SMEM),
                  hp_spec,
                  pl.BlockSpec((1, cap, LANES), lambda g0, g1: (be(g0, g1)[0], 0, 0)),
                  pl.BlockSpec((1, d, ff), weight),
                  pl.BlockSpec((1, d, ff), weight),
                  pl.BlockSpec((1, ff, d), weight)],
        out_specs=pl.BlockSpec((1, 1, cap, d), lambda g0, g1: be(g0, g1) + (0, 0)),
        scratch_shapes=[pltpu.VMEM((cap * nt, LANES), jnp.uint32), pltpu.VMEM((cap, d), BF16)],
        compiler_params=_cparams(("parallel", "arbitrary")),
        name="moe_ffn",
    )(src3, src3, hp, gates, wg, wu, wd)


def _scatter_kernel(dst_ref, bnd_ref, y_ref, f_ref, rows_ref):
    d = y_ref.shape[3]
    nt = d // LANES

    @pl.when(pl.program_id(2) == 0)
    def _():
        f_ref[...] = jnp.zeros_like(f_ref)

    rows_ref[...] = _to_tiled(y_ref[0, 0].astype(F32))
    lo = bnd_ref[0, 0, pl.program_id(1)]
    hi = bnd_ref[0, 0, pl.program_id(1) + 1]

    def row(ref, start):
        return ref[pl.ds(start, nt, stride=SUBLANES), :]

    def one(j, carry):
        dst = dst_ref[0, 0, j]
        f_ref[0, pl.ds(dst, nt, stride=SUBLANES), :] = row(f_ref.at[0], dst) + row(rows_ref, _tiled_row(j, d))
        return carry

    def batch(jb, carry):
        dsts = [dst_ref[0, 0, jb * ROW_UNROLL + u] for u in range(ROW_UNROLL)]
        sums = [row(f_ref.at[0], dsts[u]) + row(rows_ref, jb * (ROW_UNROLL * nt) + u) for u in range(ROW_UNROLL)]
        for u in range(ROW_UNROLL):
            f_ref[0, pl.ds(dsts[u], nt, stride=SUBLANES), :] = sums[u]
        return carry

    head_end = jnp.minimum((lo + ROW_UNROLL - 1) // ROW_UNROLL * ROW_UNROLL, hi)
    tail_start = jnp.maximum(head_end, hi // ROW_UNROLL * ROW_UNROLL)
    lax.fori_loop(lo, head_end, one, 0)
    lax.fori_loop(head_end // ROW_UNROLL, tail_start // ROW_UNROLL, batch, 0)
    lax.fori_loop(tail_start, hi, one, 0)


def _moe_scatter(y, dst3, bounds3, n):
    b, n_exp, cap, d = y.shape
    nt = _tile(n, TOKEN_SPLIT)
    lt = d // LANES
    return pl.pallas_call(
        _scatter_kernel,
        out_shape=jax.ShapeDtypeStruct((b, n * lt, LANES), F32),
        grid=(b, n // nt, n_exp),
        in_specs=[pl.BlockSpec((1, 1, cap), lambda bi, ki, ei: (bi * n_exp + ei, 0, 0), memory_space=pltpu.SMEM),
                  pl.BlockSpec((1, 1, n // nt + 1), lambda bi, ki, ei: (bi * n_exp + ei, 0, 0),
                               memory_space=pltpu.SMEM),
                  pl.BlockSpec((1, 1, cap, d), lambda bi, ki, ei: (bi, ei, 0, 0))],
        out_specs=pl.BlockSpec((1, nt * lt, LANES), lambda bi, ki, ei: (bi, ki, 0)),
        scratch_shapes=[pltpu.VMEM((cap * lt, LANES), F32)],
        compiler_params=_cparams(("parallel", "parallel", "arbitrary")),
        name="moe_scatter",
    )(dst3, bounds3, y)


def _post_kernel(x_ref, f_ref, g_ref, lng_ref, lnb_ref, o_ref):
    f = _from_tiled(f_ref[0], x_ref.shape[2])
    o_ref[0] = _ln(DEEPNORM_ALPHA * x_ref[0] + (1.0 + g_ref[0]) * f, lng_ref[...], lnb_ref[...])


def _post(x1, f_tiled, g, lng, lnb):
    b, n, d = x1.shape
    tm = _tile(n, TOKEN_TILE)
    blk = pl.BlockSpec((1, tm, d), lambda bi, ti: (bi, ti, 0))
    f_blk = pl.BlockSpec((1, tm * d // LANES, LANES), lambda bi, ti: (bi, ti, 0))
    full = lambda a: pl.BlockSpec(a.shape, lambda bi, ti: (0,) * a.ndim)
    return pl.pallas_call(
        _post_kernel,
        out_shape=jax.ShapeDtypeStruct((b, n, d), F32),
        grid=(b, n // tm),
        in_specs=[blk, f_blk, pl.BlockSpec((1, 1, d), lambda bi, ti: (bi, 0, 0)), full(lng), full(lnb)],
        out_specs=blk,
        compiler_params=_cparams(("parallel", "parallel")),
        name="ffn_post",
    )(x1, f_tiled, g, lng, lnb)


def _select_kernel(aff_ref, idx_ref, gate_ref, *, cap, n_exp):
    a = aff_ref[0]
    er, lanes = a.shape
    r_n = er // n_exp
    e_pad = 16
    bits = pltpu.bitcast(a, jnp.int32)

    member = (lax.broadcasted_iota(jnp.int32, (e_pad, er), 1) // r_n
              == lax.broadcasted_iota(jnp.int32, (e_pad, er), 0))
    gs = jnp.where(member, 1.0, 0.0).astype(BF16)
    member_t = (lax.broadcasted_iota(jnp.int32, (er, e_pad), 0) // r_n
                == lax.broadcasted_iota(jnp.int32, (er, e_pad), 1))
    gst = jnp.where(member_t, 1.0, 0.0).astype(BF16)
    ri = lax.broadcasted_iota(jnp.int32, (er, er), 0)
    rj = lax.broadcasted_iota(jnp.int32, (er, er), 1)
    rows_before = jnp.where((ri // r_n == rj // r_n) & (rj < ri), 1.0, 0.0).astype(BF16)
    li = lax.broadcasted_iota(jnp.int32, (lanes, lanes), 0)
    lj = lax.broadcasted_iota(jnp.int32, (lanes, lanes), 1)
    lanes_upto = jnp.where(li <= lj, 1.0, 0.0).astype(BF16)

    def expert_total(m):
        per_lane = _dot(gs, m.astype(BF16))
        return jnp.broadcast_to(jnp.sum(per_lane, axis=1, keepdims=True), per_lane.shape)

    def to_rows(ev):
        if r_n % SUBLANES == 0 and n_exp == e_pad:
            return jnp.broadcast_to(ev[:, None, :], (e_pad, r_n, lanes)).reshape(er, lanes)
        hi = jnp.floor(ev * (1.0 / 128.0))
        return 128.0 * _dot(gst, hi.astype(BF16)) + _dot(gst, (ev - 128.0 * hi).astype(BF16))

    def search(i, t):
        cand = t | lax.shift_left(jnp.int32(1), 30 - i)
        cnt = expert_total(jnp.where(bits >= cand, 1.0, 0.0))
        return jnp.where(to_rows(jnp.where(cnt >= cap, 1.0, 0.0)) > 0.5, cand, t)

    t = lax.fori_loop(0, 31, search, jnp.zeros((er, lanes), jnp.int32))
    gt = jnp.where(bits > t, 1.0, 0.0)
    eq = jnp.where(bits == t, 1.0, 0.0)

    def prefix(m):
        rc = _dot(m.astype(BF16), lanes_upto)
        off = _dot(rows_before, jnp.broadcast_to(rc[:, lanes - 1:lanes], m.shape).astype(BF16))
        return rc, off

    rc_eq, off_eq = prefix(eq)
    need = cap - to_rows(expert_total(gt))
    sel = jnp.maximum(gt, eq * jnp.where(off_eq + rc_eq - eq < need, 1.0, 0.0))
    rc, off = prefix(sel)
    row_end = off + jnp.broadcast_to(rc[:, lanes - 1:lanes], rc.shape)

    slot = lax.broadcasted_iota(jnp.int32, (cap, lanes), 0).astype(F32) + 1.0
    lane_c = lax.broadcasted_iota(jnp.int32, (cap, lanes), 1)
    lane_cf = lane_c.astype(F32)
    diag = lax.broadcasted_iota(jnp.int32, (r_n, lanes), 0) == lax.broadcasted_iota(jnp.int32, (r_n, lanes), 1)
    lane_r = lax.broadcasted_iota(jnp.int32, (1, lanes), 1)
    pad = jnp.zeros((lanes - r_n, lanes), F32)

    def table(m):
        return (jnp.concatenate([m, pad], axis=0) if r_n < lanes else m).astype(BF16)

    all_ones = jnp.ones((lanes, lanes), BF16)
    idx_all = jnp.zeros((cap, lanes), jnp.int32)
    gate_all = jnp.zeros((cap, lanes), F32)
    for e in range(n_exp):
        r0 = e * r_n
        end_lane = jnp.sum(jnp.where(diag, row_end[r0:r0 + r_n], 0.0), axis=0, keepdims=True)
        end_lane = jnp.where(lane_r < r_n, end_lane, 3.0e38)
        row_j = jnp.sum(jnp.where(end_lane < slot, 1.0, 0.0), axis=1, keepdims=True)
        oh = jnp.where(lane_cf == row_j, 1.0, 0.0).astype(BF16)
        cum_e = off[r0:r0 + r_n] + rc[r0:r0 + r_n]
        cum_hi = jnp.floor(cum_e * (1.0 / 128.0))
        cum_2 = _dot(oh, jnp.concatenate([table(cum_hi), table(cum_e - 128.0 * cum_hi)], axis=1))
        cum_j = 128.0 * cum_2[:, :lanes] + cum_2[:, lanes:]
        lane_j = _dot(jnp.where(cum_j < slot, 1.0, 0.0).astype(BF16), all_ones)
        a_e = a[r0:r0 + r_n]
        a1 = a_e.astype(BF16).astype(F32)
        a2 = (a_e - a1).astype(BF16).astype(F32)
        a3 = a_e - a1 - a2
        aff_2 = _dot(oh, jnp.concatenate([table(a1), table(a2)], axis=1))
        aff_j = aff_2[:, :lanes] + aff_2[:, lanes:] + _dot(oh, table(a3))
        gate_j = jnp.sum(jnp.where(lane_cf == lane_j, aff_j, 0.0), axis=1, keepdims=True)
        idx_j = (row_j * float(lanes) + lane_j).astype(jnp.int32)
        idx_all = jnp.where(lane_c == e, idx_j, idx_all)
        gate_all = jnp.where(lane_c == e, gate_j, gate_all)
    idx_ref[0] = idx_all
    gate_ref[0] = gate_all


def _moe_select(aff_t, cap):
    b, n_exp, n = aff_t.shape
    lanes = HEAD_SLOT
    er = n_exp * (n // lanes)
    idx, gates = pl.pallas_call(
        functools.partial(_select_kernel, cap=cap, n_exp=n_exp),
        out_shape=(jax.ShapeDtypeStruct((b, cap, lanes), jnp.int32), jax.ShapeDtypeStruct((b, cap, lanes), F32)),
        grid=(b,),
        in_specs=[pl.BlockSpec((1, er, lanes), lambda bi: (bi, 0, 0))],
        out_specs=(pl.BlockSpec((1, cap, lanes), lambda bi: (bi, 0, 0)),
                   pl.BlockSpec((1, cap, lanes), lambda bi: (bi, 0, 0))),
        compiler_params=_cparams(("parallel",)),
        name="moe_select",
    )(aff_t.reshape(b, er, lanes))
    return gates, jnp.swapaxes(idx[:, :, :n_exp], 1, 2)


def _moe(x1, aff_t, hp, gf, wg, wu, wd, lng, lnb):
    b, n, d = x1.shape
    n_exp = aff_t.shape[1]
    cap = CAPACITY_FACTOR * n // n_exp
    gates, idx = _moe_select(aff_t, cap)
    nt = _tile(n, TOKEN_SPLIT)
    edges = jnp.arange(0, n + 1, nt, dtype=jnp.int32)
    bounds3 = jnp.sum(idx[..., None] < edges, axis=2, dtype=jnp.int32).reshape(b * n_exp, 1, n // nt + 1)
    src3 = _tiled_row(idx, d // 2).reshape(b * n_exp, 1, cap)
    dst3 = _tiled_row(idx % nt, d).reshape(b * n_exp, 1, cap)
    y = _moe_ffn(hp, src3, gates, wg, wu, wd, cap)
    f_tiled = _moe_scatter(y, dst3, bounds3, n)
    return _post(x1, f_tiled, gf, lng, lnb)


def _rope_angles(n_tokens, dim):
    n_rows = n_tokens // GRID_W
    row = jnp.repeat(jnp.arange(n_rows, dtype=F32), GRID_W)
    col = jnp.tile(jnp.arange(GRID_W, dtype=F32), n_rows)
    n_freq = dim // 4
    inv_freq = ROPE_BASE ** (-jnp.arange(n_freq, dtype=F32) / n_freq)
    ang = jnp.concatenate([row[:, None] * inv_freq, col[:, None] * inv_freq], -1)
    return jnp.cos(ang), jnp.sin(ang)


def _rope_tables(n_tokens, dim, lane_offset, identity):
    half = dim // 2
    if identity:
        cos = jnp.ones((n_tokens, half), F32)
        sin = jnp.zeros((n_tokens, half), F32)
    else:
        cos, sin = _rope_angles(n_tokens, dim)
    c2 = jnp.concatenate([cos, cos], -1)
    s2 = jnp.concatenate([-sin, sin], -1)
    reps = (HEAD_SLOT - lane_offset) // dim if lane_offset == 0 else 1
    ck = jnp.zeros((n_tokens, HEAD_SLOT), F32)
    sk = jnp.zeros((n_tokens, HEAD_SLOT), F32)
    for r in range(reps):
        ck = ck.at[:, lane_offset + r * dim:lane_offset + (r + 1) * dim].set(c2)
        sk = sk.at[:, lane_offset + r * dim:lane_offset + (r + 1) * dim].set(s2)
    return ck, sk, c2.T, s2.T


def _deinterleave(w, axis):
    return jnp.concatenate([jnp.take(w, jnp.arange(0, w.shape[axis], 2), axis=axis),
                            jnp.take(w, jnp.arange(1, w.shape[axis], 2), axis=axis)], axis=axis)


def _prep_layer0(w_in, ln_g, ln_b, ws, bs, q_norm, w_uq, kv_norm, w_ukv):
    d = w_in.shape[0]
    w = GMLP_WIDTH
    o_kr = 2 * w + MLA_Q_RANK + MLA_KV_RANK
    kr = _deinterleave(w_in[:, o_kr:o_kr + MLA_ROPE], 1)
    kr_sw = jnp.concatenate([kr[:, MLA_ROPE // 2:], kr[:, :MLA_ROPE // 2]], 1)
    slot = lambda m: jnp.zeros((d, HEAD_SLOT), F32).at[:, MLA_NOPE:MLA_NOPE + MLA_ROPE].set(m)
    win = jnp.concatenate([w_in[:, :o_kr], slot(kr), slot(kr_sw)], 1).astype(BF16)

    qd = MLA_NOPE + MLA_ROPE
    wq = w_uq.reshape(MLA_Q_RANK, MLA_HEADS, qd)
    wq = jnp.concatenate([wq[..., :MLA_NOPE], _deinterleave(wq[..., MLA_NOPE:], 2),
                          jnp.zeros((MLA_Q_RANK, MLA_HEADS, HEAD_SLOT - qd), F32)], -1)
    wq_t = wq.reshape(MLA_Q_RANK, MLA_HEADS * HEAD_SLOT).T.astype(BF16)

    wkv = w_ukv.reshape(MLA_KV_RANK, MLA_HEADS, MLA_NOPE + MLA_V)
    wk = jnp.concatenate([wkv[..., :MLA_NOPE], jnp.zeros((MLA_KV_RANK, MLA_HEADS, HEAD_SLOT - MLA_NOPE), F32)], -1)
    wk = wk.reshape(MLA_KV_RANK, MLA_HEADS * HEAD_SLOT).astype(BF16)
    wv_t = wkv[..., MLA_NOPE:].reshape(MLA_KV_RANK, MLA_HEADS * MLA_V).T.astype(BF16)

    bsb = jnp.broadcast_to(bs[:, :, None], (GMLP_GROUPS, CHUNK, GMLP_GROUP_CH))
    return (win, ln_g.reshape(1, -1), ln_b.reshape(1, -1), ws.astype(BF16), bsb,
            q_norm.reshape(1, -1), wq_t, kv_norm.reshape(1, -1), wk, wv_t)


def _prep_layer1(w_in):
    d = w_in.shape[0]
    dw = 2 * DIFF_HEADS * DIFF_HEAD_DIM
    perm = lambda m: _deinterleave(m.reshape(d, 2 * DIFF_HEADS, DIFF_HEAD_DIM), 2).reshape(d, dw)
    wq_t = perm(w_in[:, :dw]).T.astype(BF16)
    wk = perm(w_in[:, dw:2 * dw]).astype(BF16)
    wv_t = w_in[:, 2 * dw:].T.astype(BF16)
    return wq_t, wk, wv_t


def _split6(m_row):
    return [v[:, None, :] for v in jnp.split(m_row, 6, axis=-1)]


def kernel(x, c, ctx, c_ctx, w_mod_0, b_mod_0, w_in_0, gmlp_ln_g_0, gmlp_ln_b_0, gmlp_ws_0, gmlp_bs_0, mla_q_norm_0, mla_w_uq_0, mla_kv_norm_0, mla_w_ukv_0, w_out_0, ln_mix_g_0, ln_mix_b_0, router_0, w_gate_0, w_up_0, w_down_0, ln_ffn_g_0, ln_ffn_b_0, w_mod_1, b_mod_1, w_in_1, lambda_q1_1, lambda_k1_1, lambda_q2_1, lambda_k2_1, subln_g_1, w_out_1, ln_mix_g_1, ln_mix_b_1, router_1, w_gate_1, w_up_1, w_down_1, ln_ffn_g_1, ln_ffn_b_1):
    b, n, d = x.shape
    m_ctx = ctx.shape[1]
    row = lambda v: v.reshape(1, -1)

    cond = jnp.concatenate([c, c_ctx[None, :], jnp.zeros((-(b + 1) % 8, d), F32)], 0)
    mod0 = _modulation(cond, w_mod_0, b_mod_0)
    mod1 = _modulation(cond, w_mod_1, b_mod_1)
    sh_a0, sc_a0, g_a0, sh_f0, sc_f0, g_f0 = _split6(mod0[:b])
    csh_a0, csc_a0, cg_a0, csh_f0, csc_f0, cg_f0 = [jnp.broadcast_to(v, (b, 1, d)) for v in _split6(mod0[b:b + 1])]
    sh_a1, sc_a1, g_a1, sh_f1, sc_f1, g_f1 = _split6(mod1[:b])
    csh_a1, csc_a1 = [jnp.broadcast_to(v, (b, 1, d)) for v in _split6(mod1[b:b + 1])[:2]]

    wts0 = _prep_layer0(w_in_0, gmlp_ln_g_0, gmlp_ln_b_0, gmlp_ws_0, gmlp_bs_0,
                        mla_q_norm_0, mla_w_uq_0, mla_kv_norm_0, mla_w_ukv_0)
    tabs_l = _rope_tables(n, MLA_ROPE, MLA_NOPE, identity=False)
    tabs_c = _rope_tables(m_ctx, MLA_ROPE, MLA_NOPE, identity=True)
    a_l, q_l, k_l, v_l = _proj0(x, sc_a0, sh_a0, wts0, tabs_l)
    a_c, q_c, k_c, v_c = _proj0(ctx, csc_a0, csh_a0, wts0, tabs_c)
    o_l = _attention(q_l, [(k_l, v_l), (k_c, v_c)], dv=MLA_V, n_sub=1, finalize=_finalize_plain, name="mla_attn")
    o_c = _attention(q_c, [(k_c, v_c)], dv=MLA_V, n_sub=1, finalize=_finalize_plain, name="mla_attn_ctx")

    w_out_a = w_out_0[:GMLP_WIDTH].astype(BF16)
    w_out_o = w_out_0[GMLP_WIDTH:].astype(BF16)
    router_t0 = router_0.T.astype(BF16)
    wg0, wu0, wd0 = w_gate_0.astype(BF16), w_up_0.astype(BF16), w_down_0.astype(BF16)
    lng, lnb = row(ln_mix_g_0), row(ln_mix_b_0)
    x1, aff, hp = _mixer_out(x, g_a0, lng, lnb, sc_f0, sh_f0, router_t0, [(a_l, w_out_a)], [(o_l, w_out_o)])
    x_lat = _moe(x1, aff, hp, g_f0, wg0, wu0, wd0, row(ln_ffn_g_0), row(ln_ffn_b_0))
    c1, caff, chp = _mixer_out(ctx, cg_a0, lng, lnb, csc_f0, csh_f0, router_t0, [(a_c, w_out_a)], [(o_c, w_out_o)])
    x_ctx = _moe(c1, caff, chp, cg_f0, wg0, wu0, wd0, row(ln_ffn_g_0), row(ln_ffn_b_0))

    lam_init = 0.8 - 0.6 * math.exp(-0.3 * 1)
    wts1 = _prep_layer1(w_in_1)
    tabs_l = _rope_tables(n, DIFF_HEAD_DIM, 0, identity=False)
    tabs_c = _rope_tables(m_ctx, DIFF_HEAD_DIM, 0, identity=True)
    q_l, k_l, v_l = _proj1(x_lat, sc_a1, sh_a1, wts1, tabs_l)
    _, k_c, v_c = _proj1(x_ctx, csc_a1, csh_a1, wts1, tabs_c)
    lams = jnp.stack([lambda_q1_1, lambda_k1_1, lambda_q2_1, lambda_k2_1])
    lam_pack = jnp.zeros((KMAX_ROWS, LANES), F32).at[:4, :lams.shape[1]].set(lams)
    extra = (lam_pack, subln_g_1.reshape(-1, 1))
    o_l = _attention(q_l, [(k_l, v_l), (k_c, v_c)], extra, dv=2 * DIFF_HEAD_DIM, n_sub=2,
                     finalize=functools.partial(_finalize_diff, lam_init=lam_init), name="diff_attn")
    x1, aff, hp = _mixer_out(x_lat, g_a1, row(ln_mix_g_1), row(ln_mix_b_1), sc_f1, sh_f1, router_1.T.astype(BF16),
                             [], [(o_l, w_out_1.astype(BF16))])
    return _moe(x1, aff, hp, g_f1, w_gate_1.astype(BF16), w_up_1.astype(BF16), w_down_1.astype(BF16),
                row(ln_ffn_g_1), row(ln_ffn_b_1))
```

```python
import functools
import math

import jax
import jax.numpy as jnp
from jax import lax
from jax.experimental import pallas as pl
from jax.experimental.pallas import tpu as pltpu

F32 = jnp.float32
BF16 = jnp.bfloat16

DEPTH = 2
GRID_W = 64
ROPE_BASE = 10000.0
CHUNK = 128
GMLP_GROUPS = 4
GMLP_GROUP_CH = 128
GMLP_WIDTH = GMLP_GROUPS * GMLP_GROUP_CH
MLA_HEADS = 8
MLA_NOPE = 64
MLA_ROPE = 32
MLA_V = 64
MLA_Q_RANK = 256
MLA_KV_RANK = 128
DIFF_HEADS = 8
DIFF_HEAD_DIM = 64
CAPACITY_FACTOR = 2
DEEPNORM_ALPHA = (2 * DEPTH) ** 0.25
LN_EPS = 1e-5
RMS_EPS = 1e-6
LOG2E = 1.4426950408889634

LANES = 128
SUBLANES = 8
HEAD_SLOT = LANES
NEG_BIG = -1e30
VMEM_LIMIT = 56 * 1024 * 1024
TOKEN_TILE = 512
MOD_TILE = 1024


def _cparams(sem):
    return pltpu.CompilerParams(dimension_semantics=sem, vmem_limit_bytes=VMEM_LIMIT)


def _tile(n, pref):
    return pref if n % pref == 0 else n


def _ln(x, g, b):
    mu = jnp.mean(x, axis=-1, keepdims=True)
    xc = x - mu
    var = jnp.mean(xc * xc, axis=-1, keepdims=True)
    return xc * lax.rsqrt(var + LN_EPS) * g + b


def _rms(x, g):
    return x * lax.rsqrt(jnp.mean(x * x, axis=-1, keepdims=True) + RMS_EPS) * g


def _gelu(x):
    return 0.5 * x * (1.0 + lax.erf(x * (2.0 ** -0.5)))


def _to_tiled(v):
    m, w = v.shape
    nt = w // LANES
    tiles = [v[:, c * LANES:(c + 1) * LANES].reshape(m // SUBLANES, SUBLANES, LANES) for c in range(nt)]
    return jnp.stack(tiles, axis=1).reshape(m * nt, LANES)


def _from_tiled(t, w):
    nt = w // LANES
    m = t.shape[0] // nt
    t4 = t.reshape(m // SUBLANES, nt, SUBLANES, LANES)
    return jnp.concatenate([t4[:, c].reshape(m, LANES) for c in range(nt)], axis=1)


def _tiled_row(i, w):
    return (i // SUBLANES) * (SUBLANES * (w // LANES)) + i % SUBLANES


def _pack_halves(v):
    half = v.shape[1] // 2
    lo = pltpu.bitcast(v[:, :half].astype(F32), jnp.uint32)
    hi = pltpu.bitcast(v[:, half:].astype(F32), jnp.uint32)
    return lax.shift_right_logical(lo, jnp.uint32(16)) | (hi & jnp.uint32(0xFFFF0000))


def _unpack_halves(w):
    return (pltpu.bitcast(lax.shift_left(w, jnp.uint32(16)), F32),
            pltpu.bitcast(w & jnp.uint32(0xFFFF0000), F32))


def _dot(a, b):
    return jnp.dot(a, b, preferred_element_type=F32)


def _dot_nt(a, b):
    return lax.dot_general(a, b, (((1,), (1,)), ((), ())), preferred_element_type=F32)


def _dot_tn(a, b):
    return lax.dot_general(a, b, (((0,), (0,)), ((), ())), preferred_element_type=F32)


def _mod_kernel(c_ref, w_ref, b_ref, o_ref):
    c = c_ref[...]
    s = c / (1.0 + jnp.exp(-c))
    o_ref[...] = _dot(s.astype(BF16), w_ref[...].astype(BF16)) + b_ref[...]


def _modulation(cond, w_mod, b_mod):
    r, d = cond.shape
    n = w_mod.shape[1]
    tn = _tile(n, MOD_TILE)
    return pl.pallas_call(
        _mod_kernel,
        out_shape=jax.ShapeDtypeStruct((r, n), F32),
        grid=(n // tn,),
        in_specs=[pl.BlockSpec((r, d), lambda j: (0, 0)),
                  pl.BlockSpec((d, tn), lambda j: (0, j)),
                  pl.BlockSpec((1, tn), lambda j: (0, j))],
        out_specs=pl.BlockSpec((r, tn), lambda j: (0, j)),
        compiler_params=_cparams(("parallel",)),
        name="modulation",
    )(cond, w_mod, b_mod.reshape(1, n))


def _proj0_kernel(x_ref, sc_ref, sh_ref, win_ref, lng_ref, lnb_ref, ws_ref, bs_ref,
                  qn_ref, wq_ref, kvn_ref, wk_ref, wv_ref, ck_ref, sk_ref, cq_ref, sq_ref,
                  a_ref, q_ref, k_ref, v_ref, *, q_scale):
    tm = x_ref.shape[1]
    h = (x_ref[0] * (1.0 + sc_ref[0]) + sh_ref[0]).astype(BF16)
    y = _dot(h, win_ref[...])
    w = GMLP_WIDTH
    u = _gelu(y[:, :w])
    vn = _ln(_gelu(y[:, w:2 * w]), lng_ref[...], lnb_ref[...]).astype(BF16)
    for ci in range(tm // CHUNK):
        r0 = ci * CHUNK
        for g in range(GMLP_GROUPS):
            c0 = g * GMLP_GROUP_CH
            mixed = _dot(ws_ref[g], vn[r0:r0 + CHUNK, c0:c0 + GMLP_GROUP_CH]) + bs_ref[g]
            a_ref[0, r0:r0 + CHUNK, c0:c0 + GMLP_GROUP_CH] = (
                u[r0:r0 + CHUNK, c0:c0 + GMLP_GROUP_CH] * mixed).astype(BF16)

    o = 2 * w
    cq = _rms(y[:, o:o + MLA_Q_RANK], qn_ref[...]).astype(BF16)
    qt = _dot_nt(wq_ref[...], cq) * q_scale
    q_ref[0] = qt.astype(BF16)
    hr = MLA_ROPE // 2
    cq_t = cq_ref[...]
    sq_t = sq_ref[...]
    for hd in range(MLA_HEADS):
        r0 = hd * HEAD_SLOT + MLA_NOPE
        x1 = qt[r0:r0 + hr]
        x2 = qt[r0 + hr:r0 + 2 * hr]
        blk = jnp.concatenate([x1, x2], axis=0)
        swp = jnp.concatenate([x2, x1], axis=0)
        q_ref[0, r0:r0 + 2 * hr, :] = (blk * cq_t + swp * sq_t).astype(BF16)

    o += MLA_Q_RANK
    ckv = _rms(y[:, o:o + MLA_KV_RANK], kvn_ref[...]).astype(BF16)
    o += MLA_KV_RANK
    kr = y[:, o:o + HEAD_SLOT] * ck_ref[...] + y[:, o + HEAD_SLOT:o + 2 * HEAD_SLOT] * sk_ref[...]
    k_ref[0] = (_dot(ckv, wk_ref[...]) + jnp.tile(kr, (1, MLA_HEADS))).astype(BF16)
    v_ref[0] = _dot_nt(wv_ref[...], ckv).astype(BF16)


def _proj0(x, sc, sh, wts, tabs):
    b, n, d = x.shape
    tm = _tile(n, TOKEN_TILE)
    ck, sk, cq, sq = tabs
    win, lng, lnb, ws, bsb, qn, wq, kvn, wk, wv = wts
    hw = MLA_HEADS * HEAD_SLOT
    vw = MLA_HEADS * MLA_V
    full = lambda a: pl.BlockSpec(a.shape, lambda bi, ti: (0,) * a.ndim)
    q_scale = (MLA_NOPE + MLA_ROPE) ** -0.5 * LOG2E
    return pl.pallas_call(
        functools.partial(_proj0_kernel, q_scale=q_scale),
        out_shape=(jax.ShapeDtypeStruct((b, n, GMLP_WIDTH), BF16),
                   jax.ShapeDtypeStruct((b, hw, n), BF16),
                   jax.ShapeDtypeStruct((b, n, hw), BF16),
                   jax.ShapeDtypeStruct((b, vw, n), BF16)),
        grid=(b, n // tm),
        in_specs=[pl.BlockSpec((1, tm, d), lambda bi, ti: (bi, ti, 0)),
                  pl.BlockSpec((1, 1, d), lambda bi, ti: (bi, 0, 0)),
                  pl.BlockSpec((1, 1, d), lambda bi, ti: (bi, 0, 0)),
                  full(win), full(lng), full(lnb), full(ws), full(bsb),
                  full(qn), full(wq), full(kvn), full(wk), full(wv),
                  pl.BlockSpec((tm, HEAD_SLOT), lambda bi, ti: (ti, 0)),
                  pl.BlockSpec((tm, HEAD_SLOT), lambda bi, ti: (ti, 0)),
                  pl.BlockSpec((MLA_ROPE, tm), lambda bi, ti: (0, ti)),
                  pl.BlockSpec((MLA_ROPE, tm), lambda bi, ti: (0, ti))],
        out_specs=(pl.BlockSpec((1, tm, GMLP_WIDTH), lambda bi, ti: (bi, ti, 0)),
                   pl.BlockSpec((1, hw, tm), lambda bi, ti: (bi, 0, ti)),
                   pl.BlockSpec((1, tm, hw), lambda bi, ti: (bi, ti, 0)),
                   pl.BlockSpec((1, vw, tm), lambda bi, ti: (bi, 0, ti))),
        compiler_params=_cparams(("parallel", "parallel")),
        name="proj0",
    )(x, sc, sh, win, lng, lnb, ws, bsb, qn, wq, kvn, wk, wv, ck, sk, cq, sq)


def _proj1_kernel(x_ref, sc_ref, sh_ref, wq_ref, wk_ref, wv_ref, ck_ref, sk_ref, cq_ref, sq_ref,
                  q_ref, k_ref, v_ref, *, q_scale):
    h = (x_ref[0] * (1.0 + sc_ref[0]) + sh_ref[0]).astype(BF16)
    n_sub = 2 * DIFF_HEADS
    hd = DIFF_HEAD_DIM
    hr = hd // 2
    qt = _dot_nt(wq_ref[...], h) * q_scale
    cq_t = cq_ref[...]
    sq_t = sq_ref[...]
    for s in range(n_sub):
        r0 = s * hd
        x1 = qt[r0:r0 + hr]
        x2 = qt[r0 + hr:r0 + hd]
        swp = jnp.concatenate([x2, x1], axis=0)
        q_ref[0, r0:r0 + hd, :] = (qt[r0:r0 + hd] * cq_t + swp * sq_t).astype(BF16)
    k = _dot(h, wk_ref[...])
    width = k.shape[1]
    lane = lax.broadcasted_iota(jnp.int32, k.shape, 1)
    partner = jnp.where((lane % hd) < hr, pltpu.roll(k, width - hr, 1), pltpu.roll(k, hr, 1))
    reps = width // HEAD_SLOT
    k_ref[0] = (k * jnp.tile(ck_ref[...], (1, reps)) + partner * jnp.tile(sk_ref[...], (1, reps))).astype(BF16)
    v_ref[0] = _dot_nt(wv_ref[...], h).astype(BF16)


def _proj1(x, sc, sh, wts, tabs):
    b, n, d = x.shape
    tm = _tile(n, TOKEN_TILE)
    wq, wk, wv = wts
    ck, sk, cq, sq = tabs
    dw = wq.shape[0]
    full = lambda a: pl.BlockSpec(a.shape, lambda bi, ti: (0,) * a.ndim)
    q_scale = DIFF_HEAD_DIM ** -0.5 * LOG2E
    return pl.pallas_call(
        functools.partial(_proj1_kernel, q_scale=q_scale),
        out_shape=(jax.ShapeDtypeStruct((b, dw, n), BF16),
                   jax.ShapeDtypeStruct((b, n, dw), BF16),
                   jax.ShapeDtypeStruct((b, dw, n), BF16)),
        grid=(b, n // tm),
        in_specs=[pl.BlockSpec((1, tm, d), lambda bi, ti: (bi, ti, 0)),
                  pl.BlockSpec((1, 1, d), lambda bi, ti: (bi, 0, 0)),
                  pl.BlockSpec((1, 1, d), lambda bi, ti: (bi, 0, 0)),
                  full(wq), full(wk), full(wv),
                  pl.BlockSpec((tm, HEAD_SLOT), lambda bi, ti: (ti, 0)),
                  pl.BlockSpec((tm, HEAD_SLOT), lambda bi, ti: (ti, 0)),
                  pl.BlockSpec((DIFF_HEAD_DIM, tm), lambda bi, ti: (0, ti)),
                  pl.BlockSpec((DIFF_HEAD_DIM, tm), lambda bi, ti: (0, ti))],
        out_specs=(pl.BlockSpec((1, dw, tm), lambda bi, ti: (bi, 0, ti)),
                   pl.BlockSpec((1, tm, dw), lambda bi, ti: (bi, ti, 0)),
                   pl.BlockSpec((1, dw, tm), lambda bi, ti: (bi, 0, ti))),
        compiler_params=_cparams(("parallel", "parallel")),
        name="proj1",
    )(x, sc, sh, wq, wk, wv, ck, sk, cq, sq)


SHIFT_OK_LO = 2.0 ** -85
SHIFT_OK_HI = 2.0 ** 100
ATTN_TQ = 1024
ATTN_TK = 256
KMAX_ROWS = 32
ATTN_UNROLL = 16


def _key_norm_max(k_ref, kmax_ref, *, tk, n_sub):
    nchunk = k_ref.shape[1] // tk
    sub_w = HEAD_SLOT // n_sub
    row = lax.broadcasted_iota(jnp.int32, (SUBLANES, HEAD_SLOT), 0)
    lane = lax.broadcasted_iota(jnp.int32, (SUBLANES, HEAD_SLOT), 1)
    sel = jnp.where(lane // sub_w == row, 1.0, 0.0).astype(BF16)
    group = max(g for g in range(1, 12) if nchunk % g == 0)
    rows = group * tk

    def body(c, mx):
        off = pl.multiple_of(c * rows, rows)
        kc = k_ref[0, pl.ds(off, rows), :].astype(F32)
        return jnp.maximum(mx, _dot_nt(sel, (kc * kc).astype(BF16)))

    mx = lax.fori_loop(0, nchunk // group, body, jnp.zeros((SUBLANES, rows), F32))
    tile8 = jnp.broadcast_to(jnp.max(mx, axis=1, keepdims=True), (SUBLANES, HEAD_SLOT))
    kmax_ref[...] = jnp.tile(tile8, (kmax_ref.shape[0] // SUBLANES, 1))


def _sweep_bounded(qs, shifts, k_ref, v_ref, s_ref, *, tk, unroll):
    tq = qs[0].shape[1]
    dv = v_ref.shape[1]
    nchunk = k_ref.shape[1] // tk

    def aligned(off):
        return off if isinstance(off, int) else pl.multiple_of(off, tk)

    def scores(off, buf):
        kc = k_ref[0, pl.ds(aligned(off), tk), :]
        for r, q in enumerate(qs):
            s_ref[buf, r] = _dot(kc, q)

    def consume(off, buf, carry):
        vc = v_ref[0, :, pl.ds(aligned(off), tk)]
        nxt = []
        for r, ((l8, acc), m) in enumerate(zip(carry, shifts)):
            p = jnp.exp2(s_ref[buf, r] - m)
            l8 = l8 + jnp.sum(p.reshape(tk // SUBLANES, SUBLANES, tq), axis=0)
            nxt.append((l8, acc + _dot(vc, p.astype(BF16))))
        return tuple(nxt)

    def body(i, carry):
        for u in range(unroll):
            off = pl.multiple_of((i * unroll + u) * tk, tk)
            scores(off + tk, (u + 1) % 2)
            carry = consume(off, u % 2, carry)
        return carry

    carry = tuple((jnp.zeros((SUBLANES, tq), F32), jnp.zeros((dv, tq), F32)) for _ in qs)
    scores(0, 0)
    n_loop = (nchunk - 1) // unroll * unroll
    if n_loop:
        carry = lax.fori_loop(0, n_loop // unroll, body, carry)
    for c in range(n_loop, nchunk):
        if c + 1 < nchunk:
            scores((c + 1) * tk, (c + 1) % 2)
        carry = consume(c * tk, c % 2, carry)
    return [(jnp.sum(l8, axis=0, keepdims=True), acc) for l8, acc in carry]


def _sweep_running_max(qs, k_ref, v_ref, *, tk):
    tq = qs[0].shape[1]
    dv = v_ref.shape[1]

    def body(c, carry):
        off = pl.multiple_of(c * tk, tk)
        kc = k_ref[0, pl.ds(off, tk), :]
        vc = v_ref[0, :, pl.ds(off, tk)]
        nxt = []
        for (m, l, acc), q in zip(carry, qs):
            s = _dot(kc, q)
            mn = jnp.maximum(m, jnp.max(s, axis=0, keepdims=True))
            alpha = jnp.exp2(m - mn)
            p = jnp.exp2(s - mn)
            nxt.append((mn, alpha * l + jnp.sum(p, axis=0, keepdims=True), alpha * acc + _dot(vc, p.astype(BF16))))
        return tuple(nxt)

    init = tuple((jnp.full((1, tq), NEG_BIG, F32), jnp.zeros((1, tq), F32), jnp.zeros((dv, tq), F32)) for _ in qs)
    res = lax.fori_loop(0, k_ref.shape[1] // tk, body, init)
    return [(l, acc) for _, l, acc in res]


def _attn_kernel(*refs, tk, unroll, n_sub, n_src, finalize):
    q_ref = refs[0]
    srcs = [(refs[1 + 2 * i], refs[2 + 2 * i]) for i in range(n_src)]
    n_scratch = 3 if n_src == 1 else 5
    extra = refs[1 + 2 * n_src:-n_scratch]
    o_ref, kmax_ref, s_ref = refs[-n_scratch:][:3]
    k_ref, v_ref = srcs[0] if n_src == 1 else refs[-2:]

    @pl.when(pl.program_id(2) == 0)
    def _():
        if n_src > 1:
            off = 0
            for ks_ref, vs_ref in srcs:
                t = ks_ref.shape[1]
                k_ref[0, off:off + t] = ks_ref[0]
                v_ref[0, :, off:off + t] = vs_ref[0]
                off += t
        _key_norm_max(k_ref, kmax_ref, tk=tk, n_sub=n_sub)

    q = q_ref[0]
    sub_w = HEAD_SLOT // n_sub
    row = lax.broadcasted_iota(jnp.int32, q.shape, 0)
    qf = q.astype(F32)
    qs, shifts = [], []
    for r in range(n_sub):
        mine = (row // sub_w) == r
        qs.append(jnp.where(mine, q, jnp.zeros_like(q)) if n_sub > 1 else q)
        qn2 = jnp.sum(jnp.where(mine, qf * qf, 0.0), axis=0, keepdims=True)
        shifts.append(jnp.sqrt(qn2 * kmax_ref[r:r + 1, 0:1]))

    res = _sweep_bounded(qs, shifts, k_ref, v_ref, s_ref, tk=tk, unroll=unroll)
    o_ref[0] = finalize(res, extra).astype(BF16)
    ok = None
    for l, _ in res:
        ok_r = (jnp.min(l) >= SHIFT_OK_LO) & (jnp.max(l) <= SHIFT_OK_HI)
        ok = ok_r if ok is None else ok & ok_r

    @pl.when(jnp.logical_not(ok))
    def _():
        o_ref[0] = finalize(_sweep_running_max(qs, k_ref, v_ref, tk=tk), extra).astype(BF16)


def _finalize_plain(res, extra):
    (l, acc), = res
    return acc / l


def _finalize_diff(res, extra, *, lam_init):
    lam_ref, g_ref = extra
    (l1, a1), (l2, a2) = res
    lam = (jnp.exp(jnp.sum(lam_ref[0:1] * lam_ref[1:2], axis=-1, keepdims=True))
           - jnp.exp(jnp.sum(lam_ref[2:3] * lam_ref[3:4], axis=-1, keepdims=True)) + lam_init)
    o = a1 / l1 - lam * (a2 / l2)
    return o * lax.rsqrt(jnp.mean(o * o, axis=0, keepdims=True) + RMS_EPS) * g_ref[...] * (1.0 - lam_init)


def _attention(qt, kv, extra=(), *, dv, n_sub, finalize, name):
    b, hw, n = qt.shape
    heads = hw // HEAD_SLOT
    tq = _tile(n, ATTN_TQ)
    tk = ATTN_TK if all(k.shape[1] % ATTN_TK == 0 for k, _ in kv) else LANES
    full = lambda a: pl.BlockSpec(a.shape, lambda bi, hi, qi: (0,) * a.ndim)
    kv_specs, kv_args = [], []
    for k, vt in kv:
        t = k.shape[1]
        kv_specs += [pl.BlockSpec((1, t, HEAD_SLOT), lambda bi, hi, qi: (bi, 0, hi)),
                     pl.BlockSpec((1, dv, t), lambda bi, hi, qi: (bi, hi, 0))]
        kv_args += [k, vt]
    scratch = [pltpu.VMEM((KMAX_ROWS, HEAD_SLOT), F32), pltpu.VMEM((2, n_sub, tk, tq), F32)]
    if len(kv) > 1:
        t_all = sum(k.shape[1] for k, _ in kv)
        scratch += [pltpu.VMEM((1, t_all, HEAD_SLOT), BF16), pltpu.VMEM((1, dv, t_all), BF16)]
    return pl.pallas_call(
        functools.partial(_attn_kernel, tk=tk, unroll=ATTN_UNROLL, n_sub=n_sub, n_src=len(kv), finalize=finalize),
        out_shape=jax.ShapeDtypeStruct((b, heads * dv, n), BF16),
        grid=(b, heads, n // tq),
        in_specs=[pl.BlockSpec((1, HEAD_SLOT, tq), lambda bi, hi, qi: (bi, hi, qi))] + kv_specs
        + [full(a) for a in extra],
        out_specs=pl.BlockSpec((1, dv, tq), lambda bi, hi, qi: (bi, hi, qi)),
        scratch_shapes=scratch,
        compiler_params=_cparams(("parallel", "parallel", "arbitrary")),
        name=name,
    )(qt, *kv_args, *extra)


def _out_kernel(*refs, n_rowmajor):
    (x_ref, g_ref, lng_ref, lnb_ref, scf_ref, shf_ref, rt_ref), rest = refs[:7], refs[7:]
    x1_ref, aff_ref, hp_ref = rest[-3:]
    ops = rest[:-3]
    m = None
    for i in range(len(ops) // 2):
        a = ops[2 * i][0]
        w = ops[2 * i + 1][...]
        part = _dot(a, w) if i < n_rowmajor else _dot_tn(a, w)
        m = part if m is None else m + part
    x1 = _ln(DEEPNORM_ALPHA * x_ref[0] + (1.0 + g_ref[0]) * m, lng_ref[...], lnb_ref[...])
    x1_ref[0] = x1
    hf = (x1 * (1.0 + scf_ref[0]) + shf_ref[0]).astype(BF16)
    lg = _dot_nt(rt_ref[...], hf)
    e = jnp.exp(lg - jnp.max(lg, axis=0, keepdims=True))
    aff_ref[0] = e / jnp.sum(e, axis=0, keepdims=True)
    dh = hf.shape[1] // 2
    lo = pltpu.bitcast(hf[:, :dh].astype(F32), jnp.uint32)
    hi = pltpu.bitcast(hf[:, dh:].astype(F32), jnp.uint32)
    hp_ref[0] = _to_tiled(lax.shift_right_logical(lo, jnp.uint32(16)) | (hi & jnp.uint32(0xFFFF0000)))


def _mixer_out(x, g, lng, lnb, scf, shf, router_t, rowmajor_ops, chanmajor_ops):
    b, n, d = x.shape
    e = router_t.shape[0]
    tm = _tile(n, TOKEN_TILE)
    vec = pl.BlockSpec((1, 1, d), lambda bi, ti: (bi, 0, 0))
    full = lambda a: pl.BlockSpec(a.shape, lambda bi, ti: (0,) * a.ndim)
    args = [x, g, lng, lnb, scf, shf, router_t]
    specs = [pl.BlockSpec((1, tm, d), lambda bi, ti: (bi, ti, 0)), vec, full(lng), full(lnb), vec, vec,
             full(router_t)]
    for a, w in rowmajor_ops:
        args += [a, w]
        specs += [pl.BlockSpec((1, tm, a.shape[2]), lambda bi, ti: (bi, ti, 0)), full(w)]
    for a, w in chanmajor_ops:
        args += [a, w]
        specs += [pl.BlockSpec((1, a.shape[1], tm), lambda bi, ti: (bi, 0, ti)), full(w)]
    return pl.pallas_call(
        functools.partial(_out_kernel, n_rowmajor=len(rowmajor_ops)),
        out_shape=(jax.ShapeDtypeStruct((b, n, d), F32), jax.ShapeDtypeStruct((b, e, n), F32),
                   jax.ShapeDtypeStruct((b, n * (d // 2) // LANES, LANES), jnp.uint32)),
        grid=(b, n // tm),
        in_specs=specs,
        out_specs=(pl.BlockSpec((1, tm, d), lambda bi, ti: (bi, ti, 0)),
                   pl.BlockSpec((1, e, tm), lambda bi, ti: (bi, 0, ti)),
                   pl.BlockSpec((1, tm * (d // 2) // LANES, LANES), lambda bi, ti: (bi, ti, 0))),
        compiler_params=_cparams(("parallel", "parallel")),
        name="mixer_out",
    )(*args)


ROW_UNROLL = 8
FFN_ROWS = 512
TOKEN_SPLIT = 4096


def _ffn_kernel(src_ref, nxt_ref, hp_ref, g_ref, wg_ref, wu_ref, wd_ref, y_ref, rows_ref, xs_ref, *, prefetch):
    cap = xs_ref.shape[0]
    nt = rows_ref.shape[0] // cap
    dh = nt * LANES

    def gather(idx_ref, j, dst):
        rows_ref[pl.ds(dst, nt, stride=SUBLANES), :] = hp_ref[0, pl.ds(idx_ref[0, 0, j], nt, stride=SUBLANES), :]

    def gather_loop():
        def body(jb, carry):
            for u in range(ROW_UNROLL):
                gather(src_ref, jb * ROW_UNROLL + u, jb * (ROW_UNROLL * nt) + u)
            return carry

        lax.fori_loop(0, cap // ROW_UNROLL, body, 0)

    if prefetch:
        pl.when(pl.program_id(1) == 0)(gather_loop)
    else:
        gather_loop()

    blk = min(cap, FFN_ROWS)
    for c0 in range(0, cap, blk):
        w = _from_tiled(rows_ref[c0 * nt:(c0 + blk) * nt], dh)
        lo = pltpu.bitcast(lax.shift_left(w, jnp.uint32(16)), F32)
        hi = pltpu.bitcast(w & jnp.uint32(0xFFFF0000), F32)
        xs_ref[c0:c0 + blk] = jnp.concatenate([lo, hi], axis=1).astype(BF16)

    if prefetch:
        for j in range(cap):
            gather(nxt_ref, j, _tiled_row(j, dh))

    expert = pl.program_id(1 if prefetch else 0)
    mine = lax.broadcasted_iota(jnp.int32, (blk, LANES), 1) == expert
    for c0 in range(0, cap, blk):
        xs = xs_ref[c0:c0 + blk]
        gate = _dot(xs, wg_ref[0])
        hid = gate / (1.0 + jnp.exp(-gate)) * _dot(xs, wu_ref[0])
        g_col = jnp.sum(jnp.where(mine, g_ref[0, c0:c0 + blk], 0.0), axis=1, keepdims=True)
        y = (_dot(hid.astype(BF16), wd_ref[0]) * g_col).astype(BF16)
        y_ref[0, 0, c0 * nt:(c0 + blk) * nt] = _to_tiled(_pack_halves(y))


def _activations_outweigh(activation_bytes, weight_bytes):
    return activation_bytes > weight_bytes


def _moe_ffn(hp, src3, gates, wg, wu, wd, cap):
    b, hp_rows, _ = hp.shape
    n_exp, d, ff = wg.shape
    nt = d // 2 // LANES
    sample_major = _activations_outweigh(hp_rows * LANES * 4, 3 * d * ff * 2)
    if sample_major:
        grid, be = (b, n_exp), (lambda g0, g1: (g0, g1))
        hp_spec = pl.BlockSpec((1, hp_rows, LANES), lambda g0, g1: (g0, 0, 0), pipeline_mode=pl.Buffered(1))
    else:
        grid, be = (n_exp, b), (lambda g0, g1: (g1, g0))
        hp_spec = pl.BlockSpec((1, hp_rows, LANES), lambda g0, g1: (g1, 0, 0))
    slot = lambda g0, g1: (be(g0, g1)[0] * n_exp + be(g0, g1)[1], 0, 0)
    weight = lambda g0, g1: (be(g0, g1)[1], 0, 0)
    nxt = lambda g0, g1: (be(g0, g1)[0] * n_exp + jnp.minimum(be(g0, g1)[1] + 1, n_exp - 1), 0, 0)
    return pl.pallas_call(
        functools.partial(_ffn_kernel, prefetch=sample_major),
        out_shape=jax.ShapeDtypeStruct((b, n_exp, cap * nt, LANES), jnp.uint32),
        grid=grid,
        in_specs=[pl.BlockSpec((1, 1, cap), slot, memory_space=pltpu.SMEM),
                  pl.BlockSpec((1, 1, cap), nxt, memory_space=pltpu.SMEM),
                  hp_spec,
                  pl.BlockSpec((1, cap, LANES), lambda g0, g1: (be(g0, g1)[0], 0, 0)),
                  pl.BlockSpec((1, d, ff), weight),
                  pl.BlockSpec((1, d, ff), weight),
                  pl.BlockSpec((1, ff, d), weight)],
        out_specs=pl.BlockSpec((1, 1, cap * nt, LANES), lambda g0, g1: be(g0, g1) + (0, 0)),
        scratch_shapes=[pltpu.VMEM((cap * nt, LANES), jnp.uint32), pltpu.VMEM((cap, d), BF16)],
        compiler_params=_cparams(("parallel", "arbitrary")),
        name="moe_ffn",
    )(src3, src3, hp, gates, wg, wu, wd)


def _scatter_kernel(dst_ref, bnd_ref, y_ref, f_ref, *, d):
    nt = d // LANES
    ntp = nt // 2

    @pl.when(pl.program_id(2) == 0)
    def _():
        f_ref[...] = jnp.zeros_like(f_ref)

    lo = bnd_ref[0, 0, pl.program_id(1)]
    hi = bnd_ref[0, 0, pl.program_id(1) + 1]

    def f_row(start):
        return f_ref[0, pl.ds(start, nt, stride=SUBLANES), :]

    def y_row(start):
        return jnp.concatenate(_unpack_halves(y_ref[0, 0, pl.ds(start, ntp, stride=SUBLANES), :]), axis=0)

    def one(j, carry):
        dst = dst_ref[0, 0, j]
        f_ref[0, pl.ds(dst, nt, stride=SUBLANES), :] = f_row(dst) + y_row(_tiled_row(j, d // 2))
        return carry

    def batch(jb, carry):
        dsts = [dst_ref[0, 0, jb * ROW_UNROLL + u] for u in range(ROW_UNROLL)]
        sums = [f_row(dsts[u]) + y_row(jb * (ROW_UNROLL * ntp) + u) for u in range(ROW_UNROLL)]
        for u in range(ROW_UNROLL):
            f_ref[0, pl.ds(dsts[u], nt, stride=SUBLANES), :] = sums[u]
        return carry

    head_end = jnp.minimum((lo + ROW_UNROLL - 1) // ROW_UNROLL * ROW_UNROLL, hi)
    tail_start = jnp.maximum(head_end, hi // ROW_UNROLL * ROW_UNROLL)
    lax.fori_loop(lo, head_end, one, 0)
    lax.fori_loop(head_end // ROW_UNROLL, tail_start // ROW_UNROLL, batch, 0)
    lax.fori_loop(tail_start, hi, one, 0)


def _moe_scatter(y, dst3, bounds3, n, d):
    b, n_exp, y_rows, _ = y.shape
    nt = _tile(n, TOKEN_SPLIT)
    lt = d // LANES
    cap = y_rows * 2 // lt
    return pl.pallas_call(
        functools.partial(_scatter_kernel, d=d),
        out_shape=jax.ShapeDtypeStruct((b, n * lt, LANES), F32),
        grid=(b, n // nt, n_exp),
        in_specs=[pl.BlockSpec((1, 1, cap), lambda bi, ki, ei: (bi * n_exp + ei, 0, 0), memory_space=pltpu.SMEM),
                  pl.BlockSpec((1, 1, n // nt + 1), lambda bi, ki, ei: (bi * n_exp + ei, 0, 0),
                               memory_space=pltpu.SMEM),
                  pl.BlockSpec((1, 1, y_rows, LANES), lambda bi, ki, ei: (bi, ei, 0, 0))],
        out_specs=pl.BlockSpec((1, nt * lt, LANES), lambda bi, ki, ei: (bi, ki, 0)),
        compiler_params=_cparams(("parallel", "parallel", "arbitrary")),
        name="moe_scatter",
    )(dst3, bounds3, y)


def _post_kernel(x_ref, f_ref, g_ref, lng_ref, lnb_ref, o_ref):
    f = _from_tiled(f_ref[0], x_ref.shape[2])
    o_ref[0] = _ln(DEEPNORM_ALPHA * x_ref[0] + (1.0 + g_ref[0]) * f, lng_ref[...], lnb_ref[...])


def _post(x1, f_tiled, g, lng, lnb):
    b, n, d = x1.shape
    tm = _tile(n, TOKEN_TILE)
    blk = pl.BlockSpec((1, tm, d), lambda bi, ti: (bi, ti, 0))
    f_blk = pl.BlockSpec((1, tm * d // LANES, LANES), lambda bi, ti: (bi, ti, 0))
    full = lambda a: pl.BlockSpec(a.shape, lambda bi, ti: (0,) * a.ndim)
    return pl.pallas_call(
        _post_kernel,
        out_shape=jax.ShapeDtypeStruct((b, n, d), F32),
        grid=(b, n // tm),
        in_specs=[blk, f_blk, pl.BlockSpec((1, 1, d), lambda bi, ti: (bi, 0, 0)), full(lng), full(lnb)],
        out_specs=blk,
        compiler_params=_cparams(("parallel", "parallel")),
        name="ffn_post",
    )(x1, f_tiled, g, lng, lnb)


def _select_kernel(aff_ref, idx_ref, gate_ref, *, cap, n_exp):
    a = aff_ref[0]
    er, lanes = a.shape
    r_n = er // n_exp
    e_pad = 16
    bits = pltpu.bitcast(a, jnp.int32)

    member = (lax.broadcasted_iota(jnp.int32, (e_pad, er), 1) // r_n
              == lax.broadcasted_iota(jnp.int32, (e_pad, er), 0))
    gs = jnp.where(member, 1.0, 0.0).astype(BF16)
    member_t = (lax.broadcasted_iota(jnp.int32, (er, e_pad), 0) // r_n
                == lax.broadcasted_iota(jnp.int32, (er, e_pad), 1))
    gst = jnp.where(member_t, 1.0, 0.0).astype(BF16)
    ri = lax.broadcasted_iota(jnp.int32, (er, er), 0)
    rj = lax.broadcasted_iota(jnp.int32, (er, er), 1)
    rows_before = jnp.where((ri // r_n == rj // r_n) & (rj < ri), 1.0, 0.0).astype(BF16)
    li = lax.broadcasted_iota(jnp.int32, (lanes, lanes), 0)
    lj = lax.broadcasted_iota(jnp.int32, (lanes, lanes), 1)
    lanes_upto = jnp.where(li <= lj, 1.0, 0.0).astype(BF16)

    def expert_total(m):
        per_lane = _dot(gs, m.astype(BF16))
        return jnp.broadcast_to(jnp.sum(per_lane, axis=1, keepdims=True), per_lane.shape)

    def to_rows(ev):
        if r_n % SUBLANES == 0 and n_exp == e_pad:
            return jnp.broadcast_to(ev[:, None, :], (e_pad, r_n, lanes)).reshape(er, lanes)
        hi = jnp.floor(ev * (1.0 / 128.0))
        return 128.0 * _dot(gst, hi.astype(BF16)) + _dot(gst, (ev - 128.0 * hi).astype(BF16))

    def search(i, t):
        cand = t | lax.shift_left(jnp.int32(1), 30 - i)
        cnt = expert_total(jnp.where(bits >= cand, 1.0, 0.0))
        return jnp.where(to_rows(jnp.where(cnt >= cap, 1.0, 0.0)) > 0.5, cand, t)

    t = lax.fori_loop(0, 31, search, jnp.zeros((er, lanes), jnp.int32))
    gt = jnp.where(bits > t, 1.0, 0.0)
    eq = jnp.where(bits == t, 1.0, 0.0)

    def prefix(m):
        rc = _dot(m.astype(BF16), lanes_upto)
        off = _dot(rows_before, jnp.broadcast_to(rc[:, lanes - 1:lanes], m.shape).astype(BF16))
        return rc, off

    rc_eq, off_eq = prefix(eq)
    need = cap - to_rows(expert_total(gt))
    sel = jnp.maximum(gt, eq * jnp.where(off_eq + rc_eq - eq < need, 1.0, 0.0))
    rc, off = prefix(sel)
    row_end = off + jnp.broadcast_to(rc[:, lanes - 1:lanes], rc.shape)

    slot = lax.broadcasted_iota(jnp.int32, (cap, lanes), 0).astype(F32) + 1.0
    lane_c = lax.broadcasted_iota(jnp.int32, (cap, lanes), 1)
    lane_cf = lane_c.astype(F32)
    diag = lax.broadcasted_iota(jnp.int32, (r_n, lanes), 0) == lax.broadcasted_iota(jnp.int32, (r_n, lanes), 1)
    lane_r = lax.broadcasted_iota(jnp.int32, (1, lanes), 1)
    pad = jnp.zeros((lanes - r_n, lanes), F32)

    def table(m):
        return (jnp.concatenate([m, pad], axis=0) if r_n < lanes else m).astype(BF16)

    all_ones = jnp.ones((lanes, lanes), BF16)
    idx_all = jnp.zeros((cap, lanes), jnp.int32)
    gate_all = jnp.zeros((cap, lanes), F32)
    for e in range(n_exp):
        r0 = e * r_n
        end_lane = jnp.sum(jnp.where(diag, row_end[r0:r0 + r_n], 0.0), axis=0, keepdims=True)
        end_lane = jnp.where(lane_r < r_n, end_lane, 3.0e38)
        row_j = jnp.sum(jnp.where(end_lane < slot, 1.0, 0.0), axis=1, keepdims=True)
        oh = jnp.where(lane_cf == row_j, 1.0, 0.0).astype(BF16)
        cum_e = off[r0:r0 + r_n] + rc[r0:r0 + r_n]
        cum_hi = jnp.floor(cum_e * (1.0 / 128.0))
        cum_2 = _dot(oh, jnp.concatenate([table(cum_hi), table(cum_e - 128.0 * cum_hi)], axis=1))
        cum_j = 128.0 * cum_2[:, :lanes] + cum_2[:, lanes:]
        lane_j = _dot(jnp.where(cum_j < slot, 1.0, 0.0).astype(BF16), all_ones)
        a_e = a[r0:r0 + r_n]
        a1 = a_e.astype(BF16).astype(F32)
        a2 = (a_e - a1).astype(BF16).astype(F32)
        a3 = a_e - a1 - a2
        aff_2 = _dot(oh, jnp.concatenate([table(a1), table(a2)], axis=1))
        aff_j = aff_2[:, :lanes] + aff_2[:, lanes:] + _dot(oh, table(a3))
        gate_j = jnp.sum(jnp.where(lane_cf == lane_j, aff_j, 0.0), axis=1, keepdims=True)
        idx_j = (row_j * float(lanes) + lane_j).astype(jnp.int32)
        idx_all = jnp.where(lane_c == e, idx_j, idx_all)
        gate_all = jnp.where(lane_c == e, gate_j, gate_all)
    idx_ref[0] = idx_all
    gate_ref[0] = gate_all


def _moe_select(aff_t, cap):
    b, n_exp, n = aff_t.shape
    lanes = HEAD_SLOT
    er = n_exp * (n // lanes)
    idx, gates = pl.pallas_call(
        functools.partial(_select_kernel, cap=cap, n_exp=n_exp),
        out_shape=(jax.ShapeDtypeStruct((b, cap, lanes), jnp.int32), jax.ShapeDtypeStruct((b, cap, lanes), F32)),
        grid=(b,),
        in_specs=[pl.BlockSpec((1, er, lanes), lambda bi: (bi, 0, 0))],
        out_specs=(pl.BlockSpec((1, cap, lanes), lambda bi: (bi, 0, 0)),
                   pl.BlockSpec((1, cap, lanes), lambda bi: (bi, 0, 0))),
        compiler_params=_cparams(("parallel",)),
        name="moe_select",
    )(aff_t.reshape(b, er, lanes))
    return gates, jnp.swapaxes(idx[:, :, :n_exp], 1, 2)


def _moe(x1, aff_t, hp, gf, wg, wu, wd, lng, lnb):
    b, n, d = x1.shape
    n_exp = aff_t.shape[1]
    cap = CAPACITY_FACTOR * n // n_exp
    gates, idx = _moe_select(aff_t, cap)
    nt = _tile(n, TOKEN_SPLIT)
    edges = jnp.arange(0, n + 1, nt, dtype=jnp.int32)
    bounds3 = jnp.sum(idx[..., None] < edges, axis=2, dtype=jnp.int32).reshape(b * n_exp, 1, n // nt + 1)
    src3 = _tiled_row(idx, d // 2).reshape(b * n_exp, 1, cap)
    dst3 = _tiled_row(idx % nt, d).reshape(b * n_exp, 1, cap)
    y = _moe_ffn(hp, src3, gates, wg, wu, wd, cap)
    f_tiled = _moe_scatter(y, dst3, bounds3, n, d)
    return _post(x1, f_tiled, gf, lng, lnb)


def _rope_angles(n_tokens, dim):
    n_rows = n_tokens // GRID_W
    row = jnp.repeat(jnp.arange(n_rows, dtype=F32), GRID_W)
    col = jnp.tile(jnp.arange(GRID_W, dtype=F32), n_rows)
    n_freq = dim // 4
    inv_freq = ROPE_BASE ** (-jnp.arange(n_freq, dtype=F32) / n_freq)
    ang = jnp.concatenate([row[:, None] * inv_freq, col[:, None] * inv_freq], -1)
    return jnp.cos(ang), jnp.sin(ang)


def _rope_tables(n_tokens, dim, lane_offset, identity):
    half = dim // 2
    if identity:
        cos = jnp.ones((n_tokens, half), F32)
        sin = jnp.zeros((n_tokens, half), F32)
    else:
        cos, sin = _rope_angles(n_tokens, dim)
    c2 = jnp.concatenate([cos, cos], -1)
    s2 = jnp.concatenate([-sin, sin], -1)
    reps = (HEAD_SLOT - lane_offset) // dim if lane_offset == 0 else 1
    ck = jnp.zeros((n_tokens, HEAD_SLOT), F32)
    sk = jnp.zeros((n_tokens, HEAD_SLOT), F32)
    for r in range(reps):
        ck = ck.at[:, lane_offset + r * dim:lane_offset + (r + 1) * dim].set(c2)
        sk = sk.at[:, lane_offset + r * dim:lane_offset + (r + 1) * dim].set(s2)
    return ck, sk, c2.T, s2.T


def _deinterleave(w, axis):
    return jnp.concatenate([jnp.take(w, jnp.arange(0, w.shape[axis], 2), axis=axis),
                            jnp.take(w, jnp.arange(1, w.shape[axis], 2), axis=axis)], axis=axis)


def _prep_layer0(w_in, ln_g, ln_b, ws, bs, q_norm, w_uq, kv_norm, w_ukv):
    d = w_in.shape[0]
    w = GMLP_WIDTH
    o_kr = 2 * w + MLA_Q_RANK + MLA_KV_RANK
    kr = _deinterleave(w_in[:, o_kr:o_kr + MLA_ROPE], 1)
    kr_sw = jnp.concatenate([kr[:, MLA_ROPE // 2:], kr[:, :MLA_ROPE // 2]], 1)
    slot = lambda m: jnp.zeros((d, HEAD_SLOT), F32).at[:, MLA_NOPE:MLA_NOPE + MLA_ROPE].set(m)
    win = jnp.concatenate([w_in[:, :o_kr], slot(kr), slot(kr_sw)], 1).astype(BF16)

    qd = MLA_NOPE + MLA_ROPE
    wq = w_uq.reshape(MLA_Q_RANK, MLA_HEADS, qd)
    wq = jnp.concatenate([wq[..., :MLA_NOPE], _deinterleave(wq[..., MLA_NOPE:], 2),
                          jnp.zeros((MLA_Q_RANK, MLA_HEADS, HEAD_SLOT - qd), F32)], -1)
    wq_t = wq.reshape(MLA_Q_RANK, MLA_HEADS * HEAD_SLOT).T.astype(BF16)

    wkv = w_ukv.reshape(MLA_KV_RANK, MLA_HEADS, MLA_NOPE + MLA_V)
    wk = jnp.concatenate([wkv[..., :MLA_NOPE], jnp.zeros((MLA_KV_RANK, MLA_HEADS, HEAD_SLOT - MLA_NOPE), F32)], -1)
    wk = wk.reshape(MLA_KV_RANK, MLA_HEADS * HEAD_SLOT).astype(BF16)
    wv_t = wkv[..., MLA_NOPE:].reshape(MLA_KV_RANK, MLA_HEADS * MLA_V).T.astype(BF16)

    bsb = jnp.broadcast_to(bs[:, :, None], (GMLP_GROUPS, CHUNK, GMLP_GROUP_CH))
    return (win, ln_g.reshape(1, -1), ln_b.reshape(1, -1), ws.astype(BF16), bsb,
            q_norm.reshape(1, -1), wq_t, kv_norm.reshape(1, -1), wk, wv_t)


def _prep_layer1(w_in):
    d = w_in.shape[0]
    dw = 2 * DIFF_HEADS * DIFF_HEAD_DIM
    perm = lambda m: _deinterleave(m.reshape(d, 2 * DIFF_HEADS, DIFF_HEAD_DIM), 2).reshape(d, dw)
    wq_t = perm(w_in[:, :dw]).T.astype(BF16)
    wk = perm(w_in[:, dw:2 * dw]).astype(BF16)
    wv_t = w_in[:, 2 * dw:].T.astype(BF16)
    return wq_t, wk, wv_t


def _split6(m_row):
    return [v[:, None, :] for v in jnp.split(m_row, 6, axis=-1)]


def kernel(x, c, ctx, c_ctx, w_mod_0, b_mod_0, w_in_0, gmlp_ln_g_0, gmlp_ln_b_0, gmlp_ws_0, gmlp_bs_0, mla_q_norm_0, mla_w_uq_0, mla_kv_norm_0, mla_w_ukv_0, w_out_0, ln_mix_g_0, ln_mix_b_0, router_0, w_gate_0, w_up_0, w_down_0, ln_ffn_g_0, ln_ffn_b_0, w_mod_1, b_mod_1, w_in_1, lambda_q1_1, lambda_k1_1, lambda_q2_1, lambda_k2_1, subln_g_1, w_out_1, ln_mix_g_1, ln_mix_b_1, router_1, w_gate_1, w_up_1, w_down_1, ln_ffn_g_1, ln_ffn_b_1):
    b, n, d = x.shape
    m_ctx = ctx.shape[1]
    row = lambda v: v.reshape(1, -1)

    cond = jnp.concatenate([c, c_ctx[None, :], jnp.zeros((-(b + 1) % 8, d), F32)], 0)
    mod0 = _modulation(cond, w_mod_0, b_mod_0)
    mod1 = _modulation(cond, w_mod_1, b_mod_1)
    sh_a0, sc_a0, g_a0, sh_f0, sc_f0, g_f0 = _split6(mod0[:b])
    csh_a0, csc_a0, cg_a0, csh_f0, csc_f0, cg_f0 = [jnp.broadcast_to(v, (b, 1, d)) for v in _split6(mod0[b:b + 1])]
    sh_a1, sc_a1, g_a1, sh_f1, sc_f1, g_f1 = _split6(mod1[:b])
    csh_a1, csc_a1 = [jnp.broadcast_to(v, (b, 1, d)) for v in _split6(mod1[b:b + 1])[:2]]

    wts0 = _prep_layer0(w_in_0, gmlp_ln_g_0, gmlp_ln_b_0, gmlp_ws_0, gmlp_bs_0,
                        mla_q_norm_0, mla_w_uq_0, mla_kv_norm_0, mla_w_ukv_0)
    tabs_l = _rope_tables(n, MLA_ROPE, MLA_NOPE, identity=False)
    tabs_c = _rope_tables(m_ctx, MLA_ROPE, MLA_NOPE, identity=True)
    a_l, q_l, k_l, v_l = _proj0(x, sc_a0, sh_a0, wts0, tabs_l)
    a_c, q_c, k_c, v_c = _proj0(ctx, csc_a0, csh_a0, wts0, tabs_c)
    o_l = _attention(q_l, [(k_l, v_l), (k_c, v_c)], dv=MLA_V, n_sub=1, finalize=_finalize_plain, name="mla_attn")
    o_c = _attention(q_c, [(k_c, v_c)], dv=MLA_V, n_sub=1, finalize=_finalize_plain, name="mla_attn_ctx")

    w_out_a = w_out_0[:GMLP_WIDTH].astype(BF16)
    w_out_o = w_out_0[GMLP_WIDTH:].astype(BF16)
    router_t0 = router_0.T.astype(BF16)
    wg0, wu0, wd0 = w_gate_0.astype(BF16), w_up_0.astype(BF16), w_down_0.astype(BF16)
    lng, lnb = row(ln_mix_g_0), row(ln_mix_b_0)
    x1, aff, hp = _mixer_out(x, g_a0, lng, lnb, sc_f0, sh_f0, router_t0, [(a_l, w_out_a)], [(o_l, w_out_o)])
    x_lat = _moe(x1, aff, hp, g_f0, wg0, wu0, wd0, row(ln_ffn_g_0), row(ln_ffn_b_0))
    c1, caff, chp = _mixer_out(ctx, cg_a0, lng, lnb, csc_f0, csh_f0, router_t0, [(a_c, w_out_a)], [(o_c, w_out_o)])
    x_ctx = _moe(c1, caff, chp, cg_f0, wg0, wu0, wd0, row(ln_ffn_g_0), row(ln_ffn_b_0))

    lam_init = 0.8 - 0.6 * math.exp(-0.3 * 1)
    wts1 = _prep_layer1(w_in_1)
    tabs_l = _rope_tables(n, DIFF_HEAD_DIM, 0, identity=False)
    tabs_c = _rope_tables(m_ctx, DIFF_HEAD_DIM, 0, identity=True)
    q_l, k_l, v_l = _proj1(x_lat, sc_a1, sh_a1, wts1, tabs_l)
    _, k_c, v_c = _proj1(x_ctx, csc_a1, csh_a1, wts1, tabs_c)
    lams = jnp.stack([lambda_q1_1, lambda_k1_1, lambda_q2_1, lambda_k2_1])
    lam_pack = jnp.zeros((KMAX_ROWS, LANES), F32).at[:4, :lams.shape[1]].set(lams)
    extra = (lam_pack, subln_g_1.reshape(-1, 1))
    o_l = _attention(q_l, [(k_l, v_l), (k_c, v_c)], extra, dv=2 * DIFF_HEAD_DIM, n_sub=2,
                     finalize=functools.partial(_finalize_diff, lam_init=lam_init), name="diff_attn")
    x1, aff, hp = _mixer_out(x_lat, g_a1, row(ln_mix_g_1), row(ln_mix_b_1), sc_f1, sh_f1, router_1.T.astype(BF16),
                             [], [(o_l, w_out_1.astype(BF16))])
    return _moe(x1, aff, hp, g_f1, w_gate_1.astype(BF16), w_up_1.astype(BF16), w_down_1.astype(BF16),
                row(ln_ffn_g_1), row(ln_ffn_b_1))
```

```python
import functools
import math

import jax
import jax.numpy as jnp
from jax import lax
from jax.experimental import pallas as pl
from jax.experimental.pallas import tpu as pltpu

F32 = jnp.float32
BF16 = jnp.bfloat16

DEPTH = 2
GRID_W = 64
ROPE_BASE = 10000.0
CHUNK = 128
GMLP_GROUPS = 4
GMLP_GROUP_CH = 128
GMLP_WIDTH = GMLP_GROUPS * GMLP_GROUP_CH
MLA_HEADS = 8
MLA_NOPE = 64
MLA_ROPE = 32
MLA_V = 64
MLA_Q_RANK = 256
MLA_KV_RANK = 128
DIFF_HEADS = 8
DIFF_HEAD_DIM = 64
CAPACITY_FACTOR = 2
DEEPNORM_ALPHA = (2 * DEPTH) ** 0.25
LN_EPS = 1e-5
RMS_EPS = 1e-6
LOG2E = 1.4426950408889634

LANES = 128
SUBLANES = 8
HEAD_SLOT = LANES
NEG_BIG = -1e30
VMEM_LIMIT = 56 * 1024 * 1024
TOKEN_TILE = 512
MOD_TILE = 1024


def _cparams(sem):
    return pltpu.CompilerParams(dimension_semantics=sem, vmem_limit_bytes=VMEM_LIMIT)


def _tile(n, pref):
    return pref if n % pref == 0 else n


def _ln(x, g, b):
    mu = jnp.mean(x, axis=-1, keepdims=True)
    xc = x - mu
    var = jnp.mean(xc * xc, axis=-1, keepdims=True)
    return xc * lax.rsqrt(var + LN_EPS) * g + b


def _rms(x, g):
    return x * lax.rsqrt(jnp.mean(x * x, axis=-1, keepdims=True) + RMS_EPS) * g


def _gelu(x):
    return 0.5 * x * (1.0 + lax.erf(x * (2.0 ** -0.5)))


def _to_tiled(v):
    m, w = v.shape
    nt = w // LANES
    tiles = [v[:, c * LANES:(c + 1) * LANES].reshape(m // SUBLANES, SUBLANES, LANES) for c in range(nt)]
    return jnp.stack(tiles, axis=1).reshape(m * nt, LANES)


def _from_tiled(t, w):
    nt = w // LANES
    m = t.shape[0] // nt
    t4 = t.reshape(m // SUBLANES, nt, SUBLANES, LANES)
    return jnp.concatenate([t4[:, c].reshape(m, LANES) for c in range(nt)], axis=1)


def _tiled_row(i, w):
    return (i // SUBLANES) * (SUBLANES * (w // LANES)) + i % SUBLANES


def _pack_halves(v):
    half = v.shape[1] // 2
    lo = pltpu.bitcast(v[:, :half].astype(F32), jnp.uint32)
    hi = pltpu.bitcast(v[:, half:].astype(F32), jnp.uint32)
    return lax.shift_right_logical(lo, jnp.uint32(16)) | (hi & jnp.uint32(0xFFFF0000))


def _unpack_halves(w):
    return (pltpu.bitcast(lax.shift_left(w, jnp.uint32(16)), F32),
            pltpu.bitcast(w & jnp.uint32(0xFFFF0000), F32))


def _dot(a, b):
    return jnp.dot(a, b, preferred_element_type=F32)


def _dot_nt(a, b):
    return lax.dot_general(a, b, (((1,), (1,)), ((), ())), preferred_element_type=F32)


def _dot_tn(a, b):
    return lax.dot_general(a, b, (((0,), (0,)), ((), ())), preferred_element_type=F32)


def _mod_kernel(c_ref, w_ref, b_ref, o_ref):
    c = c_ref[...]
    s = c / (1.0 + jnp.exp(-c))
    o_ref[...] = _dot(s.astype(BF16), w_ref[...].astype(BF16)) + b_ref[...]


def _modulation(cond, w_mod, b_mod):
    r, d = cond.shape
    n = w_mod.shape[1]
    tn = _tile(n, MOD_TILE)
    return pl.pallas_call(
        _mod_kernel,
        out_shape=jax.ShapeDtypeStruct((r, n), F32),
        grid=(n // tn,),
        in_specs=[pl.BlockSpec((r, d), lambda j: (0, 0)),
                  pl.BlockSpec((d, tn), lambda j: (0, j)),
                  pl.BlockSpec((1, tn), lambda j: (0, j))],
        out_specs=pl.BlockSpec((r, tn), lambda j: (0, j)),
        compiler_params=_cparams(("parallel",)),
        name="modulation",
    )(cond, w_mod, b_mod.reshape(1, n))


def _proj0_kernel(x_ref, sc_ref, sh_ref, win_ref, lng_ref, lnb_ref, ws_ref, bs_ref,
                  qn_ref, wq_ref, kvn_ref, wk_ref, wv_ref, ck_ref, sk_ref, cq_ref, sq_ref,
                  a_ref, q_ref, k_ref, v_ref, *, q_scale):
    tm = x_ref.shape[1]
    h = (x_ref[0] * (1.0 + sc_ref[0]) + sh_ref[0]).astype(BF16)
    y = _dot(h, win_ref[...])
    w = GMLP_WIDTH
    u = _gelu(y[:, :w])
    vn = _ln(_gelu(y[:, w:2 * w]), lng_ref[...], lnb_ref[...]).astype(BF16)
    for ci in range(tm // CHUNK):
        r0 = ci * CHUNK
        for g in range(GMLP_GROUPS):
            c0 = g * GMLP_GROUP_CH
            mixed = _dot(ws_ref[g], vn[r0:r0 + CHUNK, c0:c0 + GMLP_GROUP_CH]) + bs_ref[g]
            a_ref[0, r0:r0 + CHUNK, c0:c0 + GMLP_GROUP_CH] = (
                u[r0:r0 + CHUNK, c0:c0 + GMLP_GROUP_CH] * mixed).astype(BF16)

    o = 2 * w
    cq = _rms(y[:, o:o + MLA_Q_RANK], qn_ref[...]).astype(BF16)
    qt = _dot_nt(wq_ref[...], cq) * q_scale
    q_ref[0] = qt.astype(BF16)
    hr = MLA_ROPE // 2
    cq_t = cq_ref[...]
    sq_t = sq_ref[...]
    for hd in range(MLA_HEADS):
        r0 = hd * HEAD_SLOT + MLA_NOPE
        x1 = qt[r0:r0 + hr]
        x2 = qt[r0 + hr:r0 + 2 * hr]
        blk = jnp.concatenate([x1, x2], axis=0)
        swp = jnp.concatenate([x2, x1], axis=0)
        q_ref[0, r0:r0 + 2 * hr, :] = (blk * cq_t + swp * sq_t).astype(BF16)

    o += MLA_Q_RANK
    ckv = _rms(y[:, o:o + MLA_KV_RANK], kvn_ref[...]).astype(BF16)
    o += MLA_KV_RANK
    kr = y[:, o:o + HEAD_SLOT] * ck_ref[...] + y[:, o + HEAD_SLOT:o + 2 * HEAD_SLOT] * sk_ref[...]
    k_ref[0] = (_dot(ckv, wk_ref[...]) + jnp.tile(kr, (1, MLA_HEADS))).astype(BF16)
    v_ref[0] = _dot_nt(wv_ref[...], ckv).astype(BF16)


def _proj0(x, sc, sh, wts, tabs):
    b, n, d = x.shape
    tm = _tile(n, TOKEN_TILE)
    ck, sk, cq, sq = tabs
    win, lng, lnb, ws, bsb, qn, wq, kvn, wk, wv = wts
    hw = MLA_HEADS * HEAD_SLOT
    vw = MLA_HEADS * MLA_V
    full = lambda a: pl.BlockSpec(a.shape, lambda bi, ti: (0,) * a.ndim)
    q_scale = (MLA_NOPE + MLA_ROPE) ** -0.5 * LOG2E
    return pl.pallas_call(
        functools.partial(_proj0_kernel, q_scale=q_scale),
        out_shape=(jax.ShapeDtypeStruct((b, n, GMLP_WIDTH), BF16),
                   jax.ShapeDtypeStruct((b, hw, n), BF16),
                   jax.ShapeDtypeStruct((b, n, hw), BF16),
                   jax.ShapeDtypeStruct((b, vw, n), BF16)),
        grid=(b, n // tm),
        in_specs=[pl.BlockSpec((1, tm, d), lambda bi, ti: (bi, ti, 0)),
                  pl.BlockSpec((1, 1, d), lambda bi, ti: (bi, 0, 0)),
                  pl.BlockSpec((1, 1, d), lambda bi, ti: (bi, 0, 0)),
                  full(win), full(lng), full(lnb), full(ws), full(bsb),
                  full(qn), full(wq), full(kvn), full(wk), full(wv),
                  pl.BlockSpec((tm, HEAD_SLOT), lambda bi, ti: (ti, 0)),
                  pl.BlockSpec((tm, HEAD_SLOT), lambda bi, ti: (ti, 0)),
                  pl.BlockSpec((MLA_ROPE, tm), lambda bi, ti: (0, ti)),
                  pl.BlockSpec((MLA_ROPE, tm), lambda bi, ti: (0, ti))],
        out_specs=(pl.BlockSpec((1, tm, GMLP_WIDTH), lambda bi, ti: (bi, ti, 0)),
                   pl.BlockSpec((1, hw, tm), lambda bi, ti: (bi, 0, ti)),
                   pl.BlockSpec((1, tm, hw), lambda bi, ti: (bi, ti, 0)),
                   pl.BlockSpec((1, vw, tm), lambda bi, ti: (bi, 0, ti))),
        compiler_params=_cparams(("parallel", "parallel")),
        name="proj0",
    )(x, sc, sh, win, lng, lnb, ws, bsb, qn, wq, kvn, wk, wv, ck, sk, cq, sq)


def _proj1_kernel(x_ref, sc_ref, sh_ref, wq_ref, wk_ref, wv_ref, ck_ref, sk_ref, cq_ref, sq_ref,
                  q_ref, k_ref, v_ref, *, q_scale):
    h = (x_ref[0] * (1.0 + sc_ref[0]) + sh_ref[0]).astype(BF16)
    n_sub = 2 * DIFF_HEADS
    hd = DIFF_HEAD_DIM
    hr = hd // 2
    qt = _dot_nt(wq_ref[...], h) * q_scale
    cq_t = cq_ref[...]
    sq_t = sq_ref[...]
    for s in range(n_sub):
        r0 = s * hd
        x1 = qt[r0:r0 + hr]
        x2 = qt[r0 + hr:r0 + hd]
        swp = jnp.concatenate([x2, x1], axis=0)
        q_ref[0, r0:r0 + hd, :] = (qt[r0:r0 + hd] * cq_t + swp * sq_t).astype(BF16)
    k = _dot(h, wk_ref[...])
    width = k.shape[1]
    lane = lax.broadcasted_iota(jnp.int32, k.shape, 1)
    partner = jnp.where((lane % hd) < hr, pltpu.roll(k, width - hr, 1), pltpu.roll(k, hr, 1))
    reps = width // HEAD_SLOT
    k_ref[0] = (k * jnp.tile(ck_ref[...], (1, reps)) + partner * jnp.tile(sk_ref[...], (1, reps))).astype(BF16)
    v_ref[0] = _dot_nt(wv_ref[...], h).astype(BF16)


def _proj1(x, sc, sh, wts, tabs):
    b, n, d = x.shape
    tm = _tile(n, TOKEN_TILE)
    wq, wk, wv = wts
    ck, sk, cq, sq = tabs
    dw = wq.shape[0]
    full = lambda a: pl.BlockSpec(a.shape, lambda bi, ti: (0,) * a.ndim)
    q_scale = DIFF_HEAD_DIM ** -0.5 * LOG2E
    return pl.pallas_call(
        functools.partial(_proj1_kernel, q_scale=q_scale),
        out_shape=(jax.ShapeDtypeStruct((b, dw, n), BF16),
                   jax.ShapeDtypeStruct((b, n, dw), BF16),
                   jax.ShapeDtypeStruct((b, dw, n), BF16)),
        grid=(b, n // tm),
        in_specs=[pl.BlockSpec((1, tm, d), lambda bi, ti: (bi, ti, 0)),
                  pl.BlockSpec((1, 1, d), lambda bi, ti: (bi, 0, 0)),
                  pl.BlockSpec((1, 1, d), lambda bi, ti: (bi, 0, 0)),
                  full(wq), full(wk), full(wv),
                  pl.BlockSpec((tm, HEAD_SLOT), lambda bi, ti: (ti, 0)),
                  pl.BlockSpec((tm, HEAD_SLOT), lambda bi, ti: (ti, 0)),
                  pl.BlockSpec((DIFF_HEAD_DIM, tm), lambda bi, ti: (0, ti)),
                  pl.BlockSpec((DIFF_HEAD_DIM, tm), lambda bi, ti: (0, ti))],
        out_specs=(pl.BlockSpec((1, dw, tm), lambda bi, ti: (bi, 0, ti)),
                   pl.BlockSpec((1, tm, dw), lambda bi, ti: (bi, ti, 0)),
                   pl.BlockSpec((1, dw, tm), lambda bi, ti: (bi, 0, ti))),
        compiler_params=_cparams(("parallel", "parallel")),
        name="proj1",
    )(x, sc, sh, wq, wk, wv, ck, sk, cq, sq)


SHIFT_OK_LO = 2.0 ** -85
SHIFT_OK_HI = 2.0 ** 100
ATTN_TQ = 1024
ATTN_TK = 256
KMAX_ROWS = 32
ATTN_UNROLL = 16


def _key_norm_max(k_ref, kmax_ref, *, tk, n_sub):
    nchunk = k_ref.shape[1] // tk
    sub_w = HEAD_SLOT // n_sub
    row = lax.broadcasted_iota(jnp.int32, (SUBLANES, HEAD_SLOT), 0)
    lane = lax.broadcasted_iota(jnp.int32, (SUBLANES, HEAD_SLOT), 1)
    sel = jnp.where(lane // sub_w == row, 1.0, 0.0).astype(BF16)
    group = max(g for g in range(1, 12) if nchunk % g == 0)
    rows = group * tk

    def body(c, mx):
        off = pl.multiple_of(c * rows, rows)
        kc = k_ref[0, pl.ds(off, rows), :].astype(F32)
        return jnp.maximum(mx, _dot_nt(sel, (kc * kc).astype(BF16)))

    mx = lax.fori_loop(0, nchunk // group, body, jnp.zeros((SUBLANES, rows), F32))
    tile8 = jnp.broadcast_to(jnp.max(mx, axis=1, keepdims=True), (SUBLANES, HEAD_SLOT))
    kmax_ref[...] = jnp.tile(tile8, (kmax_ref.shape[0] // SUBLANES, 1))


def _sweep_bounded(qs, shifts, k_ref, v_ref, s_ref, *, tk, unroll):
    tq = qs[0].shape[1]
    dv = v_ref.shape[1]
    nchunk = k_ref.shape[1] // tk

    def aligned(off):
        return off if isinstance(off, int) else pl.multiple_of(off, tk)

    def scores(off, buf):
        kc = k_ref[0, pl.ds(aligned(off), tk), :]
        for r, q in enumerate(qs):
            s_ref[buf, r] = _dot(kc, q)

    def consume(off, buf, carry):
        vc = v_ref[0, :, pl.ds(aligned(off), tk)]
        nxt = []
        for r, ((l8, acc), m) in enumerate(zip(carry, shifts)):
            p = jnp.exp2(s_ref[buf, r] - m)
            l8 = l8 + jnp.sum(p.reshape(tk // SUBLANES, SUBLANES, tq), axis=0)
            nxt.append((l8, acc + _dot(vc, p.astype(BF16))))
        return tuple(nxt)

    def body(i, carry):
        for u in range(unroll):
            off = pl.multiple_of((i * unroll + u) * tk, tk)
            scores(off + tk, (u + 1) % 2)
            carry = consume(off, u % 2, carry)
        return carry

    carry = tuple((jnp.zeros((SUBLANES, tq), F32), jnp.zeros((dv, tq), F32)) for _ in qs)
    scores(0, 0)
    n_loop = (nchunk - 1) // unroll * unroll
    if n_loop:
        carry = lax.fori_loop(0, n_loop // unroll, body, carry)
    for c in range(n_loop, nchunk):
        if c + 1 < nchunk:
            scores((c + 1) * tk, (c + 1) % 2)
        carry = consume(c * tk, c % 2, carry)
    return [(jnp.sum(l8, axis=0, keepdims=True), acc) for l8, acc in carry]


def _sweep_running_max(qs, k_ref, v_ref, *, tk):
    tq = qs[0].shape[1]
    dv = v_ref.shape[1]

    def body(c, carry):
        off = pl.multiple_of(c * tk, tk)
        kc = k_ref[0, pl.ds(off, tk), :]
        vc = v_ref[0, :, pl.ds(off, tk)]
        nxt = []
        for (m, l, acc), q in zip(carry, qs):
            s = _dot(kc, q)
            mn = jnp.maximum(m, jnp.max(s, axis=0, keepdims=True))
            alpha = jnp.exp2(m - mn)
            p = jnp.exp2(s - mn)
            nxt.append((mn, alpha * l + jnp.sum(p, axis=0, keepdims=True), alpha * acc + _dot(vc, p.astype(BF16))))
        return tuple(nxt)

    init = tuple((jnp.full((1, tq), NEG_BIG, F32), jnp.zeros((1, tq), F32), jnp.zeros((dv, tq), F32)) for _ in qs)
    res = lax.fori_loop(0, k_ref.shape[1] // tk, body, init)
    return [(l, acc) for _, l, acc in res]


def _attn_kernel(*refs, tk, unroll, n_sub, n_src, finalize):
    q_ref = refs[0]
    srcs = [(refs[1 + 2 * i], refs[2 + 2 * i]) for i in range(n_src)]
    n_scratch = 3 if n_src == 1 else 5
    extra = refs[1 + 2 * n_src:-n_scratch]
    o_ref, kmax_ref, s_ref = refs[-n_scratch:][:3]
    k_ref, v_ref = srcs[0] if n_src == 1 else refs[-2:]

    @pl.when(pl.program_id(2) == 0)
    def _():
        if n_src > 1:
            off = 0
            for ks_ref, vs_ref in srcs:
                t = ks_ref.shape[1]
                k_ref[0, off:off + t] = ks_ref[0]
                v_ref[0, :, off:off + t] = vs_ref[0]
                off += t
        _key_norm_max(k_ref, kmax_ref, tk=tk, n_sub=n_sub)

    q = q_ref[0]
    sub_w = HEAD_SLOT // n_sub
    row = lax.broadcasted_iota(jnp.int32, q.shape, 0)
    qf = q.astype(F32)
    qs, shifts = [], []
    for r in range(n_sub):
        mine = (row // sub_w) == r
        qs.append(jnp.where(mine, q, jnp.zeros_like(q)) if n_sub > 1 else q)
        qn2 = jnp.sum(jnp.where(mine, qf * qf, 0.0), axis=0, keepdims=True)
        shifts.append(jnp.sqrt(qn2 * kmax_ref[r:r + 1, 0:1]))

    res = _sweep_bounded(qs, shifts, k_ref, v_ref, s_ref, tk=tk, unroll=unroll)
    o_ref[0] = finalize(res, extra).astype(BF16)
    ok = None
    for l, _ in res:
        ok_r = (jnp.min(l) >= SHIFT_OK_LO) & (jnp.max(l) <= SHIFT_OK_HI)
        ok = ok_r if ok is None else ok & ok_r

    @pl.when(jnp.logical_not(ok))
    def _():
        o_ref[0] = finalize(_sweep_running_max(qs, k_ref, v_ref, tk=tk), extra).astype(BF16)


def _finalize_plain(res, extra):
    (l, acc), = res
    return acc / l


def _finalize_diff(res, extra, *, lam_init):
    lam_ref, g_ref = extra
    (l1, a1), (l2, a2) = res
    lam = (jnp.exp(jnp.sum(lam_ref[0:1] * lam_ref[1:2], axis=-1, keepdims=True))
           - jnp.exp(jnp.sum(lam_ref[2:3] * lam_ref[3:4], axis=-1, keepdims=True)) + lam_init)
    o = a1 / l1 - lam * (a2 / l2)
    return o * lax.rsqrt(jnp.mean(o * o, axis=0, keepdims=True) + RMS_EPS) * g_ref[...] * (1.0 - lam_init)


def _attention(qt, kv, extra=(), *, dv, n_sub, finalize, name):
    b, hw, n = qt.shape
    heads = hw // HEAD_SLOT
    tq = _tile(n, ATTN_TQ)
    tk = ATTN_TK if all(k.shape[1] % ATTN_TK == 0 for k, _ in kv) else LANES
    full = lambda a: pl.BlockSpec(a.shape, lambda bi, hi, qi: (0,) * a.ndim)
    kv_specs, kv_args = [], []
    for k, vt in kv:
        t = k.shape[1]
        kv_specs += [pl.BlockSpec((1, t, HEAD_SLOT), lambda bi, hi, qi: (bi, 0, hi)),
                     pl.BlockSpec((1, dv, t), lambda bi, hi, qi: (bi, hi, 0))]
        kv_args += [k, vt]
    scratch = [pltpu.VMEM((KMAX_ROWS, HEAD_SLOT), F32), pltpu.VMEM((2, n_sub, tk, tq), F32)]
    if len(kv) > 1:
        t_all = sum(k.shape[1] for k, _ in kv)
        scratch += [pltpu.VMEM((1, t_all, HEAD_SLOT), BF16), pltpu.VMEM((1, dv, t_all), BF16)]
    return pl.pallas_call(
        functools.partial(_attn_kernel, tk=tk, unroll=ATTN_UNROLL, n_sub=n_sub, n_src=len(kv), finalize=finalize),
        out_shape=jax.ShapeDtypeStruct((b, heads * dv, n), BF16),
        grid=(b, heads, n // tq),
        in_specs=[pl.BlockSpec((1, HEAD_SLOT, tq), lambda bi, hi, qi: (bi, hi, qi))] + kv_specs
        + [full(a) for a in extra],
        out_specs=pl.BlockSpec((1, dv, tq), lambda bi, hi, qi: (bi, hi, qi)),
        scratch_shapes=scratch,
        compiler_params=_cparams(("parallel", "parallel", "arbitrary")),
        name=name,
    )(qt, *kv_args, *extra)


def _out_kernel(*refs, n_rowmajor):
    (x_ref, g_ref, lng_ref, lnb_ref, scf_ref, shf_ref, rt_ref), rest = refs[:7], refs[7:]
    x1_ref, aff_ref, hp_ref = rest[-3:]
    ops = rest[:-3]
    m = None
    for i in range(len(ops) // 2):
        a = ops[2 * i][0]
        w = ops[2 * i + 1][...]
        part = _dot(a, w) if i < n_rowmajor else _dot_tn(a, w)
        m = part if m is None else m + part
    x1 = _ln(DEEPNORM_ALPHA * x_ref[0] + (1.0 + g_ref[0]) * m, lng_ref[...], lnb_ref[...])
    x1_ref[0] = x1
    hf = (x1 * (1.0 + scf_ref[0]) + shf_ref[0]).astype(BF16)
    lg = _dot_nt(rt_ref[...], hf)
    e = jnp.exp(lg - jnp.max(lg, axis=0, keepdims=True))
    aff_ref[0] = e / jnp.sum(e, axis=0, keepdims=True)
    dh = hf.shape[1] // 2
    lo = pltpu.bitcast(hf[:, :dh].astype(F32), jnp.uint32)
    hi = pltpu.bitcast(hf[:, dh:].astype(F32), jnp.uint32)
    hp_ref[0] = _to_tiled(lax.shift_right_logical(lo, jnp.uint32(16)) | (hi & jnp.uint32(0xFFFF0000)))


def _mixer_out(x, g, lng, lnb, scf, shf, router_t, rowmajor_ops, chanmajor_ops):
    b, n, d = x.shape
    e = router_t.shape[0]
    tm = _tile(n, TOKEN_TILE)
    vec = pl.BlockSpec((1, 1, d), lambda bi, ti: (bi, 0, 0))
    full = lambda a: pl.BlockSpec(a.shape, lambda bi, ti: (0,) * a.ndim)
    args = [x, g, lng, lnb, scf, shf, router_t]
    specs = [pl.BlockSpec((1, tm, d), lambda bi, ti: (bi, ti, 0)), vec, full(lng), full(lnb), vec, vec,
             full(router_t)]
    for a, w in rowmajor_ops:
        args += [a, w]
        specs += [pl.BlockSpec((1, tm, a.shape[2]), lambda bi, ti: (bi, ti, 0)), full(w)]
    for a, w in chanmajor_ops:
        args += [a, w]
        specs += [pl.BlockSpec((1, a.shape[1], tm), lambda bi, ti: (bi, 0, ti)), full(w)]
    return pl.pallas_call(
        functools.partial(_out_kernel, n_rowmajor=len(rowmajor_ops)),
        out_shape=(jax.ShapeDtypeStruct((b, n, d), F32), jax.ShapeDtypeStruct((b, e, n), F32),
                   jax.ShapeDtypeStruct((b, n * (d // 2) // LANES, LANES), jnp.uint32)),
        grid=(b, n // tm),
        in_specs=specs,
        out_specs=(pl.BlockSpec((1, tm, d), lambda bi, ti: (bi, ti, 0)),
                   pl.BlockSpec((1, e, tm), lambda bi, ti: (bi, 0, ti)),
                   pl.BlockSpec((1, tm * (d // 2) // LANES, LANES), lambda bi, ti: (bi, ti, 0))),
        compiler_params=_cparams(("parallel", "parallel")),
        name="mixer_out",
    )(*args)


ROW_UNROLL = 8
FFN_ROWS = 512
TOKEN_SPLIT = 4096


def _ffn_kernel(src_ref, nxt_ref, hp_ref, g_ref, wg_ref, wu_ref, wd_ref, y_ref, rows_ref, xs_ref, *, prefetch):
    cap = xs_ref.shape[0]
    nt = rows_ref.shape[0] // cap
    dh = nt * LANES

    def gather(idx_ref, j, dst):
        rows_ref[pl.ds(dst, nt, stride=SUBLANES), :] = hp_ref[0, pl.ds(idx_ref[0, 0, j], nt, stride=SUBLANES), :]

    def gather_loop():
        def body(jb, carry):
            for u in range(ROW_UNROLL):
                gather(src_ref, jb * ROW_UNROLL + u, jb * (ROW_UNROLL * nt) + u)
            return carry

        lax.fori_loop(0, cap // ROW_UNROLL, body, 0)

    if prefetch:
        pl.when(pl.program_id(1) == 0)(gather_loop)
    else:
        gather_loop()

    blk = min(cap, FFN_ROWS)
    for c0 in range(0, cap, blk):
        w = _from_tiled(rows_ref[c0 * nt:(c0 + blk) * nt], dh)
        lo = pltpu.bitcast(lax.shift_left(w, jnp.uint32(16)), F32)
        hi = pltpu.bitcast(w & jnp.uint32(0xFFFF0000), F32)
        xs_ref[c0:c0 + blk] = jnp.concatenate([lo, hi], axis=1).astype(BF16)

    if prefetch:
        for j in range(cap):
            gather(nxt_ref, j, _tiled_row(j, dh))

    expert = pl.program_id(1 if prefetch else 0)
    mine = lax.broadcasted_iota(jnp.int32, (blk, LANES), 1) == expert
    for c0 in range(0, cap, blk):
        xs = xs_ref[c0:c0 + blk]
        gate = _dot(xs, wg_ref[0])
        hid = gate / (1.0 + jnp.exp(-gate)) * _dot(xs, wu_ref[0])
        g_col = jnp.sum(jnp.where(mine, g_ref[0, c0:c0 + blk], 0.0), axis=1, keepdims=True)
        y = (_dot(hid.astype(BF16), wd_ref[0]) * g_col).astype(BF16)
        y_ref[0, 0, c0 * nt:(c0 + blk) * nt] = _to_tiled(_pack_halves(y))


def _activations_outweigh(activation_bytes, weight_bytes):
    return activation_bytes > weight_bytes


def _moe_ffn(hp, src3, gates, wg, wu, wd, cap):
    b, hp_rows, _ = hp.shape
    n_exp, d, ff = wg.shape
    nt = d // 2 // LANES
    sample_major = _activations_outweigh(hp_rows * LANES * 4, 3 * d * ff * 2)
    if sample_major:
        grid, be = (b, n_exp), (lambda g0, g1: (g0, g1))
        hp_spec = pl.BlockSpec((1, hp_rows, LANES), lambda g0, g1: (g0, 0, 0), pipeline_mode=pl.Buffered(1))
    else:
        grid, be = (n_exp, b), (lambda g0, g1: (g1, g0))
        hp_spec = pl.BlockSpec((1, hp_rows, LANES), lambda g0, g1: (g1, 0, 0))
    slot = lambda g0, g1: (be(g0, g1)[0] * n_exp + be(g0, g1)[1], 0, 0)
    weight = lambda g0, g1: (be(g0, g1)[1], 0, 0)
    nxt = lambda g0, g1: (be(g0, g1)[0] * n_exp + jnp.minimum(be(g0, g1)[1] + 1, n_exp - 1), 0, 0)
    return pl.pallas_call(
        functools.partial(_ffn_kernel, prefetch=sample_major),
        out_shape=jax.ShapeDtypeStruct((b, n_exp, cap * nt, LANES), jnp.uint32),
        grid=grid,
        in_specs=[pl.BlockSpec((1, 1, cap), slot, memory_space=pltpu.SMEM),
                  pl.BlockSpec((1, 1, cap), nxt, memory_space=pltpu.SMEM),
                  hp_spec,
                  pl.BlockSpec((1, cap, LANES), lambda g0, g1: (be(g0, g1)[0], 0, 0)),
                  pl.BlockSpec((1, d, ff), weight),
                  pl.BlockSpec((1, d, ff), weight),
                  pl.BlockSpec((1, ff, d), weight)],
        out_specs=pl.BlockSpec((1, 1, cap * nt, LANES), lambda g0, g1: be(g0, g1) + (0, 0)),
        scratch_shapes=[pltpu.VMEM((cap * nt, LANES), jnp.uint32), pltpu.VMEM((cap, d), BF16)],
        compiler_params=_cparams(("parallel", "arbitrary")),
        name="moe_ffn",
    )(src3, src3, hp, gates, wg, wu, wd)


def _scatter_kernel(dst_ref, bnd_ref, y_ref, f_ref, *, d):
    nt = d // LANES
    ntp = nt // 2

    @pl.when(pl.program_id(2) == 0)
    def _():
        f_ref[...] = jnp.zeros_like(f_ref)

    lo = bnd_ref[0, 0, pl.program_id(1)]
    hi = bnd_ref[0, 0, pl.program_id(1) + 1]

    def f_row(start):
        return f_ref[0, pl.ds(start, nt, stride=SUBLANES), :]

    def y_row(start):
        return jnp.concatenate(_unpack_halves(y_ref[0, 0, pl.ds(start, ntp, stride=SUBLANES), :]), axis=0)

    def one(j, carry):
        dst = dst_ref[0, 0, j]
        f_ref[0, pl.ds(dst, nt, stride=SUBLANES), :] = f_row(dst) + y_row(_tiled_row(j, d // 2))
        return carry

    def batch(jb, carry):
        dsts = [dst_ref[0, 0, jb * ROW_UNROLL + u] for u in range(ROW_UNROLL)]
        sums = [f_row(dsts[u]) + y_row(jb * (ROW_UNROLL * ntp) + u) for u in range(ROW_UNROLL)]
        for u in range(ROW_UNROLL):
            f_ref[0, pl.ds(dsts[u], nt, stride=SUBLANES), :] = sums[u]
        return carry

    head_end = jnp.minimum((lo + ROW_UNROLL - 1) // ROW_UNROLL * ROW_UNROLL, hi)
    tail_start = jnp.maximum(head_end, hi // ROW_UNROLL * ROW_UNROLL)
    lax.fori_loop(lo, head_end, one, 0)
    lax.fori_loop(head_end // ROW_UNROLL, tail_start // ROW_UNROLL, batch, 0)
    lax.fori_loop(tail_start, hi, one, 0)


def _moe_scatter(y, dst3, bounds3, n, d):
    b, n_exp, y_rows, _ = y.shape
    nt = _tile(n, TOKEN_SPLIT)
    lt = d // LANES
    cap = y_rows * 2 // lt
    return pl.pallas_call(
        functools.partial(_scatter_kernel, d=d),
        out_shape=jax.ShapeDtypeStruct((b, n * lt, LANES), F32),
        grid=(b, n // nt, n_exp),
        in_specs=[pl.BlockSpec((1, 1, cap), lambda bi, ki, ei: (bi * n_exp + ei, 0, 0), memory_space=pltpu.SMEM),
                  pl.BlockSpec((1, 1, n // nt + 1), lambda bi, ki, ei: (bi * n_exp + ei, 0, 0),
                               memory_space=pltpu.SMEM),
                  pl.BlockSpec((1, 1, y_rows, LANES), lambda bi, ki, ei: (bi, ei, 0, 0))],
        out_specs=pl.BlockSpec((1, nt * lt, LANES), lambda bi, ki, ei: (bi, ki, 0)),
        compiler_params=_cparams(("parallel", "parallel", "arbitrary")),
        name="moe_scatter",
    )(dst3, bounds3, y)


def _post_kernel(x_ref, f_ref, g_ref, lng_ref, lnb_ref, o_ref):
    f = _from_tiled(f_ref[0], x_ref.shape[2])
    o_ref[0] = _ln(DEEPNORM_ALPHA * x_ref[0] + (1.0 + g_ref[0]) * f, lng_ref[...], lnb_ref[...])


def _post(x1, f_tiled, g, lng, lnb):
    b, n, d = x1.shape
    tm = _tile(n, TOKEN_TILE)
    blk = pl.BlockSpec((1, tm, d), lambda bi, ti: (bi, ti, 0))
    f_blk = pl.BlockSpec((1, tm * d // LANES, LANES), lambda bi, ti: (bi, ti, 0))
    full = lambda a: pl.BlockSpec(a.shape, lambda bi, ti: (0,) * a.ndim)
    return pl.pallas_call(
        _post_kernel,
        out_shape=jax.ShapeDtypeStruct((b, n, d), F32),
        grid=(b, n // tm),
        in_specs=[blk, f_blk, pl.BlockSpec((1, 1, d), lambda bi, ti: (bi, 0, 0)), full(lng), full(lnb)],
        out_specs=blk,
        compiler_params=_cparams(("parallel", "parallel")),
        name="ffn_post",
    )(x1, f_tiled, g, lng, lnb)


def _select_kernel(aff_ref, idx_ref, gate_ref, *, cap, n_exp):
    a = aff_ref[0]
    er, lanes = a.shape
    r_n = er // n_exp
    e_pad = 16
    bits = pltpu.bitcast(a, jnp.int32)

    member = (lax.broadcasted_iota(jnp.int32, (e_pad, er), 1) // r_n
              == lax.broadcasted_iota(jnp.int32, (e_pad, er), 0))
    gs = jnp.where(member, 1.0, 0.0).astype(BF16)
    member_t = (lax.broadcasted_iota(jnp.int32, (er, e_pad), 0) // r_n
                == lax.broadcasted_iota(jnp.int32, (er, e_pad), 1))
    gst = jnp.where(member_t, 1.0, 0.0).astype(BF16)
    ri = lax.broadcasted_iota(jnp.int32, (er, er), 0)
    rj = lax.broadcasted_iota(jnp.int32, (er, er), 1)
    rows_before = jnp.where((ri // r_n == rj // r_n) & (rj < ri), 1.0, 0.0).astype(BF16)
    li = lax.broadcasted_iota(jnp.int32, (lanes, lanes), 0)
    lj = lax.broadcasted_iota(jnp.int32, (lanes, lanes), 1)
    lanes_upto = jnp.where(li <= lj, 1.0, 0.0).astype(BF16)

    def expert_total(m):
        per_lane = _dot(gs, m.astype(BF16))
        return jnp.broadcast_to(jnp.sum(per_lane, axis=1, keepdims=True), per_lane.shape)

    def to_rows(ev):
        if r_n % SUBLANES == 0 and n_exp == e_pad:
            return jnp.broadcast_to(ev[:, None, :], (e_pad, r_n, lanes)).reshape(er, lanes)
        hi = jnp.floor(ev * (1.0 / 128.0))
        return 128.0 * _dot(gst, hi.astype(BF16)) + _dot(gst, (ev - 128.0 * hi).astype(BF16))

    def search(i, t):
        cand = t | lax.shift_left(jnp.int32(1), 30 - i)
        cnt = expert_total(jnp.where(bits >= cand, 1.0, 0.0))
        return jnp.where(to_rows(jnp.where(cnt >= cap, 1.0, 0.0)) > 0.5, cand, t)

    t = lax.fori_loop(0, 31, search, jnp.zeros((er, lanes), jnp.int32))
    gt = jnp.where(bits > t, 1.0, 0.0)
    eq = jnp.where(bits == t, 1.0, 0.0)

    def prefix(m):
        rc = _dot(m.astype(BF16), lanes_upto)
        off = _dot(rows_before, jnp.broadcast_to(rc[:, lanes - 1:lanes], m.shape).astype(BF16))
        return rc, off

    rc_eq, off_eq = prefix(eq)
    need = cap - to_rows(expert_total(gt))
    sel = jnp.maximum(gt, eq * jnp.where(off_eq + rc_eq - eq < need, 1.0, 0.0))
    rc, off = prefix(sel)
    row_end = off + jnp.broadcast_to(rc[:, lanes - 1:lanes], rc.shape)

    slot = lax.broadcasted_iota(jnp.int32, (cap, lanes), 0).astype(F32) + 1.0
    lane_c = lax.broadcasted_iota(jnp.int32, (cap, lanes), 1)
    lane_cf = lane_c.astype(F32)
    diag = lax.broadcasted_iota(jnp.int32, (r_n, lanes), 0) == lax.broadcasted_iota(jnp.int32, (r_n, lanes), 1)
    lane_r = lax.broadcasted_iota(jnp.int32, (1, lanes), 1)
    pad = jnp.zeros((lanes - r_n, lanes), F32)

    def table(m):
        return (jnp.concatenate([m, pad], axis=0) if r_n < lanes else m).astype(BF16)

    all_ones = jnp.ones((lanes, lanes), BF16)
    idx_all = jnp.zeros((cap, lanes), jnp.int32)
    gate_all = jnp.zeros((cap, lanes), F32)
    for e in range(n_exp):
        r0 = e * r_n
        end_lane = jnp.sum(jnp.where(diag, row_end[r0:r0 + r_n], 0.0), axis=0, keepdims=True)
        end_lane = jnp.where(lane_r < r_n, end_lane, 3.0e38)
        row_j = jnp.sum(jnp.where(end_lane < slot, 1.0, 0.0), axis=1, keepdims=True)
        oh = jnp.where(lane_cf == row_j, 1.0, 0.0).astype(BF16)
        cum_e = off[r0:r0 + r_n] + rc[r0:r0 + r_n]
        cum_hi = jnp.floor(cum_e * (1.0 / 128.0))
        cum_2 = _dot(oh, jnp.concatenate([table(cum_hi), table(cum_e - 128.0 * cum_hi)], axis=1))
        cum_j = 128.0 * cum_2[:, :lanes] + cum_2[:, lanes:]
        lane_j = _dot(jnp.where(cum_j < slot, 1.0, 0.0).astype(BF16), all_ones)
        a_e = a[r0:r0 + r_n]
        a1 = a_e.astype(BF16).astype(F32)
        a2 = (a_e - a1).astype(BF16).astype(F32)
        a3 = a_e - a1 - a2
        aff_2 = _dot(oh, jnp.concatenate([table(a1), table(a2)], axis=1))
        aff_j = aff_2[:, :lanes] + aff_2[:, lanes:] + _dot(oh, table(a3))
        gate_j = jnp.sum(jnp.where(lane_cf == lane_j, aff_j, 0.0), axis=1, keepdims=True)
        idx_j = (row_j * float(lanes) + lane_j).astype(jnp.int32)
        idx_all = jnp.where(lane_c == e, idx_j, idx_all)
        gate_all = jnp.where(lane_c == e, gate_j, gate_all)
    idx_ref[0] = idx_all
    gate_ref[0] = gate_all


def _moe_select(aff_t, cap):
    b, n_exp, n = aff_t.shape
    lanes = HEAD_SLOT
    pad = -n % (SUBLANES * lanes)
    aff_t = jnp.pad(aff_t, ((0, 0), (0, 0), (0, pad)))
    n += pad
    er = n_exp * (n // lanes)
    idx, gates = pl.pallas_call(
        functools.partial(_select_kernel, cap=cap, n_exp=n_exp),
        out_shape=(jax.ShapeDtypeStruct((b, cap, lanes), jnp.int32), jax.ShapeDtypeStruct((b, cap, lanes), F32)),
        grid=(b,),
        in_specs=[pl.BlockSpec((1, er, lanes), lambda bi: (bi, 0, 0))],
        out_specs=(pl.BlockSpec((1, cap, lanes), lambda bi: (bi, 0, 0)),
                   pl.BlockSpec((1, cap, lanes), lambda bi: (bi, 0, 0))),
        compiler_params=_cparams(("parallel",)),
        name="moe_select",
    )(aff_t.reshape(b, er, lanes))
    return gates, jnp.swapaxes(idx[:, :, :n_exp], 1, 2)


def _moe(x1, aff_t, hp, gf, wg, wu, wd, lng, lnb):
    b, n, d = x1.shape
    n_exp = aff_t.shape[1]
    cap = CAPACITY_FACTOR * n // n_exp
    gates, idx = _moe_select(aff_t, cap)
    nt = _tile(n, TOKEN_SPLIT)
    edges = jnp.arange(0, n + 1, nt, dtype=jnp.int32)
    bounds3 = jnp.sum(idx[..., None] < edges, axis=2, dtype=jnp.int32).reshape(b * n_exp, 1, n // nt + 1)
    src3 = _tiled_row(idx, d // 2).reshape(b * n_exp, 1, cap)
    dst3 = _tiled_row(idx % nt, d).reshape(b * n_exp, 1, cap)
    y = _moe_ffn(hp, src3, gates, wg, wu, wd, cap)
    f_tiled = _moe_scatter(y, dst3, bounds3, n, d)
    return _post(x1, f_tiled, gf, lng, lnb)


def _rope_angles(n_tokens, dim):
    n_rows = n_tokens // GRID_W
    row = jnp.repeat(jnp.arange(n_rows, dtype=F32), GRID_W)
    col = jnp.tile(jnp.arange(GRID_W, dtype=F32), n_rows)
    n_freq = dim // 4
    inv_freq = ROPE_BASE ** (-jnp.arange(n_freq, dtype=F32) / n_freq)
    ang = jnp.concatenate([row[:, None] * inv_freq, col[:, None] * inv_freq], -1)
    return jnp.cos(ang), jnp.sin(ang)


def _rope_tables(n_tokens, dim, lane_offset, identity):
    half = dim // 2
    if identity:
        cos = jnp.ones((n_tokens, half), F32)
        sin = jnp.zeros((n_tokens, half), F32)
    else:
        cos, sin = _rope_angles(n_tokens, dim)
    c2 = jnp.concatenate([cos, cos], -1)
    s2 = jnp.concatenate([-sin, sin], -1)
    reps = (HEAD_SLOT - lane_offset) // dim if lane_offset == 0 else 1
    ck = jnp.zeros((n_tokens, HEAD_SLOT), F32)
    sk = jnp.zeros((n_tokens, HEAD_SLOT), F32)
    for r in range(reps):
        ck = ck.at[:, lane_offset + r * dim:lane_offset + (r + 1) * dim].set(c2)
        sk = sk.at[:, lane_offset + r * dim:lane_offset + (r + 1) * dim].set(s2)
    return ck, sk, c2.T, s2.T


def _deinterleave(w, axis):
    return jnp.concatenate([jnp.take(w, jnp.arange(0, w.shape[axis], 2), axis=axis),
                            jnp.take(w, jnp.arange(1, w.shape[axis], 2), axis=axis)], axis=axis)


def _prep_layer0(w_in, ln_g, ln_b, ws, bs, q_norm, w_uq, kv_norm, w_ukv):
    d = w_in.shape[0]
    w = GMLP_WIDTH
    o_kr = 2 * w + MLA_Q_RANK + MLA_KV_RANK
    kr = _deinterleave(w_in[:, o_kr:o_kr + MLA_ROPE], 1)
    kr_sw = jnp.concatenate([kr[:, MLA_ROPE // 2:], kr[:, :MLA_ROPE // 2]], 1)
    slot = lambda m: jnp.zeros((d, HEAD_SLOT), F32).at[:, MLA_NOPE:MLA_NOPE + MLA_ROPE].set(m)
    win = jnp.concatenate([w_in[:, :o_kr], slot(kr), slot(kr_sw)], 1).astype(BF16)

    qd = MLA_NOPE + MLA_ROPE
    wq = w_uq.reshape(MLA_Q_RANK, MLA_HEADS, qd)
    wq = jnp.concatenate([wq[..., :MLA_NOPE], _deinterleave(wq[..., MLA_NOPE:], 2),
                          jnp.zeros((MLA_Q_RANK, MLA_HEADS, HEAD_SLOT - qd), F32)], -1)
    wq_t = wq.reshape(MLA_Q_RANK, MLA_HEADS * HEAD_SLOT).T.astype(BF16)

    wkv = w_ukv.reshape(MLA_KV_RANK, MLA_HEADS, MLA_NOPE + MLA_V)
    wk = jnp.concatenate([wkv[..., :MLA_NOPE], jnp.zeros((MLA_KV_RANK, MLA_HEADS, HEAD_SLOT - MLA_NOPE), F32)], -1)
    wk = wk.reshape(MLA_KV_RANK, MLA_HEADS * HEAD_SLOT).astype(BF16)
    wv_t = wkv[..., MLA_NOPE:].reshape(MLA_KV_RANK, MLA_HEADS * MLA_V).T.astype(BF16)

    bsb = jnp.broadcast_to(bs[:, :, None], (GMLP_GROUPS, CHUNK, GMLP_GROUP_CH))
    return (win, ln_g.reshape(1, -1), ln_b.reshape(1, -1), ws.astype(BF16), bsb,
            q_norm.reshape(1, -1), wq_t, kv_norm.reshape(1, -1), wk, wv_t)


def _prep_layer1(w_in):
    d = w_in.shape[0]
    dw = 2 * DIFF_HEADS * DIFF_HEAD_DIM
    perm = lambda m: _deinterleave(m.reshape(d, 2 * DIFF_HEADS, DIFF_HEAD_DIM), 2).reshape(d, dw)
    wq_t = perm(w_in[:, :dw]).T.astype(BF16)
    wk = perm(w_in[:, dw:2 * dw]).astype(BF16)
    wv_t = w_in[:, 2 * dw:].T.astype(BF16)
    return wq_t, wk, wv_t


def _split6(m_row):
    return [v[:, None, :] for v in jnp.split(m_row, 6, axis=-1)]


def kernel(x, c, ctx, c_ctx, w_mod_0, b_mod_0, w_in_0, gmlp_ln_g_0, gmlp_ln_b_0, gmlp_ws_0, gmlp_bs_0, mla_q_norm_0, mla_w_uq_0, mla_kv_norm_0, mla_w_ukv_0, w_out_0, ln_mix_g_0, ln_mix_b_0, router_0, w_gate_0, w_up_0, w_down_0, ln_ffn_g_0, ln_ffn_b_0, w_mod_1, b_mod_1, w_in_1, lambda_q1_1, lambda_k1_1, lambda_q2_1, lambda_k2_1, subln_g_1, w_out_1, ln_mix_g_1, ln_mix_b_1, router_1, w_gate_1, w_up_1, w_down_1, ln_ffn_g_1, ln_ffn_b_1):
    b, n, d = x.shape
    m_ctx = ctx.shape[1]
    row = lambda v: v.reshape(1, -1)

    cond = jnp.concatenate([c, c_ctx[None, :], jnp.zeros((-(b + 1) % 8, d), F32)], 0)
    mod0 = _modulation(cond, w_mod_0, b_mod_0)
    mod1 = _modulation(cond, w_mod_1, b_mod_1)
    sh_a0, sc_a0, g_a0, sh_f0, sc_f0, g_f0 = _split6(mod0[:b])
    csh_a0, csc_a0, cg_a0, csh_f0, csc_f0, cg_f0 = [jnp.broadcast_to(v, (b, 1, d)) for v in _split6(mod0[b:b + 1])]
    sh_a1, sc_a1, g_a1, sh_f1, sc_f1, g_f1 = _split6(mod1[:b])
    csh_a1, csc_a1 = [jnp.broadcast_to(v, (b, 1, d)) for v in _split6(mod1[b:b + 1])[:2]]

    wts0 = _prep_layer0(w_in_0, gmlp_ln_g_0, gmlp_ln_b_0, gmlp_ws_0, gmlp_bs_0,
                        mla_q_norm_0, mla_w_uq_0, mla_kv_norm_0, mla_w_ukv_0)
    tabs_l = _rope_tables(n, MLA_ROPE, MLA_NOPE, identity=False)
    tabs_c = _rope_tables(m_ctx, MLA_ROPE, MLA_NOPE, identity=True)
    a_l, q_l, k_l, v_l = _proj0(x, sc_a0, sh_a0, wts0, tabs_l)
    a_c, q_c, k_c, v_c = _proj0(ctx, csc_a0, csh_a0, wts0, tabs_c)
    o_l = _attention(q_l, [(k_l, v_l), (k_c, v_c)], dv=MLA_V, n_sub=1, finalize=_finalize_plain, name="mla_attn")
    o_c = _attention(q_c, [(k_c, v_c)], dv=MLA_V, n_sub=1, finalize=_finalize_plain, name="mla_attn_ctx")

    w_out_a = w_out_0[:GMLP_WIDTH].astype(BF16)
    w_out_o = w_out_0[GMLP_WIDTH:].astype(BF16)
    router_t0 = router_0.T.astype(BF16)
    wg0, wu0, wd0 = w_gate_0.astype(BF16), w_up_0.astype(BF16), w_down_0.astype(BF16)
    lng, lnb = row(ln_mix_g_0), row(ln_mix_b_0)
    x1, aff, hp = _mixer_out(x, g_a0, lng, lnb, sc_f0, sh_f0, router_t0, [(a_l, w_out_a)], [(o_l, w_out_o)])
    x_lat = _moe(x1, aff, hp, g_f0, wg0, wu0, wd0, row(ln_ffn_g_0), row(ln_ffn_b_0))
    c1, caff, chp = _mixer_out(ctx, cg_a0, lng, lnb, csc_f0, csh_f0, router_t0, [(a_c, w_out_a)], [(o_c, w_out_o)])
    x_ctx = _moe(c1, caff, chp, cg_f0, wg0, wu0, wd0, row(ln_ffn_g_0), row(ln_ffn_b_0))

    lam_init = 0.8 - 0.6 * math.exp(-0.3 * 1)
    wts1 = _prep_layer1(w_in_1)
    tabs_l = _rope_tables(n, DIFF_HEAD_DIM, 0, identity=False)
    tabs_c = _rope_tables(m_ctx, DIFF_HEAD_DIM, 0, identity=True)
    q_l, k_l, v_l = _proj1(x_lat, sc_a1, sh_a1, wts1, tabs_l)
    _, k_c, v_c = _proj1(x_ctx, csc_a1, csh_a1, wts1, tabs_c)
    lams = jnp.stack([lambda_q1_1, lambda_k1_1, lambda_q2_1, lambda_k2_1])
    lam_pack = jnp.zeros((KMAX_ROWS, LANES), F32).at[:4, :lams.shape[1]].set(lams)
    extra = (lam_pack, subln_g_1.reshape(-1, 1))
    o_l = _attention(q_l, [(k_l, v_l), (k_c, v_c)], extra, dv=2 * DIFF_HEAD_DIM, n_sub=2,
                     finalize=functools.partial(_finalize_diff, lam_init=lam_init), name="diff_attn")
    x1, aff, hp = _mixer_out(x_lat, g_a1, row(ln_mix_g_1), row(ln_mix_b_1), sc_f1, sh_f1, router_1.T.astype(BF16),
                             [], [(o_l, w_out_1.astype(BF16))])
    return _moe(x1, aff, hp, g_f1, w_gate_1.astype(BF16), w_up_1.astype(BF16), w_down_1.astype(BF16),
                row(ln_ffn_g_1), row(ln_ffn_b_1))
```
